```python
import math
import jax
import jax.numpy as jnp
from jax import lax
import numpy as np

D_MODEL = 1024
BATCH = 8
SEQ = 2048
DEPTH = 4

GRID_W = 64
CTX_LEN = 256
N_MIXERS = 3
Q_BLOCK = 128
ROPE_THETA = 10000.0
EPS = 1e-6
NEG_INF = -1e30

MLA_V = 64
MLA_HEADS = D_MODEL // MLA_V
MLA_NOPE = 64
MLA_ROPE = 32
MLA_QK = MLA_NOPE + MLA_ROPE
MLA_Q_LORA = 12 * MLA_V
MLA_KV_LORA = 4 * MLA_V

DIFF_HD = 64
DIFF_HEADS = D_MODEL // (2 * DIFF_HD)
DIFF_WIDTH = DIFF_HEADS * 2 * DIFF_HD

SWA_HD = 64
SWA_HEADS = D_MODEL // SWA_HD
SWA_KV_HEADS = 4
SWA_GROUP = SWA_HEADS // SWA_KV_HEADS
WINDOW = 128

PEER_HEADS = 8
PEER_KEYS = 128
PEER_EXPERTS = PEER_KEYS * PEER_KEYS
PEER_KEY_DIM = 128
PEER_TOPK = 16

kernel_name = 'hybrid_mla_diff_swa_peer_dit'


def rms_norm(t, g):
    tf = t.astype(jnp.float32)
    y = tf * lax.rsqrt(jnp.mean(tf * tf, axis=-1, keepdims=True) + EPS)
    return y.astype(t.dtype) * g


def modulate(t, g, shift, scale):
    return rms_norm(t, g) * (1 + scale) + shift


def rope_1d(t, pos):
    nf = t.shape[-1] // 2
    inv = ROPE_THETA ** (-jnp.arange(nf, dtype=jnp.float32) / nf)
    ang = pos.astype(jnp.float32)[:, None] * inv[None, :]
    shape = (pos.shape[0],) + (1,) * (t.ndim - 3) + (nf,)
    cos = jnp.cos(ang).reshape(shape).astype(t.dtype)
    sin = jnp.sin(ang).reshape(shape).astype(t.dtype)
    t1, t2 = t[..., :nf], t[..., nf:]
    return jnp.concatenate([t1 * cos - t2 * sin, t1 * sin + t2 * cos], axis=-1)


def axial_rope(t, row, col):
    half = t.shape[-1] // 2
    return jnp.concatenate([rope_1d(t[..., :half], row), rope_1d(t[..., half:], col)], axis=-1)


def grid_positions(n_tokens):
    rows = n_tokens // GRID_W
    row = jnp.repeat(jnp.arange(rows, dtype=jnp.int32), GRID_W)
    col = jnp.tile(jnp.arange(GRID_W, dtype=jnp.int32), rows)
    return row, col


def to_blocks(t):
    b, n = t.shape[:2]
    return jnp.moveaxis(t.reshape((b, n // Q_BLOCK, Q_BLOCK) + t.shape[2:]), 1, 0)


def from_blocks(t):
    nb, b, qb = t.shape[:3]
    return jnp.moveaxis(t, 0, 1).reshape((b, nb * qb) + t.shape[3:])


def band_blocks(t):
    b, n = t.shape[:2]
    tb = t.reshape((b, n // Q_BLOCK, Q_BLOCK) + t.shape[2:])
    tp = jnp.pad(tb, [(0, 0), (1, 1)] + [(0, 0)] * (tb.ndim - 2))
    nbhd = jnp.concatenate([tp[:, :-2], tp[:, 1:-1], tp[:, 2:]], axis=2)
    return jnp.moveaxis(nbhd, 1, 0)


def band_mask(n):
    nb = n // Q_BLOCK
    blk = jnp.arange(nb)[:, None, None]
    qpos = blk * Q_BLOCK + jnp.arange(Q_BLOCK)[None, :, None]
    kpos = (blk - 1) * Q_BLOCK + jnp.arange(3 * Q_BLOCK)[None, None, :]
    return (jnp.abs(kpos - qpos) <= WINDOW) & (kpos >= 0) & (kpos < n)


def softmax_attn(q, k, v, scale):
    s = jnp.einsum('bqhd,bkhd->bhqk', q, k).astype(jnp.float32) * scale
    p = jax.nn.softmax(s, axis=-1).astype(v.dtype)
    return jnp.einsum('bhqk,bkhd->bqhd', p, v)


def rope_tail(t, row, col):
    return jnp.concatenate([t[..., :MLA_NOPE], axial_rope(t[..., MLA_NOPE:], row, col)], axis=-1)


def mla_project(h, p, row, col, with_q):
    w_down, g_cq, g_ckv, w_uq, w_ukv, g_q, g_k = p
    b, n, _ = h.shape
    lat = h @ w_down
    cq = lat[..., :MLA_Q_LORA]
    ckv = lat[..., MLA_Q_LORA:MLA_Q_LORA + MLA_KV_LORA]
    k_rope = lat[..., MLA_Q_LORA + MLA_KV_LORA:]
    kv = (rms_norm(ckv, g_ckv) @ w_ukv).reshape(b, n, MLA_HEADS, MLA_NOPE + MLA_V)
    k = jnp.concatenate([kv[..., :MLA_NOPE],
                         jnp.broadcast_to(k_rope[:, :, None, :], (b, n, MLA_HEADS, MLA_ROPE))], axis=-1)
    k = rms_norm(k, g_k)
    v = kv[..., MLA_NOPE:]
    q = None
    if with_q:
        q = rms_norm((rms_norm(cq, g_cq) @ w_uq).reshape(b, n, MLA_HEADS, MLA_QK), g_q)
    if row is not None:
        k = rope_tail(k, row, col)
        q = rope_tail(q, row, col)
    return q, k, v


def mla_mixer(hx, hc, row, col, p, w_o, need_ctx):
    scale = MLA_QK ** -0.5
    b, n = hx.shape[:2]
    qx, kx, vx = mla_project(hx, p, row, col, True)
    qc, kc, vc = mla_project(hc, p, None, None, need_ctx)
    k_all = jnp.concatenate([kx, kc], axis=1)
    v_all = jnp.concatenate([vx, vc], axis=1)
    ox = from_blocks(lax.map(lambda qb: softmax_attn(qb, k_all, v_all, scale), to_blocks(qx)))
    yx = ox.reshape(b, n, -1) @ w_o
    yc = softmax_attn(qc, kc, vc, scale).reshape(b, hc.shape[1], -1) @ w_o if need_ctx else None
    return yx, yc


def diff_attn(q, k, v, lam, scale):
    s = jnp.einsum('bqhmd,bkhmd->bhmqk', q, k).astype(jnp.float32) * scale
    p = jax.nn.softmax(s, axis=-1)
    a = (p[:, :, 0] - lam * p[:, :, 1]).astype(v.dtype)
    return jnp.einsum('bhqk,bkhe->bqhe', a, v)


def diff_mixer(hx, hc, row, col, w_qkv, g_q, g_k, lam_p, g_sub, w_o, lam_init, need_ctx):
    scale = DIFF_HD ** -0.5
    lp = lam_p.astype(jnp.float32)
    lam = jnp.exp(jnp.sum(lp[0] * lp[1])) - jnp.exp(jnp.sum(lp[2] * lp[3])) + lam_init

    def project(h, rope, with_q):
        b, n, _ = h.shape
        qkv = h @ w_qkv
        k = rms_norm(qkv[..., DIFF_WIDTH:2 * DIFF_WIDTH].reshape(b, n, DIFF_HEADS, 2, DIFF_HD), g_k)
        v = qkv[..., 2 * DIFF_WIDTH:].reshape(b, n, DIFF_HEADS, 2 * DIFF_HD)
        q = rms_norm(qkv[..., :DIFF_WIDTH].reshape(b, n, DIFF_HEADS, 2, DIFF_HD), g_q) if with_q else None
        if rope:
            k = axial_rope(k, row, col)
            q = axial_rope(q, row, col)
        return q, k, v

    def finish(o):
        b, n = o.shape[:2]
        return (rms_norm(o, g_sub) * (1.0 - lam_init)).reshape(b, n, -1) @ w_o

    qx, kx, vx = project(hx, True, True)
    qc, kc, vc = project(hc, False, need_ctx)
    k_all = jnp.concatenate([kx, kc], axis=1)
    v_all = jnp.concatenate([vx, vc], axis=1)
    ox = from_blocks(lax.map(lambda qb: diff_attn(qb, k_all, v_all, lam, scale), to_blocks(qx)))
    yx = finish(ox)
    yc = finish(diff_attn(qc, kc, vc, lam, scale)) if need_ctx else None
    return yx, yc


def gqa_sink_attn(q, k, v, sink, scale, mask):
    s = jnp.einsum('bqhgd,bnhd->bhgqn', q, k).astype(jnp.float32) * scale
    if mask is not None:
        s = jnp.where(mask, s, NEG_INF)
    sink_col = jnp.broadcast_to(sink.astype(jnp.float32)[None, :, :, None, None], s.shape[:-1] + (1,))
    p = jax.nn.softmax(jnp.concatenate([s, sink_col], axis=-1), axis=-1)[..., :-1]
    return jnp.einsum('bhgqn,bnhd->bqhgd', p.astype(v.dtype), v)


def swa_mixer(hx, hc, row, col, w_qkv, g_q, g_k, sink, w_o, need_ctx):
    scale = SWA_HD ** -0.5
    qw = SWA_HEADS * SWA_HD
    kw = SWA_KV_HEADS * SWA_HD

    def project(h, rope, with_q):
        b, n, _ = h.shape
        qkv = h @ w_qkv
        k = rms_norm(qkv[..., qw:qw + kw].reshape(b, n, SWA_KV_HEADS, SWA_HD), g_k)
        v = qkv[..., qw + kw:].reshape(b, n, SWA_KV_HEADS, SWA_HD)
        q = rms_norm(qkv[..., :qw].reshape(b, n, SWA_KV_HEADS, SWA_GROUP, SWA_HD), g_q) if with_q else None
        if rope:
            k = axial_rope(k, row, col)
            q = axial_rope(q, row, col)
        return q, k, v

    b, n = hx.shape[:2]
    qx, kx, vx = project(hx, True, True)
    qc, kc, vc = project(hc, False, need_ctx)
    ctx_mask = jnp.ones((Q_BLOCK, hc.shape[1]), dtype=bool)

    def block(args):
        qb, kb, vb, mb = args
        return gqa_sink_attn(qb, jnp.concatenate([kb, kc], axis=1), jnp.concatenate([vb, vc], axis=1),
                             sink, scale, jnp.concatenate([mb, ctx_mask], axis=-1))

    ox = from_blocks(lax.map(block, (to_blocks(qx), band_blocks(kx), band_blocks(vx), band_mask(n))))
    yx = ox.reshape(b, n, -1) @ w_o
    yc = gqa_sink_attn(qc, kc, vc, sink, scale, None).reshape(b, hc.shape[1], -1) @ w_o if need_ctx else None
    return yx, yc


def peer(h, w_q, sub_keys, u_tab, v_tab):
    b, n, d = h.shape
    t = h.reshape(b * n, d)
    q = (t @ w_q).reshape(b * n, PEER_HEADS, 2, PEER_KEY_DIM)
    s = jnp.einsum('thcd,hcnd->thcn', q, sub_keys)
    sv, si = lax.top_k(s, PEER_TOPK)
    cand = (sv[:, :, 0, :, None] + sv[:, :, 1, None, :]).reshape(b * n, PEER_HEADS, PEER_TOPK * PEER_TOPK)
    score, pos = lax.top_k(cand, PEER_TOPK)
    i1 = jnp.take_along_axis(si[:, :, 0], pos // PEER_TOPK, axis=-1)
    i2 = jnp.take_along_axis(si[:, :, 1], pos % PEER_TOPK, axis=-1)
    expert = i1 * PEER_KEYS + i2
    gate = jax.nn.softmax(score.astype(jnp.float32), axis=-1).astype(h.dtype)

    def block(args):
        tb, eb, gb = args
        act = jax.nn.gelu(jnp.einsum('td,thkd->thk', tb, jnp.take(u_tab, eb, axis=0)), approximate=False)
        return jnp.einsum('thk,thkd->td', gb * act, jnp.take(v_tab, eb, axis=0))

    nb = (b * n) // Q_BLOCK
    out = lax.map(block, (t.reshape(nb, Q_BLOCK, d),
                          expert.reshape(nb, Q_BLOCK, PEER_HEADS, PEER_TOPK),
                          gate.reshape(nb, Q_BLOCK, PEER_HEADS, PEER_TOPK)))
    return out.reshape(b, n, d)


def setup_inputs(seed: int = 0) -> dict:
    key = jax.random.key(seed)
    ks = iter(jax.random.split(key, 48))
    n_mla = len(range(0, DEPTH, N_MIXERS))
    n_diff = len(range(1, DEPTH, N_MIXERS))
    n_swa = len(range(2, DEPTH, N_MIXERS))
    D = D_MODEL

    def nrm(shape, std=1.0):
        return jax.random.normal(next(ks), shape, jnp.float32) * std

    def w(shape, fan_in, gain=1.0):
        return nrm(shape, gain * fan_in ** -0.5)

    def g(shape):
        return 1.0 + nrm(shape, 0.02)

    return {
        'x': nrm((BATCH, SEQ, D)),
        'c': nrm((BATCH, D)),
        'ctx': nrm((BATCH, CTX_LEN, D)),
        'c_ctx': nrm((D,)),
        'ada_w': w((DEPTH, D, 6 * D), D, 0.5),
        'ada_b': nrm((DEPTH, 6 * D), 0.02),
        'norm_mix': g((DEPTH, D)),
        'norm_ffn': g((DEPTH, D)),
        'mla_w_down': w((n_mla, D, MLA_Q_LORA + MLA_KV_LORA + MLA_ROPE), D),
        'mla_g_cq': g((n_mla, MLA_Q_LORA)),
        'mla_g_ckv': g((n_mla, MLA_KV_LORA)),
        'mla_w_uq': w((n_mla, MLA_Q_LORA, MLA_HEADS * MLA_QK), MLA_Q_LORA),
        'mla_w_ukv': w((n_mla, MLA_KV_LORA, MLA_HEADS * (MLA_NOPE + MLA_V)), MLA_KV_LORA),
        'mla_g_q': g((n_mla, MLA_QK)),
        'mla_g_k': g((n_mla, MLA_QK)),
        'mla_w_o': w((n_mla, MLA_HEADS * MLA_V, D), MLA_HEADS * MLA_V),
        'diff_w_qkv': w((n_diff, D, 3 * DIFF_WIDTH), D),
        'diff_g_q': g((n_diff, DIFF_HD)),
        'diff_g_k': g((n_diff, DIFF_HD)),
        'diff_lambda': nrm((n_diff, 4, DIFF_HD), 0.1),
        'diff_g_sub': g((n_diff, 2 * DIFF_HD)),
        'diff_w_o': w((n_diff, DIFF_WIDTH, D), DIFF_WIDTH),
        'swa_w_qkv': w((n_swa, D, (SWA_HEADS + 2 * SWA_KV_HEADS) * SWA_HD), D),
        'swa_g_q': g((n_swa, SWA_HD)),
        'swa_g_k': g((n_swa, SWA_HD)),
        'swa_sink': nrm((n_swa, SWA_KV_HEADS, SWA_GROUP), 0.5),
        'swa_w_o': w((n_swa, SWA_HEADS * SWA_HD, D), SWA_HEADS * SWA_HD),
        'peer_w_q': w((DEPTH, D, PEER_HEADS * 2 * PEER_KEY_DIM), D),
        'peer_keys': w((DEPTH, PEER_HEADS, 2, PEER_KEYS, PEER_KEY_DIM), PEER_KEY_DIM),
        'peer_u': w((DEPTH, PEER_EXPERTS, D), D),
        'peer_v': w((DEPTH, PEER_EXPERTS, D), PEER_HEADS),
    }


def reference(x, c, ctx, c_ctx, ada_w, ada_b, norm_mix, norm_ffn,
              mla_w_down, mla_g_cq, mla_g_ckv, mla_w_uq, mla_w_ukv, mla_g_q, mla_g_k, mla_w_o,
              diff_w_qkv, diff_g_q, diff_g_k, diff_lambda, diff_g_sub, diff_w_o,
              swa_w_qkv, swa_g_q, swa_g_k, swa_sink, swa_w_o,
              peer_w_q, peer_keys, peer_u, peer_v):
    n_ctx = ctx.shape[1]
    row, col = grid_positions(x.shape[1])
    silu_c = jax.nn.silu(c)
    silu_cc = jax.nn.silu(c_ctx)
    for i in range(DEPTH):
        last = i == DEPTH - 1
        kind, j = i % N_MIXERS, i // N_MIXERS
        mod_x = jnp.split((silu_c @ ada_w[i] + ada_b[i])[:, None, :], 6, axis=-1)
        mod_c = jnp.split(silu_cc @ ada_w[i] + ada_b[i], 6, axis=-1)
        hx = modulate(x, norm_mix[i], mod_x[0], mod_x[1])
        hc = modulate(ctx, norm_mix[i], mod_c[0], mod_c[1])
        if kind == 0:
            p = (mla_w_down[j], mla_g_cq[j], mla_g_ckv[j], mla_w_uq[j], mla_w_ukv[j], mla_g_q[j], mla_g_k[j])
            yx, yc = mla_mixer(hx, hc, row, col, p, mla_w_o[j], not last)
        elif kind == 1:
            lam_init = 0.8 - 0.6 * math.exp(-0.3 * i)
            yx, yc = diff_mixer(hx, hc, row, col, diff_w_qkv[j], diff_g_q[j], diff_g_k[j], diff_lambda[j],
                                diff_g_sub[j], diff_w_o[j], lam_init, not last)
        else:
            yx, yc = swa_mixer(hx, hc, row, col, swa_w_qkv[j], swa_g_q[j], swa_g_k[j], swa_sink[j],
                               swa_w_o[j], not last)
        x = x + mod_x[2] * yx
        hx = modulate(x, norm_ffn[i], mod_x[3], mod_x[4])
        if last:
            x = x + mod_x[5] * peer(hx, peer_w_q[i], peer_keys[i], peer_u[i], peer_v[i])
        else:
            ctx = ctx + mod_c[2] * yc
            hc = modulate(ctx, norm_ffn[i], mod_c[3], mod_c[4])
            y = peer(jnp.concatenate([hc, hx], axis=1), peer_w_q[i], peer_keys[i], peer_u[i], peer_v[i])
            ctx = ctx + mod_c[5] * y[:, :n_ctx]
            x = x + mod_x[5] * y[:, n_ctx:]
    return x
```

```python
import functools
import math

import jax
import jax.numpy as jnp
from jax import lax
from jax.experimental import pallas as pl
from jax.experimental.pallas import tpu as pltpu

F32 = jnp.float32
BF16 = jnp.bfloat16

LANES = 128
EPS = 1e-6
NEG_INF = -1e30
NEG_BIG = -3.0e38
ROPE_THETA = 10000.0
GRID_W = 64
Q_BLOCK = 128
WINDOW = 128

MLA_HEADS, MLA_NOPE, MLA_ROPE, MLA_V = 16, 64, 32, 64
MLA_QK = MLA_NOPE + MLA_ROPE
MLA_Q_LORA, MLA_KV_LORA = 768, 256
DIFF_HEADS, DIFF_HD = 8, 64
SWA_HEADS, SWA_KV_HEADS, SWA_HD = 16, 4, 64
SWA_GROUP = SWA_HEADS // SWA_KV_HEADS
PEER_HEADS, PEER_KEYS, PEER_TOPK = 8, 128, 16
PEER_EXPERTS = PEER_KEYS * PEER_KEYS

VMEM_LIMIT = 56 * 1024 * 1024

TM = 256
TQ = 256
PEER_TM = 512
SUBLANES = 8
PEER_TE = SUBLANES * LANES


def _cparams(sem):
    return pltpu.CompilerParams(dimension_semantics=sem, vmem_limit_bytes=VMEM_LIMIT)


def _ada_kernel(c_ref, w_ref, b_ref, o_ref):
    c = c_ref[...]
    sc = c * (1.0 / (1.0 + jnp.exp(-c)))
    o_ref[0] = jnp.dot(sc, w_ref[0], preferred_element_type=F32,
                       precision=lax.Precision.HIGHEST) + b_ref[0]


def _ada_mods(cc, ada_w, ada_b):
    depth, d, d6 = ada_w.shape
    rows = cc.shape[0]
    tn = 1536
    return pl.pallas_call(
        _ada_kernel,
        grid=(depth, d6 // tn),
        in_specs=[pl.BlockSpec((rows, d), lambda i, n: (0, 0)),
                  pl.BlockSpec((1, d, tn), lambda i, n: (i, 0, n)),
                  pl.BlockSpec((1, 1, tn), lambda i, n: (i, 0, n))],
        out_specs=pl.BlockSpec((1, rows, tn), lambda i, n: (i, 0, n)),
        out_shape=jax.ShapeDtypeStruct((depth, rows, d6), F32),
        compiler_params=_cparams(("arbitrary", "arbitrary")),
        name="ada_mods",
    )(cc, ada_w, ada_b.reshape(depth, 1, d6))


def _norm_mod(x, g, mod, shift_row, scale_row):
    ms = jnp.mean(x * x, axis=-1, keepdims=True)
    y = x * lax.rsqrt(ms + EPS) * g
    if mod is not None:
        y = y * (1.0 + mod[scale_row:scale_row + 1]) + mod[shift_row:shift_row + 1]
    return y


def _nm_kernel(x_ref, g_ref, mod_ref, w_ref, o_ref, h_scr, *, use_mod, shift_row, scale_row):
    @pl.when(pl.program_id(1) == 0)
    def _():
        mod = mod_ref[0] if use_mod else None
        h_scr[...] = _norm_mod(x_ref[...], g_ref[...], mod, shift_row, scale_row).astype(BF16)

    o_ref[...] = jnp.dot(h_scr[...], w_ref[...], preferred_element_type=F32).astype(o_ref.dtype)


def _mod_index(i, nlat_blocks, blocks_per_batch, n_batch):
    return jnp.where(i < nlat_blocks, i // blocks_per_batch, n_batch)


def _nm_matmul(x, kblk, k, g, modv, rows, w, *, n_lat_rows, n_per_batch, out_dtype=F32,
               use_mod=True, shift_row=0, scale_row=1, name="nm_matmul"):
    nout = w.shape[1]
    tn = nout
    for cand in (2048, 1536, 1280, 1024):
        if nout > 2048 and nout % cand == 0:
            tn = cand
            break
    n_batch = modv.shape[0] - 1
    nlat_blocks = n_lat_rows // TM
    bpb = n_per_batch // TM
    kern = functools.partial(_nm_kernel, use_mod=use_mod, shift_row=shift_row, scale_row=scale_row)
    return pl.pallas_call(
        kern,
        grid=(rows // TM, nout // tn),
        in_specs=[pl.BlockSpec((TM, k), lambda i, n: (i, kblk)),
                  pl.BlockSpec((1, k), lambda i, n: (0, 0)),
                  pl.BlockSpec((1, 8, k), lambda i, n: (_mod_index(i, nlat_blocks, bpb, n_batch), 0, 0)),
                  pl.BlockSpec((k, tn), lambda i, n: (0, n))],
        out_specs=pl.BlockSpec((TM, tn), lambda i, n: (i, n)),
        out_shape=jax.ShapeDtypeStruct((rows, nout), out_dtype),
        scratch_shapes=[pltpu.VMEM((TM, k), BF16)],
        compiler_params=_cparams(("parallel", "arbitrary")),
        name=name,
    )(x, g.reshape(1, k), modv if modv.shape[-1] == k else jnp.zeros((n_batch + 1, 8, k), F32), w)


def _proj_resid_kernel(a_ref, w_ref, s_ref, mod_ref, o_ref, *, gate_row):
    y = jnp.dot(a_ref[...].astype(BF16), w_ref[...], preferred_element_type=F32)
    o_ref[...] = s_ref[...] + mod_ref[0][gate_row:gate_row + 1] * y


def _proj_resid(a, w, s, modv, rows, *, n_lat_rows, n_per_batch, gate_row, name="proj_resid"):
    ka = a.shape[1]
    d = w.shape[1]
    n_batch = modv.shape[0] - 1
    nlat_blocks = n_lat_rows // TM
    bpb = n_per_batch // TM
    return pl.pallas_call(
        functools.partial(_proj_resid_kernel, gate_row=gate_row),
        grid=(rows // TM,),
        in_specs=[pl.BlockSpec((TM, ka), lambda i: (i, 0)),
                  pl.BlockSpec((ka, d), lambda i: (0, 0)),
                  pl.BlockSpec((TM, d), lambda i: (i, 0)),
                  pl.BlockSpec((1, 8, d), lambda i: (_mod_index(i, nlat_blocks, bpb, n_batch), 0, 0))],
        out_specs=pl.BlockSpec((TM, d), lambda i: (i, 0)),
        out_shape=jax.ShapeDtypeStruct((rows, d), F32),
        compiler_params=_cparams(("parallel",)),
        name=name,
    )(a, w, s, modv)


def _rope_tables(n_lat, extra_rows, segments, nf):
    pos = jnp.arange(n_lat, dtype=jnp.int32)
    row = (pos // GRID_W).astype(F32)
    col = (pos % GRID_W).astype(F32)
    inv = ROPE_THETA ** (-jnp.arange(nf, dtype=F32) / nf)
    cos = jnp.ones((n_lat, LANES), F32)
    sa = jnp.zeros((n_lat, LANES), F32)
    sb = jnp.zeros((n_lat, LANES), F32)
    for first, use_row in segments:
        ang = (row if use_row else col)[:, None] * inv[None, :]
        c, s = jnp.cos(ang), jnp.sin(ang)
        cos = cos.at[:, first:first + nf].set(c).at[:, first + nf:first + 2 * nf].set(c)
        sa = sa.at[:, first:first + nf].set(-s)
        sb = sb.at[:, first + nf:first + 2 * nf].set(s)
    pad = lambda t, v: jnp.concatenate([t, jnp.full((extra_rows, LANES), v, F32)], axis=0)
    return pad(cos, 1.0), pad(sa, 0.0), pad(sb, 0.0)


def _head_norm(x, g, d):
    ssq = jnp.sum(x * x, axis=-1, keepdims=True) * (1.0 / d)
    return x * lax.rsqrt(ssq + EPS) * g


def _rope(y, cos, sa, sb, nf):
    return y * cos + pltpu.roll(y, LANES - nf, 1) * sa + pltpu.roll(y, nf, 1) * sb


def _row_block(b, qi, nlat_q, nctx_q, n_batch):
    return jnp.where(qi < nlat_q, b * nlat_q + qi, n_batch * nlat_q + b * nctx_q + (qi - nlat_q))


def _prep_keys(k_scr, v_scr, n_lat, n_ctx, load_k_lat, load_k_ctx, load_v_lat, load_v_ctx,
               gk, tabs, d, nf):
    cos_ref, sa_ref, sb_ref = tabs
    ch = 256

    def lat_body(c, carry):
        r = pl.ds(pl.multiple_of(c * ch, ch), ch)
        kk = _head_norm(load_k_lat(r), gk, d)
        kk = _rope(kk, cos_ref[r, :], sa_ref[r, :], sb_ref[r, :], nf)
        k_scr[r, :] = kk.astype(BF16)
        if v_scr is not None:
            v_scr[r, :] = load_v_lat(r).astype(BF16)
        return carry

    lax.fori_loop(0, n_lat // ch, lat_body, 0)

    def ctx_body(c, carry):
        r = pl.ds(pl.multiple_of(c * ch, ch), ch)
        ro = pl.ds(pl.multiple_of(n_lat + c * ch, ch), ch)
        k_scr[ro, :] = _head_norm(load_k_ctx(r), gk, d).astype(BF16)
        if v_scr is not None:
            v_scr[ro, :] = load_v_ctx(r).astype(BF16)
        return carry

    lax.fori_loop(0, n_ctx // ch, ctx_body, 0)


def _nt_dot(a, b):
    return lax.dot_general(a, b, (((1,), (1,)), ((), ())), preferred_element_type=F32)


def _mla_attn_kernel(q_ref, kl_ref, kc_ref, krl_ref, krc_ref, vl_ref, vc_ref,
                     cq_ref, saq_ref, sbq_ref, ck_ref, sak_ref, sbk_ref, gq_ref, gk_ref,
                     o_ref, k_scr, v_scr, *, n_lat, n_ctx, nlat_q, scale):
    qi = pl.program_id(2)
    nf = MLA_ROPE // 4

    @pl.when(qi == 0)
    def _():
        _prep_keys(k_scr, v_scr, n_lat, n_ctx,
                   lambda r: kl_ref[r, :] + krl_ref[r, :],
                   lambda r: kc_ref[r, :] + krc_ref[r, :],
                   lambda r: vl_ref[r, :], lambda r: vc_ref[r, :],
                   gk_ref[...], (ck_ref, sak_ref, sbk_ref), MLA_QK, nf)

    q = _head_norm(q_ref[...], gq_ref[...], MLA_QK)
    q = _rope(q, cq_ref[...], saq_ref[...], sbq_ref[...], nf) * scale
    s = _nt_dot(q.astype(BF16), k_scr[...])
    col = lax.broadcasted_iota(jnp.int32, s.shape, 1)
    s = jnp.where(col >= jnp.where(qi >= nlat_q, n_lat, 0), s, NEG_INF)
    m = jnp.max(s, axis=-1, keepdims=True)
    p = jnp.exp(s - m)
    l = jnp.sum(p, axis=-1, keepdims=True)
    o = jnp.dot(p.astype(BF16), v_scr[...], preferred_element_type=F32)
    o_ref[...] = (o * (1.0 / l)).astype(o_ref.dtype)


def _mla_attention(q, kv, lat, g_q, g_k, *, n_batch, n_lat, n_ctx, need_ctx):
    nlat_q, nctx_q = n_lat // TQ, n_ctx // TQ
    nq = nlat_q + (nctx_q if need_ctx else 0)
    rows = n_batch * (n_lat + (n_ctx if need_ctx else 0))
    nf = MLA_ROPE // 4
    tabs = _rope_tables(n_lat, TQ, ((MLA_NOPE, True), (MLA_NOPE + 2 * nf, False)), nf)
    h = MLA_HEADS
    ctx0 = n_batch * n_lat // n_ctx
    kr_blk = (MLA_Q_LORA + MLA_KV_LORA) // LANES
    gq = jnp.zeros((1, LANES), F32).at[0, :MLA_QK].set(g_q)
    gk = jnp.zeros((1, LANES), F32).at[0, :MLA_QK].set(g_k)
    qspec = pl.BlockSpec((TQ, LANES), lambda b, hh, qi: (_row_block(b, qi, nlat_q, nctx_q, n_batch), hh))
    tq_spec = pl.BlockSpec((TQ, LANES), lambda b, hh, qi: (jnp.minimum(qi, nlat_q), 0))
    tk_spec = pl.BlockSpec((n_lat, LANES), lambda b, hh, qi: (0, 0))
    vec = pl.BlockSpec((1, LANES), lambda b, hh, qi: (0, 0))
    kern = functools.partial(_mla_attn_kernel, n_lat=n_lat, n_ctx=n_ctx, nlat_q=nlat_q,
                             scale=MLA_QK ** -0.5)
    return pl.pallas_call(
        kern,
        grid=(n_batch, h, nq),
        in_specs=[qspec,
                  pl.BlockSpec((n_lat, LANES), lambda b, hh, qi: (b, hh)),
                  pl.BlockSpec((n_ctx, LANES), lambda b, hh, qi: (ctx0 + b, hh)),
                  pl.BlockSpec((n_lat, LANES), lambda b, hh, qi: (b, kr_blk)),
                  pl.BlockSpec((n_ctx, LANES), lambda b, hh, qi: (ctx0 + b, kr_blk)),
                  pl.BlockSpec((n_lat, LANES), lambda b, hh, qi: (b, h + hh)),
                  pl.BlockSpec((n_ctx, LANES), lambda b, hh, qi: (ctx0 + b, h + hh)),
                  tq_spec, tq_spec, tq_spec, tk_spec, tk_spec, tk_spec, vec, vec],
        out_specs=qspec,
        out_shape=jax.ShapeDtypeStruct((rows, h * LANES), BF16),
        scratch_shapes=[pltpu.VMEM((n_lat + n_ctx, LANES), BF16),
                        pltpu.VMEM((n_lat + n_ctx, LANES), BF16)],
        compiler_params=_cparams(("parallel", "parallel", "arbitrary")),
        name="mla_attention",
    )(q, kv, kv, lat, lat, kv, kv, *tabs, *tabs, gq, gk)


def _pad_cols(w, groups, width, first=0):
    k = w.shape[0]
    w3 = w.reshape(k, groups, width)
    out = jnp.zeros((k, groups, LANES), w.dtype).at[:, :, first:first + width].set(w3)
    return out.reshape(k, groups * LANES)


def _pad_rows(w, groups, width):
    d = w.shape[1]
    w3 = w.reshape(groups, width, d)
    return jnp.zeros((groups, LANES, d), w.dtype).at[:, :width, :].set(w3).reshape(groups * LANES, d)


def _mla_mixer(s, modv, g_mix, p, w_o, *, n_batch, n_lat, n_ctx, need_ctx):
    w_down, g_cq, g_ckv, w_uq, w_ukv, g_q, g_k = p
    d = s.shape[1]
    rows_all = n_batch * (n_lat + n_ctx)
    geo = dict(n_lat_rows=n_batch * n_lat, n_per_batch=n_lat)
    wd = jnp.concatenate([w_down[:, :MLA_Q_LORA + MLA_KV_LORA],
                          _pad_cols(w_down[:, MLA_Q_LORA + MLA_KV_LORA:], 1, MLA_ROPE, MLA_NOPE)], axis=1)
    lat = _nm_matmul(s, 0, d, g_mix, modv, rows_all, wd.astype(BF16), name="mla_down", **geo)
    wq = _pad_cols(w_uq, MLA_HEADS, MLA_QK).astype(BF16)
    q = _nm_matmul(lat, 0, MLA_Q_LORA, g_cq, modv, rows_all, wq, use_mod=False, name="mla_uq", **geo)
    wkv3 = w_ukv.reshape(MLA_KV_LORA, MLA_HEADS, MLA_NOPE + MLA_V)
    wkv = jnp.concatenate([_pad_cols(wkv3[:, :, :MLA_NOPE].reshape(MLA_KV_LORA, -1), MLA_HEADS, MLA_NOPE),
                           _pad_cols(wkv3[:, :, MLA_NOPE:].reshape(MLA_KV_LORA, -1), MLA_HEADS, MLA_V)],
                          axis=1).astype(BF16)
    kv = _nm_matmul(lat, MLA_Q_LORA // MLA_KV_LORA, MLA_KV_LORA, g_ckv, modv, rows_all, wkv,
                    use_mod=False, name="mla_ukv", **geo)
    o = _mla_attention(q, kv, lat, g_q, g_k, n_batch=n_batch, n_lat=n_lat, n_ctx=n_ctx, need_ctx=need_ctx)
    wo = _pad_rows(w_o, MLA_HEADS, MLA_V).astype(BF16)
    return o, wo


def _diff_attn_kernel(q0_ref, q1_ref, k0l_ref, k0c_ref, k1l_ref, k1c_ref, vl_ref, vc_ref,
                      cq_ref, saq_ref, sbq_ref, ck_ref, sak_ref, sbk_ref, gq_ref, gk_ref,
                      lam_ref, gsub_ref, o_ref, k0_scr, k1_scr, v_scr,
                      *, n_lat, n_ctx, nlat_q, scale, lam_init):
    qi = pl.program_id(2)
    nf = DIFF_HD // 4
    tabs = (ck_ref, sak_ref, sbk_ref)

    @pl.when(qi == 0)
    def _():
        _prep_keys(k0_scr, v_scr, n_lat, n_ctx, lambda r: k0l_ref[r, :], lambda r: k0c_ref[r, :],
                   lambda r: vl_ref[r, :], lambda r: vc_ref[r, :], gk_ref[...], tabs, DIFF_HD, nf)
        _prep_keys(k1_scr, None, n_lat, n_ctx, lambda r: k1l_ref[r, :], lambda r: k1c_ref[r, :],
                   None, None, gk_ref[...], tabs, DIFF_HD, nf)

    lp = lam_ref[...]
    lam = (jnp.exp(jnp.sum(lp[0:1] * lp[1:2], axis=-1, keepdims=True))
           - jnp.exp(jnp.sum(lp[2:3] * lp[3:4], axis=-1, keepdims=True)) + lam_init)
    lo = jnp.where(qi >= nlat_q, n_lat, 0)

    def probs(q_ref, k_scr):
        q = _head_norm(q_ref[...], gq_ref[...], DIFF_HD)
        q = _rope(q, cq_ref[...], saq_ref[...], sbq_ref[...], nf) * scale
        s = _nt_dot(q.astype(BF16), k_scr[...])
        col = lax.broadcasted_iota(jnp.int32, s.shape, 1)
        s = jnp.where(col >= lo, s, NEG_INF)
        m = jnp.max(s, axis=-1, keepdims=True)
        p = jnp.exp(s - m)
        return p, jnp.sum(p, axis=-1, keepdims=True)

    p0, l0 = probs(q0_ref, k0_scr)
    p1, l1 = probs(q1_ref, k1_scr)
    a = p0 * (1.0 / l0) - p1 * (lam / l1)
    o = jnp.dot(a.astype(BF16), v_scr[...], preferred_element_type=F32)
    o = _head_norm(o, gsub_ref[...], 2 * DIFF_HD) * (1.0 - lam_init)
    o_ref[...] = o.astype(o_ref.dtype)


def _diff_attention(qkv, g_q, g_k, lam_p, g_sub, lam_init, *, n_batch, n_lat, n_ctx, need_ctx):
    nlat_q, nctx_q = n_lat // TQ, n_ctx // TQ
    nq = nlat_q + (nctx_q if need_ctx else 0)
    rows = n_batch * (n_lat + (n_ctx if need_ctx else 0))
    nf = DIFF_HD // 4
    tabs = _rope_tables(n_lat, TQ, ((0, True), (2 * nf, False)), nf)
    nh = DIFF_HEADS
    ctx0 = n_batch * n_lat // n_ctx
    gq = jnp.zeros((1, LANES), F32).at[0, :DIFF_HD].set(g_q)
    gk = jnp.zeros((1, LANES), F32).at[0, :DIFF_HD].set(g_k)
    lam8 = jnp.zeros((8, LANES), F32).at[:4, :DIFF_HD].set(lam_p.astype(F32))
    rb = lambda b, qi: _row_block(b, qi, nlat_q, nctx_q, n_batch)
    tq_spec = pl.BlockSpec((TQ, LANES), lambda b, hh, qi: (jnp.minimum(qi, nlat_q), 0))
    tk_spec = pl.BlockSpec((n_lat, LANES), lambda b, hh, qi: (0, 0))
    vec = pl.BlockSpec((1, LANES), lambda b, hh, qi: (0, 0))

    def kspec(off, n_rows, ctx):
        if ctx:
            return pl.BlockSpec((n_rows, LANES), lambda b, hh, qi: (ctx0 + b, off(hh)))
        return pl.BlockSpec((n_rows, LANES), lambda b, hh, qi: (b, off(hh)))

    k0 = lambda hh: 2 * nh + 2 * hh
    k1 = lambda hh: 2 * nh + 2 * hh + 1
    vv = lambda hh: 4 * nh + hh
    kern = functools.partial(_diff_attn_kernel, n_lat=n_lat, n_ctx=n_ctx, nlat_q=nlat_q,
                             scale=DIFF_HD ** -0.5, lam_init=lam_init)
    n_all = n_lat + n_ctx
    return pl.pallas_call(
        kern,
        grid=(n_batch, nh, nq),
        in_specs=[pl.BlockSpec((TQ, LANES), lambda b, hh, qi: (rb(b, qi), 2 * hh)),
                  pl.BlockSpec((TQ, LANES), lambda b, hh, qi: (rb(b, qi), 2 * hh + 1)),
                  kspec(k0, n_lat, False), kspec(k0, n_ctx, True),
                  kspec(k1, n_lat, False), kspec(k1, n_ctx, True),
                  kspec(vv, n_lat, False), kspec(vv, n_ctx, True),
                  tq_spec, tq_spec, tq_spec, tk_spec, tk_spec, tk_spec, vec, vec,
                  pl.BlockSpec((8, LANES), lambda b, hh, qi: (0, 0)), vec],
        out_specs=pl.BlockSpec((TQ, LANES), lambda b, hh, qi: (rb(b, qi), hh)),
        out_shape=jax.ShapeDtypeStruct((rows, nh * LANES), BF16),
        scratch_shapes=[pltpu.VMEM((n_all, LANES), BF16), pltpu.VMEM((n_all, LANES), BF16),
                        pltpu.VMEM((n_all, LANES), BF16)],
        compiler_params=_cparams(("parallel", "parallel", "arbitrary")),
        name="diff_attention",
    )(qkv, qkv, qkv, qkv, qkv, qkv, qkv, qkv, *tabs, *tabs, gq, gk, lam8, g_sub.reshape(1, LANES))


def _diff_mixer(s, modv, g_mix, w_qkv, g_q, g_k, lam_p, g_sub, w_o, lam_init,
                *, n_batch, n_lat, n_ctx, need_ctx):
    d = s.shape[1]
    width = DIFF_HEADS * 2 * DIFF_HD
    rows_all = n_batch * (n_lat + n_ctx)
    w = jnp.concatenate([_pad_cols(w_qkv[:, :width], 2 * DIFF_HEADS, DIFF_HD),
                         _pad_cols(w_qkv[:, width:2 * width], 2 * DIFF_HEADS, DIFF_HD),
                         w_qkv[:, 2 * width:]], axis=1).astype(BF16)
    qkv = _nm_matmul(s, 0, d, g_mix, modv, rows_all, w, n_lat_rows=n_batch * n_lat,
                     n_per_batch=n_lat, name="diff_qkv")
    o = _diff_attention(qkv, g_q, g_k, lam_p, g_sub, lam_init,
                        n_batch=n_batch, n_lat=n_lat, n_ctx=n_ctx, need_ctx=need_ctx)
    return o, w_o.astype(BF16)


def _swa_attn_kernel(q_ref, kl_ref, kc_ref, vl_ref, vc_ref,
                     cq_ref, saq_ref, sbq_ref, ck_ref, sak_ref, sbk_ref, gq_ref, gk_ref, sink_ref,
                     o_ref, k_scr, v_scr, *, n_lat, n_ctx, nlat_q, scale):
    qi = pl.program_id(2)
    nf = SWA_HD // 4
    tq = Q_BLOCK
    band = 3 * Q_BLOCK

    @pl.when(qi == 0)
    def _():
        _prep_keys(k_scr, v_scr, n_lat, n_ctx, lambda r: kl_ref[r, :], lambda r: kc_ref[r, :],
                   lambda r: vl_ref[r, :], lambda r: vc_ref[r, :], gk_ref[...],
                   (ck_ref, sak_ref, sbk_ref), SWA_HD, nf)

    q = _head_norm(q_ref[...], gq_ref[...], SWA_HD)
    q = (_rope(q, cq_ref[...], saq_ref[...], sbq_ref[...], nf) * scale).astype(BF16)
    k0 = pl.multiple_of(jnp.clip((qi - 1) * tq, 0, n_lat - band), tq)
    sb = _nt_dot(q, k_scr[pl.ds(k0, band), :])
    kpos = k0 + lax.broadcasted_iota(jnp.int32, sb.shape, 1)
    qpos = qi * tq + lax.broadcasted_iota(jnp.int32, sb.shape, 0)
    reach = jnp.where(qi < nlat_q, WINDOW, -1)
    sb = jnp.where(jnp.abs(kpos - qpos) <= reach, sb, NEG_INF)
    sc = _nt_dot(q, k_scr[pl.ds(n_lat, n_ctx), :])
    sink = sink_ref[0][0:1, 0:1]
    m = jnp.maximum(jnp.maximum(jnp.max(sb, axis=-1, keepdims=True),
                                jnp.max(sc, axis=-1, keepdims=True)), sink)
    pb = jnp.exp(sb - m)
    pc = jnp.exp(sc - m)
    l = jnp.sum(pb, axis=-1, keepdims=True) + jnp.sum(pc, axis=-1, keepdims=True) + jnp.exp(sink - m)
    o = (jnp.dot(pb.astype(BF16), v_scr[pl.ds(k0, band), :], preferred_element_type=F32)
         + jnp.dot(pc.astype(BF16), v_scr[pl.ds(n_lat, n_ctx), :], preferred_element_type=F32))
    o_ref[...] = (o * (1.0 / l)).astype(o_ref.dtype)


def _swa_attention(qkv, g_q, g_k, sink, *, n_batch, n_lat, n_ctx, need_ctx):
    tq = Q_BLOCK
    nlat_q, nctx_q = n_lat // tq, n_ctx // tq
    nq = nlat_q + (nctx_q if need_ctx else 0)
    rows = n_batch * (n_lat + (n_ctx if need_ctx else 0))
    nf = SWA_HD // 4
    tabs = _rope_tables(n_lat, tq, ((0, True), (2 * nf, False)), nf)
    nh, nkv = SWA_HEADS, SWA_KV_HEADS
    ctx0 = n_batch * n_lat // n_ctx
    gq = jnp.zeros((1, LANES), F32).at[0, :SWA_HD].set(g_q)
    gk = jnp.zeros((1, LANES), F32).at[0, :SWA_HD].set(g_k)
    sink_t = jnp.broadcast_to(sink.astype(F32).reshape(nh, 1, 1), (nh, 8, LANES))
    rb = lambda b, qi: _row_block(b, qi, nlat_q, nctx_q, n_batch)
    tq_spec = pl.BlockSpec((tq, LANES), lambda b, hh, qi: (jnp.minimum(qi, nlat_q), 0))
    tk_spec = pl.BlockSpec((n_lat, LANES), lambda b, hh, qi: (0, 0))
    vec = pl.BlockSpec((1, LANES), lambda b, hh, qi: (0, 0))
    kern = functools.partial(_swa_attn_kernel, n_lat=n_lat, n_ctx=n_ctx, nlat_q=nlat_q,
                             scale=SWA_HD ** -0.5)
    n_all = n_lat + n_ctx
    return pl.pallas_call(
        kern,
        grid=(n_batch, nh, nq),
        in_specs=[pl.BlockSpec((tq, LANES), lambda b, hh, qi: (rb(b, qi), hh)),
                  pl.BlockSpec((n_lat, LANES), lambda b, hh, qi: (b, nh + hh // SWA_GROUP)),
                  pl.BlockSpec((n_ctx, LANES), lambda b, hh, qi: (ctx0 + b, nh + hh // SWA_GROUP)),
                  pl.BlockSpec((n_lat, LANES), lambda b, hh, qi: (b, nh + nkv + hh // SWA_GROUP)),
                  pl.BlockSpec((n_ctx, LANES), lambda b, hh, qi: (ctx0 + b, nh + nkv + hh // SWA_GROUP)),
                  tq_spec, tq_spec, tq_spec, tk_spec, tk_spec, tk_spec, vec, vec,
                  pl.BlockSpec((1, 8, LANES), lambda b, hh, qi: (hh, 0, 0))],
        out_specs=pl.BlockSpec((tq, LANES), lambda b, hh, qi: (rb(b, qi), hh)),
        out_shape=jax.ShapeDtypeStruct((rows, nh * LANES), BF16),
        scratch_shapes=[pltpu.VMEM((n_all, LANES), BF16), pltpu.VMEM((n_all, LANES), BF16)],
        compiler_params=_cparams(("parallel", "parallel", "arbitrary")),
        name="swa_attention",
    )(qkv, qkv, qkv, qkv, qkv, *tabs, *tabs, gq, gk, sink_t)


def _swa_mixer(s, modv, g_mix, w_qkv, g_q, g_k, sink, w_o, *, n_batch, n_lat, n_ctx, need_ctx):
    d = s.shape[1]
    qw, kw = SWA_HEADS * SWA_HD, SWA_KV_HEADS * SWA_HD
    rows_all = n_batch * (n_lat + n_ctx)
    w = jnp.concatenate([_pad_cols(w_qkv[:, :qw], SWA_HEADS, SWA_HD),
                         _pad_cols(w_qkv[:, qw:qw + kw], SWA_KV_HEADS, SWA_HD),
                         _pad_cols(w_qkv[:, qw + kw:], SWA_KV_HEADS, SWA_HD)], axis=1).astype(BF16)
    qkv = _nm_matmul(s, 0, d, g_mix, modv, rows_all, w, n_lat_rows=n_batch * n_lat,
                     n_per_batch=n_lat, name="swa_qkv")
    o = _swa_attention(qkv, g_q, g_k, sink, n_batch=n_batch, n_lat=n_lat, n_ctx=n_ctx, need_ctx=need_ctx)
    return o, _pad_rows(w_o, SWA_HEADS, SWA_HD).astype(BF16)


def _top_values(s, count):
    vals = []
    for _ in range(count):
        m = jnp.max(s, axis=0, keepdims=True)
        vals.append(m)
        s = jnp.where(s == m, NEG_BIG, s)
    return vals


def _peer_score_kernel(s_ref, g_ref, mod_ref, wq_ref, keys_ref,
                       ht_ref, c_ref, e1_ref, s2_ref, e2_ref, qt_scr):
    k = PEER_TOPK
    h = _norm_mod(s_ref[...], g_ref[...], mod_ref[0], 3, 4)
    ht = h.T.astype(BF16)
    ht_ref[...] = ht
    qt_scr[...] = jnp.dot(wq_ref[...], ht, preferred_element_type=F32)

    def head_body(hh, carry):
        r1 = pl.ds(pl.multiple_of(hh * 2 * LANES, LANES), LANES)
        r2 = pl.ds(pl.multiple_of(hh * 2 * LANES + LANES, LANES), LANES)
        s1 = jnp.dot(keys_ref[hh, 0], qt_scr[r1, :].astype(BF16), preferred_element_type=F32)
        s2 = jnp.dot(keys_ref[hh, 1], qt_scr[r2, :].astype(BF16), preferred_element_type=F32)
        v1 = _top_values(s1, k + 1)
        v2 = _top_values(s2, k + 1)
        top8 = jnp.concatenate(v2[:8], axis=0)
        pieces = [v1[0] + jnp.concatenate(v2[:k], axis=0)]
        pieces += [v1[i] + top8 for i in range(1, 8)]
        pieces += [jnp.concatenate(v1[8:k], axis=0) + v2[0]]
        pieces += [jnp.concatenate([v1[0] + v2[k], v1[k] + v2[0]] + [jnp.full_like(v1[0], NEG_BIG)] * 6, axis=0)]
        ms = _top_values(jnp.concatenate(pieces, axis=0), k + 1)
        thr = 0.5 * (ms[k - 1] + ms[k])
        z = jnp.ones_like(ms[0])
        for r in range(1, k):
            z = z + jnp.exp(ms[r] - ms[0])
        grp = (PEER_KEYS // SUBLANES, SUBLANES, s1.shape[1])
        c_ref[hh] = (thr - s1).reshape(grp)
        e1_ref[hh] = (jnp.exp(s1 - v1[0]) * (1.0 / z)).reshape(grp)
        s2_ref[hh] = s2
        e2_ref[hh] = jnp.exp(s2 - v2[0])
        return carry

    lax.fori_loop(0, PEER_HEADS, head_body, 0)


def _peer_scores(s, modv, g_ffn, wq_t, keys, rows, *, n_lat_rows, n_per_batch):
    d = s.shape[1]
    n_batch = modv.shape[0] - 1
    nlat_blocks = n_lat_rows // TM
    bpb = n_per_batch // TM
    nq = wq_t.shape[0]
    hk = (PEER_HEADS, PEER_KEYS, rows)
    hk_spec = pl.BlockSpec((PEER_HEADS, PEER_KEYS, TM), lambda i: (0, 0, i))
    hg = (PEER_HEADS, PEER_KEYS // SUBLANES, SUBLANES, rows)
    hg_spec = pl.BlockSpec((PEER_HEADS, PEER_KEYS // SUBLANES, SUBLANES, TM), lambda i: (0, 0, 0, i))
    return pl.pallas_call(
        _peer_score_kernel,
        grid=(rows // TM,),
        in_specs=[pl.BlockSpec((TM, d), lambda i: (i, 0)),
                  pl.BlockSpec((1, d), lambda i: (0, 0)),
                  pl.BlockSpec((1, 8, d), lambda i: (_mod_index(i, nlat_blocks, bpb, n_batch), 0, 0)),
                  pl.BlockSpec((nq, d), lambda i: (0, 0)),
                  pl.BlockSpec((PEER_HEADS, 2, PEER_KEYS, LANES), lambda i: (0, 0, 0, 0))],
        out_specs=[pl.BlockSpec((d, TM), lambda i: (0, i)), hg_spec, hg_spec, hk_spec, hk_spec],
        out_shape=[jax.ShapeDtypeStruct((d, rows), BF16)] + [jax.ShapeDtypeStruct(hg, F32)] * 2
                  + [jax.ShapeDtypeStruct(hk, F32)] * 2,
        scratch_shapes=[pltpu.VMEM((nq, TM), F32)],
        compiler_params=_cparams(("parallel",)),
        name="peer_scores",
    )(s, g_ffn.reshape(1, d), modv, wq_t, keys)


def _peer_expert_kernel(ht_ref, u_ref, vt_ref, c_ref, e1_ref, s2_ref, e2_ref, s_ref, mod_ref,
                        o_ref, acc_scr, st_scr, at_scr):
    j = pl.program_id(1)
    te, tm = st_scr.shape
    tw = LANES

    @pl.when(j == 0)
    def _():
        acc_scr[...] = jnp.zeros_like(acc_scr)

    st_scr[...] = jnp.dot(u_ref[...], ht_ref[...], preferred_element_type=F32)

    def tile_body(tc, carry):
        cols = pl.ds(pl.multiple_of(tc * tw, tw), tw)
        for al in range(te // LANES):
            rows = slice(al * LANES, (al + 1) * LANES)
            g = jnp.zeros((LANES, tw), F32)
            for hh in range(PEER_HEADS):
                c = c_ref[hh, 0, al:al + 1, cols]
                e1 = e1_ref[hh, 0, al:al + 1, cols]
                g = g + jnp.where(s2_ref[hh, :, cols] >= c, e2_ref[hh, :, cols], 0.0) * e1
            sv = st_scr[rows, cols]
            act = 0.5 * sv * (1.0 + lax.erf(sv * (2.0 ** -0.5)))
            at_scr[rows, cols] = (g * act).astype(BF16)
        return carry

    lax.fori_loop(0, tm // tw, tile_body, 0)
    acc_scr[...] += jnp.dot(vt_ref[...], at_scr[...], preferred_element_type=F32)

    @pl.when(j == pl.num_programs(1) - 1)
    def _():
        o_ref[...] = s_ref[...] + mod_ref[0][5:6] * acc_scr[...].T


def _peer_experts(ht, u, vt, c, e1, s2, e2, s, modv, rows, *, n_lat_rows, n_per_batch):
    d = s.shape[1]
    n_exp = u.shape[0]
    n_batch = modv.shape[0] - 1
    tm, te = PEER_TM, PEER_TE
    nlat_blocks = n_lat_rows // tm
    bpb = n_per_batch // tm
    hk_spec = pl.BlockSpec((PEER_HEADS, PEER_KEYS, tm), lambda i, j: (0, 0, i))
    hg_spec = pl.BlockSpec((PEER_HEADS, 1, SUBLANES, tm), lambda i, j: (0, j, 0, i))
    return pl.pallas_call(
        _peer_expert_kernel,
        grid=(rows // tm, n_exp // te),
        in_specs=[pl.BlockSpec((d, tm), lambda i, j: (0, i)),
                  pl.BlockSpec((te, d), lambda i, j: (j, 0)),
                  pl.BlockSpec((d, te), lambda i, j: (0, j)),
                  hg_spec, hg_spec, hk_spec, hk_spec,
                  pl.BlockSpec((tm, d), lambda i, j: (i, 0)),
                  pl.BlockSpec((1, 8, d), lambda i, j: (_mod_index(i, nlat_blocks, bpb, n_batch), 0, 0))],
        out_specs=pl.BlockSpec((tm, d), lambda i, j: (i, 0)),
        out_shape=jax.ShapeDtypeStruct((rows, d), F32),
        scratch_shapes=[pltpu.VMEM((d, tm), F32), pltpu.VMEM((te, tm), F32), pltpu.VMEM((te, tm), BF16)],
        compiler_params=_cparams(("parallel", "arbitrary")),
        name="peer_experts",
    )(ht, u, vt, c, e1, s2, e2, s, modv)


def _peer(s, modv, g_ffn, w_q, keys, u_tab, v_tab, rows, *, n_lat_rows, n_per_batch):
    geo = dict(n_lat_rows=n_lat_rows, n_per_batch=n_per_batch)
    ht, c, e1, s2, e2 = _peer_scores(s, modv, g_ffn, w_q.T.astype(BF16), keys.astype(BF16), rows, **geo)
    return _peer_experts(ht, u_tab.astype(BF16), v_tab.T.astype(BF16), c, e1, s2, e2, s, modv, rows, **geo)


def kernel(x, c, ctx, c_ctx, ada_w, ada_b, norm_mix, norm_ffn, mla_w_down, mla_g_cq, mla_g_ckv, mla_w_uq, mla_w_ukv, mla_g_q, mla_g_k, mla_w_o, diff_w_qkv, diff_g_q, diff_g_k, diff_lambda, diff_g_sub, diff_w_o, swa_w_qkv, swa_g_q, swa_g_k, swa_sink, swa_w_o, peer_w_q, peer_keys, peer_u, peer_v):
    n_batch, n_lat, d = x.shape
    n_ctx = ctx.shape[1]
    depth = ada_w.shape[0]
    rows_lat = n_batch * n_lat
    rows_all = rows_lat + n_batch * n_ctx
    assert n_lat % 512 == 0 and n_ctx % 256 == 0 and (n_batch * n_ctx) % PEER_TM == 0
    assert n_lat % GRID_W == 0 and rows_lat % n_ctx == 0 and n_lat >= 3 * Q_BLOCK

    s = jnp.concatenate([x.reshape(rows_lat, d), ctx.reshape(n_batch * n_ctx, d)], axis=0)
    cc = jnp.zeros((16, d), F32).at[:n_batch].set(c).at[n_batch].set(c_ctx)
    mods = _ada_mods(cc, ada_w, ada_b)
    geo = dict(n_lat_rows=rows_lat, n_per_batch=n_lat)

    for i in range(depth):
        last = i == depth - 1
        kind, j = i % 3, i // 3
        modv = jnp.zeros((n_batch + 1, 8, d), F32).at[:, :6, :].set(
            mods[i, :n_batch + 1].reshape(n_batch + 1, 6, d))
        dims = dict(n_batch=n_batch, n_lat=n_lat, n_ctx=n_ctx, need_ctx=not last)
        if kind == 0:
            p = (mla_w_down[j], mla_g_cq[j], mla_g_ckv[j], mla_w_uq[j], mla_w_ukv[j], mla_g_q[j], mla_g_k[j])
            o, wo = _mla_mixer(s, modv, norm_mix[i], p, mla_w_o[j], **dims)
        elif kind == 1:
            lam_init = 0.8 - 0.6 * math.exp(-0.3 * i)
            o, wo = _diff_mixer(s, modv, norm_mix[i], diff_w_qkv[j], diff_g_q[j], diff_g_k[j],
                                diff_lambda[j], diff_g_sub[j], diff_w_o[j], lam_init, **dims)
        else:
            o, wo = _swa_mixer(s, modv, norm_mix[i], swa_w_qkv[j], swa_g_q[j], swa_g_k[j],
                               swa_sink[j], swa_w_o[j], **dims)
        rows = rows_lat if last else rows_all
        s = _proj_resid(o, wo, s, modv, rows, gate_row=2, **geo)
        s = _peer(s, modv, norm_ffn[i], peer_w_q[i], peer_keys[i], peer_u[i], peer_v[i], rows, **geo)
    return s.reshape(n_batch, n_lat, d)
```

```python
import functools
import math

import jax
import jax.numpy as jnp
from jax import lax
from jax.experimental import pallas as pl
from jax.experimental.pallas import tpu as pltpu

F32 = jnp.float32
BF16 = jnp.bfloat16

LANES = 128
EPS = 1e-6
NEG_INF = -1e30
NEG_BIG = -3.0e38
ROPE_THETA = 10000.0
GRID_W = 64
Q_BLOCK = 128
WINDOW = 128

MLA_HEADS, MLA_NOPE, MLA_ROPE, MLA_V = 16, 64, 32, 64
MLA_QK = MLA_NOPE + MLA_ROPE
MLA_Q_LORA, MLA_KV_LORA = 768, 256
DIFF_HEADS, DIFF_HD = 8, 64
SWA_HEADS, SWA_KV_HEADS, SWA_HD = 16, 4, 64
SWA_GROUP = SWA_HEADS // SWA_KV_HEADS
PEER_HEADS, PEER_KEYS, PEER_TOPK = 8, 128, 16
PEER_EXPERTS = PEER_KEYS * PEER_KEYS

VMEM_LIMIT = 56 * 1024 * 1024

TM = 256
TQ = 256
PEER_TM = 512
SUBLANES = 8
PEER_TE = SUBLANES * LANES


def _cparams(sem):
    return pltpu.CompilerParams(dimension_semantics=sem, vmem_limit_bytes=VMEM_LIMIT)


def _ada_kernel(c_ref, w_ref, b_ref, o_ref):
    c = c_ref[...]
    sc = c * (1.0 / (1.0 + jnp.exp(-c)))
    o_ref[0] = jnp.dot(sc, w_ref[0], preferred_element_type=F32,
                       precision=lax.Precision.HIGHEST) + b_ref[0]


def _ada_mods(cc, ada_w, ada_b):
    depth, d, d6 = ada_w.shape
    rows = cc.shape[0]
    tn = 1536
    return pl.pallas_call(
        _ada_kernel,
        grid=(depth, d6 // tn),
        in_specs=[pl.BlockSpec((rows, d), lambda i, n: (0, 0)),
                  pl.BlockSpec((1, d, tn), lambda i, n: (i, 0, n)),
                  pl.BlockSpec((1, 1, tn), lambda i, n: (i, 0, n))],
        out_specs=pl.BlockSpec((1, rows, tn), lambda i, n: (i, 0, n)),
        out_shape=jax.ShapeDtypeStruct((depth, rows, d6), F32),
        compiler_params=_cparams(("arbitrary", "arbitrary")),
        name="ada_mods",
    )(cc, ada_w, ada_b.reshape(depth, 1, d6))


def _norm_mod(x, g, mod, shift_row, scale_row):
    ms = jnp.mean(x * x, axis=-1, keepdims=True)
    y = x * lax.rsqrt(ms + EPS) * g
    if mod is not None:
        y = y * (1.0 + mod[scale_row:scale_row + 1]) + mod[shift_row:shift_row + 1]
    return y


def _nm_kernel(x_ref, g_ref, mod_ref, w_ref, o_ref, h_scr, *, use_mod, shift_row, scale_row):
    @pl.when(pl.program_id(1) == 0)
    def _():
        mod = mod_ref[0] if use_mod else None
        h_scr[...] = _norm_mod(x_ref[...], g_ref[...], mod, shift_row, scale_row).astype(BF16)

    o_ref[...] = jnp.dot(h_scr[...], w_ref[...], preferred_element_type=F32).astype(o_ref.dtype)


def _mod_index(i, nlat_blocks, blocks_per_batch, n_batch):
    return jnp.where(i < nlat_blocks, i // blocks_per_batch, n_batch)


def _nm_matmul(x, kblk, k, g, modv, rows, w, *, n_lat_rows, n_per_batch, out_dtype=F32,
               use_mod=True, shift_row=0, scale_row=1, name="nm_matmul"):
    nout = w.shape[1]
    tn = nout
    for cand in (2048, 1536, 1280, 1024):
        if nout > 2048 and nout % cand == 0:
            tn = cand
            break
    n_batch = modv.shape[0] - 1
    nlat_blocks = n_lat_rows // TM
    bpb = n_per_batch // TM
    kern = functools.partial(_nm_kernel, use_mod=use_mod, shift_row=shift_row, scale_row=scale_row)
    return pl.pallas_call(
        kern,
        grid=(rows // TM, nout // tn),
        in_specs=[pl.BlockSpec((TM, k), lambda i, n: (i, kblk)),
                  pl.BlockSpec((1, k), lambda i, n: (0, 0)),
                  pl.BlockSpec((1, 8, k), lambda i, n: (_mod_index(i, nlat_blocks, bpb, n_batch), 0, 0)),
                  pl.BlockSpec((k, tn), lambda i, n: (0, n))],
        out_specs=pl.BlockSpec((TM, tn), lambda i, n: (i, n)),
        out_shape=jax.ShapeDtypeStruct((rows, nout), out_dtype),
        scratch_shapes=[pltpu.VMEM((TM, k), BF16)],
        compiler_params=_cparams(("parallel", "arbitrary")),
        name=name,
    )(x, g.reshape(1, k), modv if modv.shape[-1] == k else jnp.zeros((n_batch + 1, 8, k), F32), w)


def _proj_resid_kernel(a_ref, w_ref, s_ref, mod_ref, o_ref, *, gate_row):
    y = jnp.dot(a_ref[...].astype(BF16), w_ref[...], preferred_element_type=F32)
    o_ref[...] = s_ref[...] + mod_ref[0][gate_row:gate_row + 1] * y


def _proj_resid(a, w, s, modv, rows, *, n_lat_rows, n_per_batch, gate_row, name="proj_resid"):
    ka = a.shape[1]
    d = w.shape[1]
    n_batch = modv.shape[0] - 1
    nlat_blocks = n_lat_rows // TM
    bpb = n_per_batch // TM
    return pl.pallas_call(
        functools.partial(_proj_resid_kernel, gate_row=gate_row),
        grid=(rows // TM,),
        in_specs=[pl.BlockSpec((TM, ka), lambda i: (i, 0)),
                  pl.BlockSpec((ka, d), lambda i: (0, 0)),
                  pl.BlockSpec((TM, d), lambda i: (i, 0)),
                  pl.BlockSpec((1, 8, d), lambda i: (_mod_index(i, nlat_blocks, bpb, n_batch), 0, 0))],
        out_specs=pl.BlockSpec((TM, d), lambda i: (i, 0)),
        out_shape=jax.ShapeDtypeStruct((rows, d), F32),
        compiler_params=_cparams(("parallel",)),
        name=name,
    )(a, w, s, modv)


def _rope_tables(n_lat, extra_rows, segments, nf):
    pos = jnp.arange(n_lat, dtype=jnp.int32)
    row = (pos // GRID_W).astype(F32)
    col = (pos % GRID_W).astype(F32)
    inv = ROPE_THETA ** (-jnp.arange(nf, dtype=F32) / nf)
    cos = jnp.ones((n_lat, LANES), F32)
    sa = jnp.zeros((n_lat, LANES), F32)
    sb = jnp.zeros((n_lat, LANES), F32)
    for first, use_row in segments:
        ang = (row if use_row else col)[:, None] * inv[None, :]
        c, s = jnp.cos(ang), jnp.sin(ang)
        cos = cos.at[:, first:first + nf].set(c).at[:, first + nf:first + 2 * nf].set(c)
        sa = sa.at[:, first:first + nf].set(-s)
        sb = sb.at[:, first + nf:first + 2 * nf].set(s)
    pad = lambda t, v: jnp.concatenate([t, jnp.full((extra_rows, LANES), v, F32)], axis=0)
    return pad(cos, 1.0), pad(sa, 0.0), pad(sb, 0.0)


def _head_norm(x, g, d):
    ssq = jnp.sum(x * x, axis=-1, keepdims=True) * (1.0 / d)
    return x * lax.rsqrt(ssq + EPS) * g


def _rope(y, cos, sa, sb, nf):
    return y * cos + pltpu.roll(y, LANES - nf, 1) * sa + pltpu.roll(y, nf, 1) * sb


def _row_block(b, qi, nlat_q, nctx_q, n_batch):
    return jnp.where(qi < nlat_q, b * nlat_q + qi, n_batch * nlat_q + b * nctx_q + (qi - nlat_q))


def _prep_keys(k_scr, v_scr, n_lat, n_ctx, load_k_lat, load_k_ctx, load_v_lat, load_v_ctx,
               gk, tabs, d, nf):
    cos_ref, sa_ref, sb_ref = tabs
    ch = 256

    def lat_body(c, carry):
        r = pl.ds(pl.multiple_of(c * ch, ch), ch)
        kk = _head_norm(load_k_lat(r), gk, d)
        kk = _rope(kk, cos_ref[r, :], sa_ref[r, :], sb_ref[r, :], nf)
        k_scr[r, :] = kk.astype(BF16)
        if v_scr is not None:
            v_scr[r, :] = load_v_lat(r).astype(BF16)
        return carry

    lax.fori_loop(0, n_lat // ch, lat_body, 0)

    def ctx_body(c, carry):
        r = pl.ds(pl.multiple_of(c * ch, ch), ch)
        ro = pl.ds(pl.multiple_of(n_lat + c * ch, ch), ch)
        k_scr[ro, :] = _head_norm(load_k_ctx(r), gk, d).astype(BF16)
        if v_scr is not None:
            v_scr[ro, :] = load_v_ctx(r).astype(BF16)
        return carry

    lax.fori_loop(0, n_ctx // ch, ctx_body, 0)


def _nt_dot(a, b):
    return lax.dot_general(a, b, (((1,), (1,)), ((), ())), preferred_element_type=F32)


LOG2E = math.log2(math.e)
CHAIN_ROWS = 128


def _row_chains(n_rows):
    return [slice(r, r + CHAIN_ROWS) for r in range(0, n_rows, CHAIN_ROWS)]


def _exp2_scores(q, k):
    s = _nt_dot(q, k)
    p = jnp.exp2(s - jnp.max(s, axis=-1, keepdims=True))
    return p, jnp.sum(p, axis=-1, keepdims=True)


def _mla_attn_kernel(q_ref, kl_ref, kc_ref, krl_ref, krc_ref, vl_ref, vc_ref,
                     cq_ref, saq_ref, sbq_ref, ck_ref, sak_ref, sbk_ref, gq_ref, gk_ref,
                     o_ref, k_scr, v_scr, *, n_lat, n_ctx, nlat_q, scale):
    qi = pl.program_id(2)
    nf = MLA_ROPE // 4

    @pl.when(qi == 0)
    def _():
        _prep_keys(k_scr, v_scr, n_lat, n_ctx,
                   lambda r: kl_ref[r, :] + krl_ref[r, :],
                   lambda r: kc_ref[r, :] + krc_ref[r, :],
                   lambda r: vl_ref[r, :], lambda r: vc_ref[r, :],
                   gk_ref[...], (ck_ref, sak_ref, sbk_ref), MLA_QK, nf)

    def attend(keys):
        for rows in _row_chains(q_ref.shape[0]):
            q = _head_norm(q_ref[rows, :], gq_ref[...], MLA_QK)
            q = _rope(q, cq_ref[rows, :], saq_ref[rows, :], sbq_ref[rows, :], nf) * (scale * LOG2E)
            p, l = _exp2_scores(q.astype(BF16), k_scr[keys, :])
            o = jnp.dot(p.astype(BF16), v_scr[keys, :], preferred_element_type=F32)
            o_ref[rows, :] = (o * (1.0 / l)).astype(o_ref.dtype)

    @pl.when(qi < nlat_q)
    def _():
        attend(slice(0, n_lat + n_ctx))

    @pl.when(qi >= nlat_q)
    def _():
        attend(slice(n_lat, n_lat + n_ctx))


def _mla_attention(q, kv, lat, g_q, g_k, *, n_batch, n_lat, n_ctx, need_ctx):
    nlat_q, nctx_q = n_lat // TQ, n_ctx // TQ
    nq = nlat_q + (nctx_q if need_ctx else 0)
    rows = n_batch * (n_lat + (n_ctx if need_ctx else 0))
    nf = MLA_ROPE // 4
    tabs = _rope_tables(n_lat, TQ, ((MLA_NOPE, True), (MLA_NOPE + 2 * nf, False)), nf)
    h = MLA_HEADS
    ctx0 = n_batch * n_lat // n_ctx
    kr_blk = (MLA_Q_LORA + MLA_KV_LORA) // LANES
    gq = jnp.zeros((1, LANES), F32).at[0, :MLA_QK].set(g_q)
    gk = jnp.zeros((1, LANES), F32).at[0, :MLA_QK].set(g_k)
    qspec = pl.BlockSpec((TQ, LANES), lambda b, hh, qi: (_row_block(b, qi, nlat_q, nctx_q, n_batch), hh))
    tq_spec = pl.BlockSpec((TQ, LANES), lambda b, hh, qi: (jnp.minimum(qi, nlat_q), 0))
    tk_spec = pl.BlockSpec((n_lat, LANES), lambda b, hh, qi: (0, 0))
    vec = pl.BlockSpec((1, LANES), lambda b, hh, qi: (0, 0))
    kern = functools.partial(_mla_attn_kernel, n_lat=n_lat, n_ctx=n_ctx, nlat_q=nlat_q,
                             scale=MLA_QK ** -0.5)
    return pl.pallas_call(
        kern,
        grid=(n_batch, h, nq),
        in_specs=[qspec,
                  pl.BlockSpec((n_lat, LANES), lambda b, hh, qi: (b, hh)),
                  pl.BlockSpec((n_ctx, LANES), lambda b, hh, qi: (ctx0 + b, hh)),
                  pl.BlockSpec((n_lat, LANES), lambda b, hh, qi: (b, kr_blk)),
                  pl.BlockSpec((n_ctx, LANES), lambda b, hh, qi: (ctx0 + b, kr_blk)),
                  pl.BlockSpec((n_lat, LANES), lambda b, hh, qi: (b, h + hh)),
                  pl.BlockSpec((n_ctx, LANES), lambda b, hh, qi: (ctx0 + b, h + hh)),
                  tq_spec, tq_spec, tq_spec, tk_spec, tk_spec, tk_spec, vec, vec],
        out_specs=qspec,
        out_shape=jax.ShapeDtypeStruct((rows, h * LANES), BF16),
        scratch_shapes=[pltpu.VMEM((n_lat + n_ctx, LANES), BF16),
                        pltpu.VMEM((n_lat + n_ctx, LANES), BF16)],
        compiler_params=_cparams(("parallel", "parallel", "arbitrary")),
        name="mla_attention",
    )(q, kv, kv, lat, lat, kv, kv, *tabs, *tabs, gq, gk)


def _pad_cols(w, groups, width, first=0):
    k = w.shape[0]
    w3 = w.reshape(k, groups, width)
    out = jnp.zeros((k, groups, LANES), w.dtype).at[:, :, first:first + width].set(w3)
    return out.reshape(k, groups * LANES)


def _pad_rows(w, groups, width):
    d = w.shape[1]
    w3 = w.reshape(groups, width, d)
    return jnp.zeros((groups, LANES, d), w.dtype).at[:, :width, :].set(w3).reshape(groups * LANES, d)


def _mla_mixer(s, modv, g_mix, p, w_o, *, n_batch, n_lat, n_ctx, need_ctx):
    w_down, g_cq, g_ckv, w_uq, w_ukv, g_q, g_k = p
    d = s.shape[1]
    rows_all = n_batch * (n_lat + n_ctx)
    geo = dict(n_lat_rows=n_batch * n_lat, n_per_batch=n_lat)
    wd = jnp.concatenate([w_down[:, :MLA_Q_LORA + MLA_KV_LORA],
                          _pad_cols(w_down[:, MLA_Q_LORA + MLA_KV_LORA:], 1, MLA_ROPE, MLA_NOPE)], axis=1)
    lat = _nm_matmul(s, 0, d, g_mix, modv, rows_all, wd.astype(BF16), name="mla_down", **geo)
    wq = _pad_cols(w_uq, MLA_HEADS, MLA_QK).astype(BF16)
    q = _nm_matmul(lat, 0, MLA_Q_LORA, g_cq, modv, rows_all, wq, use_mod=False, name="mla_uq", **geo)
    wkv3 = w_ukv.reshape(MLA_KV_LORA, MLA_HEADS, MLA_NOPE + MLA_V)
    wkv = jnp.concatenate([_pad_cols(wkv3[:, :, :MLA_NOPE].reshape(MLA_KV_LORA, -1), MLA_HEADS, MLA_NOPE),
                           _pad_cols(wkv3[:, :, MLA_NOPE:].reshape(MLA_KV_LORA, -1), MLA_HEADS, MLA_V)],
                          axis=1).astype(BF16)
    kv = _nm_matmul(lat, MLA_Q_LORA // MLA_KV_LORA, MLA_KV_LORA, g_ckv, modv, rows_all, wkv,
                    use_mod=False, name="mla_ukv", **geo)
    o = _mla_attention(q, kv, lat, g_q, g_k, n_batch=n_batch, n_lat=n_lat, n_ctx=n_ctx, need_ctx=need_ctx)
    wo = _pad_rows(w_o, MLA_HEADS, MLA_V).astype(BF16)
    return o, wo


def _diff_attn_kernel(q0_ref, q1_ref, k0l_ref, k0c_ref, k1l_ref, k1c_ref, vl_ref, vc_ref,
                      cq_ref, saq_ref, sbq_ref, ck_ref, sak_ref, sbk_ref, gq_ref, gk_ref,
                      lam_ref, gsub_ref, o_ref, k0_scr, k1_scr, v_scr,
                      *, n_lat, n_ctx, nlat_q, scale, lam_init):
    qi = pl.program_id(2)
    nf = DIFF_HD // 4
    tabs = (ck_ref, sak_ref, sbk_ref)

    @pl.when(qi == 0)
    def _():
        _prep_keys(k0_scr, v_scr, n_lat, n_ctx, lambda r: k0l_ref[r, :], lambda r: k0c_ref[r, :],
                   lambda r: vl_ref[r, :], lambda r: vc_ref[r, :], gk_ref[...], tabs, DIFF_HD, nf)
        _prep_keys(k1_scr, None, n_lat, n_ctx, lambda r: k1l_ref[r, :], lambda r: k1c_ref[r, :],
                   None, None, gk_ref[...], tabs, DIFF_HD, nf)

    lp = lam_ref[...]
    lam = (jnp.exp(jnp.sum(lp[0:1] * lp[1:2], axis=-1, keepdims=True))
           - jnp.exp(jnp.sum(lp[2:3] * lp[3:4], axis=-1, keepdims=True)) + lam_init)

    def attend(keys):
        def probs(q_ref, k_scr, rows):
            q = _head_norm(q_ref[rows, :], gq_ref[...], DIFF_HD)
            q = _rope(q, cq_ref[rows, :], saq_ref[rows, :], sbq_ref[rows, :], nf) * (scale * LOG2E)
            return _exp2_scores(q.astype(BF16), k_scr[keys, :])

        for rows in _row_chains(q0_ref.shape[0]):
            p0, l0 = probs(q0_ref, k0_scr, rows)
            p1, l1 = probs(q1_ref, k1_scr, rows)
            a = p0 * (1.0 / l0) - p1 * (lam / l1)
            o = jnp.dot(a.astype(BF16), v_scr[keys, :], preferred_element_type=F32)
            o = _head_norm(o, gsub_ref[...], 2 * DIFF_HD) * (1.0 - lam_init)
            o_ref[rows, :] = o.astype(o_ref.dtype)

    @pl.when(qi < nlat_q)
    def _():
        attend(slice(0, n_lat + n_ctx))

    @pl.when(qi >= nlat_q)
    def _():
        attend(slice(n_lat, n_lat + n_ctx))


def _diff_attention(qkv, g_q, g_k, lam_p, g_sub, lam_init, *, n_batch, n_lat, n_ctx, need_ctx):
    nlat_q, nctx_q = n_lat // TQ, n_ctx // TQ
    nq = nlat_q + (nctx_q if need_ctx else 0)
    rows = n_batch * (n_lat + (n_ctx if need_ctx else 0))
    nf = DIFF_HD // 4
    tabs = _rope_tables(n_lat, TQ, ((0, True), (2 * nf, False)), nf)
    nh = DIFF_HEADS
    ctx0 = n_batch * n_lat // n_ctx
    gq = jnp.zeros((1, LANES), F32).at[0, :DIFF_HD].set(g_q)
    gk = jnp.zeros((1, LANES), F32).at[0, :DIFF_HD].set(g_k)
    lam8 = jnp.zeros((8, LANES), F32).at[:4, :DIFF_HD].set(lam_p.astype(F32))
    rb = lambda b, qi: _row_block(b, qi, nlat_q, nctx_q, n_batch)
    tq_spec = pl.BlockSpec((TQ, LANES), lambda b, hh, qi: (jnp.minimum(qi, nlat_q), 0))
    tk_spec = pl.BlockSpec((n_lat, LANES), lambda b, hh, qi: (0, 0))
    vec = pl.BlockSpec((1, LANES), lambda b, hh, qi: (0, 0))

    def kspec(off, n_rows, ctx):
        if ctx:
            return pl.BlockSpec((n_rows, LANES), lambda b, hh, qi: (ctx0 + b, off(hh)))
        return pl.BlockSpec((n_rows, LANES), lambda b, hh, qi: (b, off(hh)))

    k0 = lambda hh: 2 * nh + 2 * hh
    k1 = lambda hh: 2 * nh + 2 * hh + 1
    vv = lambda hh: 4 * nh + hh
    kern = functools.partial(_diff_attn_kernel, n_lat=n_lat, n_ctx=n_ctx, nlat_q=nlat_q,
                             scale=DIFF_HD ** -0.5, lam_init=lam_init)
    n_all = n_lat + n_ctx
    return pl.pallas_call(
        kern,
        grid=(n_batch, nh, nq),
        in_specs=[pl.BlockSpec((TQ, LANES), lambda b, hh, qi: (rb(b, qi), 2 * hh)),
                  pl.BlockSpec((TQ, LANES), lambda b, hh, qi: (rb(b, qi), 2 * hh + 1)),
                  kspec(k0, n_lat, False), kspec(k0, n_ctx, True),
                  kspec(k1, n_lat, False), kspec(k1, n_ctx, True),
                  kspec(vv, n_lat, False), kspec(vv, n_ctx, True),
                  tq_spec, tq_spec, tq_spec, tk_spec, tk_spec, tk_spec, vec, vec,
                  pl.BlockSpec((8, LANES), lambda b, hh, qi: (0, 0)), vec],
        out_specs=pl.BlockSpec((TQ, LANES), lambda b, hh, qi: (rb(b, qi), hh)),
        out_shape=jax.ShapeDtypeStruct((rows, nh * LANES), BF16),
        scratch_shapes=[pltpu.VMEM((n_all, LANES), BF16), pltpu.VMEM((n_all, LANES), BF16),
                        pltpu.VMEM((n_all, LANES), BF16)],
        compiler_params=_cparams(("parallel", "parallel", "arbitrary")),
        name="diff_attention",
    )(qkv, qkv, qkv, qkv, qkv, qkv, qkv, qkv, *tabs, *tabs, gq, gk, lam8, g_sub.reshape(1, LANES))


def _diff_mixer(s, modv, g_mix, w_qkv, g_q, g_k, lam_p, g_sub, w_o, lam_init,
                *, n_batch, n_lat, n_ctx, need_ctx):
    d = s.shape[1]
    width = DIFF_HEADS * 2 * DIFF_HD
    rows_all = n_batch * (n_lat + n_ctx)
    w = jnp.concatenate([_pad_cols(w_qkv[:, :width], 2 * DIFF_HEADS, DIFF_HD),
                         _pad_cols(w_qkv[:, width:2 * width], 2 * DIFF_HEADS, DIFF_HD),
                         w_qkv[:, 2 * width:]], axis=1).astype(BF16)
    qkv = _nm_matmul(s, 0, d, g_mix, modv, rows_all, w, n_lat_rows=n_batch * n_lat,
                     n_per_batch=n_lat, name="diff_qkv")
    o = _diff_attention(qkv, g_q, g_k, lam_p, g_sub, lam_init,
                        n_batch=n_batch, n_lat=n_lat, n_ctx=n_ctx, need_ctx=need_ctx)
    return o, w_o.astype(BF16)


def _swa_attn_kernel(q_ref, kl_ref, kc_ref, vl_ref, vc_ref,
                     cq_ref, saq_ref, sbq_ref, ck_ref, sak_ref, sbk_ref, gq_ref, gk_ref, sink_ref,
                     o_ref, k_scr, v_scr, *, n_lat, n_ctx, nlat_q, scale):
    qi = pl.program_id(2)
    nf = SWA_HD // 4
    band = 3 * Q_BLOCK

    @pl.when(qi == 0)
    def _():
        _prep_keys(k_scr, v_scr, n_lat, n_ctx, lambda r: kl_ref[r, :], lambda r: kc_ref[r, :],
                   lambda r: vl_ref[r, :], lambda r: vc_ref[r, :], gk_ref[...],
                   (ck_ref, sak_ref, sbk_ref), SWA_HD, nf)

    chains = _row_chains(q_ref.shape[0])
    for ci, rows in enumerate(chains):
        qb = qi * len(chains) + ci
        k0 = pl.multiple_of(jnp.clip((qb - 1) * Q_BLOCK, 0, n_lat - band), Q_BLOCK)
        kpos = k0 + lax.broadcasted_iota(jnp.int32, (Q_BLOCK, band), 1)
        qpos = qb * Q_BLOCK + lax.broadcasted_iota(jnp.int32, (Q_BLOCK, band), 0)
        reach = jnp.where(qi < nlat_q, WINDOW, -1)
        in_band = jnp.abs(kpos - qpos) <= reach
        kb, vb = k_scr[pl.ds(k0, band), :], v_scr[pl.ds(k0, band), :]
        kc, vc = k_scr[n_lat:n_lat + n_ctx, :], v_scr[n_lat:n_lat + n_ctx, :]
        for gi in range(SWA_GROUP):
            lanes = slice(gi * LANES, (gi + 1) * LANES)
            q = _head_norm(q_ref[rows, lanes], gq_ref[...], SWA_HD)
            q = _rope(q, cq_ref[rows, :], saq_ref[rows, :], sbq_ref[rows, :], nf) * (scale * LOG2E)
            q = q.astype(BF16)
            sb = jnp.where(in_band, _nt_dot(q, kb), NEG_INF)
            sc = _nt_dot(q, kc)
            sink = sink_ref[gi][0:1, 0:1] * LOG2E
            m = jnp.maximum(jnp.maximum(jnp.max(sb, axis=-1, keepdims=True),
                                        jnp.max(sc, axis=-1, keepdims=True)), sink)
            pb = jnp.exp2(sb - m)
            pc = jnp.exp2(sc - m)
            l = (jnp.sum(pb, axis=-1, keepdims=True) + jnp.sum(pc, axis=-1, keepdims=True)
                 + jnp.exp2(sink - m))
            o = (jnp.dot(pb.astype(BF16), vb, preferred_element_type=F32)
                 + jnp.dot(pc.astype(BF16), vc, preferred_element_type=F32))
            o_ref[rows, lanes] = (o * (1.0 / l)).astype(o_ref.dtype)


def _swa_attention(qkv, g_q, g_k, sink, *, n_batch, n_lat, n_ctx, need_ctx):
    nlat_q, nctx_q = n_lat // TQ, n_ctx // TQ
    nq = nlat_q + (nctx_q if need_ctx else 0)
    rows = n_batch * (n_lat + (n_ctx if need_ctx else 0))
    nf = SWA_HD // 4
    tabs = _rope_tables(n_lat, TQ, ((0, True), (2 * nf, False)), nf)
    nh, nkv, grp = SWA_HEADS, SWA_KV_HEADS, SWA_GROUP
    ctx0 = n_batch * n_lat // n_ctx
    gq = jnp.zeros((1, LANES), F32).at[0, :SWA_HD].set(g_q)
    gk = jnp.zeros((1, LANES), F32).at[0, :SWA_HD].set(g_k)
    sink_t = jnp.broadcast_to(sink.astype(F32).reshape(nh, 1, 1), (nh, 8, LANES))
    rb = lambda b, qi: _row_block(b, qi, nlat_q, nctx_q, n_batch)
    tq_spec = pl.BlockSpec((TQ, LANES), lambda b, kh, qi: (jnp.minimum(qi, nlat_q), 0))
    tk_spec = pl.BlockSpec((n_lat, LANES), lambda b, kh, qi: (0, 0))
    vec = pl.BlockSpec((1, LANES), lambda b, kh, qi: (0, 0))
    kern = functools.partial(_swa_attn_kernel, n_lat=n_lat, n_ctx=n_ctx, nlat_q=nlat_q,
                             scale=SWA_HD ** -0.5)
    n_all = n_lat + n_ctx
    return pl.pallas_call(
        kern,
        grid=(n_batch, nkv, nq),
        in_specs=[pl.BlockSpec((TQ, grp * LANES), lambda b, kh, qi: (rb(b, qi), kh)),
                  pl.BlockSpec((n_lat, LANES), lambda b, kh, qi: (b, nh + kh)),
                  pl.BlockSpec((n_ctx, LANES), lambda b, kh, qi: (ctx0 + b, nh + kh)),
                  pl.BlockSpec((n_lat, LANES), lambda b, kh, qi: (b, nh + nkv + kh)),
                  pl.BlockSpec((n_ctx, LANES), lambda b, kh, qi: (ctx0 + b, nh + nkv + kh)),
                  tq_spec, tq_spec, tq_spec, tk_spec, tk_spec, tk_spec, vec, vec,
                  pl.BlockSpec((grp, 8, LANES), lambda b, kh, qi: (kh, 0, 0))],
        out_specs=pl.BlockSpec((TQ, grp * LANES), lambda b, kh, qi: (rb(b, qi), kh)),
        out_shape=jax.ShapeDtypeStruct((rows, nh * LANES), BF16),
        scratch_shapes=[pltpu.VMEM((n_all, LANES), BF16), pltpu.VMEM((n_all, LANES), BF16)],
        compiler_params=_cparams(("parallel", "parallel", "arbitrary")),
        name="swa_attention",
    )(qkv, qkv, qkv, qkv, qkv, *tabs, *tabs, gq, gk, sink_t)


def _swa_mixer(s, modv, g_mix, w_qkv, g_q, g_k, sink, w_o, *, n_batch, n_lat, n_ctx, need_ctx):
    d = s.shape[1]
    qw, kw = SWA_HEADS * SWA_HD, SWA_KV_HEADS * SWA_HD
    rows_all = n_batch * (n_lat + n_ctx)
    w = jnp.concatenate([_pad_cols(w_qkv[:, :qw], SWA_HEADS, SWA_HD),
                         _pad_cols(w_qkv[:, qw:qw + kw], SWA_KV_HEADS, SWA_HD),
                         _pad_cols(w_qkv[:, qw + kw:], SWA_KV_HEADS, SWA_HD)], axis=1).astype(BF16)
    qkv = _nm_matmul(s, 0, d, g_mix, modv, rows_all, w, n_lat_rows=n_batch * n_lat,
                     n_per_batch=n_lat, name="swa_qkv")
    o = _swa_attention(qkv, g_q, g_k, sink, n_batch=n_batch, n_lat=n_lat, n_ctx=n_ctx, need_ctx=need_ctx)
    return o, _pad_rows(w_o, SWA_HEADS, SWA_HD).astype(BF16)


def _top_values(s, count):
    vals = []
    for _ in range(count):
        m = jnp.max(s, axis=0, keepdims=True)
        vals.append(m)
        s = jnp.where(s == m, NEG_BIG, s)
    return vals


def _peer_score_kernel(s_ref, g_ref, mod_ref, wq_ref, keys_ref,
                       ht_ref, c_ref, e1_ref, s2_ref, e2_ref, qt_scr):
    k = PEER_TOPK
    h = _norm_mod(s_ref[...], g_ref[...], mod_ref[0], 3, 4)
    ht = h.T.astype(BF16)
    ht_ref[...] = ht
    qt_scr[...] = jnp.dot(wq_ref[...], ht, preferred_element_type=F32)

    def head_body(hh, carry):
        r1 = pl.ds(pl.multiple_of(hh * 2 * LANES, LANES), LANES)
        r2 = pl.ds(pl.multiple_of(hh * 2 * LANES + LANES, LANES), LANES)
        s1 = jnp.dot(keys_ref[hh, 0], qt_scr[r1, :].astype(BF16), preferred_element_type=F32)
        s2 = jnp.dot(keys_ref[hh, 1], qt_scr[r2, :].astype(BF16), preferred_element_type=F32)
        v1 = _top_values(s1, k + 1)
        v2 = _top_values(s2, k + 1)
        top8 = jnp.concatenate(v2[:8], axis=0)
        pieces = [v1[0] + jnp.concatenate(v2[:k], axis=0)]
        pieces += [v1[i] + top8 for i in range(1, 8)]
        pieces += [jnp.concatenate(v1[8:k], axis=0) + v2[0]]
        pieces += [jnp.concatenate([v1[0] + v2[k], v1[k] + v2[0]] + [jnp.full_like(v1[0], NEG_BIG)] * 6, axis=0)]
        ms = _top_values(jnp.concatenate(pieces, axis=0), k + 1)
        thr = 0.5 * (ms[k - 1] + ms[k])
        z = jnp.ones_like(ms[0])
        for r in range(1, k):
            z = z + jnp.exp(ms[r] - ms[0])
        grp = (PEER_KEYS // SUBLANES, SUBLANES, s1.shape[1])
        c_ref[hh] = (thr - s1).reshape(grp)
        e1_ref[hh] = (jnp.exp(s1 - v1[0]) * (1.0 / z)).reshape(grp)
        s2_ref[hh] = s2
        e2_ref[hh] = jnp.exp(s2 - v2[0])
        return carry

    lax.fori_loop(0, PEER_HEADS, head_body, 0)


def _peer_scores(s, modv, g_ffn, wq_t, keys, rows, *, n_lat_rows, n_per_batch):
    d = s.shape[1]
    n_batch = modv.shape[0] - 1
    nlat_blocks = n_lat_rows // TM
    bpb = n_per_batch // TM
    nq = wq_t.shape[0]
    hk = (PEER_HEADS, PEER_KEYS, rows)
    hk_spec = pl.BlockSpec((PEER_HEADS, PEER_KEYS, TM), lambda i: (0, 0, i))
    hg = (PEER_HEADS, PEER_KEYS // SUBLANES, SUBLANES, rows)
    hg_spec = pl.BlockSpec((PEER_HEADS, PEER_KEYS // SUBLANES, SUBLANES, TM), lambda i: (0, 0, 0, i))
    return pl.pallas_call(
        _peer_score_kernel,
        grid=(rows // TM,),
        in_specs=[pl.BlockSpec((TM, d), lambda i: (i, 0)),
                  pl.BlockSpec((1, d), lambda i: (0, 0)),
                  pl.BlockSpec((1, 8, d), lambda i: (_mod_index(i, nlat_blocks, bpb, n_batch), 0, 0)),
                  pl.BlockSpec((nq, d), lambda i: (0, 0)),
                  pl.BlockSpec((PEER_HEADS, 2, PEER_KEYS, LANES), lambda i: (0, 0, 0, 0))],
        out_specs=[pl.BlockSpec((d, TM), lambda i: (0, i)), hg_spec, hg_spec, hk_spec, hk_spec],
        out_shape=[jax.ShapeDtypeStruct((d, rows), BF16)] + [jax.ShapeDtypeStruct(hg, F32)] * 2
                  + [jax.ShapeDtypeStruct(hk, F32)] * 2,
        scratch_shapes=[pltpu.VMEM((nq, TM), F32)],
        compiler_params=_cparams(("parallel",)),
        name="peer_scores",
    )(s, g_ffn.reshape(1, d), modv, wq_t, keys)


def _peer_expert_kernel(ht_ref, u_ref, vt_ref, c_ref, e1_ref, s2_ref, e2_ref, s_ref, mod_ref,
                        o_ref, acc_scr, st0_scr, st1_scr, at0_scr, at1_scr, *, n_exp_blocks):
    g = pl.program_id(0)
    st_scr, at_scr = (st0_scr, st1_scr), (at0_scr, at1_scr)
    te, tm = st0_scr.shape
    jc = jnp.maximum(g - 2, 0) % n_exp_blocks

    @pl.when(g == 0)
    def _():
        for ref in st_scr + at_scr:
            ref[...] = jnp.zeros_like(ref)

    @pl.when(jc == 0)
    def _():
        acc_scr[...] = jnp.zeros_like(acc_scr)

    def stages(slot_a, slot_b):
        mxu_chunk = 2 * LANES

        def value_chunk(q):
            ks = slice(q * mxu_chunk, (q + 1) * mxu_chunk)
            acc_scr[...] += jnp.dot(vt_ref[:, ks], at_scr[slot_a][ks, :], preferred_element_type=F32)

        def score_chunk(q):
            ms = slice(q * mxu_chunk, (q + 1) * mxu_chunk)
            st_scr[slot_a][ms, :] = jnp.dot(u_ref[ms, :], ht_ref[...], preferred_element_type=F32)

        rg_rows = 2 * SUBLANES
        al_group = 4
        col_tiles = 2
        tiles = [(rg, c0, al0)
                 for rg in range(LANES // rg_rows)
                 for c0 in range(0, tm // LANES, col_tiles)
                 for al0 in range(0, te // LANES, al_group)]
        n_chunks = te // mxu_chunk
        per_chunk = len(tiles) // (2 * n_chunks)
        for t, (rg, c0, al0) in enumerate(tiles):
            if t % per_chunk == 0:
                q = t // per_chunk
                (value_chunk if q % 2 == 0 else score_chunk)(q // 2)
            brows = slice(rg * rg_rows, (rg + 1) * rg_rows)
            lanes = [slice((c0 + ct) * LANES, (c0 + ct + 1) * LANES) for ct in range(col_tiles)]
            gate = [[jnp.zeros((rg_rows, LANES), F32) for _ in lanes] for _ in range(al_group)]
            for hh in range(PEER_HEADS):
                for ct, ln in enumerate(lanes):
                    s2t = s2_ref[hh, brows, ln]
                    e2t = e2_ref[hh, brows, ln]
                    for k in range(al_group):
                        al = al0 + k
                        c = c_ref[hh, 0, al:al + 1, ln]
                        e1 = e1_ref[hh, 0, al:al + 1, ln]
                        gate[k][ct] = gate[k][ct] + jnp.where(s2t >= c, e2t, 0.0) * e1
            for k in range(al_group):
                erows = slice((al0 + k) * LANES + rg * rg_rows, (al0 + k) * LANES + (rg + 1) * rg_rows)
                for ct, ln in enumerate(lanes):
                    sv = st_scr[slot_b][erows, ln]
                    act = 0.5 * sv * (1.0 + lax.erf(sv * (2.0 ** -0.5)))
                    at_scr[slot_b][erows, ln] = (gate[k][ct] * act).astype(BF16)

    for parity in range(2):
        @pl.when(g % 2 == parity)
        def _():
            stages(parity, 1 - parity)

    @pl.when((jc == n_exp_blocks - 1) & (g >= 2))
    def _():
        o_ref[...] = s_ref[...] + mod_ref[0][5:6] * acc_scr[...].T


def _peer_experts(ht, u, vt, c, e1, s2, e2, s, modv, rows, *, n_lat_rows, n_per_batch):
    d = s.shape[1]
    n_exp = u.shape[0]
    n_batch = modv.shape[0] - 1
    tm, te = PEER_TM, PEER_TE
    nlat_blocks = n_lat_rows // tm
    bpb = n_per_batch // tm
    nj = n_exp // te
    n_pairs = (rows // tm) * nj

    def pair(g, lag):
        p = jnp.clip(g - lag, 0, n_pairs - 1)
        return p // nj, p % nj

    hk_spec = pl.BlockSpec((PEER_HEADS, PEER_KEYS, tm), lambda g: (0, 0, pair(g, 1)[0]))
    hg_spec = pl.BlockSpec((PEER_HEADS, 1, SUBLANES, tm), lambda g: (0, pair(g, 1)[1], 0, pair(g, 1)[0]))
    kern = functools.partial(_peer_expert_kernel, n_exp_blocks=nj)
    return pl.pallas_call(
        kern,
        grid=(n_pairs + 2,),
        in_specs=[pl.BlockSpec((d, tm), lambda g: (0, pair(g, 0)[0])),
                  pl.BlockSpec((te, d), lambda g: (pair(g, 0)[1], 0)),
                  pl.BlockSpec((d, te), lambda g: (0, pair(g, 2)[1])),
                  hg_spec, hg_spec, hk_spec, hk_spec,
                  pl.BlockSpec((tm, d), lambda g: (pair(g, 2)[0], 0)),
                  pl.BlockSpec((1, 8, d),
                               lambda g: (_mod_index(pair(g, 2)[0], nlat_blocks, bpb, n_batch), 0, 0))],
        out_specs=pl.BlockSpec((tm, d), lambda g: (pair(g, 2)[0], 0)),
        out_shape=jax.ShapeDtypeStruct((rows, d), F32),
        scratch_shapes=[pltpu.VMEM((d, tm), F32), pltpu.VMEM((te, tm), F32), pltpu.VMEM((te, tm), F32),
                        pltpu.VMEM((te, tm), BF16), pltpu.VMEM((te, tm), BF16)],
        compiler_params=_cparams(("arbitrary",)),
        name="peer_experts",
    )(ht, u, vt, c, e1, s2, e2, s, modv)


def _peer(s, modv, g_ffn, w_q, keys, u_tab, v_tab, rows, *, n_lat_rows, n_per_batch):
    geo = dict(n_lat_rows=n_lat_rows, n_per_batch=n_per_batch)
    ht, c, e1, s2, e2 = _peer_scores(s, modv, g_ffn, w_q.T.astype(BF16), keys.astype(BF16), rows, **geo)
    return _peer_experts(ht, u_tab.astype(BF16), v_tab.T.astype(BF16), c, e1, s2, e2, s, modv, rows, **geo)


def kernel(x, c, ctx, c_ctx, ada_w, ada_b, norm_mix, norm_ffn, mla_w_down, mla_g_cq, mla_g_ckv, mla_w_uq, mla_w_ukv, mla_g_q, mla_g_k, mla_w_o, diff_w_qkv, diff_g_q, diff_g_k, diff_lambda, diff_g_sub, diff_w_o, swa_w_qkv, swa_g_q, swa_g_k, swa_sink, swa_w_o, peer_w_q, peer_keys, peer_u, peer_v):
    n_batch, n_lat, d = x.shape
    n_ctx = ctx.shape[1]
    depth = ada_w.shape[0]
    rows_lat = n_batch * n_lat
    rows_all = rows_lat + n_batch * n_ctx
    assert n_lat % 512 == 0 and n_ctx % 256 == 0 and (n_batch * n_ctx) % PEER_TM == 0
    assert n_lat % GRID_W == 0 and rows_lat % n_ctx == 0 and n_lat >= 3 * Q_BLOCK

    s = jnp.concatenate([x.reshape(rows_lat, d), ctx.reshape(n_batch * n_ctx, d)], axis=0)
    cc = jnp.zeros((16, d), F32).at[:n_batch].set(c).at[n_batch].set(c_ctx)
    mods = _ada_mods(cc, ada_w, ada_b)
    geo = dict(n_lat_rows=rows_lat, n_per_batch=n_lat)

    for i in range(depth):
        last = i == depth - 1
        kind, j = i % 3, i // 3
        modv = jnp.zeros((n_batch + 1, 8, d), F32).at[:, :6, :].set(
            mods[i, :n_batch + 1].reshape(n_batch + 1, 6, d))
        dims = dict(n_batch=n_batch, n_lat=n_lat, n_ctx=n_ctx, need_ctx=not last)
        if kind == 0:
            p = (mla_w_down[j], mla_g_cq[j], mla_g_ckv[j], mla_w_uq[j], mla_w_ukv[j], mla_g_q[j], mla_g_k[j])
            o, wo = _mla_mixer(s, modv, norm_mix[i], p, mla_w_o[j], **dims)
        elif kind == 1:
            lam_init = 0.8 - 0.6 * math.exp(-0.3 * i)
            o, wo = _diff_mixer(s, modv, norm_mix[i], diff_w_qkv[j], diff_g_q[j], diff_g_k[j],
                                diff_lambda[j], diff_g_sub[j], diff_w_o[j], lam_init, **dims)
        else:
            o, wo = _swa_mixer(s, modv, norm_mix[i], swa_w_qkv[j], swa_g_q[j], swa_g_k[j],
                               swa_sink[j], swa_w_o[j], **dims)
        rows = rows_lat if last else rows_all
        s = _proj_resid(o, wo, s, modv, rows, gate_row=2, **geo)
        s = _peer(s, modv, norm_ffn[i], peer_w_q[i], peer_keys[i], peer_u[i], peer_v[i], rows, **geo)
    return s.reshape(n_batch, n_lat, d)
```

```python
import functools
import math

import jax
import jax.numpy as jnp
from jax import lax
from jax.experimental import pallas as pl
from jax.experimental.pallas import tpu as pltpu

F32 = jnp.float32
BF16 = jnp.bfloat16

LANES = 128
EPS = 1e-6
NEG_INF = -1e30
NEG_BIG = -3.0e38
ROPE_THETA = 10000.0
GRID_W = 64
Q_BLOCK = 128
WINDOW = 128

MLA_HEADS, MLA_NOPE, MLA_ROPE, MLA_V = 16, 64, 32, 64
MLA_QK = MLA_NOPE + MLA_ROPE
MLA_Q_LORA, MLA_KV_LORA = 768, 256
DIFF_HEADS, DIFF_HD = 8, 64
SWA_HEADS, SWA_KV_HEADS, SWA_HD = 16, 4, 64
SWA_GROUP = SWA_HEADS // SWA_KV_HEADS
PEER_HEADS, PEER_KEYS, PEER_TOPK = 8, 128, 16
PEER_EXPERTS = PEER_KEYS * PEER_KEYS

VMEM_LIMIT = 56 * 1024 * 1024

TM = 256
TQ = 256
PEER_TM = 512
SUBLANES = 8
PEER_TE = SUBLANES * LANES


def _cparams(sem):
    return pltpu.CompilerParams(dimension_semantics=sem, vmem_limit_bytes=VMEM_LIMIT)


def _ada_kernel(c_ref, w_ref, b_ref, o_ref):
    c = c_ref[...]
    sc = c * (1.0 / (1.0 + jnp.exp(-c)))
    o_ref[0] = jnp.dot(sc, w_ref[0], preferred_element_type=F32,
                       precision=lax.Precision.HIGHEST) + b_ref[0]


def _ada_mods(cc, ada_w, ada_b):
    depth, d, d6 = ada_w.shape
    rows = cc.shape[0]
    tn = 1536
    return pl.pallas_call(
        _ada_kernel,
        grid=(depth, d6 // tn),
        in_specs=[pl.BlockSpec((rows, d), lambda i, n: (0, 0)),
                  pl.BlockSpec((1, d, tn), lambda i, n: (i, 0, n)),
                  pl.BlockSpec((1, 1, tn), lambda i, n: (i, 0, n))],
        out_specs=pl.BlockSpec((1, rows, tn), lambda i, n: (i, 0, n)),
        out_shape=jax.ShapeDtypeStruct((depth, rows, d6), F32),
        compiler_params=_cparams(("arbitrary", "arbitrary")),
        name="ada_mods",
    )(cc, ada_w, ada_b.reshape(depth, 1, d6))


def _norm_mod(x, g, mod, shift_row, scale_row):
    ms = jnp.mean(x * x, axis=-1, keepdims=True)
    y = x * lax.rsqrt(ms + EPS) * g
    if mod is not None:
        y = y * (1.0 + mod[scale_row:scale_row + 1]) + mod[shift_row:shift_row + 1]
    return y


def _nm_kernel(x_ref, g_ref, mod_ref, w_ref, o_ref, h_scr, *, use_mod, shift_row, scale_row):
    @pl.when(pl.program_id(1) == 0)
    def _():
        mod = mod_ref[0] if use_mod else None
        h_scr[...] = _norm_mod(x_ref[...], g_ref[...], mod, shift_row, scale_row).astype(BF16)

    o_ref[...] = jnp.dot(h_scr[...], w_ref[...], preferred_element_type=F32).astype(o_ref.dtype)


def _mod_index(i, nlat_blocks, blocks_per_batch, n_batch):
    return jnp.where(i < nlat_blocks, i // blocks_per_batch, n_batch)


def _nm_matmul(x, kblk, k, g, modv, rows, w, *, n_lat_rows, n_per_batch, out_dtype=F32,
               use_mod=True, shift_row=0, scale_row=1, name="nm_matmul"):
    nout = w.shape[1]
    tn = nout
    for cand in (2048, 1536, 1280, 1024):
        if nout > 2048 and nout % cand == 0:
            tn = cand
            break
    n_batch = modv.shape[0] - 1
    nlat_blocks = n_lat_rows // TM
    bpb = n_per_batch // TM
    kern = functools.partial(_nm_kernel, use_mod=use_mod, shift_row=shift_row, scale_row=scale_row)
    return pl.pallas_call(
        kern,
        grid=(rows // TM, nout // tn),
        in_specs=[pl.BlockSpec((TM, k), lambda i, n: (i, kblk)),
                  pl.BlockSpec((1, k), lambda i, n: (0, 0)),
                  pl.BlockSpec((1, 8, k), lambda i, n: (_mod_index(i, nlat_blocks, bpb, n_batch), 0, 0)),
                  pl.BlockSpec((k, tn), lambda i, n: (0, n))],
        out_specs=pl.BlockSpec((TM, tn), lambda i, n: (i, n)),
        out_shape=jax.ShapeDtypeStruct((rows, nout), out_dtype),
        scratch_shapes=[pltpu.VMEM((TM, k), BF16)],
        compiler_params=_cparams(("parallel", "arbitrary")),
        name=name,
    )(x, g.reshape(1, k), modv if modv.shape[-1] == k else jnp.zeros((n_batch + 1, 8, k), F32), w)


def _proj_resid_kernel(a_ref, w_ref, s_ref, mod_ref, o_ref, *, gate_row):
    y = jnp.dot(a_ref[...].astype(BF16), w_ref[...], preferred_element_type=F32)
    o_ref[...] = s_ref[...] + mod_ref[0][gate_row:gate_row + 1] * y


def _proj_resid(a, w, s, modv, rows, *, n_lat_rows, n_per_batch, gate_row, name="proj_resid"):
    ka = a.shape[1]
    d = w.shape[1]
    n_batch = modv.shape[0] - 1
    nlat_blocks = n_lat_rows // TM
    bpb = n_per_batch // TM
    return pl.pallas_call(
        functools.partial(_proj_resid_kernel, gate_row=gate_row),
        grid=(rows // TM,),
        in_specs=[pl.BlockSpec((TM, ka), lambda i: (i, 0)),
                  pl.BlockSpec((ka, d), lambda i: (0, 0)),
                  pl.BlockSpec((TM, d), lambda i: (i, 0)),
                  pl.BlockSpec((1, 8, d), lambda i: (_mod_index(i, nlat_blocks, bpb, n_batch), 0, 0))],
        out_specs=pl.BlockSpec((TM, d), lambda i: (i, 0)),
        out_shape=jax.ShapeDtypeStruct((rows, d), F32),
        compiler_params=_cparams(("parallel",)),
        name=name,
    )(a, w, s, modv)


def _rope_tables(n_lat, extra_rows, segments, nf):
    pos = jnp.arange(n_lat, dtype=jnp.int32)
    row = (pos // GRID_W).astype(F32)
    col = (pos % GRID_W).astype(F32)
    inv = ROPE_THETA ** (-jnp.arange(nf, dtype=F32) / nf)
    cos = jnp.ones((n_lat, LANES), F32)
    sa = jnp.zeros((n_lat, LANES), F32)
    sb = jnp.zeros((n_lat, LANES), F32)
    for first, use_row in segments:
        ang = (row if use_row else col)[:, None] * inv[None, :]
        c, s = jnp.cos(ang), jnp.sin(ang)
        cos = cos.at[:, first:first + nf].set(c).at[:, first + nf:first + 2 * nf].set(c)
        sa = sa.at[:, first:first + nf].set(-s)
        sb = sb.at[:, first + nf:first + 2 * nf].set(s)
    pad = lambda t, v: jnp.concatenate([t, jnp.full((extra_rows, LANES), v, F32)], axis=0)
    return pad(cos, 1.0), pad(sa, 0.0), pad(sb, 0.0)


def _head_norm(x, g, d):
    ssq = jnp.sum(x * x, axis=-1, keepdims=True) * (1.0 / d)
    return x * lax.rsqrt(ssq + EPS) * g


def _rope(y, cos, sa, sb, nf):
    return y * cos + pltpu.roll(y, LANES - nf, 1) * sa + pltpu.roll(y, nf, 1) * sb


def _row_block(b, qi, nlat_q, nctx_q, n_batch):
    return jnp.where(qi < nlat_q, b * nlat_q + qi, n_batch * nlat_q + b * nctx_q + (qi - nlat_q))


def _prep_keys(k_scr, v_scr, n_lat, n_ctx, load_k_lat, load_k_ctx, load_v_lat, load_v_ctx,
               gk, tabs, d, nf):
    cos_ref, sa_ref, sb_ref = tabs
    ch = 256

    def lat_body(c, carry):
        r = pl.ds(pl.multiple_of(c * ch, ch), ch)
        kk = _head_norm(load_k_lat(r), gk, d)
        kk = _rope(kk, cos_ref[r, :], sa_ref[r, :], sb_ref[r, :], nf)
        k_scr[r, :] = kk.astype(BF16)
        if v_scr is not None:
            v_scr[r, :] = load_v_lat(r).astype(BF16)
        return carry

    lax.fori_loop(0, n_lat // ch, lat_body, 0)

    def ctx_body(c, carry):
        r = pl.ds(pl.multiple_of(c * ch, ch), ch)
        ro = pl.ds(pl.multiple_of(n_lat + c * ch, ch), ch)
        k_scr[ro, :] = _head_norm(load_k_ctx(r), gk, d).astype(BF16)
        if v_scr is not None:
            v_scr[ro, :] = load_v_ctx(r).astype(BF16)
        return carry

    lax.fori_loop(0, n_ctx // ch, ctx_body, 0)


def _nt_dot(a, b):
    return lax.dot_general(a, b, (((1,), (1,)), ((), ())), preferred_element_type=F32)


LOG2E = math.log2(math.e)
CHAIN_ROWS = 128
HEADS_PER_STEP = 2


def _row_chains(n_rows):
    return [slice(r, r + CHAIN_ROWS) for r in range(0, n_rows, CHAIN_ROWS)]


def _emit_pipelined(chains):
    pending, active = list(chains), []
    while pending or active:
        if pending:
            active.append(pending.pop(0))
        for ch in list(active):
            try:
                next(ch)
            except StopIteration:
                active.remove(ch)


def _exp2_scores(q, k):
    s = _nt_dot(q, k)
    p = jnp.exp2(s - jnp.max(s, axis=-1, keepdims=True))
    return p, jnp.sum(p, axis=-1, keepdims=True)


def _mla_attn_kernel(q_ref, kl_ref, kc_ref, krl_ref, krc_ref, vl_ref, vc_ref,
                     cq_ref, saq_ref, sbq_ref, ck_ref, sak_ref, sbk_ref, gq_ref, gk_ref,
                     o_ref, k_scr, v_scr, *, n_lat, n_ctx, nlat_q, scale):
    qi = pl.program_id(2)
    nf = MLA_ROPE // 4
    heads = [slice(hd * LANES, (hd + 1) * LANES) for hd in range(q_ref.shape[1] // LANES)]

    @pl.when(qi == 0)
    def _():
        for hd, ln in enumerate(heads):
            _prep_keys(k_scr.at[hd], v_scr.at[hd], n_lat, n_ctx,
                       lambda r, ln=ln: kl_ref[r, ln] + krl_ref[r, :],
                       lambda r, ln=ln: kc_ref[r, ln] + krc_ref[r, :],
                       lambda r, ln=ln: vl_ref[r, ln], lambda r, ln=ln: vc_ref[r, ln],
                       gk_ref[...], (ck_ref, sak_ref, sbk_ref), MLA_QK, nf)

    def attend(keys):
        def chain(hd, ln, rows):
            q = _head_norm(q_ref[rows, ln], gq_ref[...], MLA_QK)
            q = _rope(q, cq_ref[rows, :], saq_ref[rows, :], sbq_ref[rows, :], nf) * (scale * LOG2E)
            s = _nt_dot(q.astype(BF16), k_scr[hd, keys, :])
            yield
            p = jnp.exp2(s - jnp.max(s, axis=-1, keepdims=True))
            l = jnp.sum(p, axis=-1, keepdims=True)
            yield
            o = jnp.dot(p.astype(BF16), v_scr[hd, keys, :], preferred_element_type=F32)
            o_ref[rows, ln] = (o * (1.0 / l)).astype(o_ref.dtype)

        _emit_pipelined([chain(hd, ln, rows) for rows in _row_chains(q_ref.shape[0])
                         for hd, ln in enumerate(heads)])

    @pl.when(qi < nlat_q)
    def _():
        attend(slice(0, n_lat + n_ctx))

    @pl.when(qi >= nlat_q)
    def _():
        attend(slice(n_lat, n_lat + n_ctx))


def _mla_attention(q, kv, lat, g_q, g_k, *, n_batch, n_lat, n_ctx, need_ctx):
    nlat_q, nctx_q = n_lat // TQ, n_ctx // TQ
    nq = nlat_q + (nctx_q if need_ctx else 0)
    rows = n_batch * (n_lat + (n_ctx if need_ctx else 0))
    nf = MLA_ROPE // 4
    tabs = _rope_tables(n_lat, TQ, ((MLA_NOPE, True), (MLA_NOPE + 2 * nf, False)), nf)
    h = MLA_HEADS
    ctx0 = n_batch * n_lat // n_ctx
    kr_blk = (MLA_Q_LORA + MLA_KV_LORA) // LANES
    gq = jnp.zeros((1, LANES), F32).at[0, :MLA_QK].set(g_q)
    gk = jnp.zeros((1, LANES), F32).at[0, :MLA_QK].set(g_k)
    hps = HEADS_PER_STEP
    hw = hps * LANES
    qspec = pl.BlockSpec((TQ, hw), lambda b, hh, qi: (_row_block(b, qi, nlat_q, nctx_q, n_batch), hh))
    tq_spec = pl.BlockSpec((TQ, LANES), lambda b, hh, qi: (jnp.minimum(qi, nlat_q), 0))
    tk_spec = pl.BlockSpec((n_lat, LANES), lambda b, hh, qi: (0, 0))
    vec = pl.BlockSpec((1, LANES), lambda b, hh, qi: (0, 0))
    kern = functools.partial(_mla_attn_kernel, n_lat=n_lat, n_ctx=n_ctx, nlat_q=nlat_q,
                             scale=MLA_QK ** -0.5)
    return pl.pallas_call(
        kern,
        grid=(n_batch, h // hps, nq),
        in_specs=[qspec,
                  pl.BlockSpec((n_lat, hw), lambda b, hh, qi: (b, hh)),
                  pl.BlockSpec((n_ctx, hw), lambda b, hh, qi: (ctx0 + b, hh)),
                  pl.BlockSpec((n_lat, LANES), lambda b, hh, qi: (b, kr_blk)),
                  pl.BlockSpec((n_ctx, LANES), lambda b, hh, qi: (ctx0 + b, kr_blk)),
                  pl.BlockSpec((n_lat, hw), lambda b, hh, qi: (b, h // hps + hh)),
                  pl.BlockSpec((n_ctx, hw), lambda b, hh, qi: (ctx0 + b, h // hps + hh)),
                  tq_spec, tq_spec, tq_spec, tk_spec, tk_spec, tk_spec, vec, vec],
        out_specs=qspec,
        out_shape=jax.ShapeDtypeStruct((rows, h * LANES), BF16),
        scratch_shapes=[pltpu.VMEM((hps, n_lat + n_ctx, LANES), BF16),
                        pltpu.VMEM((hps, n_lat + n_ctx, LANES), BF16)],
        compiler_params=_cparams(("parallel", "parallel", "arbitrary")),
        name="mla_attention",
    )(q, kv, kv, lat, lat, kv, kv, *tabs, *tabs, gq, gk)


def _pad_cols(w, groups, width, first=0):
    k = w.shape[0]
    w3 = w.reshape(k, groups, width)
    out = jnp.zeros((k, groups, LANES), w.dtype).at[:, :, first:first + width].set(w3)
    return out.reshape(k, groups * LANES)


def _pad_rows(w, groups, width):
    d = w.shape[1]
    w3 = w.reshape(groups, width, d)
    return jnp.zeros((groups, LANES, d), w.dtype).at[:, :width, :].set(w3).reshape(groups * LANES, d)


def _mla_mixer(s, modv, g_mix, p, w_o, *, n_batch, n_lat, n_ctx, need_ctx):
    w_down, g_cq, g_ckv, w_uq, w_ukv, g_q, g_k = p
    d = s.shape[1]
    rows_all = n_batch * (n_lat + n_ctx)
    geo = dict(n_lat_rows=n_batch * n_lat, n_per_batch=n_lat)
    wd = jnp.concatenate([w_down[:, :MLA_Q_LORA + MLA_KV_LORA],
                          _pad_cols(w_down[:, MLA_Q_LORA + MLA_KV_LORA:], 1, MLA_ROPE, MLA_NOPE)], axis=1)
    lat = _nm_matmul(s, 0, d, g_mix, modv, rows_all, wd.astype(BF16), name="mla_down", **geo)
    wq = _pad_cols(w_uq, MLA_HEADS, MLA_QK).astype(BF16)
    q = _nm_matmul(lat, 0, MLA_Q_LORA, g_cq, modv, rows_all, wq, use_mod=False, name="mla_uq", **geo)
    wkv3 = w_ukv.reshape(MLA_KV_LORA, MLA_HEADS, MLA_NOPE + MLA_V)
    wkv = jnp.concatenate([_pad_cols(wkv3[:, :, :MLA_NOPE].reshape(MLA_KV_LORA, -1), MLA_HEADS, MLA_NOPE),
                           _pad_cols(wkv3[:, :, MLA_NOPE:].reshape(MLA_KV_LORA, -1), MLA_HEADS, MLA_V)],
                          axis=1).astype(BF16)
    kv = _nm_matmul(lat, MLA_Q_LORA // MLA_KV_LORA, MLA_KV_LORA, g_ckv, modv, rows_all, wkv,
                    use_mod=False, name="mla_ukv", **geo)
    o = _mla_attention(q, kv, lat, g_q, g_k, n_batch=n_batch, n_lat=n_lat, n_ctx=n_ctx, need_ctx=need_ctx)
    wo = _pad_rows(w_o, MLA_HEADS, MLA_V).astype(BF16)
    return o, wo


def _diff_attn_kernel(q_ref, kl_ref, kc_ref, vl_ref, vc_ref,
                      cq_ref, saq_ref, sbq_ref, ck_ref, sak_ref, sbk_ref, gq_ref, gk_ref,
                      lam_ref, gsub_ref, o_ref, k_scr, v_scr,
                      *, n_lat, n_ctx, nlat_q, scale, lam_init):
    qi = pl.program_id(2)
    nf = DIFF_HD // 4
    tabs = (ck_ref, sak_ref, sbk_ref)
    tile = lambda t: slice(t * LANES, (t + 1) * LANES)
    n_heads = o_ref.shape[1] // LANES

    @pl.when(qi == 0)
    def _():
        for hd in range(n_heads):
            ln = tile(hd)
            for sub in range(2):
                kt = tile(2 * hd + sub)
                _prep_keys(k_scr.at[2 * hd + sub], v_scr.at[hd] if sub == 0 else None, n_lat, n_ctx,
                           lambda r, kt=kt: kl_ref[r, kt], lambda r, kt=kt: kc_ref[r, kt],
                           lambda r, ln=ln: vl_ref[r, ln], lambda r, ln=ln: vc_ref[r, ln],
                           gk_ref[...], tabs, DIFF_HD, nf)

    lp = lam_ref[...]
    lam = (jnp.exp(jnp.sum(lp[0:1] * lp[1:2], axis=-1, keepdims=True))
           - jnp.exp(jnp.sum(lp[2:3] * lp[3:4], axis=-1, keepdims=True)) + lam_init)

    def attend(keys):
        def scores(t, rows):
            q = _head_norm(q_ref[rows, tile(t)], gq_ref[...], DIFF_HD)
            q = _rope(q, cq_ref[rows, :], saq_ref[rows, :], sbq_ref[rows, :], nf) * (scale * LOG2E)
            return _nt_dot(q.astype(BF16), k_scr[t, keys, :])

        def chain(hd, rows):
            s0 = scores(2 * hd, rows)
            s1 = scores(2 * hd + 1, rows)
            yield
            p0 = jnp.exp2(s0 - jnp.max(s0, axis=-1, keepdims=True))
            p1 = jnp.exp2(s1 - jnp.max(s1, axis=-1, keepdims=True))
            l0 = jnp.sum(p0, axis=-1, keepdims=True)
            l1 = jnp.sum(p1, axis=-1, keepdims=True)
            a = (p0 * (1.0 / l0) - p1 * (lam / l1)).astype(BF16)
            yield
            o = jnp.dot(a, v_scr[hd, keys, :], preferred_element_type=F32)
            o = _head_norm(o, gsub_ref[...], 2 * DIFF_HD) * (1.0 - lam_init)
            o_ref[rows, tile(hd)] = o.astype(o_ref.dtype)

        _emit_pipelined([chain(hd, rows) for rows in _row_chains(q_ref.shape[0])
                         for hd in range(n_heads)])

    @pl.when(qi < nlat_q)
    def _():
        attend(slice(0, n_lat + n_ctx))

    @pl.when(qi >= nlat_q)
    def _():
        attend(slice(n_lat, n_lat + n_ctx))


def _diff_attention(qkv, g_q, g_k, lam_p, g_sub, lam_init, *, n_batch, n_lat, n_ctx, need_ctx):
    nlat_q, nctx_q = n_lat // TQ, n_ctx // TQ
    nq = nlat_q + (nctx_q if need_ctx else 0)
    rows = n_batch * (n_lat + (n_ctx if need_ctx else 0))
    nf = DIFF_HD // 4
    tabs = _rope_tables(n_lat, TQ, ((0, True), (2 * nf, False)), nf)
    nh = DIFF_HEADS
    ctx0 = n_batch * n_lat // n_ctx
    gq = jnp.zeros((1, LANES), F32).at[0, :DIFF_HD].set(g_q)
    gk = jnp.zeros((1, LANES), F32).at[0, :DIFF_HD].set(g_k)
    lam8 = jnp.zeros((8, LANES), F32).at[:4, :DIFF_HD].set(lam_p.astype(F32))
    rb = lambda b, qi: _row_block(b, qi, nlat_q, nctx_q, n_batch)
    tq_spec = pl.BlockSpec((TQ, LANES), lambda b, hh, qi: (jnp.minimum(qi, nlat_q), 0))
    tk_spec = pl.BlockSpec((n_lat, LANES), lambda b, hh, qi: (0, 0))
    vec = pl.BlockSpec((1, LANES), lambda b, hh, qi: (0, 0))

    hps = HEADS_PER_STEP
    qk_w, v_w = 2 * hps * LANES, hps * LANES
    k_blk0 = 2 * nh // (2 * hps)
    v_blk0 = 4 * nh // hps
    kern = functools.partial(_diff_attn_kernel, n_lat=n_lat, n_ctx=n_ctx, nlat_q=nlat_q,
                             scale=DIFF_HD ** -0.5, lam_init=lam_init)
    n_all = n_lat + n_ctx
    return pl.pallas_call(
        kern,
        grid=(n_batch, nh // hps, nq),
        in_specs=[pl.BlockSpec((TQ, qk_w), lambda b, hh, qi: (rb(b, qi), hh)),
                  pl.BlockSpec((n_lat, qk_w), lambda b, hh, qi: (b, k_blk0 + hh)),
                  pl.BlockSpec((n_ctx, qk_w), lambda b, hh, qi: (ctx0 + b, k_blk0 + hh)),
                  pl.BlockSpec((n_lat, v_w), lambda b, hh, qi: (b, v_blk0 + hh)),
                  pl.BlockSpec((n_ctx, v_w), lambda b, hh, qi: (ctx0 + b, v_blk0 + hh)),
                  tq_spec, tq_spec, tq_spec, tk_spec, tk_spec, tk_spec, vec, vec,
                  pl.BlockSpec((8, LANES), lambda b, hh, qi: (0, 0)), vec],
        out_specs=pl.BlockSpec((TQ, v_w), lambda b, hh, qi: (rb(b, qi), hh)),
        out_shape=jax.ShapeDtypeStruct((rows, nh * LANES), BF16),
        scratch_shapes=[pltpu.VMEM((2 * hps, n_all, LANES), BF16), pltpu.VMEM((hps, n_all, LANES), BF16)],
        compiler_params=_cparams(("parallel", "parallel", "arbitrary")),
        name="diff_attention",
    )(qkv, qkv, qkv, qkv, qkv, *tabs, *tabs, gq, gk, lam8, g_sub.reshape(1, LANES))


def _diff_mixer(s, modv, g_mix, w_qkv, g_q, g_k, lam_p, g_sub, w_o, lam_init,
                *, n_batch, n_lat, n_ctx, need_ctx):
    d = s.shape[1]
    width = DIFF_HEADS * 2 * DIFF_HD
    rows_all = n_batch * (n_lat + n_ctx)
    w = jnp.concatenate([_pad_cols(w_qkv[:, :width], 2 * DIFF_HEADS, DIFF_HD),
                         _pad_cols(w_qkv[:, width:2 * width], 2 * DIFF_HEADS, DIFF_HD),
                         w_qkv[:, 2 * width:]], axis=1).astype(BF16)
    qkv = _nm_matmul(s, 0, d, g_mix, modv, rows_all, w, n_lat_rows=n_batch * n_lat,
                     n_per_batch=n_lat, name="diff_qkv")
    o = _diff_attention(qkv, g_q, g_k, lam_p, g_sub, lam_init,
                        n_batch=n_batch, n_lat=n_lat, n_ctx=n_ctx, need_ctx=need_ctx)
    return o, w_o.astype(BF16)


def _swa_attn_kernel(q_ref, kl_ref, kc_ref, vl_ref, vc_ref,
                     cq_ref, saq_ref, sbq_ref, ck_ref, sak_ref, sbk_ref, gq_ref, gk_ref, sink_ref,
                     o_ref, k_scr, v_scr, *, n_lat, n_ctx, nlat_q, scale):
    qi = pl.program_id(2)
    nf = SWA_HD // 4
    band = 3 * Q_BLOCK

    @pl.when(qi == 0)
    def _():
        _prep_keys(k_scr, v_scr, n_lat, n_ctx, lambda r: kl_ref[r, :], lambda r: kc_ref[r, :],
                   lambda r: vl_ref[r, :], lambda r: vc_ref[r, :], gk_ref[...],
                   (ck_ref, sak_ref, sbk_ref), SWA_HD, nf)

    row_chains = _row_chains(q_ref.shape[0])
    reach = jnp.where(qi < nlat_q, WINDOW, -1)

    def chain(ci, rows, gi):
        qb = qi * len(row_chains) + ci
        k0 = pl.multiple_of(jnp.clip((qb - 1) * Q_BLOCK, 0, n_lat - band), Q_BLOCK)
        kpos = k0 + lax.broadcasted_iota(jnp.int32, (Q_BLOCK, band), 1)
        qpos = qb * Q_BLOCK + lax.broadcasted_iota(jnp.int32, (Q_BLOCK, band), 0)
        lanes = slice(gi * LANES, (gi + 1) * LANES)
        q = _head_norm(q_ref[rows, lanes], gq_ref[...], SWA_HD)
        q = _rope(q, cq_ref[rows, :], saq_ref[rows, :], sbq_ref[rows, :], nf) * (scale * LOG2E)
        q = q.astype(BF16)
        sb = _nt_dot(q, k_scr[pl.ds(k0, band), :])
        sc = _nt_dot(q, k_scr[n_lat:n_lat + n_ctx, :])
        yield
        sb = jnp.where(jnp.abs(kpos - qpos) <= reach, sb, NEG_INF)
        sink = sink_ref[gi][0:1, 0:1] * LOG2E
        m = jnp.maximum(jnp.maximum(jnp.max(sb, axis=-1, keepdims=True),
                                    jnp.max(sc, axis=-1, keepdims=True)), sink)
        pb = jnp.exp2(sb - m)
        pc = jnp.exp2(sc - m)
        l = (jnp.sum(pb, axis=-1, keepdims=True) + jnp.sum(pc, axis=-1, keepdims=True)
             + jnp.exp2(sink - m))
        yield
        o = (jnp.dot(pb.astype(BF16), v_scr[pl.ds(k0, band), :], preferred_element_type=F32)
             + jnp.dot(pc.astype(BF16), v_scr[n_lat:n_lat + n_ctx, :], preferred_element_type=F32))
        o_ref[rows, lanes] = (o * (1.0 / l)).astype(o_ref.dtype)

    _emit_pipelined([chain(ci, rows, gi) for ci, rows in enumerate(row_chains)
                     for gi in range(SWA_GROUP)])


def _swa_attention(qkv, g_q, g_k, sink, *, n_batch, n_lat, n_ctx, need_ctx):
    nlat_q, nctx_q = n_lat // TQ, n_ctx // TQ
    nq = nlat_q + (nctx_q if need_ctx else 0)
    rows = n_batch * (n_lat + (n_ctx if need_ctx else 0))
    nf = SWA_HD // 4
    tabs = _rope_tables(n_lat, TQ, ((0, True), (2 * nf, False)), nf)
    nh, nkv, grp = SWA_HEADS, SWA_KV_HEADS, SWA_GROUP
    ctx0 = n_batch * n_lat // n_ctx
    gq = jnp.zeros((1, LANES), F32).at[0, :SWA_HD].set(g_q)
    gk = jnp.zeros((1, LANES), F32).at[0, :SWA_HD].set(g_k)
    sink_t = jnp.broadcast_to(sink.astype(F32).reshape(nh, 1, 1), (nh, 8, LANES))
    rb = lambda b, qi: _row_block(b, qi, nlat_q, nctx_q, n_batch)
    tq_spec = pl.BlockSpec((TQ, LANES), lambda b, kh, qi: (jnp.minimum(qi, nlat_q), 0))
    tk_spec = pl.BlockSpec((n_lat, LANES), lambda b, kh, qi: (0, 0))
    vec = pl.BlockSpec((1, LANES), lambda b, kh, qi: (0, 0))
    kern = functools.partial(_swa_attn_kernel, n_lat=n_lat, n_ctx=n_ctx, nlat_q=nlat_q,
                             scale=SWA_HD ** -0.5)
    n_all = n_lat + n_ctx
    return pl.pallas_call(
        kern,
        grid=(n_batch, nkv, nq),
        in_specs=[pl.BlockSpec((TQ, grp * LANES), lambda b, kh, qi: (rb(b, qi), kh)),
                  pl.BlockSpec((n_lat, LANES), lambda b, kh, qi: (b, nh + kh)),
                  pl.BlockSpec((n_ctx, LANES), lambda b, kh, qi: (ctx0 + b, nh + kh)),
                  pl.BlockSpec((n_lat, LANES), lambda b, kh, qi: (b, nh + nkv + kh)),
                  pl.BlockSpec((n_ctx, LANES), lambda b, kh, qi: (ctx0 + b, nh + nkv + kh)),
                  tq_spec, tq_spec, tq_spec, tk_spec, tk_spec, tk_spec, vec, vec,
                  pl.BlockSpec((grp, 8, LANES), lambda b, kh, qi: (kh, 0, 0))],
        out_specs=pl.BlockSpec((TQ, grp * LANES), lambda b, kh, qi: (rb(b, qi), kh)),
        out_shape=jax.ShapeDtypeStruct((rows, nh * LANES), BF16),
        scratch_shapes=[pltpu.VMEM((n_all, LANES), BF16), pltpu.VMEM((n_all, LANES), BF16)],
        compiler_params=_cparams(("parallel", "parallel", "arbitrary")),
        name="swa_attention",
    )(qkv, qkv, qkv, qkv, qkv, *tabs, *tabs, gq, gk, sink_t)


def _swa_mixer(s, modv, g_mix, w_qkv, g_q, g_k, sink, w_o, *, n_batch, n_lat, n_ctx, need_ctx):
    d = s.shape[1]
    qw, kw = SWA_HEADS * SWA_HD, SWA_KV_HEADS * SWA_HD
    rows_all = n_batch * (n_lat + n_ctx)
    w = jnp.concatenate([_pad_cols(w_qkv[:, :qw], SWA_HEADS, SWA_HD),
                         _pad_cols(w_qkv[:, qw:qw + kw], SWA_KV_HEADS, SWA_HD),
                         _pad_cols(w_qkv[:, qw + kw:], SWA_KV_HEADS, SWA_HD)], axis=1).astype(BF16)
    qkv = _nm_matmul(s, 0, d, g_mix, modv, rows_all, w, n_lat_rows=n_batch * n_lat,
                     n_per_batch=n_lat, name="swa_qkv")
    o = _swa_attention(qkv, g_q, g_k, sink, n_batch=n_batch, n_lat=n_lat, n_ctx=n_ctx, need_ctx=need_ctx)
    return o, _pad_rows(w_o, SWA_HEADS, SWA_HD).astype(BF16)


def _top_values(s, count):
    vals = []
    for _ in range(count):
        m = jnp.max(s, axis=0, keepdims=True)
        vals.append(m)
        s = jnp.where(s == m, NEG_BIG, s)
    return vals


def _peer_score_kernel(s_ref, g_ref, mod_ref, wq_ref, keys_ref,
                       ht_ref, c_ref, e1_ref, s2_ref, e2_ref, qt_scr):
    k = PEER_TOPK
    h = _norm_mod(s_ref[...], g_ref[...], mod_ref[0], 3, 4)
    ht = h.T.astype(BF16)
    ht_ref[...] = ht
    qt_scr[...] = jnp.dot(wq_ref[...], ht, preferred_element_type=F32)

    tm = ht.shape[1]
    group = 2

    def group_body(gidx, carry):
        heads = [gidx * group + i for i in range(group)]
        scores = []
        for hh in heads:
            for half in range(2):
                r = pl.ds(pl.multiple_of((hh * 2 + half) * LANES, LANES), LANES)
                scores.append(jnp.dot(keys_ref[hh, half], qt_scr[r, :].astype(BF16),
                                      preferred_element_type=F32))
        tops = _top_values(jnp.concatenate(scores, axis=1), k + 1)
        cands = []
        for i in range(group):
            v1 = [t[:, (2 * i) * tm:(2 * i + 1) * tm] for t in tops]
            v2 = [t[:, (2 * i + 1) * tm:(2 * i + 2) * tm] for t in tops]
            top8 = jnp.concatenate(v2[:8], axis=0)
            pieces = [v1[0] + jnp.concatenate(v2[:k], axis=0)]
            pieces += [v1[j] + top8 for j in range(1, 8)]
            pieces += [jnp.concatenate(v1[8:k], axis=0) + v2[0]]
            pieces += [jnp.concatenate([v1[0] + v2[k], v1[k] + v2[0]]
                                       + [jnp.full_like(v1[0], NEG_BIG)] * 6, axis=0)]
            cands.append(jnp.concatenate(pieces, axis=0))
        best = _top_values(jnp.concatenate(cands, axis=1), k + 1)
        grp = (PEER_KEYS // SUBLANES, SUBLANES, tm)
        for i, hh in enumerate(heads):
            ms = [t[:, i * tm:(i + 1) * tm] for t in best]
            s1, s2 = scores[2 * i], scores[2 * i + 1]
            m1, m2 = tops[0][:, (2 * i) * tm:(2 * i + 1) * tm], tops[0][:, (2 * i + 1) * tm:(2 * i + 2) * tm]
            thr = 0.5 * (ms[k - 1] + ms[k])
            z = jnp.ones_like(ms[0])
            for r in range(1, k):
                z = z + jnp.exp(ms[r] - ms[0])
            c_ref[hh] = (thr - s1).reshape(grp)
            e1_ref[hh] = (jnp.exp(s1 - m1) * (1.0 / z)).reshape(grp)
            s2_ref[hh] = s2
            e2_ref[hh] = jnp.exp(s2 - m2)
        return carry

    lax.fori_loop(0, PEER_HEADS // group, group_body, 0)


def _peer_scores(s, modv, g_ffn, wq_t, keys, rows, *, n_lat_rows, n_per_batch):
    d = s.shape[1]
    n_batch = modv.shape[0] - 1
    nlat_blocks = n_lat_rows // TM
    bpb = n_per_batch // TM
    nq = wq_t.shape[0]
    hk = (PEER_HEADS, PEER_KEYS, rows)
    hk_spec = pl.BlockSpec((PEER_HEADS, PEER_KEYS, TM), lambda i: (0, 0, i))
    hg = (PEER_HEADS, PEER_KEYS // SUBLANES, SUBLANES, rows)
    hg_spec = pl.BlockSpec((PEER_HEADS, PEER_KEYS // SUBLANES, SUBLANES, TM), lambda i: (0, 0, 0, i))
    return pl.pallas_call(
        _peer_score_kernel,
        grid=(rows // TM,),
        in_specs=[pl.BlockSpec((TM, d), lambda i: (i, 0)),
                  pl.BlockSpec((1, d), lambda i: (0, 0)),
                  pl.BlockSpec((1, 8, d), lambda i: (_mod_index(i, nlat_blocks, bpb, n_batch), 0, 0)),
                  pl.BlockSpec((nq, d), lambda i: (0, 0)),
                  pl.BlockSpec((PEER_HEADS, 2, PEER_KEYS, LANES), lambda i: (0, 0, 0, 0))],
        out_specs=[pl.BlockSpec((d, TM), lambda i: (0, i)), hg_spec, hg_spec, hk_spec, hk_spec],
        out_shape=[jax.ShapeDtypeStruct((d, rows), BF16)] + [jax.ShapeDtypeStruct(hg, F32)] * 2
                  + [jax.ShapeDtypeStruct(hk, F32)] * 2,
        scratch_shapes=[pltpu.VMEM((nq, TM), F32)],
        compiler_params=_cparams(("parallel",)),
        name="peer_scores",
    )(s, g_ffn.reshape(1, d), modv, wq_t, keys)


def _peer_expert_kernel(ht_ref, u_ref, vt_ref, c_ref, e1_ref, s2_ref, e2_ref, s_ref, mod_ref,
                        o_ref, acc_scr, st0_scr, st1_scr, at0_scr, at1_scr, *, n_exp_blocks):
    g = pl.program_id(0)
    st_scr, at_scr = (st0_scr, st1_scr), (at0_scr, at1_scr)
    te, tm = st0_scr.shape
    jc = jnp.maximum(g - 2, 0) % n_exp_blocks

    @pl.when(g == 0)
    def _():
        for ref in st_scr + at_scr:
            ref[...] = jnp.zeros_like(ref)

    @pl.when(jc == 0)
    def _():
        acc_scr[...] = jnp.zeros_like(acc_scr)

    def stages(slot_a, slot_b):
        mxu_chunk = 2 * LANES

        def value_chunk(q):
            ks = slice(q * mxu_chunk, (q + 1) * mxu_chunk)
            acc_scr[...] += jnp.dot(vt_ref[:, ks], at_scr[slot_a][ks, :], preferred_element_type=F32)

        def score_chunk(q):
            ms = slice(q * mxu_chunk, (q + 1) * mxu_chunk)
            st_scr[slot_a][ms, :] = jnp.dot(u_ref[ms, :], ht_ref[...], preferred_element_type=F32)

        rg_rows = 2 * SUBLANES
        al_group = 4
        col_tiles = 2
        tiles = [(rg, c0, al0)
                 for rg in range(LANES // rg_rows)
                 for c0 in range(0, tm // LANES, col_tiles)
                 for al0 in range(0, te // LANES, al_group)]
        n_chunks = te // mxu_chunk
        per_chunk = len(tiles) // (2 * n_chunks)
        for t, (rg, c0, al0) in enumerate(tiles):
            if t % per_chunk == 0:
                q = t // per_chunk
                (value_chunk if q % 2 == 0 else score_chunk)(q // 2)
            brows = slice(rg * rg_rows, (rg + 1) * rg_rows)
            lanes = [slice((c0 + ct) * LANES, (c0 + ct + 1) * LANES) for ct in range(col_tiles)]
            gate = [[jnp.zeros((rg_rows, LANES), F32) for _ in lanes] for _ in range(al_group)]
            for hh in range(PEER_HEADS):
                for ct, ln in enumerate(lanes):
                    s2t = s2_ref[hh, brows, ln]
                    e2t = e2_ref[hh, brows, ln]
                    for k in range(al_group):
                        al = al0 + k
                        c = c_ref[hh, 0, al:al + 1, ln]
                        e1 = e1_ref[hh, 0, al:al + 1, ln]
                        gate[k][ct] = gate[k][ct] + jnp.where(s2t >= c, e2t, 0.0) * e1
            for k in range(al_group):
                erows = slice((al0 + k) * LANES + rg * rg_rows, (al0 + k) * LANES + (rg + 1) * rg_rows)
                for ct, ln in enumerate(lanes):
                    sv = st_scr[slot_b][erows, ln]
                    act = 0.5 * sv * (1.0 + lax.erf(sv * (2.0 ** -0.5)))
                    at_scr[slot_b][erows, ln] = (gate[k][ct] * act).astype(BF16)

    for parity in range(2):
        @pl.when(g % 2 == parity)
        def _():
            stages(parity, 1 - parity)

    @pl.when((jc == n_exp_blocks - 1) & (g >= 2))
    def _():
        o_ref[...] = s_ref[...] + mod_ref[0][5:6] * acc_scr[...].T


def _peer_experts(ht, u, vt, c, e1, s2, e2, s, modv, rows, *, n_lat_rows, n_per_batch):
    d = s.shape[1]
    n_exp = u.shape[0]
    n_batch = modv.shape[0] - 1
    tm, te = PEER_TM, PEER_TE
    nlat_blocks = n_lat_rows // tm
    bpb = n_per_batch // tm
    nj = n_exp // te
    n_pairs = (rows // tm) * nj

    def pair(g, lag):
        p = jnp.clip(g - lag, 0, n_pairs - 1)
        return p // nj, p % nj

    hk_spec = pl.BlockSpec((PEER_HEADS, PEER_KEYS, tm), lambda g: (0, 0, pair(g, 1)[0]))
    hg_spec = pl.BlockSpec((PEER_HEADS, 1, SUBLANES, tm), lambda g: (0, pair(g, 1)[1], 0, pair(g, 1)[0]))
    kern = functools.partial(_peer_expert_kernel, n_exp_blocks=nj)
    return pl.pallas_call(
        kern,
        grid=(n_pairs + 2,),
        in_specs=[pl.BlockSpec((d, tm), lambda g: (0, pair(g, 0)[0])),
                  pl.BlockSpec((te, d), lambda g: (pair(g, 0)[1], 0)),
                  pl.BlockSpec((d, te), lambda g: (0, pair(g, 2)[1])),
                  hg_spec, hg_spec, hk_spec, hk_spec,
                  pl.BlockSpec((tm, d), lambda g: (pair(g, 2)[0], 0)),
                  pl.BlockSpec((1, 8, d),
                               lambda g: (_mod_index(pair(g, 2)[0], nlat_blocks, bpb, n_batch), 0, 0))],
        out_specs=pl.BlockSpec((tm, d), lambda g: (pair(g, 2)[0], 0)),
        out_shape=jax.ShapeDtypeStruct((rows, d), F32),
        scratch_shapes=[pltpu.VMEM((d, tm), F32), pltpu.VMEM((te, tm), F32), pltpu.VMEM((te, tm), F32),
                        pltpu.VMEM((te, tm), BF16), pltpu.VMEM((te, tm), BF16)],
        compiler_params=_cparams(("arbitrary",)),
        name="peer_experts",
    )(ht, u, vt, c, e1, s2, e2, s, modv)


def _peer(s, modv, g_ffn, w_q, keys, u_tab, v_tab, rows, *, n_lat_rows, n_per_batch):
    geo = dict(n_lat_rows=n_lat_rows, n_per_batch=n_per_batch)
    ht, c, e1, s2, e2 = _peer_scores(s, modv, g_ffn, w_q.T.astype(BF16), keys.astype(BF16), rows, **geo)
    return _peer_experts(ht, u_tab.astype(BF16), v_tab.T.astype(BF16), c, e1, s2, e2, s, modv, rows, **geo)


def kernel(x, c, ctx, c_ctx, ada_w, ada_b, norm_mix, norm_ffn, mla_w_down, mla_g_cq, mla_g_ckv, mla_w_uq, mla_w_ukv, mla_g_q, mla_g_k, mla_w_o, diff_w_qkv, diff_g_q, diff_g_k, diff_lambda, diff_g_sub, diff_w_o, swa_w_qkv, swa_g_q, swa_g_k, swa_sink, swa_w_o, peer_w_q, peer_keys, peer_u, peer_v):
    n_batch, n_lat, d = x.shape
    n_ctx = ctx.shape[1]
    depth = ada_w.shape[0]
    rows_lat = n_batch * n_lat
    rows_all = rows_lat + n_batch * n_ctx
    assert n_lat % 512 == 0 and n_ctx % 256 == 0 and (n_batch * n_ctx) % PEER_TM == 0
    assert n_lat % GRID_W == 0 and rows_lat % n_ctx == 0 and n_lat >= 3 * Q_BLOCK

    s = jnp.concatenate([x.reshape(rows_lat, d), ctx.reshape(n_batch * n_ctx, d)], axis=0)
    cc = jnp.zeros((16, d), F32).at[:n_batch].set(c).at[n_batch].set(c_ctx)
    mods = _ada_mods(cc, ada_w, ada_b)
    geo = dict(n_lat_rows=rows_lat, n_per_batch=n_lat)

    for i in range(depth):
        last = i == depth - 1
        kind, j = i % 3, i // 3
        modv = jnp.zeros((n_batch + 1, 8, d), F32).at[:, :6, :].set(
            mods[i, :n_batch + 1].reshape(n_batch + 1, 6, d))
        dims = dict(n_batch=n_batch, n_lat=n_lat, n_ctx=n_ctx, need_ctx=not last)
        if kind == 0:
            p = (mla_w_down[j], mla_g_cq[j], mla_g_ckv[j], mla_w_uq[j], mla_w_ukv[j], mla_g_q[j], mla_g_k[j])
            o, wo = _mla_mixer(s, modv, norm_mix[i], p, mla_w_o[j], **dims)
        elif kind == 1:
            lam_init = 0.8 - 0.6 * math.exp(-0.3 * i)
            o, wo = _diff_mixer(s, modv, norm_mix[i], diff_w_qkv[j], diff_g_q[j], diff_g_k[j],
                                diff_lambda[j], diff_g_sub[j], diff_w_o[j], lam_init, **dims)
        else:
            o, wo = _swa_mixer(s, modv, norm_mix[i], swa_w_qkv[j], swa_g_q[j], swa_g_k[j],
                               swa_sink[j], swa_w_o[j], **dims)
        rows = rows_lat if last else rows_all
        s = _proj_resid(o, wo, s, modv, rows, gate_row=2, **geo)
        s = _peer(s, modv, norm_ffn[i], peer_w_q[i], peer_keys[i], peer_u[i], peer_v[i], rows, **geo)
    return s.reshape(n_batch, n_lat, d)
```

```python
import functools
import math

import jax
import jax.numpy as jnp
from jax import lax
from jax.experimental import pallas as pl
from jax.experimental.pallas import tpu as pltpu

F32 = jnp.float32
BF16 = jnp.bfloat16

LANES = 128
EPS = 1e-6
NEG_INF = -1e30
NEG_BIG = -3.0e38
ROPE_THETA = 10000.0
GRID_W = 64
Q_BLOCK = 128
WINDOW = 128

MLA_HEADS, MLA_NOPE, MLA_ROPE, MLA_V = 16, 64, 32, 64
MLA_QK = MLA_NOPE + MLA_ROPE
MLA_Q_LORA, MLA_KV_LORA = 768, 256
DIFF_HEADS, DIFF_HD = 8, 64
SWA_HEADS, SWA_KV_HEADS, SWA_HD = 16, 4, 64
SWA_GROUP = SWA_HEADS // SWA_KV_HEADS
PEER_HEADS, PEER_KEYS, PEER_TOPK = 8, 128, 16
PEER_EXPERTS = PEER_KEYS * PEER_KEYS

VMEM_LIMIT = 56 * 1024 * 1024

TM = 256
TQ = 256
PEER_TM = 512
SUBLANES = 8
PEER_TE = SUBLANES * LANES
MXU_CHUNK_ROWS = 128


def _cparams(sem):
    return pltpu.CompilerParams(dimension_semantics=sem, vmem_limit_bytes=VMEM_LIMIT)


def _ada_kernel(c_ref, w_ref, b_ref, o_ref):
    c = c_ref[...]
    sc = c * (1.0 / (1.0 + jnp.exp(-c)))
    o_ref[0] = jnp.dot(sc, w_ref[0], preferred_element_type=F32,
                       precision=lax.Precision.HIGHEST) + b_ref[0]


def _ada_mods(cc, ada_w, ada_b):
    depth, d, d6 = ada_w.shape
    rows = cc.shape[0]
    tn = 1536
    return pl.pallas_call(
        _ada_kernel,
        grid=(depth, d6 // tn),
        in_specs=[pl.BlockSpec((rows, d), lambda i, n: (0, 0)),
                  pl.BlockSpec((1, d, tn), lambda i, n: (i, 0, n)),
                  pl.BlockSpec((1, 1, tn), lambda i, n: (i, 0, n))],
        out_specs=pl.BlockSpec((1, rows, tn), lambda i, n: (i, 0, n)),
        out_shape=jax.ShapeDtypeStruct((depth, rows, d6), F32),
        compiler_params=_cparams(("arbitrary", "arbitrary")),
        name="ada_mods",
    )(cc, ada_w, ada_b.reshape(depth, 1, d6))


def _norm_mod(x, g, mod, shift_row, scale_row):
    ms = jnp.mean(x * x, axis=-1, keepdims=True)
    y = x * lax.rsqrt(ms + EPS) * g
    if mod is not None:
        y = y * (1.0 + mod[scale_row:scale_row + 1]) + mod[shift_row:shift_row + 1]
    return y


def _nm_kernel(x_ref, g_ref, mod_ref, w_ref, o_ref, h_scr, *, use_mod, shift_row, scale_row):
    @pl.when(pl.program_id(1) == 0)
    def _():
        mod = mod_ref[0] if use_mod else None
        h_scr[...] = _norm_mod(x_ref[...], g_ref[...], mod, shift_row, scale_row).astype(BF16)

    o_ref[...] = jnp.dot(h_scr[...], w_ref[...], preferred_element_type=F32).astype(o_ref.dtype)


def _mod_index(i, nlat_blocks, blocks_per_batch, n_batch):
    return jnp.where(i < nlat_blocks, i // blocks_per_batch, n_batch)


def _nm_matmul(x, kblk, k, g, modv, rows, w, *, n_lat_rows, n_per_batch, out_dtype=F32,
               use_mod=True, shift_row=0, scale_row=1, name="nm_matmul"):
    nout = w.shape[1]
    tn = nout
    for cand in (2048, 1536, 1280, 1024):
        if nout > 2048 and nout % cand == 0:
            tn = cand
            break
    n_batch = modv.shape[0] - 1
    nlat_blocks = n_lat_rows // TM
    bpb = n_per_batch // TM
    kern = functools.partial(_nm_kernel, use_mod=use_mod, shift_row=shift_row, scale_row=scale_row)
    return pl.pallas_call(
        kern,
        grid=(rows // TM, nout // tn),
        in_specs=[pl.BlockSpec((TM, k), lambda i, n: (i, kblk)),
                  pl.BlockSpec((1, k), lambda i, n: (0, 0)),
                  pl.BlockSpec((1, 8, k), lambda i, n: (_mod_index(i, nlat_blocks, bpb, n_batch), 0, 0)),
                  pl.BlockSpec((k, tn), lambda i, n: (0, n))],
        out_specs=pl.BlockSpec((TM, tn), lambda i, n: (i, n)),
        out_shape=jax.ShapeDtypeStruct((rows, nout), out_dtype),
        scratch_shapes=[pltpu.VMEM((TM, k), BF16)],
        compiler_params=_cparams(("parallel", "arbitrary")),
        name=name,
    )(x, g.reshape(1, k), modv if modv.shape[-1] == k else jnp.zeros((n_batch + 1, 8, k), F32), w)


def _proj_resid_kernel(a_ref, w_ref, s_ref, mod_ref, o_ref, *, gate_row):
    y = jnp.dot(a_ref[...].astype(BF16), w_ref[...], preferred_element_type=F32)
    o_ref[...] = s_ref[...] + mod_ref[0][gate_row:gate_row + 1] * y


def _proj_resid(a, w, s, modv, rows, *, n_lat_rows, n_per_batch, gate_row, name="proj_resid"):
    ka = a.shape[1]
    d = w.shape[1]
    n_batch = modv.shape[0] - 1
    nlat_blocks = n_lat_rows // TM
    bpb = n_per_batch // TM
    return pl.pallas_call(
        functools.partial(_proj_resid_kernel, gate_row=gate_row),
        grid=(rows // TM,),
        in_specs=[pl.BlockSpec((TM, ka), lambda i: (i, 0)),
                  pl.BlockSpec((ka, d), lambda i: (0, 0)),
                  pl.BlockSpec((TM, d), lambda i: (i, 0)),
                  pl.BlockSpec((1, 8, d), lambda i: (_mod_index(i, nlat_blocks, bpb, n_batch), 0, 0))],
        out_specs=pl.BlockSpec((TM, d), lambda i: (i, 0)),
        out_shape=jax.ShapeDtypeStruct((rows, d), F32),
        compiler_params=_cparams(("parallel",)),
        name=name,
    )(a, w, s, modv)


def _rope_tables(n_lat, extra_rows, segments, nf):
    pos = jnp.arange(n_lat, dtype=jnp.int32)
    row = (pos // GRID_W).astype(F32)
    col = (pos % GRID_W).astype(F32)
    inv = ROPE_THETA ** (-jnp.arange(nf, dtype=F32) / nf)
    cos = jnp.ones((n_lat, LANES), F32)
    sa = jnp.zeros((n_lat, LANES), F32)
    sb = jnp.zeros((n_lat, LANES), F32)
    for first, use_row in segments:
        ang = (row if use_row else col)[:, None] * inv[None, :]
        c, s = jnp.cos(ang), jnp.sin(ang)
        cos = cos.at[:, first:first + nf].set(c).at[:, first + nf:first + 2 * nf].set(c)
        sa = sa.at[:, first:first + nf].set(-s)
        sb = sb.at[:, first + nf:first + 2 * nf].set(s)
    pad = lambda t, v: jnp.concatenate([t, jnp.full((extra_rows, LANES), v, F32)], axis=0)
    return pad(cos, 1.0), pad(sa, 0.0), pad(sb, 0.0)


def _head_norm(x, g, d):
    ssq = jnp.sum(x * x, axis=-1, keepdims=True) * (1.0 / d)
    return x * lax.rsqrt(ssq + EPS) * g


def _rope(y, cos, sa, sb, nf):
    return y * cos + pltpu.roll(y, LANES - nf, 1) * sa + pltpu.roll(y, nf, 1) * sb


def _row_block(b, qi, nlat_q, nctx_q, n_batch):
    return jnp.where(qi < nlat_q, b * nlat_q + qi, n_batch * nlat_q + b * nctx_q + (qi - nlat_q))


def _prep_keys(k_scr, v_scr, n_lat, n_ctx, load_k_lat, load_k_ctx, load_v_lat, load_v_ctx,
               gk, tabs, d, nf):
    cos_ref, sa_ref, sb_ref = tabs
    ch = 256

    def lat_body(c, carry):
        r = pl.ds(pl.multiple_of(c * ch, ch), ch)
        kk = _head_norm(load_k_lat(r), gk, d)
        kk = _rope(kk, cos_ref[r, :], sa_ref[r, :], sb_ref[r, :], nf)
        k_scr[r, :] = kk.astype(BF16)
        if v_scr is not None:
            v_scr[r, :] = load_v_lat(r).astype(BF16)
        return carry

    lax.fori_loop(0, n_lat // ch, lat_body, 0)

    def ctx_body(c, carry):
        r = pl.ds(pl.multiple_of(c * ch, ch), ch)
        ro = pl.ds(pl.multiple_of(n_lat + c * ch, ch), ch)
        k_scr[ro, :] = _head_norm(load_k_ctx(r), gk, d).astype(BF16)
        if v_scr is not None:
            v_scr[ro, :] = load_v_ctx(r).astype(BF16)
        return carry

    lax.fori_loop(0, n_ctx // ch, ctx_body, 0)


def _nt_dot(a, b):
    return lax.dot_general(a, b, (((1,), (1,)), ((), ())), preferred_element_type=F32)


LOG2E = math.log2(math.e)
CHAIN_ROWS = 128
HEADS_PER_STEP = 2


def _row_chains(n_rows):
    return [slice(r, r + CHAIN_ROWS) for r in range(0, n_rows, CHAIN_ROWS)]


def _emit_pipelined(chains):
    pending, active = list(chains), []
    while pending or active:
        if pending:
            active.append(pending.pop(0))
        for ch in list(active):
            try:
                next(ch)
            except StopIteration:
                active.remove(ch)


def _exp2_scores(q, k):
    s = _nt_dot(q, k)
    p = jnp.exp2(s - jnp.max(s, axis=-1, keepdims=True))
    return p, jnp.sum(p, axis=-1, keepdims=True)


def _mla_attn_kernel(q_ref, kl_ref, kc_ref, krl_ref, krc_ref, vl_ref, vc_ref,
                     cq_ref, saq_ref, sbq_ref, ck_ref, sak_ref, sbk_ref, gq_ref, gk_ref,
                     o_ref, k_scr, v_scr, *, n_lat, n_ctx, nlat_q, scale):
    qi = pl.program_id(2)
    nf = MLA_ROPE // 4
    heads = [slice(hd * LANES, (hd + 1) * LANES) for hd in range(q_ref.shape[1] // LANES)]

    @pl.when(qi == 0)
    def _():
        for hd, ln in enumerate(heads):
            _prep_keys(k_scr.at[hd], v_scr.at[hd], n_lat, n_ctx,
                       lambda r, ln=ln: kl_ref[r, ln] + krl_ref[r, :],
                       lambda r, ln=ln: kc_ref[r, ln] + krc_ref[r, :],
                       lambda r, ln=ln: vl_ref[r, ln], lambda r, ln=ln: vc_ref[r, ln],
                       gk_ref[...], (ck_ref, sak_ref, sbk_ref), MLA_QK, nf)

    def attend(keys):
        def chain(hd, ln, rows):
            q = _head_norm(q_ref[rows, ln], gq_ref[...], MLA_QK)
            q = _rope(q, cq_ref[rows, :], saq_ref[rows, :], sbq_ref[rows, :], nf) * (scale * LOG2E)
            s = _nt_dot(q.astype(BF16), k_scr[hd, keys, :])
            yield
            p = jnp.exp2(s - jnp.max(s, axis=-1, keepdims=True))
            l = jnp.sum(p, axis=-1, keepdims=True)
            yield
            o = jnp.dot(p.astype(BF16), v_scr[hd, keys, :], preferred_element_type=F32)
            o_ref[rows, ln] = (o * (1.0 / l)).astype(o_ref.dtype)

        _emit_pipelined([chain(hd, ln, rows) for rows in _row_chains(q_ref.shape[0])
                         for hd, ln in enumerate(heads)])

    @pl.when(qi < nlat_q)
    def _():
        attend(slice(0, n_lat + n_ctx))

    @pl.when(qi >= nlat_q)
    def _():
        attend(slice(n_lat, n_lat + n_ctx))


def _mla_attention(q, kv, lat, g_q, g_k, *, n_batch, n_lat, n_ctx, need_ctx):
    nlat_q, nctx_q = n_lat // TQ, n_ctx // TQ
    nq = nlat_q + (nctx_q if need_ctx else 0)
    rows = n_batch * (n_lat + (n_ctx if need_ctx else 0))
    nf = MLA_ROPE // 4
    tabs = _rope_tables(n_lat, TQ, ((MLA_NOPE, True), (MLA_NOPE + 2 * nf, False)), nf)
    h = MLA_HEADS
    ctx0 = n_batch * n_lat // n_ctx
    kr_blk = (MLA_Q_LORA + MLA_KV_LORA) // LANES
    gq = jnp.zeros((1, LANES), F32).at[0, :MLA_QK].set(g_q)
    gk = jnp.zeros((1, LANES), F32).at[0, :MLA_QK].set(g_k)
    hps = HEADS_PER_STEP
    hw = hps * LANES
    qspec = pl.BlockSpec((TQ, hw), lambda b, hh, qi: (_row_block(b, qi, nlat_q, nctx_q, n_batch), hh))
    tq_spec = pl.BlockSpec((TQ, LANES), lambda b, hh, qi: (jnp.minimum(qi, nlat_q), 0))
    tk_spec = pl.BlockSpec((n_lat, LANES), lambda b, hh, qi: (0, 0))
    vec = pl.BlockSpec((1, LANES), lambda b, hh, qi: (0, 0))
    kern = functools.partial(_mla_attn_kernel, n_lat=n_lat, n_ctx=n_ctx, nlat_q=nlat_q,
                             scale=MLA_QK ** -0.5)
    return pl.pallas_call(
        kern,
        grid=(n_batch, h // hps, nq),
        in_specs=[qspec,
                  pl.BlockSpec((n_lat, hw), lambda b, hh, qi: (b, hh)),
                  pl.BlockSpec((n_ctx, hw), lambda b, hh, qi: (ctx0 + b, hh)),
                  pl.BlockSpec((n_lat, LANES), lambda b, hh, qi: (b, kr_blk)),
                  pl.BlockSpec((n_ctx, LANES), lambda b, hh, qi: (ctx0 + b, kr_blk)),
                  pl.BlockSpec((n_lat, hw), lambda b, hh, qi: (b, h // hps + hh)),
                  pl.BlockSpec((n_ctx, hw), lambda b, hh, qi: (ctx0 + b, h // hps + hh)),
                  tq_spec, tq_spec, tq_spec, tk_spec, tk_spec, tk_spec, vec, vec],
        out_specs=qspec,
        out_shape=jax.ShapeDtypeStruct((rows, h * LANES), BF16),
        scratch_shapes=[pltpu.VMEM((hps, n_lat + n_ctx, LANES), BF16),
                        pltpu.VMEM((hps, n_lat + n_ctx, LANES), BF16)],
        compiler_params=_cparams(("parallel", "parallel", "arbitrary")),
        name="mla_attention",
    )(q, kv, kv, lat, lat, kv, kv, *tabs, *tabs, gq, gk)


def _pad_cols(w, groups, width, first=0):
    k = w.shape[0]
    w3 = w.reshape(k, groups, width)
    out = jnp.zeros((k, groups, LANES), w.dtype).at[:, :, first:first + width].set(w3)
    return out.reshape(k, groups * LANES)


def _pad_rows(w, groups, width):
    d = w.shape[1]
    w3 = w.reshape(groups, width, d)
    return jnp.zeros((groups, LANES, d), w.dtype).at[:, :width, :].set(w3).reshape(groups * LANES, d)


def _mla_mixer(s, modv, g_mix, p, w_o, *, n_batch, n_lat, n_ctx, need_ctx):
    w_down, g_cq, g_ckv, w_uq, w_ukv, g_q, g_k = p
    d = s.shape[1]
    rows_all = n_batch * (n_lat + n_ctx)
    geo = dict(n_lat_rows=n_batch * n_lat, n_per_batch=n_lat)
    wd = jnp.concatenate([w_down[:, :MLA_Q_LORA + MLA_KV_LORA],
                          _pad_cols(w_down[:, MLA_Q_LORA + MLA_KV_LORA:], 1, MLA_ROPE, MLA_NOPE)], axis=1)
    lat = _nm_matmul(s, 0, d, g_mix, modv, rows_all, wd.astype(BF16), name="mla_down", **geo)
    wq = _pad_cols(w_uq, MLA_HEADS, MLA_QK).astype(BF16)
    q = _nm_matmul(lat, 0, MLA_Q_LORA, g_cq, modv, rows_all, wq, use_mod=False, name="mla_uq", **geo)
    wkv3 = w_ukv.reshape(MLA_KV_LORA, MLA_HEADS, MLA_NOPE + MLA_V)
    wkv = jnp.concatenate([_pad_cols(wkv3[:, :, :MLA_NOPE].reshape(MLA_KV_LORA, -1), MLA_HEADS, MLA_NOPE),
                           _pad_cols(wkv3[:, :, MLA_NOPE:].reshape(MLA_KV_LORA, -1), MLA_HEADS, MLA_V)],
                          axis=1).astype(BF16)
    kv = _nm_matmul(lat, MLA_Q_LORA // MLA_KV_LORA, MLA_KV_LORA, g_ckv, modv, rows_all, wkv,
                    use_mod=False, name="mla_ukv", **geo)
    o = _mla_attention(q, kv, lat, g_q, g_k, n_batch=n_batch, n_lat=n_lat, n_ctx=n_ctx, need_ctx=need_ctx)
    wo = _pad_rows(w_o, MLA_HEADS, MLA_V).astype(BF16)
    return o, wo


def _diff_attn_kernel(q_ref, kl_ref, kc_ref, vl_ref, vc_ref,
                      cq_ref, saq_ref, sbq_ref, ck_ref, sak_ref, sbk_ref, gq_ref, gk_ref,
                      lam_ref, gsub_ref, o_ref, k_scr, v_scr,
                      *, n_lat, n_ctx, nlat_q, scale, lam_init):
    qi = pl.program_id(2)
    nf = DIFF_HD // 4
    tabs = (ck_ref, sak_ref, sbk_ref)
    tile = lambda t: slice(t * LANES, (t + 1) * LANES)
    n_heads = o_ref.shape[1] // LANES

    @pl.when(qi == 0)
    def _():
        for hd in range(n_heads):
            ln = tile(hd)
            for sub in range(2):
                kt = tile(2 * hd + sub)
                _prep_keys(k_scr.at[2 * hd + sub], v_scr.at[hd] if sub == 0 else None, n_lat, n_ctx,
                           lambda r, kt=kt: kl_ref[r, kt], lambda r, kt=kt: kc_ref[r, kt],
                           lambda r, ln=ln: vl_ref[r, ln], lambda r, ln=ln: vc_ref[r, ln],
                           gk_ref[...], tabs, DIFF_HD, nf)

    lp = lam_ref[...]
    lam = (jnp.exp(jnp.sum(lp[0:1] * lp[1:2], axis=-1, keepdims=True))
           - jnp.exp(jnp.sum(lp[2:3] * lp[3:4], axis=-1, keepdims=True)) + lam_init)

    def attend(keys):
        def scores(t, rows):
            q = _head_norm(q_ref[rows, tile(t)], gq_ref[...], DIFF_HD)
            q = _rope(q, cq_ref[rows, :], saq_ref[rows, :], sbq_ref[rows, :], nf) * (scale * LOG2E)
            return _nt_dot(q.astype(BF16), k_scr[t, keys, :])

        def chain(hd, rows):
            s0 = scores(2 * hd, rows)
            s1 = scores(2 * hd + 1, rows)
            yield
            p0 = jnp.exp2(s0 - jnp.max(s0, axis=-1, keepdims=True))
            p1 = jnp.exp2(s1 - jnp.max(s1, axis=-1, keepdims=True))
            l0 = jnp.sum(p0, axis=-1, keepdims=True)
            l1 = jnp.sum(p1, axis=-1, keepdims=True)
            a = (p0 * (1.0 / l0) - p1 * (lam / l1)).astype(BF16)
            yield
            o = jnp.dot(a, v_scr[hd, keys, :], preferred_element_type=F32)
            o = _head_norm(o, gsub_ref[...], 2 * DIFF_HD) * (1.0 - lam_init)
            o_ref[rows, tile(hd)] = o.astype(o_ref.dtype)

        _emit_pipelined([chain(hd, rows) for rows in _row_chains(q_ref.shape[0])
                         for hd in range(n_heads)])

    @pl.when(qi < nlat_q)
    def _():
        attend(slice(0, n_lat + n_ctx))

    @pl.when(qi >= nlat_q)
    def _():
        attend(slice(n_lat, n_lat + n_ctx))


def _diff_attention(qkv, g_q, g_k, lam_p, g_sub, lam_init, *, n_batch, n_lat, n_ctx, need_ctx):
    nlat_q, nctx_q = n_lat // TQ, n_ctx // TQ
    nq = nlat_q + (nctx_q if need_ctx else 0)
    rows = n_batch * (n_lat + (n_ctx if need_ctx else 0))
    nf = DIFF_HD // 4
    tabs = _rope_tables(n_lat, TQ, ((0, True), (2 * nf, False)), nf)
    nh = DIFF_HEADS
    ctx0 = n_batch * n_lat // n_ctx
    gq = jnp.zeros((1, LANES), F32).at[0, :DIFF_HD].set(g_q)
    gk = jnp.zeros((1, LANES), F32).at[0, :DIFF_HD].set(g_k)
    lam8 = jnp.zeros((8, LANES), F32).at[:4, :DIFF_HD].set(lam_p.astype(F32))
    rb = lambda b, qi: _row_block(b, qi, nlat_q, nctx_q, n_batch)
    tq_spec = pl.BlockSpec((TQ, LANES), lambda b, hh, qi: (jnp.minimum(qi, nlat_q), 0))
    tk_spec = pl.BlockSpec((n_lat, LANES), lambda b, hh, qi: (0, 0))
    vec = pl.BlockSpec((1, LANES), lambda b, hh, qi: (0, 0))

    hps = HEADS_PER_STEP
    qk_w, v_w = 2 * hps * LANES, hps * LANES
    k_blk0 = 2 * nh // (2 * hps)
    v_blk0 = 4 * nh // hps
    kern = functools.partial(_diff_attn_kernel, n_lat=n_lat, n_ctx=n_ctx, nlat_q=nlat_q,
                             scale=DIFF_HD ** -0.5, lam_init=lam_init)
    n_all = n_lat + n_ctx
    return pl.pallas_call(
        kern,
        grid=(n_batch, nh // hps, nq),
        in_specs=[pl.BlockSpec((TQ, qk_w), lambda b, hh, qi: (rb(b, qi), hh)),
                  pl.BlockSpec((n_lat, qk_w), lambda b, hh, qi: (b, k_blk0 + hh)),
                  pl.BlockSpec((n_ctx, qk_w), lambda b, hh, qi: (ctx0 + b, k_blk0 + hh)),
                  pl.BlockSpec((n_lat, v_w), lambda b, hh, qi: (b, v_blk0 + hh)),
                  pl.BlockSpec((n_ctx, v_w), lambda b, hh, qi: (ctx0 + b, v_blk0 + hh)),
                  tq_spec, tq_spec, tq_spec, tk_spec, tk_spec, tk_spec, vec, vec,
                  pl.BlockSpec((8, LANES), lambda b, hh, qi: (0, 0)), vec],
        out_specs=pl.BlockSpec((TQ, v_w), lambda b, hh, qi: (rb(b, qi), hh)),
        out_shape=jax.ShapeDtypeStruct((rows, nh * LANES), BF16),
        scratch_shapes=[pltpu.VMEM((2 * hps, n_all, LANES), BF16), pltpu.VMEM((hps, n_all, LANES), BF16)],
        compiler_params=_cparams(("parallel", "parallel", "arbitrary")),
        name="diff_attention",
    )(qkv, qkv, qkv, qkv, qkv, *tabs, *tabs, gq, gk, lam8, g_sub.reshape(1, LANES))


def _diff_mixer(s, modv, g_mix, w_qkv, g_q, g_k, lam_p, g_sub, w_o, lam_init,
                *, n_batch, n_lat, n_ctx, need_ctx):
    d = s.shape[1]
    width = DIFF_HEADS * 2 * DIFF_HD
    rows_all = n_batch * (n_lat + n_ctx)
    w = jnp.concatenate([_pad_cols(w_qkv[:, :width], 2 * DIFF_HEADS, DIFF_HD),
                         _pad_cols(w_qkv[:, width:2 * width], 2 * DIFF_HEADS, DIFF_HD),
                         w_qkv[:, 2 * width:]], axis=1).astype(BF16)
    qkv = _nm_matmul(s, 0, d, g_mix, modv, rows_all, w, n_lat_rows=n_batch * n_lat,
                     n_per_batch=n_lat, name="diff_qkv")
    o = _diff_attention(qkv, g_q, g_k, lam_p, g_sub, lam_init,
                        n_batch=n_batch, n_lat=n_lat, n_ctx=n_ctx, need_ctx=need_ctx)
    return o, w_o.astype(BF16)


def _swa_attn_kernel(q_ref, kl_ref, kc_ref, vl_ref, vc_ref,
                     cq_ref, saq_ref, sbq_ref, ck_ref, sak_ref, sbk_ref, gq_ref, gk_ref, sink_ref,
                     o_ref, k_scr, v_scr, *, n_lat, n_ctx, nlat_q, scale):
    qi = pl.program_id(2)
    nf = SWA_HD // 4
    band = 3 * Q_BLOCK

    @pl.when(qi == 0)
    def _():
        _prep_keys(k_scr, v_scr, n_lat, n_ctx, lambda r: kl_ref[r, :], lambda r: kc_ref[r, :],
                   lambda r: vl_ref[r, :], lambda r: vc_ref[r, :], gk_ref[...],
                   (ck_ref, sak_ref, sbk_ref), SWA_HD, nf)

    row_chains = _row_chains(q_ref.shape[0])
    reach = jnp.where(qi < nlat_q, WINDOW, -1)

    def chain(ci, rows, gi):
        qb = qi * len(row_chains) + ci
        k0 = pl.multiple_of(jnp.clip((qb - 1) * Q_BLOCK, 0, n_lat - band), Q_BLOCK)
        kpos = k0 + lax.broadcasted_iota(jnp.int32, (Q_BLOCK, band), 1)
        qpos = qb * Q_BLOCK + lax.broadcasted_iota(jnp.int32, (Q_BLOCK, band), 0)
        lanes = slice(gi * LANES, (gi + 1) * LANES)
        q = _head_norm(q_ref[rows, lanes], gq_ref[...], SWA_HD)
        q = _rope(q, cq_ref[rows, :], saq_ref[rows, :], sbq_ref[rows, :], nf) * (scale * LOG2E)
        q = q.astype(BF16)
        sb = _nt_dot(q, k_scr[pl.ds(k0, band), :])
        sc = _nt_dot(q, k_scr[n_lat:n_lat + n_ctx, :])
        yield
        sb = jnp.where(jnp.abs(kpos - qpos) <= reach, sb, NEG_INF)
        sink = sink_ref[gi][0:1, 0:1] * LOG2E
        m = jnp.maximum(jnp.maximum(jnp.max(sb, axis=-1, keepdims=True),
                                    jnp.max(sc, axis=-1, keepdims=True)), sink)
        pb = jnp.exp2(sb - m)
        pc = jnp.exp2(sc - m)
        l = (jnp.sum(pb, axis=-1, keepdims=True) + jnp.sum(pc, axis=-1, keepdims=True)
             + jnp.exp2(sink - m))
        yield
        o = (jnp.dot(pb.astype(BF16), v_scr[pl.ds(k0, band), :], preferred_element_type=F32)
             + jnp.dot(pc.astype(BF16), v_scr[n_lat:n_lat + n_ctx, :], preferred_element_type=F32))
        o_ref[rows, lanes] = (o * (1.0 / l)).astype(o_ref.dtype)

    _emit_pipelined([chain(ci, rows, gi) for ci, rows in enumerate(row_chains)
                     for gi in range(SWA_GROUP)])


def _swa_attention(qkv, g_q, g_k, sink, *, n_batch, n_lat, n_ctx, need_ctx):
    nlat_q, nctx_q = n_lat // TQ, n_ctx // TQ
    nq = nlat_q + (nctx_q if need_ctx else 0)
    rows = n_batch * (n_lat + (n_ctx if need_ctx else 0))
    nf = SWA_HD // 4
    tabs = _rope_tables(n_lat, TQ, ((0, True), (2 * nf, False)), nf)
    nh, nkv, grp = SWA_HEADS, SWA_KV_HEADS, SWA_GROUP
    ctx0 = n_batch * n_lat // n_ctx
    gq = jnp.zeros((1, LANES), F32).at[0, :SWA_HD].set(g_q)
    gk = jnp.zeros((1, LANES), F32).at[0, :SWA_HD].set(g_k)
    sink_t = jnp.broadcast_to(sink.astype(F32).reshape(nh, 1, 1), (nh, 8, LANES))
    rb = lambda b, qi: _row_block(b, qi, nlat_q, nctx_q, n_batch)
    tq_spec = pl.BlockSpec((TQ, LANES), lambda b, kh, qi: (jnp.minimum(qi, nlat_q), 0))
    tk_spec = pl.BlockSpec((n_lat, LANES), lambda b, kh, qi: (0, 0))
    vec = pl.BlockSpec((1, LANES), lambda b, kh, qi: (0, 0))
    kern = functools.partial(_swa_attn_kernel, n_lat=n_lat, n_ctx=n_ctx, nlat_q=nlat_q,
                             scale=SWA_HD ** -0.5)
    n_all = n_lat + n_ctx
    return pl.pallas_call(
        kern,
        grid=(n_batch, nkv, nq),
        in_specs=[pl.BlockSpec((TQ, grp * LANES), lambda b, kh, qi: (rb(b, qi), kh)),
                  pl.BlockSpec((n_lat, LANES), lambda b, kh, qi: (b, nh + kh)),
                  pl.BlockSpec((n_ctx, LANES), lambda b, kh, qi: (ctx0 + b, nh + kh)),
                  pl.BlockSpec((n_lat, LANES), lambda b, kh, qi: (b, nh + nkv + kh)),
                  pl.BlockSpec((n_ctx, LANES), lambda b, kh, qi: (ctx0 + b, nh + nkv + kh)),
                  tq_spec, tq_spec, tq_spec, tk_spec, tk_spec, tk_spec, vec, vec,
                  pl.BlockSpec((grp, 8, LANES), lambda b, kh, qi: (kh, 0, 0))],
        out_specs=pl.BlockSpec((TQ, grp * LANES), lambda b, kh, qi: (rb(b, qi), kh)),
        out_shape=jax.ShapeDtypeStruct((rows, nh * LANES), BF16),
        scratch_shapes=[pltpu.VMEM((n_all, LANES), BF16), pltpu.VMEM((n_all, LANES), BF16)],
        compiler_params=_cparams(("parallel", "parallel", "arbitrary")),
        name="swa_attention",
    )(qkv, qkv, qkv, qkv, qkv, *tabs, *tabs, gq, gk, sink_t)


def _swa_mixer(s, modv, g_mix, w_qkv, g_q, g_k, sink, w_o, *, n_batch, n_lat, n_ctx, need_ctx):
    d = s.shape[1]
    qw, kw = SWA_HEADS * SWA_HD, SWA_KV_HEADS * SWA_HD
    rows_all = n_batch * (n_lat + n_ctx)
    w = jnp.concatenate([_pad_cols(w_qkv[:, :qw], SWA_HEADS, SWA_HD),
                         _pad_cols(w_qkv[:, qw:qw + kw], SWA_KV_HEADS, SWA_HD),
                         _pad_cols(w_qkv[:, qw + kw:], SWA_KV_HEADS, SWA_HD)], axis=1).astype(BF16)
    qkv = _nm_matmul(s, 0, d, g_mix, modv, rows_all, w, n_lat_rows=n_batch * n_lat,
                     n_per_batch=n_lat, name="swa_qkv")
    o = _swa_attention(qkv, g_q, g_k, sink, n_batch=n_batch, n_lat=n_lat, n_ctx=n_ctx, need_ctx=need_ctx)
    return o, _pad_rows(w_o, SWA_HEADS, SWA_HD).astype(BF16)


def _merge_exchange_pairs(n):
    pairs = []
    t = max(1, (n - 1).bit_length())
    p = 1 << (t - 1)
    while p > 0:
        q, r, d = 1 << (t - 1), 0, p
        while d > 0:
            pairs += [(i, i + d) for i in range(n - d) if (i & p) == r]
            d, q, r = q - p, q >> 1, p
        p >>= 1
    return pairs


def _top_values(s, count):
    n = s.shape[0] // SUBLANES
    v = [s[j * SUBLANES:(j + 1) * SUBLANES, :] for j in range(n)]
    for i, j in _merge_exchange_pairs(n):
        v[i], v[j] = jnp.maximum(v[i], v[j]), jnp.minimum(v[i], v[j])
    vals = []
    for r in range(count):
        m = jnp.max(v[0], axis=0, keepdims=True)
        vals.append(m)
        depth = min(n, count - 1 - r)
        head = v[0] == m
        for j in range(depth):
            nxt = v[j + 1] if j + 1 < n else jnp.full_like(v[j], NEG_BIG)
            v[j] = jnp.where(head, nxt, v[j])
    return vals


def _peer_score_kernel(s_ref, g_ref, mod_ref, wq_ref, keys_ref,
                       ht_ref, c_ref, e1_ref, s2_ref, e2_ref, qt_scr):
    k = PEER_TOPK
    h = _norm_mod(s_ref[...], g_ref[...], mod_ref[0], 3, 4)
    ht = h.T.astype(BF16)
    ht_ref[...] = ht
    qt_scr[...] = jnp.dot(wq_ref[...], ht, preferred_element_type=F32)

    tm = ht.shape[1]
    group = 2

    def group_body(gidx, carry):
        heads = [gidx * group + i for i in range(group)]
        scores = []
        for hh in heads:
            for half in range(2):
                r = pl.ds(pl.multiple_of((hh * 2 + half) * LANES, LANES), LANES)
                scores.append(jnp.dot(keys_ref[hh, half], qt_scr[r, :].astype(BF16),
                                      preferred_element_type=F32))
        tops = _top_values(jnp.concatenate(scores, axis=1), k + 1)
        cands = []
        for i in range(group):
            v1 = [t[:, (2 * i) * tm:(2 * i + 1) * tm] for t in tops]
            v2 = [t[:, (2 * i + 1) * tm:(2 * i + 2) * tm] for t in tops]
            top8 = jnp.concatenate(v2[:8], axis=0)
            pieces = [v1[0] + jnp.concatenate(v2[:k], axis=0)]
            pieces += [v1[j] + top8 for j in range(1, 8)]
            pieces += [jnp.concatenate(v1[8:k], axis=0) + v2[0]]
            pieces += [jnp.concatenate([v1[0] + v2[k], v1[k] + v2[0]]
                                       + [jnp.full_like(v1[0], NEG_BIG)] * 6, axis=0)]
            cands.append(jnp.concatenate(pieces, axis=0))
        best = _top_values(jnp.concatenate(cands, axis=1), k + 1)
        grp = (PEER_KEYS // SUBLANES, SUBLANES, tm)
        for i, hh in enumerate(heads):
            ms = [t[:, i * tm:(i + 1) * tm] for t in best]
            s1, s2 = scores[2 * i], scores[2 * i + 1]
            m1, m2 = tops[0][:, (2 * i) * tm:(2 * i + 1) * tm], tops[0][:, (2 * i + 1) * tm:(2 * i + 2) * tm]
            thr = 0.5 * (ms[k - 1] + ms[k])
            z = jnp.ones_like(ms[0])
            for r in range(1, k):
                z = z + jnp.exp(ms[r] - ms[0])
            c_ref[hh] = (thr - s1).reshape(grp)
            e1_ref[hh] = (jnp.exp(s1 - m1) * (0.5 / z)).reshape(grp)
            s2_ref[hh] = s2
            e2_ref[hh] = jnp.exp(s2 - m2)
        return carry

    lax.fori_loop(0, PEER_HEADS // group, group_body, 0)


def _peer_scores(s, modv, g_ffn, wq_t, keys, rows, *, n_lat_rows, n_per_batch):
    d = s.shape[1]
    n_batch = modv.shape[0] - 1
    nlat_blocks = n_lat_rows // TM
    bpb = n_per_batch // TM
    nq = wq_t.shape[0]
    hk = (PEER_HEADS, PEER_KEYS, rows)
    hk_spec = pl.BlockSpec((PEER_HEADS, PEER_KEYS, TM), lambda i: (0, 0, i))
    hg = (PEER_HEADS, PEER_KEYS // SUBLANES, SUBLANES, rows)
    hg_spec = pl.BlockSpec((PEER_HEADS, PEER_KEYS // SUBLANES, SUBLANES, TM), lambda i: (0, 0, 0, i))
    return pl.pallas_call(
        _peer_score_kernel,
        grid=(rows // TM,),
        in_specs=[pl.BlockSpec((TM, d), lambda i: (i, 0)),
                  pl.BlockSpec((1, d), lambda i: (0, 0)),
                  pl.BlockSpec((1, 8, d), lambda i: (_mod_index(i, nlat_blocks, bpb, n_batch), 0, 0)),
                  pl.BlockSpec((nq, d), lambda i: (0, 0)),
                  pl.BlockSpec((PEER_HEADS, 2, PEER_KEYS, LANES), lambda i: (0, 0, 0, 0))],
        out_specs=[pl.BlockSpec((d, TM), lambda i: (0, i)), hg_spec, hg_spec, hk_spec, hk_spec],
        out_shape=[jax.ShapeDtypeStruct((d, rows), BF16)] + [jax.ShapeDtypeStruct(hg, F32)] * 2
                  + [jax.ShapeDtypeStruct(hk, F32)] * 2,
        scratch_shapes=[pltpu.VMEM((nq, TM), F32)],
        compiler_params=_cparams(("parallel",)),
        name="peer_scores",
    )(s, g_ffn.reshape(1, d), modv, wq_t, keys)


def _peer_expert_kernel(ht_ref, u_ref, vt_ref, c_ref, e1_ref, s2_ref, e2_ref, s_ref, mod_ref,
                        o_ref, acc_scr, st0_scr, st1_scr, at0_scr, at1_scr, *, n_exp_blocks):
    g = pl.program_id(0)
    st_scr, at_scr = (st0_scr, st1_scr), (at0_scr, at1_scr)
    te, tm = st0_scr.shape
    jc = jnp.maximum(g - 2, 0) % n_exp_blocks

    @pl.when(g == 0)
    def _():
        for ref in st_scr + at_scr:
            ref[...] = jnp.zeros_like(ref)

    @pl.when(jc == 0)
    def _():
        acc_scr[...] = jnp.zeros_like(acc_scr)

    def stages(slot_a, slot_b):
        mxu_chunk = MXU_CHUNK_ROWS

        def value_chunk(q):
            ms = slice(q * mxu_chunk, (q + 1) * mxu_chunk)
            acc_scr[ms, :] += jnp.dot(vt_ref[ms, :], at_scr[slot_a][...], preferred_element_type=F32)

        def score_chunk(q):
            ms = slice(q * mxu_chunk, (q + 1) * mxu_chunk)
            st_scr[slot_a][ms, :] = jnp.dot(u_ref[ms, :], ht_ref[...], preferred_element_type=F32)

        rg_rows = 2 * SUBLANES
        al_group = 4
        col_tiles = 2
        tiles = [(rg, c0, al0)
                 for rg in range(LANES // rg_rows)
                 for c0 in range(0, tm // LANES, col_tiles)
                 for al0 in range(0, te // LANES, al_group)]
        n_chunks = te // mxu_chunk
        assert acc_scr.shape[0] == te
        per_chunk = len(tiles) // (2 * n_chunks)
        for t, (rg, c0, al0) in enumerate(tiles):
            if t % per_chunk == 0:
                q = t // per_chunk
                (value_chunk if q % 2 == 0 else score_chunk)(q // 2)
            brows = slice(rg * rg_rows, (rg + 1) * rg_rows)
            lanes = [slice((c0 + ct) * LANES, (c0 + ct + 1) * LANES) for ct in range(col_tiles)]
            gate = [[jnp.zeros((rg_rows, LANES), F32) for _ in lanes] for _ in range(al_group)]
            for hh in range(PEER_HEADS):
                for ct, ln in enumerate(lanes):
                    s2t = s2_ref[hh, brows, ln]
                    e2t = e2_ref[hh, brows, ln]
                    for k in range(al_group):
                        al = al0 + k
                        c = c_ref[hh, 0, al:al + 1, ln]
                        e1 = e1_ref[hh, 0, al:al + 1, ln]
                        gate[k][ct] = gate[k][ct] + jnp.where(s2t >= c, e2t, 0.0) * e1
            for k in range(al_group):
                erows = slice((al0 + k) * LANES + rg * rg_rows, (al0 + k) * LANES + (rg + 1) * rg_rows)
                for ct, ln in enumerate(lanes):
                    sv = st_scr[slot_b][erows, ln]
                    act = sv * (1.0 + lax.erf(sv * (2.0 ** -0.5)))
                    at_scr[slot_b][erows, ln] = (gate[k][ct] * act).astype(BF16)

    for parity in range(2):
        @pl.when(g % 2 == parity)
        def _():
            stages(parity, 1 - parity)

    @pl.when((jc == n_exp_blocks - 1) & (g >= 2))
    def _():
        o_ref[...] = s_ref[...] + mod_ref[0][5:6] * acc_scr[...].T


def _peer_experts(ht, u, vt, c, e1, s2, e2, s, modv, rows, *, n_lat_rows, n_per_batch):
    d = s.shape[1]
    n_exp = u.shape[0]
    n_batch = modv.shape[0] - 1
    tm, te = PEER_TM, PEER_TE
    nlat_blocks = n_lat_rows // tm
    bpb = n_per_batch // tm
    nj = n_exp // te
    n_pairs = (rows // tm) * nj

    def pair(g, lag):
        p = jnp.clip(g - lag, 0, n_pairs - 1)
        return p // nj, p % nj

    hk_spec = pl.BlockSpec((PEER_HEADS, PEER_KEYS, tm), lambda g: (0, 0, pair(g, 1)[0]))
    hg_spec = pl.BlockSpec((PEER_HEADS, 1, SUBLANES, tm), lambda g: (0, pair(g, 1)[1], 0, pair(g, 1)[0]))
    kern = functools.partial(_peer_expert_kernel, n_exp_blocks=nj)
    return pl.pallas_call(
        kern,
        grid=(n_pairs + 2,),
        in_specs=[pl.BlockSpec((d, tm), lambda g: (0, pair(g, 0)[0])),
                  pl.BlockSpec((te, d), lambda g: (pair(g, 0)[1], 0)),
                  pl.BlockSpec((d, te), lambda g: (0, pair(g, 2)[1])),
                  hg_spec, hg_spec, hk_spec, hk_spec,
                  pl.BlockSpec((tm, d), lambda g: (pair(g, 2)[0], 0)),
                  pl.BlockSpec((1, 8, d),
                               lambda g: (_mod_index(pair(g, 2)[0], nlat_blocks, bpb, n_batch), 0, 0))],
        out_specs=pl.BlockSpec((tm, d), lambda g: (pair(g, 2)[0], 0)),
        out_shape=jax.ShapeDtypeStruct((rows, d), F32),
        scratch_shapes=[pltpu.VMEM((d, tm), F32), pltpu.VMEM((te, tm), F32), pltpu.VMEM((te, tm), F32),
                        pltpu.VMEM((te, tm), BF16), pltpu.VMEM((te, tm), BF16)],
        compiler_params=_cparams(("arbitrary",)),
        name="peer_experts",
    )(ht, u, vt, c, e1, s2, e2, s, modv)


def _peer(s, modv, g_ffn, w_q, keys, u_tab, v_tab, rows, *, n_lat_rows, n_per_batch):
    geo = dict(n_lat_rows=n_lat_rows, n_per_batch=n_per_batch)
    ht, c, e1, s2, e2 = _peer_scores(s, modv, g_ffn, w_q.T.astype(BF16), keys.astype(BF16), rows, **geo)
    return _peer_experts(ht, u_tab.astype(BF16), v_tab.T.astype(BF16), c, e1, s2, e2, s, modv, rows, **geo)


def kernel(x, c, ctx, c_ctx, ada_w, ada_b, norm_mix, norm_ffn, mla_w_down, mla_g_cq, mla_g_ckv, mla_w_uq, mla_w_ukv, mla_g_q, mla_g_k, mla_w_o, diff_w_qkv, diff_g_q, diff_g_k, diff_lambda, diff_g_sub, diff_w_o, swa_w_qkv, swa_g_q, swa_g_k, swa_sink, swa_w_o, peer_w_q, peer_keys, peer_u, peer_v):
    n_batch, n_lat, d = x.shape
    n_ctx = ctx.shape[1]
    depth = ada_w.shape[0]
    rows_lat = n_batch * n_lat
    rows_all = rows_lat + n_batch * n_ctx
    assert n_lat % 512 == 0 and n_ctx % 256 == 0 and (n_batch * n_ctx) % PEER_TM == 0
    assert n_lat % GRID_W == 0 and rows_lat % n_ctx == 0 and n_lat >= 3 * Q_BLOCK

    s = jnp.concatenate([x.reshape(rows_lat, d), ctx.reshape(n_batch * n_ctx, d)], axis=0)
    cc = jnp.zeros((16, d), F32).at[:n_batch].set(c).at[n_batch].set(c_ctx)
    mods = _ada_mods(cc, ada_w, ada_b)
    geo = dict(n_lat_rows=rows_lat, n_per_batch=n_lat)

    for i in range(depth):
        last = i == depth - 1
        kind, j = i % 3, i // 3
        modv = jnp.zeros((n_batch + 1, 8, d), F32).at[:, :6, :].set(
            mods[i, :n_batch + 1].reshape(n_batch + 1, 6, d))
        dims = dict(n_batch=n_batch, n_lat=n_lat, n_ctx=n_ctx, need_ctx=not last)
        if kind == 0:
            p = (mla_w_down[j], mla_g_cq[j], mla_g_ckv[j], mla_w_uq[j], mla_w_ukv[j], mla_g_q[j], mla_g_k[j])
            o, wo = _mla_mixer(s, modv, norm_mix[i], p, mla_w_o[j], **dims)
        elif kind == 1:
            lam_init = 0.8 - 0.6 * math.exp(-0.3 * i)
            o, wo = _diff_mixer(s, modv, norm_mix[i], diff_w_qkv[j], diff_g_q[j], diff_g_k[j],
                                diff_lambda[j], diff_g_sub[j], diff_w_o[j], lam_init, **dims)
        else:
            o, wo = _swa_mixer(s, modv, norm_mix[i], swa_w_qkv[j], swa_g_q[j], swa_g_k[j],
                               swa_sink[j], swa_w_o[j], **dims)
        rows = rows_lat if last else rows_all
        s = _proj_resid(o, wo, s, modv, rows, gate_row=2, **geo)
        s = _peer(s, modv, norm_ffn[i], peer_w_q[i], peer_keys[i], peer_u[i], peer_v[i], rows, **geo)
    return s.reshape(n_batch, n_lat, d)
```

```python
import functools
import math

import jax
import jax.numpy as jnp
from jax import lax
from jax.experimental import pallas as pl
from jax.experimental.pallas import tpu as pltpu

F32 = jnp.float32
BF16 = jnp.bfloat16

LANES = 128
EPS = 1e-6
NEG_INF = -1e30
NEG_BIG = -3.0e38
ROPE_THETA = 10000.0
GRID_W = 64
Q_BLOCK = 128
WINDOW = 128

MLA_HEADS, MLA_NOPE, MLA_ROPE, MLA_V = 16, 64, 32, 64
MLA_QK = MLA_NOPE + MLA_ROPE
MLA_Q_LORA, MLA_KV_LORA = 768, 256
DIFF_HEADS, DIFF_HD = 8, 64
SWA_HEADS, SWA_KV_HEADS, SWA_HD = 16, 4, 64
SWA_GROUP = SWA_HEADS // SWA_KV_HEADS
PEER_HEADS, PEER_KEYS, PEER_TOPK = 8, 128, 16
PEER_EXPERTS = PEER_KEYS * PEER_KEYS

VMEM_LIMIT = 56 * 1024 * 1024

TM = 512
SCORE_TM = 256
TQ = 256
PEER_TM = 512
SUBLANES = 8
PEER_TE = SUBLANES * LANES
MXU_CHUNK_ROWS = 128


def _cparams(sem):
    return pltpu.CompilerParams(dimension_semantics=sem, vmem_limit_bytes=VMEM_LIMIT)


def _ada_kernel(c_ref, w_ref, b_ref, o_ref):
    c = c_ref[...]
    sc = c * (1.0 / (1.0 + jnp.exp(-c)))
    o_ref[0] = jnp.dot(sc, w_ref[0], preferred_element_type=F32,
                       precision=lax.Precision.HIGHEST) + b_ref[0]


def _ada_mods(cc, ada_w, ada_b):
    depth, d, d6 = ada_w.shape
    rows = cc.shape[0]
    tn = 1536
    return pl.pallas_call(
        _ada_kernel,
        grid=(depth, d6 // tn),
        in_specs=[pl.BlockSpec((rows, d), lambda i, n: (0, 0)),
                  pl.BlockSpec((1, d, tn), lambda i, n: (i, 0, n)),
                  pl.BlockSpec((1, 1, tn), lambda i, n: (i, 0, n))],
        out_specs=pl.BlockSpec((1, rows, tn), lambda i, n: (i, 0, n)),
        out_shape=jax.ShapeDtypeStruct((depth, rows, d6), F32),
        compiler_params=_cparams(("arbitrary", "arbitrary")),
        name="ada_mods",
    )(cc, ada_w, ada_b.reshape(depth, 1, d6))


def _norm_mod(x, g, mod, shift_row, scale_row):
    ms = jnp.mean(x * x, axis=-1, keepdims=True)
    y = x * lax.rsqrt(ms + EPS) * g
    if mod is not None:
        y = y * (1.0 + mod[scale_row:scale_row + 1]) + mod[shift_row:shift_row + 1]
    return y


def _nm_kernel(x_ref, g_ref, mod_ref, w_ref, o_ref, h_scr, *, use_mod, shift_row, scale_row):
    @pl.when(pl.program_id(1) == 0)
    def _():
        mod = mod_ref[0] if use_mod else None
        h_scr[...] = _norm_mod(x_ref[...], g_ref[...], mod, shift_row, scale_row).astype(BF16)

    o_ref[...] = jnp.dot(h_scr[...], w_ref[...], preferred_element_type=F32).astype(o_ref.dtype)


def _mod_index(i, nlat_blocks, blocks_per_batch, n_batch):
    return jnp.where(i < nlat_blocks, i // blocks_per_batch, n_batch)


def _nm_matmul(x, kblk, k, g, modv, rows, w, *, n_lat_rows, n_per_batch, out_dtype=F32,
               use_mod=True, shift_row=0, scale_row=1, name="nm_matmul"):
    nout = w.shape[1]
    tn = nout
    for cand in (2048, 1536, 1280, 1024):
        if nout > 2048 and nout % cand == 0:
            tn = cand
            break
    n_batch = modv.shape[0] - 1
    nlat_blocks = n_lat_rows // TM
    bpb = n_per_batch // TM
    kern = functools.partial(_nm_kernel, use_mod=use_mod, shift_row=shift_row, scale_row=scale_row)
    return pl.pallas_call(
        kern,
        grid=(rows // TM, nout // tn),
        in_specs=[pl.BlockSpec((TM, k), lambda i, n: (i, kblk)),
                  pl.BlockSpec((1, k), lambda i, n: (0, 0)),
                  pl.BlockSpec((1, 8, k), lambda i, n: (_mod_index(i, nlat_blocks, bpb, n_batch), 0, 0)),
                  pl.BlockSpec((k, tn), lambda i, n: (0, n))],
        out_specs=pl.BlockSpec((TM, tn), lambda i, n: (i, n)),
        out_shape=jax.ShapeDtypeStruct((rows, nout), out_dtype),
        scratch_shapes=[pltpu.VMEM((TM, k), BF16)],
        compiler_params=_cparams(("parallel", "arbitrary")),
        name=name,
    )(x, g.reshape(1, k), modv if modv.shape[-1] == k else jnp.zeros((n_batch + 1, 8, k), F32), w)


def _proj_resid_kernel(a_ref, w_ref, s_ref, mod_ref, o_ref, *, gate_row):
    y = jnp.dot(a_ref[...].astype(BF16), w_ref[...], preferred_element_type=F32)
    o_ref[...] = s_ref[...] + mod_ref[0][gate_row:gate_row + 1] * y


def _proj_resid(a, w, s, modv, rows, *, n_lat_rows, n_per_batch, gate_row, name="proj_resid"):
    ka = a.shape[1]
    d = w.shape[1]
    n_batch = modv.shape[0] - 1
    nlat_blocks = n_lat_rows // TM
    bpb = n_per_batch // TM
    return pl.pallas_call(
        functools.partial(_proj_resid_kernel, gate_row=gate_row),
        grid=(rows // TM,),
        in_specs=[pl.BlockSpec((TM, ka), lambda i: (i, 0)),
                  pl.BlockSpec((ka, d), lambda i: (0, 0)),
                  pl.BlockSpec((TM, d), lambda i: (i, 0)),
                  pl.BlockSpec((1, 8, d), lambda i: (_mod_index(i, nlat_blocks, bpb, n_batch), 0, 0))],
        out_specs=pl.BlockSpec((TM, d), lambda i: (i, 0)),
        out_shape=jax.ShapeDtypeStruct((rows, d), F32),
        compiler_params=_cparams(("parallel",)),
        name=name,
    )(a, w, s, modv)


def _rope_tables(n_lat, extra_rows, segments, nf):
    pos = jnp.arange(n_lat, dtype=jnp.int32)
    row = (pos // GRID_W).astype(F32)
    col = (pos % GRID_W).astype(F32)
    inv = ROPE_THETA ** (-jnp.arange(nf, dtype=F32) / nf)
    cos = jnp.ones((n_lat, LANES), F32)
    sa = jnp.zeros((n_lat, LANES), F32)
    sb = jnp.zeros((n_lat, LANES), F32)
    for first, use_row in segments:
        ang = (row if use_row else col)[:, None] * inv[None, :]
        c, s = jnp.cos(ang), jnp.sin(ang)
        cos = cos.at[:, first:first + nf].set(c).at[:, first + nf:first + 2 * nf].set(c)
        sa = sa.at[:, first:first + nf].set(-s)
        sb = sb.at[:, first + nf:first + 2 * nf].set(s)
    pad = lambda t, v: jnp.concatenate([t, jnp.full((extra_rows, LANES), v, F32)], axis=0)
    return pad(cos, 1.0), pad(sa, 0.0), pad(sb, 0.0)


def _head_norm(x, g, d):
    ssq = jnp.sum(x * x, axis=-1, keepdims=True) * (1.0 / d)
    return x * lax.rsqrt(ssq + EPS) * g


def _rope(y, cos, sa, sb, nf):
    return y * cos + pltpu.roll(y, LANES - nf, 1) * sa + pltpu.roll(y, nf, 1) * sb


def _row_block(b, qi, nlat_q, nctx_q, n_batch):
    return jnp.where(qi < nlat_q, b * nlat_q + qi, n_batch * nlat_q + b * nctx_q + (qi - nlat_q))


def _prep_keys(k_scr, v_scr, n_lat, n_ctx, load_k_lat, load_k_ctx, load_v_lat, load_v_ctx,
               gk, tabs, d, nf):
    cos_ref, sa_ref, sb_ref = tabs
    ch = 256

    def lat_body(c, carry):
        r = pl.ds(pl.multiple_of(c * ch, ch), ch)
        kk = _head_norm(load_k_lat(r).astype(F32), gk, d)
        kk = _rope(kk, cos_ref[r, :], sa_ref[r, :], sb_ref[r, :], nf)
        k_scr[r, :] = kk.astype(BF16)
        if v_scr is not None:
            v_scr[r, :] = load_v_lat(r).astype(BF16)
        return carry

    lax.fori_loop(0, n_lat // ch, lat_body, 0)

    def ctx_body(c, carry):
        r = pl.ds(pl.multiple_of(c * ch, ch), ch)
        ro = pl.ds(pl.multiple_of(n_lat + c * ch, ch), ch)
        k_scr[ro, :] = _head_norm(load_k_ctx(r).astype(F32), gk, d).astype(BF16)
        if v_scr is not None:
            v_scr[ro, :] = load_v_ctx(r).astype(BF16)
        return carry

    lax.fori_loop(0, n_ctx // ch, ctx_body, 0)


def _nt_dot(a, b):
    return lax.dot_general(a, b, (((1,), (1,)), ((), ())), preferred_element_type=F32)


LOG2E = math.log2(math.e)
CHAIN_ROWS = 128
HEADS_PER_STEP = 2


def _row_chains(n_rows):
    return [slice(r, r + CHAIN_ROWS) for r in range(0, n_rows, CHAIN_ROWS)]


def _emit_pipelined(chains):
    pending, active = list(chains), []
    while pending or active:
        if pending:
            active.append(pending.pop(0))
        for ch in list(active):
            try:
                next(ch)
            except StopIteration:
                active.remove(ch)


def _exp2_scores(q, k):
    s = _nt_dot(q, k)
    p = jnp.exp2(s - jnp.max(s, axis=-1, keepdims=True))
    return p, jnp.sum(p, axis=-1, keepdims=True)


def _mla_attn_kernel(q_ref, kl_ref, kc_ref, krl_ref, krc_ref, vl_ref, vc_ref,
                     cq_ref, saq_ref, sbq_ref, ck_ref, sak_ref, sbk_ref, gq_ref, gk_ref,
                     o_ref, k_scr, v_scr, *, n_lat, n_ctx, nlat_q, scale):
    qi = pl.program_id(2)
    nf = MLA_ROPE // 4
    heads = [slice(hd * LANES, (hd + 1) * LANES) for hd in range(q_ref.shape[1] // LANES)]

    @pl.when(qi == 0)
    def _():
        for hd, ln in enumerate(heads):
            _prep_keys(k_scr.at[hd], v_scr.at[hd], n_lat, n_ctx,
                       lambda r, ln=ln: kl_ref[r, ln].astype(F32) + krl_ref[r, :],
                       lambda r, ln=ln: kc_ref[r, ln].astype(F32) + krc_ref[r, :],
                       lambda r, ln=ln: vl_ref[r, ln], lambda r, ln=ln: vc_ref[r, ln],
                       gk_ref[...], (ck_ref, sak_ref, sbk_ref), MLA_QK, nf)

    def attend(keys):
        def chain(hd, ln, rows):
            q = _head_norm(q_ref[rows, ln].astype(F32), gq_ref[...], MLA_QK)
            q = _rope(q, cq_ref[rows, :], saq_ref[rows, :], sbq_ref[rows, :], nf) * (scale * LOG2E)
            s = _nt_dot(q.astype(BF16), k_scr[hd, keys, :])
            yield
            p = jnp.exp2(s - jnp.max(s, axis=-1, keepdims=True))
            l = jnp.sum(p, axis=-1, keepdims=True)
            yield
            o = jnp.dot(p.astype(BF16), v_scr[hd, keys, :], preferred_element_type=F32)
            o_ref[rows, ln] = (o * (1.0 / l)).astype(o_ref.dtype)

        _emit_pipelined([chain(hd, ln, rows) for rows in _row_chains(q_ref.shape[0])
                         for hd, ln in enumerate(heads)])

    @pl.when(qi < nlat_q)
    def _():
        attend(slice(0, n_lat + n_ctx))

    @pl.when(qi >= nlat_q)
    def _():
        attend(slice(n_lat, n_lat + n_ctx))


def _mla_attention(q, kv, lat, g_q, g_k, *, n_batch, n_lat, n_ctx, need_ctx):
    nlat_q, nctx_q = n_lat // TQ, n_ctx // TQ
    nq = nlat_q + (nctx_q if need_ctx else 0)
    rows = n_batch * (n_lat + (n_ctx if need_ctx else 0))
    nf = MLA_ROPE // 4
    tabs = _rope_tables(n_lat, TQ, ((MLA_NOPE, True), (MLA_NOPE + 2 * nf, False)), nf)
    h = MLA_HEADS
    ctx0 = n_batch * n_lat // n_ctx
    kr_blk = (MLA_Q_LORA + MLA_KV_LORA) // LANES
    gq = jnp.zeros((1, LANES), F32).at[0, :MLA_QK].set(g_q)
    gk = jnp.zeros((1, LANES), F32).at[0, :MLA_QK].set(g_k)
    hps = HEADS_PER_STEP
    hw = hps * LANES
    qspec = pl.BlockSpec((TQ, hw), lambda b, hh, qi: (_row_block(b, qi, nlat_q, nctx_q, n_batch), hh))
    tq_spec = pl.BlockSpec((TQ, LANES), lambda b, hh, qi: (jnp.minimum(qi, nlat_q), 0))
    tk_spec = pl.BlockSpec((n_lat, LANES), lambda b, hh, qi: (0, 0))
    vec = pl.BlockSpec((1, LANES), lambda b, hh, qi: (0, 0))
    kern = functools.partial(_mla_attn_kernel, n_lat=n_lat, n_ctx=n_ctx, nlat_q=nlat_q,
                             scale=MLA_QK ** -0.5)
    return pl.pallas_call(
        kern,
        grid=(n_batch, h // hps, nq),
        in_specs=[qspec,
                  pl.BlockSpec((n_lat, hw), lambda b, hh, qi: (b, hh)),
                  pl.BlockSpec((n_ctx, hw), lambda b, hh, qi: (ctx0 + b, hh)),
                  pl.BlockSpec((n_lat, LANES), lambda b, hh, qi: (b, kr_blk)),
                  pl.BlockSpec((n_ctx, LANES), lambda b, hh, qi: (ctx0 + b, kr_blk)),
                  pl.BlockSpec((n_lat, hw), lambda b, hh, qi: (b, h // hps + hh)),
                  pl.BlockSpec((n_ctx, hw), lambda b, hh, qi: (ctx0 + b, h // hps + hh)),
                  tq_spec, tq_spec, tq_spec, tk_spec, tk_spec, tk_spec, vec, vec],
        out_specs=qspec,
        out_shape=jax.ShapeDtypeStruct((rows, h * LANES), BF16),
        scratch_shapes=[pltpu.VMEM((hps, n_lat + n_ctx, LANES), BF16),
                        pltpu.VMEM((hps, n_lat + n_ctx, LANES), BF16)],
        compiler_params=_cparams(("parallel", "parallel", "arbitrary")),
        name="mla_attention",
    )(q, kv, kv, lat, lat, kv, kv, *tabs, *tabs, gq, gk)


def _pad_cols(w, groups, width, first=0):
    k = w.shape[0]
    w3 = w.reshape(k, groups, width)
    out = jnp.zeros((k, groups, LANES), w.dtype).at[:, :, first:first + width].set(w3)
    return out.reshape(k, groups * LANES)


def _pad_rows(w, groups, width):
    d = w.shape[1]
    w3 = w.reshape(groups, width, d)
    return jnp.zeros((groups, LANES, d), w.dtype).at[:, :width, :].set(w3).reshape(groups * LANES, d)


def _mla_mixer(s, modv, g_mix, p, w_o, *, n_batch, n_lat, n_ctx, need_ctx):
    w_down, g_cq, g_ckv, w_uq, w_ukv, g_q, g_k = p
    d = s.shape[1]
    rows_all = n_batch * (n_lat + n_ctx)
    geo = dict(n_lat_rows=n_batch * n_lat, n_per_batch=n_lat)
    wd = jnp.concatenate([w_down[:, :MLA_Q_LORA + MLA_KV_LORA],
                          _pad_cols(w_down[:, MLA_Q_LORA + MLA_KV_LORA:], 1, MLA_ROPE, MLA_NOPE)], axis=1)
    lat = _nm_matmul(s, 0, d, g_mix, modv, rows_all, wd.astype(BF16), name="mla_down", **geo)
    wq = _pad_cols(w_uq, MLA_HEADS, MLA_QK).astype(BF16)
    q = _nm_matmul(lat, 0, MLA_Q_LORA, g_cq, modv, rows_all, wq, use_mod=False, name="mla_uq",
                   out_dtype=BF16, **geo)
    wkv3 = w_ukv.reshape(MLA_KV_LORA, MLA_HEADS, MLA_NOPE + MLA_V)
    wkv = jnp.concatenate([_pad_cols(wkv3[:, :, :MLA_NOPE].reshape(MLA_KV_LORA, -1), MLA_HEADS, MLA_NOPE),
                           _pad_cols(wkv3[:, :, MLA_NOPE:].reshape(MLA_KV_LORA, -1), MLA_HEADS, MLA_V)],
                          axis=1).astype(BF16)
    kv = _nm_matmul(lat, MLA_Q_LORA // MLA_KV_LORA, MLA_KV_LORA, g_ckv, modv, rows_all, wkv,
                    use_mod=False, name="mla_ukv", out_dtype=BF16, **geo)
    o = _mla_attention(q, kv, lat, g_q, g_k, n_batch=n_batch, n_lat=n_lat, n_ctx=n_ctx, need_ctx=need_ctx)
    wo = _pad_rows(w_o, MLA_HEADS, MLA_V).astype(BF16)
    return o, wo


def _diff_attn_kernel(q_ref, kl_ref, kc_ref, vl_ref, vc_ref,
                      cq_ref, saq_ref, sbq_ref, ck_ref, sak_ref, sbk_ref, gq_ref, gk_ref,
                      lam_ref, gsub_ref, o_ref, k_scr, v_scr,
                      *, n_lat, n_ctx, nlat_q, scale, lam_init):
    qi = pl.program_id(2)
    nf = DIFF_HD // 4
    tabs = (ck_ref, sak_ref, sbk_ref)
    tile = lambda t: slice(t * LANES, (t + 1) * LANES)
    n_heads = o_ref.shape[1] // LANES

    @pl.when(qi == 0)
    def _():
        for hd in range(n_heads):
            ln = tile(hd)
            for sub in range(2):
                kt = tile(2 * hd + sub)
                _prep_keys(k_scr.at[2 * hd + sub], v_scr.at[hd] if sub == 0 else None, n_lat, n_ctx,
                           lambda r, kt=kt: kl_ref[r, kt], lambda r, kt=kt: kc_ref[r, kt],
                           lambda r, ln=ln: vl_ref[r, ln], lambda r, ln=ln: vc_ref[r, ln],
                           gk_ref[...], tabs, DIFF_HD, nf)

    lp = lam_ref[...]
    lam = (jnp.exp(jnp.sum(lp[0:1] * lp[1:2], axis=-1, keepdims=True))
           - jnp.exp(jnp.sum(lp[2:3] * lp[3:4], axis=-1, keepdims=True)) + lam_init)

    def attend(keys):
        def scores(t, rows):
            q = _head_norm(q_ref[rows, tile(t)].astype(F32), gq_ref[...], DIFF_HD)
            q = _rope(q, cq_ref[rows, :], saq_ref[rows, :], sbq_ref[rows, :], nf) * (scale * LOG2E)
            return _nt_dot(q.astype(BF16), k_scr[t, keys, :])

        def chain(hd, rows):
            s0 = scores(2 * hd, rows)
            s1 = scores(2 * hd + 1, rows)
            yield
            p0 = jnp.exp2(s0 - jnp.max(s0, axis=-1, keepdims=True))
            p1 = jnp.exp2(s1 - jnp.max(s1, axis=-1, keepdims=True))
            l0 = jnp.sum(p0, axis=-1, keepdims=True)
            l1 = jnp.sum(p1, axis=-1, keepdims=True)
            a = (p0 * (1.0 / l0) - p1 * (lam / l1)).astype(BF16)
            yield
            o = jnp.dot(a, v_scr[hd, keys, :], preferred_element_type=F32)
            o = _head_norm(o, gsub_ref[...], 2 * DIFF_HD) * (1.0 - lam_init)
            o_ref[rows, tile(hd)] = o.astype(o_ref.dtype)

        _emit_pipelined([chain(hd, rows) for rows in _row_chains(q_ref.shape[0])
                         for hd in range(n_heads)])

    @pl.when(qi < nlat_q)
    def _():
        attend(slice(0, n_lat + n_ctx))

    @pl.when(qi >= nlat_q)
    def _():
        attend(slice(n_lat, n_lat + n_ctx))


def _diff_attention(qkv, g_q, g_k, lam_p, g_sub, lam_init, *, n_batch, n_lat, n_ctx, need_ctx):
    nlat_q, nctx_q = n_lat // TQ, n_ctx // TQ
    nq = nlat_q + (nctx_q if need_ctx else 0)
    rows = n_batch * (n_lat + (n_ctx if need_ctx else 0))
    nf = DIFF_HD // 4
    tabs = _rope_tables(n_lat, TQ, ((0, True), (2 * nf, False)), nf)
    nh = DIFF_HEADS
    ctx0 = n_batch * n_lat // n_ctx
    gq = jnp.zeros((1, LANES), F32).at[0, :DIFF_HD].set(g_q)
    gk = jnp.zeros((1, LANES), F32).at[0, :DIFF_HD].set(g_k)
    lam8 = jnp.zeros((8, LANES), F32).at[:4, :DIFF_HD].set(lam_p.astype(F32))
    rb = lambda b, qi: _row_block(b, qi, nlat_q, nctx_q, n_batch)
    tq_spec = pl.BlockSpec((TQ, LANES), lambda b, hh, qi: (jnp.minimum(qi, nlat_q), 0))
    tk_spec = pl.BlockSpec((n_lat, LANES), lambda b, hh, qi: (0, 0))
    vec = pl.BlockSpec((1, LANES), lambda b, hh, qi: (0, 0))
    hps = HEADS_PER_STEP
    qk_w, v_w = 2 * hps * LANES, hps * LANES
    k_blk0 = 2 * nh // (2 * hps)
    v_blk0 = 4 * nh // hps
    kern = functools.partial(_diff_attn_kernel, n_lat=n_lat, n_ctx=n_ctx, nlat_q=nlat_q,
                             scale=DIFF_HD ** -0.5, lam_init=lam_init)
    n_all = n_lat + n_ctx
    return pl.pallas_call(
        kern,
        grid=(n_batch, nh // hps, nq),
        in_specs=[pl.BlockSpec((TQ, qk_w), lambda b, hh, qi: (rb(b, qi), hh)),
                  pl.BlockSpec((n_lat, qk_w), lambda b, hh, qi: (b, k_blk0 + hh)),
                  pl.BlockSpec((n_ctx, qk_w), lambda b, hh, qi: (ctx0 + b, k_blk0 + hh)),
                  pl.BlockSpec((n_lat, v_w), lambda b, hh, qi: (b, v_blk0 + hh)),
                  pl.BlockSpec((n_ctx, v_w), lambda b, hh, qi: (ctx0 + b, v_blk0 + hh)),
                  tq_spec, tq_spec, tq_spec, tk_spec, tk_spec, tk_spec, vec, vec,
                  pl.BlockSpec((8, LANES), lambda b, hh, qi: (0, 0)), vec],
        out_specs=pl.BlockSpec((TQ, v_w), lambda b, hh, qi: (rb(b, qi), hh)),
        out_shape=jax.ShapeDtypeStruct((rows, nh * LANES), BF16),
        scratch_shapes=[pltpu.VMEM((2 * hps, n_all, LANES), BF16), pltpu.VMEM((hps, n_all, LANES), BF16)],
        compiler_params=_cparams(("parallel", "parallel", "arbitrary")),
        name="diff_attention",
    )(qkv, qkv, qkv, qkv, qkv, *tabs, *tabs, gq, gk, lam8, g_sub.reshape(1, LANES))


def _diff_mixer(s, modv, g_mix, w_qkv, g_q, g_k, lam_p, g_sub, w_o, lam_init,
                *, n_batch, n_lat, n_ctx, need_ctx):
    d = s.shape[1]
    width = DIFF_HEADS * 2 * DIFF_HD
    rows_all = n_batch * (n_lat + n_ctx)
    w = jnp.concatenate([_pad_cols(w_qkv[:, :width], 2 * DIFF_HEADS, DIFF_HD),
                         _pad_cols(w_qkv[:, width:2 * width], 2 * DIFF_HEADS, DIFF_HD),
                         w_qkv[:, 2 * width:]], axis=1).astype(BF16)
    qkv = _nm_matmul(s, 0, d, g_mix, modv, rows_all, w, n_lat_rows=n_batch * n_lat,
                     n_per_batch=n_lat, name="diff_qkv", out_dtype=BF16)
    o = _diff_attention(qkv, g_q, g_k, lam_p, g_sub, lam_init,
                        n_batch=n_batch, n_lat=n_lat, n_ctx=n_ctx, need_ctx=need_ctx)
    return o, w_o.astype(BF16)


def _swa_attn_kernel(q_ref, kl_ref, kc_ref, vl_ref, vc_ref,
                     cq_ref, saq_ref, sbq_ref, ck_ref, sak_ref, sbk_ref, gq_ref, gk_ref, sink_ref,
                     o_ref, k_scr, v_scr, *, n_lat, n_ctx, nlat_q, scale):
    qi = pl.program_id(2)
    nf = SWA_HD // 4
    band = 3 * Q_BLOCK

    @pl.when(qi == 0)
    def _():
        _prep_keys(k_scr, v_scr, n_lat, n_ctx, lambda r: kl_ref[r, :], lambda r: kc_ref[r, :],
                   lambda r: vl_ref[r, :], lambda r: vc_ref[r, :], gk_ref[...],
                   (ck_ref, sak_ref, sbk_ref), SWA_HD, nf)

    row_chains = _row_chains(q_ref.shape[0])
    reach = jnp.where(qi < nlat_q, WINDOW, -1)

    def chain(ci, rows, gi):
        qb = qi * len(row_chains) + ci
        k0 = pl.multiple_of(jnp.clip((qb - 1) * Q_BLOCK, 0, n_lat - band), Q_BLOCK)
        kpos = k0 + lax.broadcasted_iota(jnp.int32, (Q_BLOCK, band), 1)
        qpos = qb * Q_BLOCK + lax.broadcasted_iota(jnp.int32, (Q_BLOCK, band), 0)
        lanes = slice(gi * LANES, (gi + 1) * LANES)
        q = _head_norm(q_ref[rows, lanes].astype(F32), gq_ref[...], SWA_HD)
        q = _rope(q, cq_ref[rows, :], saq_ref[rows, :], sbq_ref[rows, :], nf) * (scale * LOG2E)
        q = q.astype(BF16)
        sb = _nt_dot(q, k_scr[pl.ds(k0, band), :])
        sc = _nt_dot(q, k_scr[n_lat:n_lat + n_ctx, :])
        yield
        sb = jnp.where(jnp.abs(kpos - qpos) <= reach, sb, NEG_INF)
        sink = sink_ref[gi][0:1, 0:1] * LOG2E
        m = jnp.maximum(jnp.maximum(jnp.max(sb, axis=-1, keepdims=True),
                                    jnp.max(sc, axis=-1, keepdims=True)), sink)
        pb = jnp.exp2(sb - m)
        pc = jnp.exp2(sc - m)
        l = (jnp.sum(pb, axis=-1, keepdims=True) + jnp.sum(pc, axis=-1, keepdims=True)
             + jnp.exp2(sink - m))
        yield
        o = (jnp.dot(pb.astype(BF16), v_scr[pl.ds(k0, band), :], preferred_element_type=F32)
             + jnp.dot(pc.astype(BF16), v_scr[n_lat:n_lat + n_ctx, :], preferred_element_type=F32))
        o_ref[rows, lanes] = (o * (1.0 / l)).astype(o_ref.dtype)

    _emit_pipelined([chain(ci, rows, gi) for ci, rows in enumerate(row_chains)
                     for gi in range(SWA_GROUP)])


def _swa_attention(qkv, g_q, g_k, sink, *, n_batch, n_lat, n_ctx, need_ctx):
    nlat_q, nctx_q = n_lat // TQ, n_ctx // TQ
    nq = nlat_q + (nctx_q if need_ctx else 0)
    rows = n_batch * (n_lat + (n_ctx if need_ctx else 0))
    nf = SWA_HD // 4
    tabs = _rope_tables(n_lat, TQ, ((0, True), (2 * nf, False)), nf)
    nh, nkv, grp = SWA_HEADS, SWA_KV_HEADS, SWA_GROUP
    ctx0 = n_batch * n_lat // n_ctx
    gq = jnp.zeros((1, LANES), F32).at[0, :SWA_HD].set(g_q)
    gk = jnp.zeros((1, LANES), F32).at[0, :SWA_HD].set(g_k)
    sink_t = jnp.broadcast_to(sink.astype(F32).reshape(nh, 1, 1), (nh, 8, LANES))
    rb = lambda b, qi: _row_block(b, qi, nlat_q, nctx_q, n_batch)
    tq_spec = pl.BlockSpec((TQ, LANES), lambda b, kh, qi: (jnp.minimum(qi, nlat_q), 0))
    tk_spec = pl.BlockSpec((n_lat, LANES), lambda b, kh, qi: (0, 0))
    vec = pl.BlockSpec((1, LANES), lambda b, kh, qi: (0, 0))
    kern = functools.partial(_swa_attn_kernel, n_lat=n_lat, n_ctx=n_ctx, nlat_q=nlat_q,
                             scale=SWA_HD ** -0.5)
    n_all = n_lat + n_ctx
    return pl.pallas_call(
        kern,
        grid=(n_batch, nkv, nq),
        in_specs=[pl.BlockSpec((TQ, grp * LANES), lambda b, kh, qi: (rb(b, qi), kh)),
                  pl.BlockSpec((n_lat, LANES), lambda b, kh, qi: (b, nh + kh)),
                  pl.BlockSpec((n_ctx, LANES), lambda b, kh, qi: (ctx0 + b, nh + kh)),
                  pl.BlockSpec((n_lat, LANES), lambda b, kh, qi: (b, nh + nkv + kh)),
                  pl.BlockSpec((n_ctx, LANES), lambda b, kh, qi: (ctx0 + b, nh + nkv + kh)),
                  tq_spec, tq_spec, tq_spec, tk_spec, tk_spec, tk_spec, vec, vec,
                  pl.BlockSpec((grp, 8, LANES), lambda b, kh, qi: (kh, 0, 0))],
        out_specs=pl.BlockSpec((TQ, grp * LANES), lambda b, kh, qi: (rb(b, qi), kh)),
        out_shape=jax.ShapeDtypeStruct((rows, nh * LANES), BF16),
        scratch_shapes=[pltpu.VMEM((n_all, LANES), BF16), pltpu.VMEM((n_all, LANES), BF16)],
        compiler_params=_cparams(("parallel", "parallel", "arbitrary")),
        name="swa_attention",
    )(qkv, qkv, qkv, qkv, qkv, *tabs, *tabs, gq, gk, sink_t)


def _swa_mixer(s, modv, g_mix, w_qkv, g_q, g_k, sink, w_o, *, n_batch, n_lat, n_ctx, need_ctx):
    d = s.shape[1]
    qw, kw = SWA_HEADS * SWA_HD, SWA_KV_HEADS * SWA_HD
    rows_all = n_batch * (n_lat + n_ctx)
    w = jnp.concatenate([_pad_cols(w_qkv[:, :qw], SWA_HEADS, SWA_HD),
                         _pad_cols(w_qkv[:, qw:qw + kw], SWA_KV_HEADS, SWA_HD),
                         _pad_cols(w_qkv[:, qw + kw:], SWA_KV_HEADS, SWA_HD)], axis=1).astype(BF16)
    qkv = _nm_matmul(s, 0, d, g_mix, modv, rows_all, w, n_lat_rows=n_batch * n_lat,
                     n_per_batch=n_lat, name="swa_qkv", out_dtype=BF16)
    o = _swa_attention(qkv, g_q, g_k, sink, n_batch=n_batch, n_lat=n_lat, n_ctx=n_ctx, need_ctx=need_ctx)
    return o, _pad_rows(w_o, SWA_HEADS, SWA_HD).astype(BF16)


def _merge_exchange_pairs(n):
    pairs = []
    t = max(1, (n - 1).bit_length())
    p = 1 << (t - 1)
    while p > 0:
        q, r, d = 1 << (t - 1), 0, p
        while d > 0:
            pairs += [(i, i + d) for i in range(n - d) if (i & p) == r]
            d, q, r = q - p, q >> 1, p
        p >>= 1
    return pairs


def _top_values(s, count):
    n = s.shape[0] // SUBLANES
    v = [s[j * SUBLANES:(j + 1) * SUBLANES, :] for j in range(n)]
    for i, j in _merge_exchange_pairs(n):
        v[i], v[j] = jnp.maximum(v[i], v[j]), jnp.minimum(v[i], v[j])
    vals = []
    for r in range(count):
        m = jnp.max(v[0], axis=0, keepdims=True)
        vals.append(m)
        depth = min(n, count - 1 - r)
        head = v[0] == m
        for j in range(depth):
            nxt = v[j + 1] if j + 1 < n else jnp.full_like(v[j], NEG_BIG)
            v[j] = jnp.where(head, nxt, v[j])
    return vals


def _peer_score_kernel(s_ref, g_ref, mod_ref, wq_ref, keys_ref,
                       ht_ref, c_ref, e1_ref, s2_ref, e2_ref, qt_scr):
    k = PEER_TOPK
    h = _norm_mod(s_ref[...], g_ref[...], mod_ref[0], 3, 4)
    ht = h.T.astype(BF16)
    ht_ref[...] = ht
    qt_scr[...] = jnp.dot(wq_ref[...], ht, preferred_element_type=F32)

    tm = ht.shape[1]
    group = 2

    def group_body(gidx, carry):
        heads = [gidx * group + i for i in range(group)]
        scores = []
        for hh in heads:
            for half in range(2):
                r = pl.ds(pl.multiple_of((hh * 2 + half) * LANES, LANES), LANES)
                scores.append(jnp.dot(keys_ref[hh, half], qt_scr[r, :].astype(BF16),
                                      preferred_element_type=F32))
        tops = _top_values(jnp.concatenate(scores, axis=1), k + 1)
        cands = []
        for i in range(group):
            v1 = [t[:, (2 * i) * tm:(2 * i + 1) * tm] for t in tops]
            v2 = [t[:, (2 * i + 1) * tm:(2 * i + 2) * tm] for t in tops]
            top8 = jnp.concatenate(v2[:8], axis=0)
            pieces = [v1[0] + jnp.concatenate(v2[:k], axis=0)]
            pieces += [v1[j] + top8 for j in range(1, 8)]
            pieces += [jnp.concatenate(v1[8:k], axis=0) + v2[0]]
            pieces += [jnp.concatenate([v1[0] + v2[k], v1[k] + v2[0]]
                                       + [jnp.full_like(v1[0], NEG_BIG)] * 6, axis=0)]
            cands.append(jnp.concatenate(pieces, axis=0))
        best = _top_values(jnp.concatenate(cands, axis=1), k + 1)
        grp = (PEER_KEYS // SUBLANES, SUBLANES, tm)
        for i, hh in enumerate(heads):
            ms = [t[:, i * tm:(i + 1) * tm] for t in best]
            s1, s2 = scores[2 * i], scores[2 * i + 1]
            m1, m2 = tops[0][:, (2 * i) * tm:(2 * i + 1) * tm], tops[0][:, (2 * i + 1) * tm:(2 * i + 2) * tm]
            thr = 0.5 * (ms[k - 1] + ms[k])
            z = jnp.ones_like(ms[0])
            for r in range(1, k):
                z = z + jnp.exp(ms[r] - ms[0])
            c_ref[hh] = (thr - s1).reshape(grp)
            e1_ref[hh] = (jnp.exp(s1 - m1) * (0.5 / z)).reshape(grp)
            s2_ref[hh] = s2
            e2_ref[hh] = jnp.exp(s2 - m2)
        return carry

    lax.fori_loop(0, PEER_HEADS // group, group_body, 0)


def _peer_scores(s, modv, g_ffn, wq_t, keys, rows, *, n_lat_rows, n_per_batch):
    d = s.shape[1]
    n_batch = modv.shape[0] - 1
    tm = SCORE_TM
    nlat_blocks = n_lat_rows // tm
    bpb = n_per_batch // tm
    nq = wq_t.shape[0]
    hk = (PEER_HEADS, PEER_KEYS, rows)
    hk_spec = pl.BlockSpec((PEER_HEADS, PEER_KEYS, tm), lambda i: (0, 0, i))
    hg = (PEER_HEADS, PEER_KEYS // SUBLANES, SUBLANES, rows)
    hg_spec = pl.BlockSpec((PEER_HEADS, PEER_KEYS // SUBLANES, SUBLANES, tm), lambda i: (0, 0, 0, i))
    return pl.pallas_call(
        _peer_score_kernel,
        grid=(rows // tm,),
        in_specs=[pl.BlockSpec((tm, d), lambda i: (i, 0)),
                  pl.BlockSpec((1, d), lambda i: (0, 0)),
                  pl.BlockSpec((1, 8, d), lambda i: (_mod_index(i, nlat_blocks, bpb, n_batch), 0, 0)),
                  pl.BlockSpec((nq, d), lambda i: (0, 0)),
                  pl.BlockSpec((PEER_HEADS, 2, PEER_KEYS, LANES), lambda i: (0, 0, 0, 0))],
        out_specs=[pl.BlockSpec((d, tm), lambda i: (0, i)), hg_spec, hg_spec, hk_spec, hk_spec],
        out_shape=[jax.ShapeDtypeStruct((d, rows), BF16)] + [jax.ShapeDtypeStruct(hg, F32)] * 2
                  + [jax.ShapeDtypeStruct(hk, F32)] * 2,
        scratch_shapes=[pltpu.VMEM((nq, tm), F32)],
        compiler_params=_cparams(("parallel",)),
        name="peer_scores",
    )(s, g_ffn.reshape(1, d), modv, wq_t, keys)


def _peer_expert_kernel(ht_ref, u_ref, vt_ref, c_ref, e1_ref, s2_ref, e2_ref, s_ref, mod_ref,
                        o_ref, acc_scr, st0_scr, st1_scr, at0_scr, at1_scr, *, n_exp_blocks):
    g = pl.program_id(0)
    st_scr, at_scr = (st0_scr, st1_scr), (at0_scr, at1_scr)
    te, tm = st0_scr.shape
    jc = jnp.maximum(g - 2, 0) % n_exp_blocks

    @pl.when(g == 0)
    def _():
        for ref in st_scr + at_scr:
            ref[...] = jnp.zeros_like(ref)

    @pl.when(jc == 0)
    def _():
        acc_scr[...] = jnp.zeros_like(acc_scr)

    def stages(slot_a, slot_b):
        n_chunks = te // MXU_CHUNK_ROWS

        def value_chunk(q):
            ms = slice(q * MXU_CHUNK_ROWS, (q + 1) * MXU_CHUNK_ROWS)
            acc_scr[ms, :] += jnp.dot(vt_ref[ms, :], at_scr[slot_a][...], preferred_element_type=F32)

        def score_chunk(q):
            ms = slice(q * MXU_CHUNK_ROWS, (q + 1) * MXU_CHUNK_ROWS)
            st_scr[slot_a][ms, :] = jnp.dot(u_ref[ms, :], ht_ref[...], preferred_element_type=F32)

        rg_rows = 4 * SUBLANES
        al_group = 4
        col_tiles = 1
        tiles = [(rg, c0, al0)
                 for rg in range(LANES // rg_rows)
                 for c0 in range(0, tm // LANES, col_tiles)
                 for al0 in range(0, te // LANES, al_group)]
        assert acc_scr.shape[0] == te
        per_chunk = len(tiles) // (2 * n_chunks)
        for t, (rg, c0, al0) in enumerate(tiles):
            if t % per_chunk == 0:
                q = t // per_chunk
                (value_chunk if q % 2 == 0 else score_chunk)(q // 2)
            brows = slice(rg * rg_rows, (rg + 1) * rg_rows)
            lanes = [slice((c0 + ct) * LANES, (c0 + ct + 1) * LANES) for ct in range(col_tiles)]
            gate = [[jnp.zeros((rg_rows, LANES), F32) for _ in lanes] for _ in range(al_group)]
            for hh in range(PEER_HEADS):
                for ct, ln in enumerate(lanes):
                    s2t = s2_ref[hh, brows, ln]
                    e2t = e2_ref[hh, brows, ln]
                    for k in range(al_group):
                        al = al0 + k
                        c = c_ref[hh, 0, al:al + 1, ln]
                        e1 = e1_ref[hh, 0, al:al + 1, ln]
                        gate[k][ct] = gate[k][ct] + jnp.where(s2t >= c, e2t, 0.0) * e1
            for k in range(al_group):
                erows = slice((al0 + k) * LANES + rg * rg_rows, (al0 + k) * LANES + (rg + 1) * rg_rows)
                for ct, ln in enumerate(lanes):
                    sv = st_scr[slot_b][erows, ln]
                    act = sv * (1.0 + lax.erf(sv * (2.0 ** -0.5)))
                    at_scr[slot_b][erows, ln] = (gate[k][ct] * act).astype(BF16)

    for parity in range(2):
        @pl.when(g % 2 == parity)
        def _():
            stages(parity, 1 - parity)

    @pl.when((jc == n_exp_blocks - 1) & (g >= 2))
    def _():
        o_ref[...] = s_ref[...] + mod_ref[0][5:6] * acc_scr[...].T


def _peer_experts(ht, u, vt, c, e1, s2, e2, s, modv, rows, *, n_lat_rows, n_per_batch):
    d = s.shape[1]
    n_exp = u.shape[0]
    n_batch = modv.shape[0] - 1
    tm, te = PEER_TM, PEER_TE
    nlat_blocks = n_lat_rows // tm
    bpb = n_per_batch // tm
    nj = n_exp // te
    n_pairs = (rows // tm) * nj

    def pair(g, lag):
        p = jnp.clip(g - lag, 0, n_pairs - 1)
        return p // nj, p % nj

    hk_spec = pl.BlockSpec((PEER_HEADS, PEER_KEYS, tm), lambda g: (0, 0, pair(g, 1)[0]))
    hg_spec = pl.BlockSpec((PEER_HEADS, 1, SUBLANES, tm), lambda g: (0, pair(g, 1)[1], 0, pair(g, 1)[0]))
    kern = functools.partial(_peer_expert_kernel, n_exp_blocks=nj)
    return pl.pallas_call(
        kern,
        grid=(n_pairs + 2,),
        in_specs=[pl.BlockSpec((d, tm), lambda g: (0, pair(g, 0)[0])),
                  pl.BlockSpec((te, d), lambda g: (pair(g, 0)[1], 0)),
                  pl.BlockSpec((d, te), lambda g: (0, pair(g, 2)[1])),
                  hg_spec, hg_spec, hk_spec, hk_spec,
                  pl.BlockSpec((tm, d), lambda g: (pair(g, 2)[0], 0)),
                  pl.BlockSpec((1, 8, d),
                               lambda g: (_mod_index(pair(g, 2)[0], nlat_blocks, bpb, n_batch), 0, 0))],
        out_specs=pl.BlockSpec((tm, d), lambda g: (pair(g, 2)[0], 0)),
        out_shape=jax.ShapeDtypeStruct((rows, d), F32),
        scratch_shapes=[pltpu.VMEM((d, tm), F32), pltpu.VMEM((te, tm), F32), pltpu.VMEM((te, tm), F32),
                        pltpu.VMEM((te, tm), BF16), pltpu.VMEM((te, tm), BF16)],
        compiler_params=_cparams(("arbitrary",)),
        name="peer_experts",
    )(ht, u, vt, c, e1, s2, e2, s, modv)


def _peer(s, modv, g_ffn, w_q, keys, u_tab, v_tab, rows, *, n_lat_rows, n_per_batch):
    geo = dict(n_lat_rows=n_lat_rows, n_per_batch=n_per_batch)
    ht, c, e1, s2, e2 = _peer_scores(s, modv, g_ffn, w_q.T.astype(BF16), keys.astype(BF16), rows, **geo)
    return _peer_experts(ht, u_tab.astype(BF16), v_tab.T.astype(BF16), c, e1, s2, e2, s, modv, rows, **geo)


def kernel(x, c, ctx, c_ctx, ada_w, ada_b, norm_mix, norm_ffn, mla_w_down, mla_g_cq, mla_g_ckv, mla_w_uq, mla_w_ukv, mla_g_q, mla_g_k, mla_w_o, diff_w_qkv, diff_g_q, diff_g_k, diff_lambda, diff_g_sub, diff_w_o, swa_w_qkv, swa_g_q, swa_g_k, swa_sink, swa_w_o, peer_w_q, peer_keys, peer_u, peer_v):
    n_batch, n_lat, d = x.shape
    n_ctx = ctx.shape[1]
    depth = ada_w.shape[0]
    rows_lat = n_batch * n_lat
    rows_all = rows_lat + n_batch * n_ctx
    assert n_lat % 512 == 0 and n_ctx % 256 == 0 and (n_batch * n_ctx) % PEER_TM == 0
    assert n_lat % GRID_W == 0 and rows_lat % n_ctx == 0 and n_lat >= 3 * Q_BLOCK

    s = jnp.concatenate([x.reshape(rows_lat, d), ctx.reshape(n_batch * n_ctx, d)], axis=0)
    cc = jnp.zeros((16, d), F32).at[:n_batch].set(c).at[n_batch].set(c_ctx)
    mods = _ada_mods(cc, ada_w, ada_b)
    geo = dict(n_lat_rows=rows_lat, n_per_batch=n_lat)

    for i in range(depth):
        last = i == depth - 1
        kind, j = i % 3, i // 3
        modv = jnp.zeros((n_batch + 1, 8, d), F32).at[:, :6, :].set(
            mods[i, :n_batch + 1].reshape(n_batch + 1, 6, d))
        dims = dict(n_batch=n_batch, n_lat=n_lat, n_ctx=n_ctx, need_ctx=not last)
        if kind == 0:
            p = (mla_w_down[j], mla_g_cq[j], mla_g_ckv[j], mla_w_uq[j], mla_w_ukv[j], mla_g_q[j], mla_g_k[j])
            o, wo = _mla_mixer(s, modv, norm_mix[i], p, mla_w_o[j], **dims)
        elif kind == 1:
            lam_init = 0.8 - 0.6 * math.exp(-0.3 * i)
            o, wo = _diff_mixer(s, modv, norm_mix[i], diff_w_qkv[j], diff_g_q[j], diff_g_k[j],
                                diff_lambda[j], diff_g_sub[j], diff_w_o[j], lam_init, **dims)
        else:
            o, wo = _swa_mixer(s, modv, norm_mix[i], swa_w_qkv[j], swa_g_q[j], swa_g_k[j],
                               swa_sink[j], swa_w_o[j], **dims)
        rows = rows_lat if last else rows_all
        s = _proj_resid(o, wo, s, modv, rows, gate_row=2, **geo)
        s = _peer(s, modv, norm_ffn[i], peer_w_q[i], peer_keys[i], peer_u[i], peer_v[i], rows, **geo)
    return s.reshape(n_batch, n_lat, d)
```

```python
import functools
import math

import jax
import jax.numpy as jnp
from jax import lax
from jax.experimental import pallas as pl
from jax.experimental.pallas import tpu as pltpu

F32 = jnp.float32
BF16 = jnp.bfloat16

LANES = 128
EPS = 1e-6
NEG_INF = -1e30
NEG_BIG = -3.0e38
ROPE_THETA = 10000.0
GRID_W = 64
Q_BLOCK = 128
WINDOW = 128

MLA_HEADS, MLA_NOPE, MLA_ROPE, MLA_V = 16, 64, 32, 64
MLA_QK = MLA_NOPE + MLA_ROPE
MLA_Q_LORA, MLA_KV_LORA = 768, 256
DIFF_HEADS, DIFF_HD = 8, 64
SWA_HEADS, SWA_KV_HEADS, SWA_HD = 16, 4, 64
SWA_GROUP = SWA_HEADS // SWA_KV_HEADS
PEER_HEADS, PEER_KEYS, PEER_TOPK = 8, 128, 16
PEER_EXPERTS = PEER_KEYS * PEER_KEYS

VMEM_LIMIT = 56 * 1024 * 1024

TM = 512
SCORE_TM = 512
TQ = 256
PEER_TM = 512
SUBLANES = 8
PEER_TE = SUBLANES * LANES
MXU_CHUNK_ROWS = 128


def _cparams(sem):
    return pltpu.CompilerParams(dimension_semantics=sem, vmem_limit_bytes=VMEM_LIMIT)


def _ada_kernel(c_ref, w_ref, b_ref, o_ref):
    c = c_ref[...]
    sc = c * (1.0 / (1.0 + jnp.exp(-c)))
    o_ref[0] = jnp.dot(sc, w_ref[0], preferred_element_type=F32,
                       precision=lax.Precision.HIGHEST) + b_ref[0]


def _ada_mods(cc, ada_w, ada_b):
    depth, d, d6 = ada_w.shape
    rows = cc.shape[0]
    tn = 1536
    return pl.pallas_call(
        _ada_kernel,
        grid=(depth, d6 // tn),
        in_specs=[pl.BlockSpec((rows, d), lambda i, n: (0, 0)),
                  pl.BlockSpec((1, d, tn), lambda i, n: (i, 0, n)),
                  pl.BlockSpec((1, 1, tn), lambda i, n: (i, 0, n))],
        out_specs=pl.BlockSpec((1, rows, tn), lambda i, n: (i, 0, n)),
        out_shape=jax.ShapeDtypeStruct((depth, rows, d6), F32),
        compiler_params=_cparams(("arbitrary", "arbitrary")),
        name="ada_mods",
    )(cc, ada_w, ada_b.reshape(depth, 1, d6))


def _norm_mod(x, g, mod, shift_row, scale_row):
    ms = jnp.mean(x * x, axis=-1, keepdims=True)
    y = x * lax.rsqrt(ms + EPS) * g
    if mod is not None:
        y = y * (1.0 + mod[scale_row:scale_row + 1]) + mod[shift_row:shift_row + 1]
    return y


def _nm_kernel(x_ref, g_ref, mod_ref, w_ref, o_ref, h_scr, *, use_mod, shift_row, scale_row):
    @pl.when(pl.program_id(1) == 0)
    def _():
        mod = mod_ref[0] if use_mod else None
        h_scr[...] = _norm_mod(x_ref[...], g_ref[...], mod, shift_row, scale_row).astype(BF16)

    o_ref[...] = jnp.dot(h_scr[...], w_ref[...], preferred_element_type=F32).astype(o_ref.dtype)


def _mod_index(i, nlat_blocks, blocks_per_batch, n_batch):
    return jnp.where(i < nlat_blocks, i // blocks_per_batch, n_batch)


def _nm_matmul(x, kblk, k, g, modv, rows, w, *, n_lat_rows, n_per_batch, out_dtype=F32,
               use_mod=True, shift_row=0, scale_row=1, name="nm_matmul"):
    nout = w.shape[1]
    tn = nout
    for cand in (2048, 1536, 1280, 1024):
        if nout > 2048 and nout % cand == 0:
            tn = cand
            break
    n_batch = modv.shape[0] - 1
    nlat_blocks = n_lat_rows // TM
    bpb = n_per_batch // TM
    kern = functools.partial(_nm_kernel, use_mod=use_mod, shift_row=shift_row, scale_row=scale_row)
    return pl.pallas_call(
        kern,
        grid=(rows // TM, nout // tn),
        in_specs=[pl.BlockSpec((TM, k), lambda i, n: (i, kblk)),
                  pl.BlockSpec((1, k), lambda i, n: (0, 0)),
                  pl.BlockSpec((1, 8, k), lambda i, n: (_mod_index(i, nlat_blocks, bpb, n_batch), 0, 0)),
                  pl.BlockSpec((k, tn), lambda i, n: (0, n))],
        out_specs=pl.BlockSpec((TM, tn), lambda i, n: (i, n)),
        out_shape=jax.ShapeDtypeStruct((rows, nout), out_dtype),
        scratch_shapes=[pltpu.VMEM((TM, k), BF16)],
        compiler_params=_cparams(("parallel", "arbitrary")),
        name=name,
    )(x, g.reshape(1, k), modv if modv.shape[-1] == k else jnp.zeros((n_batch + 1, 8, k), F32), w)


def _proj_resid_kernel(a_ref, w_ref, s_ref, mod_ref, o_ref, *, gate_row):
    y = jnp.dot(a_ref[...].astype(BF16), w_ref[...], preferred_element_type=F32)
    o_ref[...] = s_ref[...] + mod_ref[0][gate_row:gate_row + 1] * y


def _proj_resid(a, w, s, modv, rows, *, n_lat_rows, n_per_batch, gate_row, name="proj_resid"):
    ka = a.shape[1]
    d = w.shape[1]
    n_batch = modv.shape[0] - 1
    nlat_blocks = n_lat_rows // TM
    bpb = n_per_batch // TM
    return pl.pallas_call(
        functools.partial(_proj_resid_kernel, gate_row=gate_row),
        grid=(rows // TM,),
        in_specs=[pl.BlockSpec((TM, ka), lambda i: (i, 0)),
                  pl.BlockSpec((ka, d), lambda i: (0, 0)),
                  pl.BlockSpec((TM, d), lambda i: (i, 0)),
                  pl.BlockSpec((1, 8, d), lambda i: (_mod_index(i, nlat_blocks, bpb, n_batch), 0, 0))],
        out_specs=pl.BlockSpec((TM, d), lambda i: (i, 0)),
        out_shape=jax.ShapeDtypeStruct((rows, d), F32),
        compiler_params=_cparams(("parallel",)),
        name=name,
    )(a, w, s, modv)


def _rope_tables(n_lat, extra_rows, segments, nf):
    pos = jnp.arange(n_lat, dtype=jnp.int32)
    row = (pos // GRID_W).astype(F32)
    col = (pos % GRID_W).astype(F32)
    inv = ROPE_THETA ** (-jnp.arange(nf, dtype=F32) / nf)
    cos = jnp.ones((n_lat, LANES), F32)
    sa = jnp.zeros((n_lat, LANES), F32)
    sb = jnp.zeros((n_lat, LANES), F32)
    for first, use_row in segments:
        ang = (row if use_row else col)[:, None] * inv[None, :]
        c, s = jnp.cos(ang), jnp.sin(ang)
        cos = cos.at[:, first:first + nf].set(c).at[:, first + nf:first + 2 * nf].set(c)
        sa = sa.at[:, first:first + nf].set(-s)
        sb = sb.at[:, first + nf:first + 2 * nf].set(s)
    pad = lambda t, v: jnp.concatenate([t, jnp.full((extra_rows, LANES), v, F32)], axis=0)
    return pad(cos, 1.0), pad(sa, 0.0), pad(sb, 0.0)


def _head_norm(x, g, d):
    ssq = jnp.sum(x * x, axis=-1, keepdims=True) * (1.0 / d)
    return x * lax.rsqrt(ssq + EPS) * g


def _rope(y, cos, sa, sb, nf):
    return y * cos + pltpu.roll(y, LANES - nf, 1) * sa + pltpu.roll(y, nf, 1) * sb


def _row_block(b, qi, nlat_q, nctx_q, n_batch):
    return jnp.where(qi < nlat_q, b * nlat_q + qi, n_batch * nlat_q + b * nctx_q + (qi - nlat_q))


def _prep_keys(k_scr, v_scr, n_lat, n_ctx, load_k_lat, load_k_ctx, load_v_lat, load_v_ctx,
               gk, tabs, d, nf):
    cos_ref, sa_ref, sb_ref = tabs
    ch = 256

    def lat_body(c, carry):
        r = pl.ds(pl.multiple_of(c * ch, ch), ch)
        kk = _head_norm(load_k_lat(r).astype(F32), gk, d)
        kk = _rope(kk, cos_ref[r, :], sa_ref[r, :], sb_ref[r, :], nf)
        k_scr[r, :] = kk.astype(BF16)
        if v_scr is not None:
            v_scr[r, :] = load_v_lat(r).astype(BF16)
        return carry

    lax.fori_loop(0, n_lat // ch, lat_body, 0)

    def ctx_body(c, carry):
        r = pl.ds(pl.multiple_of(c * ch, ch), ch)
        ro = pl.ds(pl.multiple_of(n_lat + c * ch, ch), ch)
        k_scr[ro, :] = _head_norm(load_k_ctx(r).astype(F32), gk, d).astype(BF16)
        if v_scr is not None:
            v_scr[ro, :] = load_v_ctx(r).astype(BF16)
        return carry

    lax.fori_loop(0, n_ctx // ch, ctx_body, 0)


def _nt_dot(a, b):
    return lax.dot_general(a, b, (((1,), (1,)), ((), ())), preferred_element_type=F32)


LOG2E = math.log2(math.e)
CHAIN_ROWS = 128
HEADS_PER_STEP = 2


def _row_chains(n_rows):
    return [slice(r, r + CHAIN_ROWS) for r in range(0, n_rows, CHAIN_ROWS)]


def _emit_pipelined(chains):
    pending, active = list(chains), []
    while pending or active:
        if pending:
            active.append(pending.pop(0))
        for ch in list(active):
            try:
                next(ch)
            except StopIteration:
                active.remove(ch)


def _exp2_scores(q, k):
    s = _nt_dot(q, k)
    p = jnp.exp2(s - jnp.max(s, axis=-1, keepdims=True))
    return p, jnp.sum(p, axis=-1, keepdims=True)


def _mla_attn_kernel(q_ref, kl_ref, kc_ref, krl_ref, krc_ref, vl_ref, vc_ref,
                     cq_ref, saq_ref, sbq_ref, ck_ref, sak_ref, sbk_ref, gq_ref, gk_ref,
                     o_ref, k_scr, v_scr, *, n_lat, n_ctx, nlat_q, scale):
    qi = pl.program_id(2)
    nf = MLA_ROPE // 4
    heads = [slice(hd * LANES, (hd + 1) * LANES) for hd in range(q_ref.shape[1] // LANES)]

    @pl.when(qi == 0)
    def _():
        for hd, ln in enumerate(heads):
            _prep_keys(k_scr.at[hd], v_scr.at[hd], n_lat, n_ctx,
                       lambda r, ln=ln: kl_ref[r, ln].astype(F32) + krl_ref[r, :],
                       lambda r, ln=ln: kc_ref[r, ln].astype(F32) + krc_ref[r, :],
                       lambda r, ln=ln: vl_ref[r, ln], lambda r, ln=ln: vc_ref[r, ln],
                       gk_ref[...], (ck_ref, sak_ref, sbk_ref), MLA_QK, nf)

    def attend(keys):
        def chain(hd, ln, rows):
            q = _head_norm(q_ref[rows, ln].astype(F32), gq_ref[...], MLA_QK)
            q = _rope(q, cq_ref[rows, :], saq_ref[rows, :], sbq_ref[rows, :], nf) * (scale * LOG2E)
            s = _nt_dot(q.astype(BF16), k_scr[hd, keys, :])
            yield
            p = jnp.exp2(s - jnp.max(s, axis=-1, keepdims=True))
            l = jnp.sum(p, axis=-1, keepdims=True)
            yield
            o = jnp.dot(p.astype(BF16), v_scr[hd, keys, :], preferred_element_type=F32)
            o_ref[rows, ln] = (o * (1.0 / l)).astype(o_ref.dtype)

        _emit_pipelined([chain(hd, ln, rows) for rows in _row_chains(q_ref.shape[0])
                         for hd, ln in enumerate(heads)])

    @pl.when(qi < nlat_q)
    def _():
        attend(slice(0, n_lat + n_ctx))

    @pl.when(qi >= nlat_q)
    def _():
        attend(slice(n_lat, n_lat + n_ctx))


def _mla_attention(q, kv, lat, g_q, g_k, *, n_batch, n_lat, n_ctx, need_ctx):
    nlat_q, nctx_q = n_lat // TQ, n_ctx // TQ
    nq = nlat_q + (nctx_q if need_ctx else 0)
    rows = n_batch * (n_lat + (n_ctx if need_ctx else 0))
    nf = MLA_ROPE // 4
    tabs = _rope_tables(n_lat, TQ, ((MLA_NOPE, True), (MLA_NOPE + 2 * nf, False)), nf)
    h = MLA_HEADS
    ctx0 = n_batch * n_lat // n_ctx
    kr_blk = (MLA_Q_LORA + MLA_KV_LORA) // LANES
    gq = jnp.zeros((1, LANES), F32).at[0, :MLA_QK].set(g_q)
    gk = jnp.zeros((1, LANES), F32).at[0, :MLA_QK].set(g_k)
    hps = HEADS_PER_STEP
    hw = hps * LANES
    qspec = pl.BlockSpec((TQ, hw), lambda b, hh, qi: (_row_block(b, qi, nlat_q, nctx_q, n_batch), hh))
    tq_spec = pl.BlockSpec((TQ, LANES), lambda b, hh, qi: (jnp.minimum(qi, nlat_q), 0))
    tk_spec = pl.BlockSpec((n_lat, LANES), lambda b, hh, qi: (0, 0))
    vec = pl.BlockSpec((1, LANES), lambda b, hh, qi: (0, 0))
    kern = functools.partial(_mla_attn_kernel, n_lat=n_lat, n_ctx=n_ctx, nlat_q=nlat_q,
                             scale=MLA_QK ** -0.5)
    return pl.pallas_call(
        kern,
        grid=(n_batch, h // hps, nq),
        in_specs=[qspec,
                  pl.BlockSpec((n_lat, hw), lambda b, hh, qi: (b, hh)),
                  pl.BlockSpec((n_ctx, hw), lambda b, hh, qi: (ctx0 + b, hh)),
                  pl.BlockSpec((n_lat, LANES), lambda b, hh, qi: (b, kr_blk)),
                  pl.BlockSpec((n_ctx, LANES), lambda b, hh, qi: (ctx0 + b, kr_blk)),
                  pl.BlockSpec((n_lat, hw), lambda b, hh, qi: (b, h // hps + hh)),
                  pl.BlockSpec((n_ctx, hw), lambda b, hh, qi: (ctx0 + b, h // hps + hh)),
                  tq_spec, tq_spec, tq_spec, tk_spec, tk_spec, tk_spec, vec, vec],
        out_specs=qspec,
        out_shape=jax.ShapeDtypeStruct((rows, h * LANES), BF16),
        scratch_shapes=[pltpu.VMEM((hps, n_lat + n_ctx, LANES), BF16),
                        pltpu.VMEM((hps, n_lat + n_ctx, LANES), BF16)],
        compiler_params=_cparams(("parallel", "parallel", "arbitrary")),
        name="mla_attention",
    )(q, kv, kv, lat, lat, kv, kv, *tabs, *tabs, gq, gk)


def _pad_cols(w, groups, width, first=0):
    k = w.shape[0]
    w3 = w.reshape(k, groups, width)
    out = jnp.zeros((k, groups, LANES), w.dtype).at[:, :, first:first + width].set(w3)
    return out.reshape(k, groups * LANES)


def _pad_rows(w, groups, width):
    d = w.shape[1]
    w3 = w.reshape(groups, width, d)
    return jnp.zeros((groups, LANES, d), w.dtype).at[:, :width, :].set(w3).reshape(groups * LANES, d)


def _mla_mixer(s, modv, g_mix, p, w_o, *, n_batch, n_lat, n_ctx, need_ctx):
    w_down, g_cq, g_ckv, w_uq, w_ukv, g_q, g_k = p
    d = s.shape[1]
    rows_all = n_batch * (n_lat + n_ctx)
    geo = dict(n_lat_rows=n_batch * n_lat, n_per_batch=n_lat)
    wd = jnp.concatenate([w_down[:, :MLA_Q_LORA + MLA_KV_LORA],
                          _pad_cols(w_down[:, MLA_Q_LORA + MLA_KV_LORA:], 1, MLA_ROPE, MLA_NOPE)], axis=1)
    lat = _nm_matmul(s, 0, d, g_mix, modv, rows_all, wd.astype(BF16), name="mla_down", **geo)
    wq = _pad_cols(w_uq, MLA_HEADS, MLA_QK).astype(BF16)
    q = _nm_matmul(lat, 0, MLA_Q_LORA, g_cq, modv, rows_all, wq, use_mod=False, name="mla_uq",
                   out_dtype=BF16, **geo)
    wkv3 = w_ukv.reshape(MLA_KV_LORA, MLA_HEADS, MLA_NOPE + MLA_V)
    wkv = jnp.concatenate([_pad_cols(wkv3[:, :, :MLA_NOPE].reshape(MLA_KV_LORA, -1), MLA_HEADS, MLA_NOPE),
                           _pad_cols(wkv3[:, :, MLA_NOPE:].reshape(MLA_KV_LORA, -1), MLA_HEADS, MLA_V)],
                          axis=1).astype(BF16)
    kv = _nm_matmul(lat, MLA_Q_LORA // MLA_KV_LORA, MLA_KV_LORA, g_ckv, modv, rows_all, wkv,
                    use_mod=False, name="mla_ukv", out_dtype=BF16, **geo)
    o = _mla_attention(q, kv, lat, g_q, g_k, n_batch=n_batch, n_lat=n_lat, n_ctx=n_ctx, need_ctx=need_ctx)
    wo = _pad_rows(w_o, MLA_HEADS, MLA_V).astype(BF16)
    return o, wo


def _diff_attn_kernel(q_ref, kl_ref, kc_ref, vl_ref, vc_ref,
                      cq_ref, saq_ref, sbq_ref, ck_ref, sak_ref, sbk_ref, gq_ref, gk_ref,
                      lam_ref, gsub_ref, o_ref, k_scr, v_scr,
                      *, n_lat, n_ctx, nlat_q, scale, lam_init):
    qi = pl.program_id(2)
    nf = DIFF_HD // 4
    tabs = (ck_ref, sak_ref, sbk_ref)
    tile = lambda t: slice(t * LANES, (t + 1) * LANES)
    n_heads = o_ref.shape[1] // LANES

    @pl.when(qi == 0)
    def _():
        for hd in range(n_heads):
            ln = tile(hd)
            for sub in range(2):
                kt = tile(2 * hd + sub)
                _prep_keys(k_scr.at[2 * hd + sub], v_scr.at[hd] if sub == 0 else None, n_lat, n_ctx,
                           lambda r, kt=kt: kl_ref[r, kt], lambda r, kt=kt: kc_ref[r, kt],
                           lambda r, ln=ln: vl_ref[r, ln], lambda r, ln=ln: vc_ref[r, ln],
                           gk_ref[...], tabs, DIFF_HD, nf)

    lp = lam_ref[...]
    lam = (jnp.exp(jnp.sum(lp[0:1] * lp[1:2], axis=-1, keepdims=True))
           - jnp.exp(jnp.sum(lp[2:3] * lp[3:4], axis=-1, keepdims=True)) + lam_init)

    def attend(keys):
        def scores(t, rows):
            q = _head_norm(q_ref[rows, tile(t)].astype(F32), gq_ref[...], DIFF_HD)
            q = _rope(q, cq_ref[rows, :], saq_ref[rows, :], sbq_ref[rows, :], nf) * (scale * LOG2E)
            return _nt_dot(q.astype(BF16), k_scr[t, keys, :])

        def chain(hd, rows):
            s0 = scores(2 * hd, rows)
            s1 = scores(2 * hd + 1, rows)
            yield
            p0 = jnp.exp2(s0 - jnp.max(s0, axis=-1, keepdims=True))
            p1 = jnp.exp2(s1 - jnp.max(s1, axis=-1, keepdims=True))
            l0 = jnp.sum(p0, axis=-1, keepdims=True)
            l1 = jnp.sum(p1, axis=-1, keepdims=True)
            a = (p0 * (1.0 / l0) - p1 * (lam / l1)).astype(BF16)
            yield
            o = jnp.dot(a, v_scr[hd, keys, :], preferred_element_type=F32)
            o = _head_norm(o, gsub_ref[...], 2 * DIFF_HD) * (1.0 - lam_init)
            o_ref[rows, tile(hd)] = o.astype(o_ref.dtype)

        _emit_pipelined([chain(hd, rows) for rows in _row_chains(q_ref.shape[0])
                         for hd in range(n_heads)])

    @pl.when(qi < nlat_q)
    def _():
        attend(slice(0, n_lat + n_ctx))

    @pl.when(qi >= nlat_q)
    def _():
        attend(slice(n_lat, n_lat + n_ctx))


def _diff_attention(qkv, g_q, g_k, lam_p, g_sub, lam_init, *, n_batch, n_lat, n_ctx, need_ctx):
    nlat_q, nctx_q = n_lat // TQ, n_ctx // TQ
    nq = nlat_q + (nctx_q if need_ctx else 0)
    rows = n_batch * (n_lat + (n_ctx if need_ctx else 0))
    nf = DIFF_HD // 4
    tabs = _rope_tables(n_lat, TQ, ((0, True), (2 * nf, False)), nf)
    nh = DIFF_HEADS
    ctx0 = n_batch * n_lat // n_ctx
    gq = jnp.zeros((1, LANES), F32).at[0, :DIFF_HD].set(g_q)
    gk = jnp.zeros((1, LANES), F32).at[0, :DIFF_HD].set(g_k)
    lam8 = jnp.zeros((8, LANES), F32).at[:4, :DIFF_HD].set(lam_p.astype(F32))
    rb = lambda b, qi: _row_block(b, qi, nlat_q, nctx_q, n_batch)
    tq_spec = pl.BlockSpec((TQ, LANES), lambda b, hh, qi: (jnp.minimum(qi, nlat_q), 0))
    tk_spec = pl.BlockSpec((n_lat, LANES), lambda b, hh, qi: (0, 0))
    vec = pl.BlockSpec((1, LANES), lambda b, hh, qi: (0, 0))
    hps = HEADS_PER_STEP
    qk_w, v_w = 2 * hps * LANES, hps * LANES
    k_blk0 = 2 * nh // (2 * hps)
    v_blk0 = 4 * nh // hps
    kern = functools.partial(_diff_attn_kernel, n_lat=n_lat, n_ctx=n_ctx, nlat_q=nlat_q,
                             scale=DIFF_HD ** -0.5, lam_init=lam_init)
    n_all = n_lat + n_ctx
    return pl.pallas_call(
        kern,
        grid=(n_batch, nh // hps, nq),
        in_specs=[pl.BlockSpec((TQ, qk_w), lambda b, hh, qi: (rb(b, qi), hh)),
                  pl.BlockSpec((n_lat, qk_w), lambda b, hh, qi: (b, k_blk0 + hh)),
                  pl.BlockSpec((n_ctx, qk_w), lambda b, hh, qi: (ctx0 + b, k_blk0 + hh)),
                  pl.BlockSpec((n_lat, v_w), lambda b, hh, qi: (b, v_blk0 + hh)),
                  pl.BlockSpec((n_ctx, v_w), lambda b, hh, qi: (ctx0 + b, v_blk0 + hh)),
                  tq_spec, tq_spec, tq_spec, tk_spec, tk_spec, tk_spec, vec, vec,
                  pl.BlockSpec((8, LANES), lambda b, hh, qi: (0, 0)), vec],
        out_specs=pl.BlockSpec((TQ, v_w), lambda b, hh, qi: (rb(b, qi), hh)),
        out_shape=jax.ShapeDtypeStruct((rows, nh * LANES), BF16),
        scratch_shapes=[pltpu.VMEM((2 * hps, n_all, LANES), BF16), pltpu.VMEM((hps, n_all, LANES), BF16)],
        compiler_params=_cparams(("parallel", "parallel", "arbitrary")),
        name="diff_attention",
    )(qkv, qkv, qkv, qkv, qkv, *tabs, *tabs, gq, gk, lam8, g_sub.reshape(1, LANES))


def _diff_mixer(s, modv, g_mix, w_qkv, g_q, g_k, lam_p, g_sub, w_o, lam_init,
                *, n_batch, n_lat, n_ctx, need_ctx):
    d = s.shape[1]
    width = DIFF_HEADS * 2 * DIFF_HD
    rows_all = n_batch * (n_lat + n_ctx)
    w = jnp.concatenate([_pad_cols(w_qkv[:, :width], 2 * DIFF_HEADS, DIFF_HD),
                         _pad_cols(w_qkv[:, width:2 * width], 2 * DIFF_HEADS, DIFF_HD),
                         w_qkv[:, 2 * width:]], axis=1).astype(BF16)
    qkv = _nm_matmul(s, 0, d, g_mix, modv, rows_all, w, n_lat_rows=n_batch * n_lat,
                     n_per_batch=n_lat, name="diff_qkv", out_dtype=BF16)
    o = _diff_attention(qkv, g_q, g_k, lam_p, g_sub, lam_init,
                        n_batch=n_batch, n_lat=n_lat, n_ctx=n_ctx, need_ctx=need_ctx)
    return o, w_o.astype(BF16)


def _swa_attn_kernel(q_ref, kl_ref, kc_ref, vl_ref, vc_ref,
                     cq_ref, saq_ref, sbq_ref, ck_ref, sak_ref, sbk_ref, gq_ref, gk_ref, sink_ref,
                     o_ref, k_scr, v_scr, *, n_lat, n_ctx, nlat_q, scale):
    qi = pl.program_id(2)
    nf = SWA_HD // 4
    band = 3 * Q_BLOCK

    @pl.when(qi == 0)
    def _():
        _prep_keys(k_scr, v_scr, n_lat, n_ctx, lambda r: kl_ref[r, :], lambda r: kc_ref[r, :],
                   lambda r: vl_ref[r, :], lambda r: vc_ref[r, :], gk_ref[...],
                   (ck_ref, sak_ref, sbk_ref), SWA_HD, nf)

    row_chains = _row_chains(q_ref.shape[0])
    reach = jnp.where(qi < nlat_q, WINDOW, -1)

    def chain(ci, rows, gi):
        qb = qi * len(row_chains) + ci
        k0 = pl.multiple_of(jnp.clip((qb - 1) * Q_BLOCK, 0, n_lat - band), Q_BLOCK)
        kpos = k0 + lax.broadcasted_iota(jnp.int32, (Q_BLOCK, band), 1)
        qpos = qb * Q_BLOCK + lax.broadcasted_iota(jnp.int32, (Q_BLOCK, band), 0)
        lanes = slice(gi * LANES, (gi + 1) * LANES)
        q = _head_norm(q_ref[rows, lanes].astype(F32), gq_ref[...], SWA_HD)
        q = _rope(q, cq_ref[rows, :], saq_ref[rows, :], sbq_ref[rows, :], nf) * (scale * LOG2E)
        q = q.astype(BF16)
        sb = _nt_dot(q, k_scr[pl.ds(k0, band), :])
        sc = _nt_dot(q, k_scr[n_lat:n_lat + n_ctx, :])
        yield
        sb = jnp.where(jnp.abs(kpos - qpos) <= reach, sb, NEG_INF)
        sink = sink_ref[gi][0:1, 0:1] * LOG2E
        m = jnp.maximum(jnp.maximum(jnp.max(sb, axis=-1, keepdims=True),
                                    jnp.max(sc, axis=-1, keepdims=True)), sink)
        pb = jnp.exp2(sb - m)
        pc = jnp.exp2(sc - m)
        l = (jnp.sum(pb, axis=-1, keepdims=True) + jnp.sum(pc, axis=-1, keepdims=True)
             + jnp.exp2(sink - m))
        yield
        o = (jnp.dot(pb.astype(BF16), v_scr[pl.ds(k0, band), :], preferred_element_type=F32)
             + jnp.dot(pc.astype(BF16), v_scr[n_lat:n_lat + n_ctx, :], preferred_element_type=F32))
        o_ref[rows, lanes] = (o * (1.0 / l)).astype(o_ref.dtype)

    _emit_pipelined([chain(ci, rows, gi) for ci, rows in enumerate(row_chains)
                     for gi in range(SWA_GROUP)])


def _swa_attention(qkv, g_q, g_k, sink, *, n_batch, n_lat, n_ctx, need_ctx):
    nlat_q, nctx_q = n_lat // TQ, n_ctx // TQ
    nq = nlat_q + (nctx_q if need_ctx else 0)
    rows = n_batch * (n_lat + (n_ctx if need_ctx else 0))
    nf = SWA_HD // 4
    tabs = _rope_tables(n_lat, TQ, ((0, True), (2 * nf, False)), nf)
    nh, nkv, grp = SWA_HEADS, SWA_KV_HEADS, SWA_GROUP
    ctx0 = n_batch * n_lat // n_ctx
    gq = jnp.zeros((1, LANES), F32).at[0, :SWA_HD].set(g_q)
    gk = jnp.zeros((1, LANES), F32).at[0, :SWA_HD].set(g_k)
    sink_t = jnp.broadcast_to(sink.astype(F32).reshape(nh, 1, 1), (nh, 8, LANES))
    rb = lambda b, qi: _row_block(b, qi, nlat_q, nctx_q, n_batch)
    tq_spec = pl.BlockSpec((TQ, LANES), lambda b, kh, qi: (jnp.minimum(qi, nlat_q), 0))
    tk_spec = pl.BlockSpec((n_lat, LANES), lambda b, kh, qi: (0, 0))
    vec = pl.BlockSpec((1, LANES), lambda b, kh, qi: (0, 0))
    kern = functools.partial(_swa_attn_kernel, n_lat=n_lat, n_ctx=n_ctx, nlat_q=nlat_q,
                             scale=SWA_HD ** -0.5)
    n_all = n_lat + n_ctx
    return pl.pallas_call(
        kern,
        grid=(n_batch, nkv, nq),
        in_specs=[pl.BlockSpec((TQ, grp * LANES), lambda b, kh, qi: (rb(b, qi), kh)),
                  pl.BlockSpec((n_lat, LANES), lambda b, kh, qi: (b, nh + kh)),
                  pl.BlockSpec((n_ctx, LANES), lambda b, kh, qi: (ctx0 + b, nh + kh)),
                  pl.BlockSpec((n_lat, LANES), lambda b, kh, qi: (b, nh + nkv + kh)),
                  pl.BlockSpec((n_ctx, LANES), lambda b, kh, qi: (ctx0 + b, nh + nkv + kh)),
                  tq_spec, tq_spec, tq_spec, tk_spec, tk_spec, tk_spec, vec, vec,
                  pl.BlockSpec((grp, 8, LANES), lambda b, kh, qi: (kh, 0, 0))],
        out_specs=pl.BlockSpec((TQ, grp * LANES), lambda b, kh, qi: (rb(b, qi), kh)),
        out_shape=jax.ShapeDtypeStruct((rows, nh * LANES), BF16),
        scratch_shapes=[pltpu.VMEM((n_all, LANES), BF16), pltpu.VMEM((n_all, LANES), BF16)],
        compiler_params=_cparams(("parallel", "parallel", "arbitrary")),
        name="swa_attention",
    )(qkv, qkv, qkv, qkv, qkv, *tabs, *tabs, gq, gk, sink_t)


def _swa_mixer(s, modv, g_mix, w_qkv, g_q, g_k, sink, w_o, *, n_batch, n_lat, n_ctx, need_ctx):
    d = s.shape[1]
    qw, kw = SWA_HEADS * SWA_HD, SWA_KV_HEADS * SWA_HD
    rows_all = n_batch * (n_lat + n_ctx)
    w = jnp.concatenate([_pad_cols(w_qkv[:, :qw], SWA_HEADS, SWA_HD),
                         _pad_cols(w_qkv[:, qw:qw + kw], SWA_KV_HEADS, SWA_HD),
                         _pad_cols(w_qkv[:, qw + kw:], SWA_KV_HEADS, SWA_HD)], axis=1).astype(BF16)
    qkv = _nm_matmul(s, 0, d, g_mix, modv, rows_all, w, n_lat_rows=n_batch * n_lat,
                     n_per_batch=n_lat, name="swa_qkv", out_dtype=BF16)
    o = _swa_attention(qkv, g_q, g_k, sink, n_batch=n_batch, n_lat=n_lat, n_ctx=n_ctx, need_ctx=need_ctx)
    return o, _pad_rows(w_o, SWA_HEADS, SWA_HD).astype(BF16)


def _merge_exchange_pairs(n):
    pairs = []
    t = max(1, (n - 1).bit_length())
    p = 1 << (t - 1)
    while p > 0:
        q, r, d = 1 << (t - 1), 0, p
        while d > 0:
            pairs += [(i, i + d) for i in range(n - d) if (i & p) == r]
            d, q, r = q - p, q >> 1, p
        p >>= 1
    return pairs


def _top_values(s, count):
    n = s.shape[0] // SUBLANES
    v = [s[j * SUBLANES:(j + 1) * SUBLANES, :] for j in range(n)]
    for i, j in _merge_exchange_pairs(n):
        v[i], v[j] = jnp.maximum(v[i], v[j]), jnp.minimum(v[i], v[j])
    vals = []
    for r in range(count):
        m = jnp.max(v[0], axis=0, keepdims=True)
        vals.append(m)
        depth = min(n, count - 1 - r)
        head = v[0] == m
        for j in range(depth):
            nxt = v[j + 1] if j + 1 < n else jnp.full_like(v[j], NEG_BIG)
            v[j] = jnp.where(head, nxt, v[j])
    return vals


def _peer_score_kernel(s_ref, g_ref, mod_ref, wq_ref, keys_ref,
                       ht_ref, c_ref, e1_ref, s2_ref, e2_ref, qt_scr):
    k = PEER_TOPK
    h = _norm_mod(s_ref[...], g_ref[...], mod_ref[0], 3, 4)
    ht = h.T.astype(BF16)
    ht_ref[0] = ht
    qt_scr[...] = jnp.dot(wq_ref[...], ht, preferred_element_type=F32)

    tm = ht.shape[1]
    group = 2

    def group_body(gidx, carry):
        heads = [gidx * group + i for i in range(group)]
        scores = []
        for hh in heads:
            for half in range(2):
                r = pl.ds(pl.multiple_of((hh * 2 + half) * LANES, LANES), LANES)
                scores.append(jnp.dot(keys_ref[hh, half], qt_scr[r, :].astype(BF16),
                                      preferred_element_type=F32))
        tops = _top_values(jnp.concatenate(scores, axis=1), k + 1)
        cands = []
        for i in range(group):
            v1 = [t[:, (2 * i) * tm:(2 * i + 1) * tm] for t in tops]
            v2 = [t[:, (2 * i + 1) * tm:(2 * i + 2) * tm] for t in tops]
            top8 = jnp.concatenate(v2[:8], axis=0)
            pieces = [v1[0] + jnp.concatenate(v2[:k], axis=0)]
            pieces += [v1[j] + top8 for j in range(1, 8)]
            pieces += [jnp.concatenate(v1[8:k], axis=0) + v2[0]]
            pieces += [jnp.concatenate([v1[0] + v2[k], v1[k] + v2[0]]
                                       + [jnp.full_like(v1[0], NEG_BIG)] * 6, axis=0)]
            cands.append(jnp.concatenate(pieces, axis=0))
        best = _top_values(jnp.concatenate(cands, axis=1), k + 1)
        grp = (PEER_KEYS // SUBLANES, SUBLANES, tm)
        for i, hh in enumerate(heads):
            ms = [t[:, i * tm:(i + 1) * tm] for t in best]
            s1, s2 = scores[2 * i], scores[2 * i + 1]
            m1, m2 = tops[0][:, (2 * i) * tm:(2 * i + 1) * tm], tops[0][:, (2 * i + 1) * tm:(2 * i + 2) * tm]
            thr = 0.5 * (ms[k - 1] + ms[k])
            z = jnp.ones_like(ms[0])
            for r in range(1, k):
                z = z + jnp.exp(ms[r] - ms[0])
            c_ref[hh] = (thr - s1).reshape(grp)
            e1_ref[hh] = (jnp.exp(s1 - m1) * (0.5 / z)).reshape(grp)
            s2_ref[0, hh] = s2
            e2_ref[0, hh] = jnp.exp(s2 - m2)
        return carry

    lax.fori_loop(0, PEER_HEADS // group, group_body, 0)


def _peer_scores(s, modv, g_ffn, wq_t, keys, rows, *, n_lat_rows, n_per_batch):
    d = s.shape[1]
    n_batch = modv.shape[0] - 1
    tm = SCORE_TM
    nlat_blocks = n_lat_rows // tm
    bpb = n_per_batch // tm
    nq = wq_t.shape[0]
    hk = (rows // tm, PEER_HEADS, PEER_KEYS, tm)
    hk_spec = pl.BlockSpec((1, PEER_HEADS, PEER_KEYS, tm), lambda i: (i, 0, 0, 0))
    hg = (PEER_HEADS, PEER_KEYS // SUBLANES, SUBLANES, rows)
    hg_spec = pl.BlockSpec((PEER_HEADS, PEER_KEYS // SUBLANES, SUBLANES, tm), lambda i: (0, 0, 0, i))
    return pl.pallas_call(
        _peer_score_kernel,
        grid=(rows // tm,),
        in_specs=[pl.BlockSpec((tm, d), lambda i: (i, 0)),
                  pl.BlockSpec((1, d), lambda i: (0, 0)),
                  pl.BlockSpec((1, 8, d), lambda i: (_mod_index(i, nlat_blocks, bpb, n_batch), 0, 0)),
                  pl.BlockSpec((nq, d), lambda i: (0, 0)),
                  pl.BlockSpec((PEER_HEADS, 2, PEER_KEYS, LANES), lambda i: (0, 0, 0, 0))],
        out_specs=[pl.BlockSpec((1, d, tm), lambda i: (i, 0, 0)), hg_spec, hg_spec, hk_spec, hk_spec],
        out_shape=[jax.ShapeDtypeStruct((rows // tm, d, tm), BF16)] + [jax.ShapeDtypeStruct(hg, F32)] * 2
                  + [jax.ShapeDtypeStruct(hk, F32)] * 2,
        scratch_shapes=[pltpu.VMEM((nq, tm), F32)],
        compiler_params=_cparams(("parallel",)),
        name="peer_scores",
    )(s, g_ffn.reshape(1, d), modv, wq_t, keys)


def _peer_expert_kernel(ht_ref, u_ref, vt_ref, c_ref, e1_ref, s2_ref, e2_ref, s_ref, mod_ref,
                        o_ref, acc_scr, st0_scr, st1_scr, at0_scr, at1_scr, *, n_exp_blocks):
    g = pl.program_id(0)
    st_scr, at_scr = (st0_scr, st1_scr), (at0_scr, at1_scr)
    te, tm = st0_scr.shape
    jc = jnp.maximum(g - 2, 0) % n_exp_blocks

    @pl.when(g == 0)
    def _():
        for ref in st_scr + at_scr:
            ref[...] = jnp.zeros_like(ref)

    @pl.when(jc == 0)
    def _():
        acc_scr[...] = jnp.zeros_like(acc_scr)

    def stages(slot_a, slot_b):
        n_chunks = te // MXU_CHUNK_ROWS

        def value_chunk(q):
            ms = slice(q * MXU_CHUNK_ROWS, (q + 1) * MXU_CHUNK_ROWS)
            acc_scr[ms, :] += jnp.dot(vt_ref[0, ms, :], at_scr[slot_a][...], preferred_element_type=F32)

        def score_chunk(q):
            ms = slice(q * MXU_CHUNK_ROWS, (q + 1) * MXU_CHUNK_ROWS)
            st_scr[slot_a][ms, :] = jnp.dot(u_ref[ms, :], ht_ref[0], preferred_element_type=F32)

        rg_rows = 4 * SUBLANES
        al_group = 4
        col_tiles = 1
        tiles = [(rg, c0, al0)
                 for rg in range(LANES // rg_rows)
                 for c0 in range(0, tm // LANES, col_tiles)
                 for al0 in range(0, te // LANES, al_group)]
        assert acc_scr.shape[0] == te
        per_chunk = len(tiles) // (2 * n_chunks)
        for t, (rg, c0, al0) in enumerate(tiles):
            if t % per_chunk == 0:
                q = t // per_chunk
                (value_chunk if q % 2 == 0 else score_chunk)(q // 2)
            brows = slice(rg * rg_rows, (rg + 1) * rg_rows)
            lanes = [slice((c0 + ct) * LANES, (c0 + ct + 1) * LANES) for ct in range(col_tiles)]
            gate = [[jnp.zeros((rg_rows, LANES), F32) for _ in lanes] for _ in range(al_group)]
            for hh in range(PEER_HEADS):
                for ct, ln in enumerate(lanes):
                    s2t = s2_ref[0, hh, brows, ln]
                    e2t = e2_ref[0, hh, brows, ln]
                    for k in range(al_group):
                        al = al0 + k
                        c = c_ref[hh, 0, al:al + 1, ln]
                        e1 = e1_ref[hh, 0, al:al + 1, ln]
                        gate[k][ct] = gate[k][ct] + jnp.where(s2t >= c, e2t, 0.0) * e1
            for k in range(al_group):
                erows = slice((al0 + k) * LANES + rg * rg_rows, (al0 + k) * LANES + (rg + 1) * rg_rows)
                for ct, ln in enumerate(lanes):
                    sv = st_scr[slot_b][erows, ln]
                    act = sv * (1.0 + lax.erf(sv * (2.0 ** -0.5)))
                    at_scr[slot_b][erows, ln] = (gate[k][ct] * act).astype(BF16)

    for parity in range(2):
        @pl.when(g % 2 == parity)
        def _():
            stages(parity, 1 - parity)

    @pl.when((jc == n_exp_blocks - 1) & (g >= 2))
    def _():
        o_ref[...] = s_ref[...] + mod_ref[0][5:6] * acc_scr[...].T


def _peer_experts(ht, u, vt, c, e1, s2, e2, s, modv, rows, *, n_lat_rows, n_per_batch):
    d = s.shape[1]
    n_exp = u.shape[0]
    n_batch = modv.shape[0] - 1
    tm, te = PEER_TM, PEER_TE
    nlat_blocks = n_lat_rows // tm
    bpb = n_per_batch // tm
    nj = n_exp // te
    n_pairs = (rows // tm) * nj

    def pair(g, lag):
        p = jnp.clip(g - lag, 0, n_pairs - 1)
        return p // nj, p % nj

    assert SCORE_TM == tm
    hk_spec = pl.BlockSpec((1, PEER_HEADS, PEER_KEYS, tm), lambda g: (pair(g, 1)[0], 0, 0, 0))
    hg_spec = pl.BlockSpec((PEER_HEADS, 1, SUBLANES, tm), lambda g: (0, pair(g, 1)[1], 0, pair(g, 1)[0]))
    kern = functools.partial(_peer_expert_kernel, n_exp_blocks=nj)
    return pl.pallas_call(
        kern,
        grid=(n_pairs + 2,),
        in_specs=[pl.BlockSpec((1, d, tm), lambda g: (pair(g, 0)[0], 0, 0)),
                  pl.BlockSpec((te, d), lambda g: (pair(g, 0)[1], 0)),
                  pl.BlockSpec((1, d, te), lambda g: (pair(g, 2)[1], 0, 0)),
                  hg_spec, hg_spec, hk_spec, hk_spec,
                  pl.BlockSpec((tm, d), lambda g: (pair(g, 2)[0], 0)),
                  pl.BlockSpec((1, 8, d),
                               lambda g: (_mod_index(pair(g, 2)[0], nlat_blocks, bpb, n_batch), 0, 0))],
        out_specs=pl.BlockSpec((tm, d), lambda g: (pair(g, 2)[0], 0)),
        out_shape=jax.ShapeDtypeStruct((rows, d), F32),
        scratch_shapes=[pltpu.VMEM((d, tm), F32), pltpu.VMEM((te, tm), F32), pltpu.VMEM((te, tm), F32),
                        pltpu.VMEM((te, tm), BF16), pltpu.VMEM((te, tm), BF16)],
        compiler_params=_cparams(("arbitrary",)),
        name="peer_experts",
    )(ht, u, vt, c, e1, s2, e2, s, modv)


def _peer(s, modv, g_ffn, w_q, keys, u_tab, v_tab, rows, *, n_lat_rows, n_per_batch):
    geo = dict(n_lat_rows=n_lat_rows, n_per_batch=n_per_batch)
    ht, c, e1, s2, e2 = _peer_scores(s, modv, g_ffn, w_q.T.astype(BF16), keys.astype(BF16), rows, **geo)
    n_exp, d = v_tab.shape
    vt = v_tab.reshape(n_exp // PEER_TE, PEER_TE, d).transpose(0, 2, 1).astype(BF16)
    return _peer_experts(ht, u_tab.astype(BF16), vt, c, e1, s2, e2, s, modv, rows, **geo)


def kernel(x, c, ctx, c_ctx, ada_w, ada_b, norm_mix, norm_ffn, mla_w_down, mla_g_cq, mla_g_ckv, mla_w_uq, mla_w_ukv, mla_g_q, mla_g_k, mla_w_o, diff_w_qkv, diff_g_q, diff_g_k, diff_lambda, diff_g_sub, diff_w_o, swa_w_qkv, swa_g_q, swa_g_k, swa_sink, swa_w_o, peer_w_q, peer_keys, peer_u, peer_v):
    n_batch, n_lat, d = x.shape
    n_ctx = ctx.shape[1]
    depth = ada_w.shape[0]
    rows_lat = n_batch * n_lat
    rows_all = rows_lat + n_batch * n_ctx
    assert n_lat % 512 == 0 and n_ctx % 256 == 0 and (n_batch * n_ctx) % PEER_TM == 0
    assert n_lat % GRID_W == 0 and rows_lat % n_ctx == 0 and n_lat >= 3 * Q_BLOCK

    s = jnp.concatenate([x.reshape(rows_lat, d), ctx.reshape(n_batch * n_ctx, d)], axis=0)
    cc = jnp.zeros((16, d), F32).at[:n_batch].set(c).at[n_batch].set(c_ctx)
    mods = _ada_mods(cc, ada_w, ada_b)
    geo = dict(n_lat_rows=rows_lat, n_per_batch=n_lat)

    for i in range(depth):
        last = i == depth - 1
        kind, j = i % 3, i // 3
        modv = jnp.zeros((n_batch + 1, 8, d), F32).at[:, :6, :].set(
            mods[i, :n_batch + 1].reshape(n_batch + 1, 6, d))
        dims = dict(n_batch=n_batch, n_lat=n_lat, n_ctx=n_ctx, need_ctx=not last)
        if kind == 0:
            p = (mla_w_down[j], mla_g_cq[j], mla_g_ckv[j], mla_w_uq[j], mla_w_ukv[j], mla_g_q[j], mla_g_k[j])
            o, wo = _mla_mixer(s, modv, norm_mix[i], p, mla_w_o[j], **dims)
        elif kind == 1:
            lam_init = 0.8 - 0.6 * math.exp(-0.3 * i)
            o, wo = _diff_mixer(s, modv, norm_mix[i], diff_w_qkv[j], diff_g_q[j], diff_g_k[j],
                                diff_lambda[j], diff_g_sub[j], diff_w_o[j], lam_init, **dims)
        else:
            o, wo = _swa_mixer(s, modv, norm_mix[i], swa_w_qkv[j], swa_g_q[j], swa_g_k[j],
                               swa_sink[j], swa_w_o[j], **dims)
        rows = rows_lat if last else rows_all
        s = _proj_resid(o, wo, s, modv, rows, gate_row=2, **geo)
        s = _peer(s, modv, norm_ffn[i], peer_w_q[i], peer_keys[i], peer_u[i], peer_v[i], rows, **geo)
    return s.reshape(n_batch, n_lat, d)
```

```python
import functools
import math

import jax
import jax.numpy as jnp
from jax import lax
from jax.experimental import pallas as pl
from jax.experimental.pallas import tpu as pltpu

F32 = jnp.float32
BF16 = jnp.bfloat16

LANES = 128
EPS = 1e-6
NEG_INF = -1e30
NEG_BIG = -3.0e38
ROPE_THETA = 10000.0
GRID_W = 64
Q_BLOCK = 128
WINDOW = 128

MLA_HEADS, MLA_NOPE, MLA_ROPE, MLA_V = 16, 64, 32, 64
MLA_QK = MLA_NOPE + MLA_ROPE
MLA_Q_LORA, MLA_KV_LORA = 768, 256
DIFF_HEADS, DIFF_HD = 8, 64
SWA_HEADS, SWA_KV_HEADS, SWA_HD = 16, 4, 64
SWA_GROUP = SWA_HEADS // SWA_KV_HEADS
PEER_HEADS, PEER_KEYS, PEER_TOPK = 8, 128, 16
PEER_EXPERTS = PEER_KEYS * PEER_KEYS

VMEM_LIMIT = 56 * 1024 * 1024

TM = 512
SCORE_TM = 512
TQ = 256
PEER_TM = 512
SUBLANES = 8
PEER_TE = SUBLANES * LANES
MXU_CHUNK_ROWS = 128


def _cparams(sem):
    return pltpu.CompilerParams(dimension_semantics=sem, vmem_limit_bytes=VMEM_LIMIT)


def _ada_kernel(c_ref, w_ref, b_ref, o_ref):
    c = c_ref[...]
    sc = c * (1.0 / (1.0 + jnp.exp(-c)))
    o_ref[0] = jnp.dot(sc, w_ref[0], preferred_element_type=F32,
                       precision=lax.Precision.HIGHEST) + b_ref[0]


def _ada_mods(cc, ada_w, ada_b):
    depth, d, d6 = ada_w.shape
    rows = cc.shape[0]
    tn = 1536
    return pl.pallas_call(
        _ada_kernel,
        grid=(depth, d6 // tn),
        in_specs=[pl.BlockSpec((rows, d), lambda i, n: (0, 0)),
                  pl.BlockSpec((1, d, tn), lambda i, n: (i, 0, n)),
                  pl.BlockSpec((1, 1, tn), lambda i, n: (i, 0, n))],
        out_specs=pl.BlockSpec((1, rows, tn), lambda i, n: (i, 0, n)),
        out_shape=jax.ShapeDtypeStruct((depth, rows, d6), F32),
        compiler_params=_cparams(("arbitrary", "arbitrary")),
        name="ada_mods",
    )(cc, ada_w, ada_b.reshape(depth, 1, d6))


def _norm_mod(x, g, mod, shift_row, scale_row):
    ms = jnp.mean(x * x, axis=-1, keepdims=True)
    y = x * lax.rsqrt(ms + EPS) * g
    if mod is not None:
        y = y * (1.0 + mod[scale_row:scale_row + 1]) + mod[shift_row:shift_row + 1]
    return y


def _nm_kernel(x_ref, g_ref, mod_ref, w_ref, o_ref, h_scr, *, use_mod, shift_row, scale_row):
    @pl.when(pl.program_id(1) == 0)
    def _():
        mod = mod_ref[0] if use_mod else None
        h_scr[...] = _norm_mod(x_ref[...], g_ref[...], mod, shift_row, scale_row).astype(BF16)

    o_ref[...] = jnp.dot(h_scr[...], w_ref[...], preferred_element_type=F32).astype(o_ref.dtype)


def _mod_index(i, nlat_blocks, blocks_per_batch, n_batch):
    return jnp.where(i < nlat_blocks, i // blocks_per_batch, n_batch)


def _nm_matmul(x, kblk, k, g, modv, rows, w, *, n_lat_rows, n_per_batch, out_dtype=F32,
               use_mod=True, shift_row=0, scale_row=1, name="nm_matmul"):
    nout = w.shape[1]
    tn = nout
    for cand in (2048, 1536, 1280, 1024):
        if nout > 2048 and nout % cand == 0:
            tn = cand
            break
    n_batch = modv.shape[0] - 1
    nlat_blocks = n_lat_rows // TM
    bpb = n_per_batch // TM
    kern = functools.partial(_nm_kernel, use_mod=use_mod, shift_row=shift_row, scale_row=scale_row)
    return pl.pallas_call(
        kern,
        grid=(rows // TM, nout // tn),
        in_specs=[pl.BlockSpec((TM, k), lambda i, n: (i, kblk)),
                  pl.BlockSpec((1, k), lambda i, n: (0, 0)),
                  pl.BlockSpec((1, 8, k), lambda i, n: (_mod_index(i, nlat_blocks, bpb, n_batch), 0, 0)),
                  pl.BlockSpec((k, tn), lambda i, n: (0, n))],
        out_specs=pl.BlockSpec((TM, tn), lambda i, n: (i, n)),
        out_shape=jax.ShapeDtypeStruct((rows, nout), out_dtype),
        scratch_shapes=[pltpu.VMEM((TM, k), BF16)],
        compiler_params=_cparams(("parallel", "arbitrary")),
        name=name,
    )(x, g.reshape(1, k), modv if modv.shape[-1] == k else jnp.zeros((n_batch + 1, 8, k), F32), w)


def _proj_resid_kernel(a_ref, w_ref, s_ref, mod_ref, o_ref, *, gate_row):
    y = jnp.dot(a_ref[...].astype(BF16), w_ref[...], preferred_element_type=F32)
    o_ref[...] = s_ref[...] + mod_ref[0][gate_row:gate_row + 1] * y


def _proj_resid(a, w, s, modv, rows, *, n_lat_rows, n_per_batch, gate_row, name="proj_resid"):
    ka = a.shape[1]
    d = w.shape[1]
    n_batch = modv.shape[0] - 1
    nlat_blocks = n_lat_rows // TM
    bpb = n_per_batch // TM
    return pl.pallas_call(
        functools.partial(_proj_resid_kernel, gate_row=gate_row),
        grid=(rows // TM,),
        in_specs=[pl.BlockSpec((TM, ka), lambda i: (i, 0)),
                  pl.BlockSpec((ka, d), lambda i: (0, 0)),
                  pl.BlockSpec((TM, d), lambda i: (i, 0)),
                  pl.BlockSpec((1, 8, d), lambda i: (_mod_index(i, nlat_blocks, bpb, n_batch), 0, 0))],
        out_specs=pl.BlockSpec((TM, d), lambda i: (i, 0)),
        out_shape=jax.ShapeDtypeStruct((rows, d), F32),
        compiler_params=_cparams(("parallel",)),
        name=name,
    )(a, w, s, modv)


def _rope_tables(n_lat, extra_rows, segments, nf):
    pos = jnp.arange(n_lat, dtype=jnp.int32)
    row = (pos // GRID_W).astype(F32)
    col = (pos % GRID_W).astype(F32)
    inv = ROPE_THETA ** (-jnp.arange(nf, dtype=F32) / nf)
    cos = jnp.ones((n_lat, LANES), F32)
    sa = jnp.zeros((n_lat, LANES), F32)
    sb = jnp.zeros((n_lat, LANES), F32)
    for first, use_row in segments:
        ang = (row if use_row else col)[:, None] * inv[None, :]
        c, s = jnp.cos(ang), jnp.sin(ang)
        cos = cos.at[:, first:first + nf].set(c).at[:, first + nf:first + 2 * nf].set(c)
        sa = sa.at[:, first:first + nf].set(-s)
        sb = sb.at[:, first + nf:first + 2 * nf].set(s)
    pad = lambda t, v: jnp.concatenate([t, jnp.full((extra_rows, LANES), v, F32)], axis=0)
    return pad(cos, 1.0), pad(sa, 0.0), pad(sb, 0.0)


def _head_norm(x, g, d):
    ssq = jnp.sum(x * x, axis=-1, keepdims=True) * (1.0 / d)
    return x * lax.rsqrt(ssq + EPS) * g


def _rope(y, cos, sa, sb, nf):
    return y * cos + pltpu.roll(y, LANES - nf, 1) * sa + pltpu.roll(y, nf, 1) * sb


def _with_ones_lane(v, lane):
    return jnp.where(lax.broadcasted_iota(jnp.int32, v.shape, 1) == lane, jnp.ones_like(v), v)


def _row_block(b, qi, nlat_q, nctx_q, n_batch):
    return jnp.where(qi < nlat_q, b * nlat_q + qi, n_batch * nlat_q + b * nctx_q + (qi - nlat_q))


def _prep_keys(k_scr, v_scr, n_lat, n_ctx, load_k_lat, load_k_ctx, load_v_lat, load_v_ctx,
               gk, tabs, d, nf):
    cos_ref, sa_ref, sb_ref = tabs
    ch = 256

    def lat_body(c, carry):
        r = pl.ds(pl.multiple_of(c * ch, ch), ch)
        kk = _head_norm(load_k_lat(r).astype(F32), gk, d)
        kk = _rope(kk, cos_ref[r, :], sa_ref[r, :], sb_ref[r, :], nf)
        k_scr[r, :] = kk.astype(BF16)
        if v_scr is not None:
            v_scr[r, :] = load_v_lat(r).astype(BF16)
        return carry

    lax.fori_loop(0, n_lat // ch, lat_body, 0, unroll=2)

    def ctx_body(c, carry):
        r = pl.ds(pl.multiple_of(c * ch, ch), ch)
        ro = pl.ds(pl.multiple_of(n_lat + c * ch, ch), ch)
        k_scr[ro, :] = _head_norm(load_k_ctx(r).astype(F32), gk, d).astype(BF16)
        if v_scr is not None:
            v_scr[ro, :] = load_v_ctx(r).astype(BF16)
        return carry

    lax.fori_loop(0, n_ctx // ch, ctx_body, 0)


def _nt_dot(a, b):
    return lax.dot_general(a, b, (((1,), (1,)), ((), ())), preferred_element_type=F32)


LOG2E = math.log2(math.e)
CHAIN_ROWS = 128
MLA_HEADS_PER_STEP = 4
HEADS_PER_STEP = 2


def _row_chains(n_rows):
    return [slice(r, r + CHAIN_ROWS) for r in range(0, n_rows, CHAIN_ROWS)]


def _emit_pipelined(chains):
    pending, active = list(chains), []
    while pending or active:
        if pending:
            active.append(pending.pop(0))
        for ch in list(active):
            try:
                next(ch)
            except StopIteration:
                active.remove(ch)


def _exp2_scores(q, k):
    s = _nt_dot(q, k)
    p = jnp.exp2(s - jnp.max(s, axis=-1, keepdims=True))
    return p, jnp.sum(p, axis=-1, keepdims=True)


def _mla_attn_kernel(q_ref, kl_ref, kc_ref, krl_ref, krc_ref, vl_ref, vc_ref,
                     cq_ref, saq_ref, sbq_ref, ck_ref, sak_ref, sbk_ref, gq_ref, gk_ref,
                     o_ref, k_scr, v_scr, *, n_lat, n_ctx, nlat_q, scale):
    qi = pl.program_id(2)
    nf = MLA_ROPE // 4
    heads = [slice(hd * LANES, (hd + 1) * LANES) for hd in range(q_ref.shape[1] // LANES)]

    @pl.when(qi == 0)
    def _():
        for hd, ln in enumerate(heads):
            _prep_keys(k_scr.at[hd], v_scr.at[hd], n_lat, n_ctx,
                       lambda r, ln=ln: kl_ref[r, ln].astype(F32) + krl_ref[r, :],
                       lambda r, ln=ln: kc_ref[r, ln].astype(F32) + krc_ref[r, :],
                       lambda r, ln=ln: _with_ones_lane(vl_ref[r, ln], MLA_V),
                       lambda r, ln=ln: _with_ones_lane(vc_ref[r, ln], MLA_V),
                       gk_ref[...], (ck_ref, sak_ref, sbk_ref), MLA_QK, nf)

    def attend(keys):
        def chain(hd, ln, rows):
            q = _head_norm(q_ref[rows, ln].astype(F32), gq_ref[...], MLA_QK)
            q = _rope(q, cq_ref[rows, :], saq_ref[rows, :], sbq_ref[rows, :], nf) * (scale * LOG2E)
            s = _nt_dot(q.astype(BF16), k_scr[hd, keys, :])
            yield
            p = jnp.exp2(s - jnp.max(s, axis=-1, keepdims=True)).astype(BF16)
            yield
            o = jnp.dot(p, v_scr[hd, keys, :], preferred_element_type=F32)
            o_ref[rows, ln] = (o * (1.0 / o[:, MLA_V:MLA_V + 1])).astype(o_ref.dtype)

        _emit_pipelined([chain(hd, ln, rows) for rows in _row_chains(q_ref.shape[0])
                         for hd, ln in enumerate(heads)])

    @pl.when(qi < nlat_q)
    def _():
        attend(slice(0, n_lat + n_ctx))

    @pl.when(qi >= nlat_q)
    def _():
        attend(slice(n_lat, n_lat + n_ctx))


def _mla_attention(q, kv, lat, g_q, g_k, *, n_batch, n_lat, n_ctx, need_ctx):
    nlat_q, nctx_q = n_lat // TQ, n_ctx // TQ
    nq = nlat_q + (nctx_q if need_ctx else 0)
    rows = n_batch * (n_lat + (n_ctx if need_ctx else 0))
    nf = MLA_ROPE // 4
    tabs = _rope_tables(n_lat, TQ, ((MLA_NOPE, True), (MLA_NOPE + 2 * nf, False)), nf)
    h = MLA_HEADS
    ctx0 = n_batch * n_lat // n_ctx
    kr_blk = (MLA_Q_LORA + MLA_KV_LORA) // LANES
    gq = jnp.zeros((1, LANES), F32).at[0, :MLA_QK].set(g_q)
    gk = jnp.zeros((1, LANES), F32).at[0, :MLA_QK].set(g_k)
    hps = MLA_HEADS_PER_STEP
    hw = hps * LANES
    qspec = pl.BlockSpec((TQ, hw), lambda b, hh, qi: (_row_block(b, qi, nlat_q, nctx_q, n_batch), hh))
    tq_spec = pl.BlockSpec((TQ, LANES), lambda b, hh, qi: (jnp.minimum(qi, nlat_q), 0))
    tk_spec = pl.BlockSpec((n_lat, LANES), lambda b, hh, qi: (0, 0))
    vec = pl.BlockSpec((1, LANES), lambda b, hh, qi: (0, 0))
    kern = functools.partial(_mla_attn_kernel, n_lat=n_lat, n_ctx=n_ctx, nlat_q=nlat_q,
                             scale=MLA_QK ** -0.5)
    return pl.pallas_call(
        kern,
        grid=(n_batch, h // hps, nq),
        in_specs=[qspec,
                  pl.BlockSpec((n_lat, hw), lambda b, hh, qi: (b, hh)),
                  pl.BlockSpec((n_ctx, hw), lambda b, hh, qi: (ctx0 + b, hh)),
                  pl.BlockSpec((n_lat, LANES), lambda b, hh, qi: (b, kr_blk)),
                  pl.BlockSpec((n_ctx, LANES), lambda b, hh, qi: (ctx0 + b, kr_blk)),
                  pl.BlockSpec((n_lat, hw), lambda b, hh, qi: (b, h // hps + hh)),
                  pl.BlockSpec((n_ctx, hw), lambda b, hh, qi: (ctx0 + b, h // hps + hh)),
                  tq_spec, tq_spec, tq_spec, tk_spec, tk_spec, tk_spec, vec, vec],
        out_specs=qspec,
        out_shape=jax.ShapeDtypeStruct((rows, h * LANES), BF16),
        scratch_shapes=[pltpu.VMEM((hps, n_lat + n_ctx, LANES), BF16),
                        pltpu.VMEM((hps, n_lat + n_ctx, LANES), BF16)],
        compiler_params=_cparams(("parallel", "parallel", "arbitrary")),
        name="mla_attention",
    )(q, kv, kv, lat, lat, kv, kv, *tabs, *tabs, gq, gk)


def _pad_cols(w, groups, width, first=0):
    k = w.shape[0]
    w3 = w.reshape(k, groups, width)
    out = jnp.zeros((k, groups, LANES), w.dtype).at[:, :, first:first + width].set(w3)
    return out.reshape(k, groups * LANES)


def _pad_rows(w, groups, width):
    d = w.shape[1]
    w3 = w.reshape(groups, width, d)
    return jnp.zeros((groups, LANES, d), w.dtype).at[:, :width, :].set(w3).reshape(groups * LANES, d)


def _mla_mixer(s, modv, g_mix, p, w_o, *, n_batch, n_lat, n_ctx, need_ctx):
    w_down, g_cq, g_ckv, w_uq, w_ukv, g_q, g_k = p
    d = s.shape[1]
    rows_all = n_batch * (n_lat + n_ctx)
    geo = dict(n_lat_rows=n_batch * n_lat, n_per_batch=n_lat)
    wd = jnp.concatenate([w_down[:, :MLA_Q_LORA + MLA_KV_LORA],
                          _pad_cols(w_down[:, MLA_Q_LORA + MLA_KV_LORA:], 1, MLA_ROPE, MLA_NOPE)], axis=1)
    lat = _nm_matmul(s, 0, d, g_mix, modv, rows_all, wd.astype(BF16), name="mla_down", **geo)
    wq = _pad_cols(w_uq, MLA_HEADS, MLA_QK).astype(BF16)
    q = _nm_matmul(lat, 0, MLA_Q_LORA, g_cq, modv, rows_all, wq, use_mod=False, name="mla_uq",
                   out_dtype=BF16, **geo)
    wkv3 = w_ukv.reshape(MLA_KV_LORA, MLA_HEADS, MLA_NOPE + MLA_V)
    wkv = jnp.concatenate([_pad_cols(wkv3[:, :, :MLA_NOPE].reshape(MLA_KV_LORA, -1), MLA_HEADS, MLA_NOPE),
                           _pad_cols(wkv3[:, :, MLA_NOPE:].reshape(MLA_KV_LORA, -1), MLA_HEADS, MLA_V)],
                          axis=1).astype(BF16)
    kv = _nm_matmul(lat, MLA_Q_LORA // MLA_KV_LORA, MLA_KV_LORA, g_ckv, modv, rows_all, wkv,
                    use_mod=False, name="mla_ukv", out_dtype=BF16, **geo)
    o = _mla_attention(q, kv, lat, g_q, g_k, n_batch=n_batch, n_lat=n_lat, n_ctx=n_ctx, need_ctx=need_ctx)
    wo = _pad_rows(w_o, MLA_HEADS, MLA_V).astype(BF16)
    return o, wo


def _diff_attn_kernel(q_ref, kl_ref, kc_ref, vl_ref, vc_ref,
                      cq_ref, saq_ref, sbq_ref, ck_ref, sak_ref, sbk_ref, gq_ref, gk_ref,
                      lam_ref, gsub_ref, o_ref, k_scr, v_scr,
                      *, n_lat, n_ctx, nlat_q, scale, lam_init):
    qi = pl.program_id(2)
    nf = DIFF_HD // 4
    tabs = (ck_ref, sak_ref, sbk_ref)
    tile = lambda t: slice(t * LANES, (t + 1) * LANES)
    n_heads = o_ref.shape[1] // LANES

    @pl.when(qi == 0)
    def _():
        for hd in range(n_heads):
            ln = tile(hd)
            for sub in range(2):
                kt = tile(2 * hd + sub)
                _prep_keys(k_scr.at[2 * hd + sub], v_scr.at[hd] if sub == 0 else None, n_lat, n_ctx,
                           lambda r, kt=kt: kl_ref[r, kt], lambda r, kt=kt: kc_ref[r, kt],
                           lambda r, ln=ln: vl_ref[r, ln], lambda r, ln=ln: vc_ref[r, ln],
                           gk_ref[...], tabs, DIFF_HD, nf)

    lp = lam_ref[...]
    lam = (jnp.exp(jnp.sum(lp[0:1] * lp[1:2], axis=-1, keepdims=True))
           - jnp.exp(jnp.sum(lp[2:3] * lp[3:4], axis=-1, keepdims=True)) + lam_init)

    def attend(keys):
        def scores(t, rows):
            q = _head_norm(q_ref[rows, tile(t)].astype(F32), gq_ref[...], DIFF_HD)
            q = _rope(q, cq_ref[rows, :], saq_ref[rows, :], sbq_ref[rows, :], nf) * (scale * LOG2E)
            return _nt_dot(q.astype(BF16), k_scr[t, keys, :])

        def chain(hd, rows):
            s0 = scores(2 * hd, rows)
            s1 = scores(2 * hd + 1, rows)
            yield
            p0 = jnp.exp2(s0 - jnp.max(s0, axis=-1, keepdims=True))
            p1 = jnp.exp2(s1 - jnp.max(s1, axis=-1, keepdims=True))
            l0 = jnp.sum(p0, axis=-1, keepdims=True)
            l1 = jnp.sum(p1, axis=-1, keepdims=True)
            a = (p0 * (1.0 / l0) - p1 * (lam / l1)).astype(BF16)
            yield
            o = jnp.dot(a, v_scr[hd, keys, :], preferred_element_type=F32)
            o = _head_norm(o, gsub_ref[...], 2 * DIFF_HD) * (1.0 - lam_init)
            o_ref[rows, tile(hd)] = o.astype(o_ref.dtype)

        _emit_pipelined([chain(hd, rows) for rows in _row_chains(q_ref.shape[0])
                         for hd in range(n_heads)])

    @pl.when(qi < nlat_q)
    def _():
        attend(slice(0, n_lat + n_ctx))

    @pl.when(qi >= nlat_q)
    def _():
        attend(slice(n_lat, n_lat + n_ctx))


def _diff_attention(qkv, g_q, g_k, lam_p, g_sub, lam_init, *, n_batch, n_lat, n_ctx, need_ctx):
    nlat_q, nctx_q = n_lat // TQ, n_ctx // TQ
    nq = nlat_q + (nctx_q if need_ctx else 0)
    rows = n_batch * (n_lat + (n_ctx if need_ctx else 0))
    nf = DIFF_HD // 4
    tabs = _rope_tables(n_lat, TQ, ((0, True), (2 * nf, False)), nf)
    nh = DIFF_HEADS
    ctx0 = n_batch * n_lat // n_ctx
    gq = jnp.zeros((1, LANES), F32).at[0, :DIFF_HD].set(g_q)
    gk = jnp.zeros((1, LANES), F32).at[0, :DIFF_HD].set(g_k)
    lam8 = jnp.zeros((8, LANES), F32).at[:4, :DIFF_HD].set(lam_p.astype(F32))
    rb = lambda b, qi: _row_block(b, qi, nlat_q, nctx_q, n_batch)
    tq_spec = pl.BlockSpec((TQ, LANES), lambda b, hh, qi: (jnp.minimum(qi, nlat_q), 0))
    tk_spec = pl.BlockSpec((n_lat, LANES), lambda b, hh, qi: (0, 0))
    vec = pl.BlockSpec((1, LANES), lambda b, hh, qi: (0, 0))
    hps = HEADS_PER_STEP
    qk_w, v_w = 2 * hps * LANES, hps * LANES
    k_blk0 = 2 * nh // (2 * hps)
    v_blk0 = 4 * nh // hps
    kern = functools.partial(_diff_attn_kernel, n_lat=n_lat, n_ctx=n_ctx, nlat_q=nlat_q,
                             scale=DIFF_HD ** -0.5, lam_init=lam_init)
    n_all = n_lat + n_ctx
    return pl.pallas_call(
        kern,
        grid=(n_batch, nh // hps, nq),
        in_specs=[pl.BlockSpec((TQ, qk_w), lambda b, hh, qi: (rb(b, qi), hh)),
                  pl.BlockSpec((n_lat, qk_w), lambda b, hh, qi: (b, k_blk0 + hh)),
                  pl.BlockSpec((n_ctx, qk_w), lambda b, hh, qi: (ctx0 + b, k_blk0 + hh)),
                  pl.BlockSpec((n_lat, v_w), lambda b, hh, qi: (b, v_blk0 + hh)),
                  pl.BlockSpec((n_ctx, v_w), lambda b, hh, qi: (ctx0 + b, v_blk0 + hh)),
                  tq_spec, tq_spec, tq_spec, tk_spec, tk_spec, tk_spec, vec, vec,
                  pl.BlockSpec((8, LANES), lambda b, hh, qi: (0, 0)), vec],
        out_specs=pl.BlockSpec((TQ, v_w), lambda b, hh, qi: (rb(b, qi), hh)),
        out_shape=jax.ShapeDtypeStruct((rows, nh * LANES), BF16),
        scratch_shapes=[pltpu.VMEM((2 * hps, n_all, LANES), BF16), pltpu.VMEM((hps, n_all, LANES), BF16)],
        compiler_params=_cparams(("parallel", "parallel", "arbitrary")),
        name="diff_attention",
    )(qkv, qkv, qkv, qkv, qkv, *tabs, *tabs, gq, gk, lam8, g_sub.reshape(1, LANES))


def _diff_mixer(s, modv, g_mix, w_qkv, g_q, g_k, lam_p, g_sub, w_o, lam_init,
                *, n_batch, n_lat, n_ctx, need_ctx):
    d = s.shape[1]
    width = DIFF_HEADS * 2 * DIFF_HD
    rows_all = n_batch * (n_lat + n_ctx)
    w = jnp.concatenate([_pad_cols(w_qkv[:, :width], 2 * DIFF_HEADS, DIFF_HD),
                         _pad_cols(w_qkv[:, width:2 * width], 2 * DIFF_HEADS, DIFF_HD),
                         w_qkv[:, 2 * width:]], axis=1).astype(BF16)
    qkv = _nm_matmul(s, 0, d, g_mix, modv, rows_all, w, n_lat_rows=n_batch * n_lat,
                     n_per_batch=n_lat, name="diff_qkv", out_dtype=BF16)
    o = _diff_attention(qkv, g_q, g_k, lam_p, g_sub, lam_init,
                        n_batch=n_batch, n_lat=n_lat, n_ctx=n_ctx, need_ctx=need_ctx)
    return o, w_o.astype(BF16)


def _swa_attn_kernel(q_ref, kl_ref, kc_ref, vl_ref, vc_ref,
                     cq_ref, saq_ref, sbq_ref, ck_ref, sak_ref, sbk_ref, gq_ref, gk_ref, sink_ref,
                     o_ref, k_scr, v_scr, *, n_lat, n_ctx, nlat_q, scale):
    qi = pl.program_id(2)
    nf = SWA_HD // 4
    band = 3 * Q_BLOCK

    @pl.when(qi == 0)
    def _():
        _prep_keys(k_scr, v_scr, n_lat, n_ctx, lambda r: kl_ref[r, :], lambda r: kc_ref[r, :],
                   lambda r: _with_ones_lane(vl_ref[r, :], SWA_HD),
                   lambda r: _with_ones_lane(vc_ref[r, :], SWA_HD), gk_ref[...],
                   (ck_ref, sak_ref, sbk_ref), SWA_HD, nf)

    row_chains = _row_chains(q_ref.shape[0])
    reach = jnp.where(qi < nlat_q, WINDOW, -1)

    def chain(ci, rows, gi):
        qb = qi * len(row_chains) + ci
        k0 = pl.multiple_of(jnp.clip((qb - 1) * Q_BLOCK, 0, n_lat - band), Q_BLOCK)
        kpos = k0 + lax.broadcasted_iota(jnp.int32, (Q_BLOCK, band), 1)
        qpos = qb * Q_BLOCK + lax.broadcasted_iota(jnp.int32, (Q_BLOCK, band), 0)
        lanes = slice(gi * LANES, (gi + 1) * LANES)
        q = _head_norm(q_ref[rows, lanes].astype(F32), gq_ref[...], SWA_HD)
        q = _rope(q, cq_ref[rows, :], saq_ref[rows, :], sbq_ref[rows, :], nf) * (scale * LOG2E)
        q = q.astype(BF16)
        sb = _nt_dot(q, k_scr[pl.ds(k0, band), :])
        sc = _nt_dot(q, k_scr[n_lat:n_lat + n_ctx, :])
        yield
        sb = jnp.where(jnp.abs(kpos - qpos) <= reach, sb, NEG_INF)
        sink = sink_ref[gi][0:1, 0:1] * LOG2E
        m = jnp.maximum(jnp.maximum(jnp.max(sb, axis=-1, keepdims=True),
                                    jnp.max(sc, axis=-1, keepdims=True)), sink)
        pb = jnp.exp2(sb - m).astype(BF16)
        pc = jnp.exp2(sc - m).astype(BF16)
        p_sink = jnp.exp2(sink - m)
        yield
        o = (jnp.dot(pb, v_scr[pl.ds(k0, band), :], preferred_element_type=F32)
             + jnp.dot(pc, v_scr[n_lat:n_lat + n_ctx, :], preferred_element_type=F32))
        l = o[:, SWA_HD:SWA_HD + 1] + p_sink
        o_ref[rows, lanes] = (o * (1.0 / l)).astype(o_ref.dtype)

    _emit_pipelined([chain(ci, rows, gi) for ci, rows in enumerate(row_chains)
                     for gi in range(SWA_GROUP)])


def _swa_attention(qkv, g_q, g_k, sink, *, n_batch, n_lat, n_ctx, need_ctx):
    nlat_q, nctx_q = n_lat // TQ, n_ctx // TQ
    nq = nlat_q + (nctx_q if need_ctx else 0)
    rows = n_batch * (n_lat + (n_ctx if need_ctx else 0))
    nf = SWA_HD // 4
    tabs = _rope_tables(n_lat, TQ, ((0, True), (2 * nf, False)), nf)
    nh, nkv, grp = SWA_HEADS, SWA_KV_HEADS, SWA_GROUP
    ctx0 = n_batch * n_lat // n_ctx
    gq = jnp.zeros((1, LANES), F32).at[0, :SWA_HD].set(g_q)
    gk = jnp.zeros((1, LANES), F32).at[0, :SWA_HD].set(g_k)
    sink_t = jnp.broadcast_to(sink.astype(F32).reshape(nh, 1, 1), (nh, 8, LANES))
    rb = lambda b, qi: _row_block(b, qi, nlat_q, nctx_q, n_batch)
    tq_spec = pl.BlockSpec((TQ, LANES), lambda b, kh, qi: (jnp.minimum(qi, nlat_q), 0))
    tk_spec = pl.BlockSpec((n_lat, LANES), lambda b, kh, qi: (0, 0))
    vec = pl.BlockSpec((1, LANES), lambda b, kh, qi: (0, 0))
    kern = functools.partial(_swa_attn_kernel, n_lat=n_lat, n_ctx=n_ctx, nlat_q=nlat_q,
                             scale=SWA_HD ** -0.5)
    n_all = n_lat + n_ctx
    return pl.pallas_call(
        kern,
        grid=(n_batch, nkv, nq),
        in_specs=[pl.BlockSpec((TQ, grp * LANES), lambda b, kh, qi: (rb(b, qi), kh)),
                  pl.BlockSpec((n_lat, LANES), lambda b, kh, qi: (b, nh + kh)),
                  pl.BlockSpec((n_ctx, LANES), lambda b, kh, qi: (ctx0 + b, nh + kh)),
                  pl.BlockSpec((n_lat, LANES), lambda b, kh, qi: (b, nh + nkv + kh)),
                  pl.BlockSpec((n_ctx, LANES), lambda b, kh, qi: (ctx0 + b, nh + nkv + kh)),
                  tq_spec, tq_spec, tq_spec, tk_spec, tk_spec, tk_spec, vec, vec,
                  pl.BlockSpec((grp, 8, LANES), lambda b, kh, qi: (kh, 0, 0))],
        out_specs=pl.BlockSpec((TQ, grp * LANES), lambda b, kh, qi: (rb(b, qi), kh)),
        out_shape=jax.ShapeDtypeStruct((rows, nh * LANES), BF16),
        scratch_shapes=[pltpu.VMEM((n_all, LANES), BF16), pltpu.VMEM((n_all, LANES), BF16)],
        compiler_params=_cparams(("parallel", "parallel", "arbitrary")),
        name="swa_attention",
    )(qkv, qkv, qkv, qkv, qkv, *tabs, *tabs, gq, gk, sink_t)


def _swa_mixer(s, modv, g_mix, w_qkv, g_q, g_k, sink, w_o, *, n_batch, n_lat, n_ctx, need_ctx):
    d = s.shape[1]
    qw, kw = SWA_HEADS * SWA_HD, SWA_KV_HEADS * SWA_HD
    rows_all = n_batch * (n_lat + n_ctx)
    w = jnp.concatenate([_pad_cols(w_qkv[:, :qw], SWA_HEADS, SWA_HD),
                         _pad_cols(w_qkv[:, qw:qw + kw], SWA_KV_HEADS, SWA_HD),
                         _pad_cols(w_qkv[:, qw + kw:], SWA_KV_HEADS, SWA_HD)], axis=1).astype(BF16)
    qkv = _nm_matmul(s, 0, d, g_mix, modv, rows_all, w, n_lat_rows=n_batch * n_lat,
                     n_per_batch=n_lat, name="swa_qkv", out_dtype=BF16)
    o = _swa_attention(qkv, g_q, g_k, sink, n_batch=n_batch, n_lat=n_lat, n_ctx=n_ctx, need_ctx=need_ctx)
    return o, _pad_rows(w_o, SWA_HEADS, SWA_HD).astype(BF16)


def _merge_exchange_pairs(n):
    pairs = []
    t = max(1, (n - 1).bit_length())
    p = 1 << (t - 1)
    while p > 0:
        q, r, d = 1 << (t - 1), 0, p
        while d > 0:
            pairs += [(i, i + d) for i in range(n - d) if (i & p) == r]
            d, q, r = q - p, q >> 1, p
        p >>= 1
    return pairs


def _top_values(s, count):
    n = s.shape[0] // SUBLANES
    v = [s[j * SUBLANES:(j + 1) * SUBLANES, :] for j in range(n)]
    for i, j in _merge_exchange_pairs(n):
        v[i], v[j] = jnp.maximum(v[i], v[j]), jnp.minimum(v[i], v[j])
    vals = []
    for r in range(count):
        m = jnp.max(v[0], axis=0, keepdims=True)
        vals.append(m)
        depth = min(n, count - 1 - r)
        head = v[0] == m
        for j in range(depth):
            nxt = v[j + 1] if j + 1 < n else jnp.full_like(v[j], NEG_BIG)
            v[j] = jnp.where(head, nxt, v[j])
    return vals


def _peer_score_kernel(s_ref, g_ref, mod_ref, wq_ref, keys_ref,
                       ht_ref, c_ref, e1_ref, s2_ref, e2_ref, qt_scr):
    k = PEER_TOPK
    h = _norm_mod(s_ref[...], g_ref[...], mod_ref[0], 3, 4)
    ht = h.T.astype(BF16)
    ht_ref[0] = ht
    qt_scr[...] = jnp.dot(wq_ref[...], ht, preferred_element_type=F32)

    tm = ht.shape[1]
    group = 2

    def group_body(gidx, carry):
        heads = [gidx * group + i for i in range(group)]
        scores = []
        for hh in heads:
            for half in range(2):
                r = pl.ds(pl.multiple_of((hh * 2 + half) * LANES, LANES), LANES)
                scores.append(jnp.dot(keys_ref[hh, half], qt_scr[r, :].astype(BF16),
                                      preferred_element_type=F32))
        tops = _top_values(jnp.concatenate(scores, axis=1), k + 1)
        cands = []
        for i in range(group):
            v1 = [t[:, (2 * i) * tm:(2 * i + 1) * tm] for t in tops]
            v2 = [t[:, (2 * i + 1) * tm:(2 * i + 2) * tm] for t in tops]
            top8 = jnp.concatenate(v2[:8], axis=0)
            pieces = [v1[0] + jnp.concatenate(v2[:k], axis=0)]
            pieces += [v1[j] + top8 for j in range(1, 8)]
            pieces += [jnp.concatenate(v1[8:k], axis=0) + v2[0]]
            pieces += [jnp.concatenate([v1[0] + v2[k], v1[k] + v2[0]]
                                       + [jnp.full_like(v1[0], NEG_BIG)] * 6, axis=0)]
            cands.append(jnp.concatenate(pieces, axis=0))
        best = _top_values(jnp.concatenate(cands, axis=1), k + 1)
        grp = (PEER_KEYS // SUBLANES, SUBLANES, tm)
        for i, hh in enumerate(heads):
            ms = [t[:, i * tm:(i + 1) * tm] for t in best]
            s1, s2 = scores[2 * i], scores[2 * i + 1]
            m1, m2 = tops[0][:, (2 * i) * tm:(2 * i + 1) * tm], tops[0][:, (2 * i + 1) * tm:(2 * i + 2) * tm]
            thr = 0.5 * (ms[k - 1] + ms[k])
            z = jnp.ones_like(ms[0])
            for r in range(1, k):
                z = z + jnp.exp(ms[r] - ms[0])
            c_ref[hh] = (thr - s1).reshape(grp)
            e1_ref[hh] = (jnp.exp(s1 - m1) * (0.5 / z)).reshape(grp)
            s2_ref[0, hh] = s2
            e2_ref[0, hh] = jnp.exp(s2 - m2)
        return carry

    lax.fori_loop(0, PEER_HEADS // group, group_body, 0)


def _peer_scores(s, modv, g_ffn, wq_t, keys, rows, *, n_lat_rows, n_per_batch):
    d = s.shape[1]
    n_batch = modv.shape[0] - 1
    tm = SCORE_TM
    nlat_blocks = n_lat_rows // tm
    bpb = n_per_batch // tm
    nq = wq_t.shape[0]
    hk = (rows // tm, PEER_HEADS, PEER_KEYS, tm)
    hk_spec = pl.BlockSpec((1, PEER_HEADS, PEER_KEYS, tm), lambda i: (i, 0, 0, 0))
    hg = (PEER_HEADS, PEER_KEYS // SUBLANES, SUBLANES, rows)
    hg_spec = pl.BlockSpec((PEER_HEADS, PEER_KEYS // SUBLANES, SUBLANES, tm), lambda i: (0, 0, 0, i))
    return pl.pallas_call(
        _peer_score_kernel,
        grid=(rows // tm,),
        in_specs=[pl.BlockSpec((tm, d), lambda i: (i, 0)),
                  pl.BlockSpec((1, d), lambda i: (0, 0)),
                  pl.BlockSpec((1, 8, d), lambda i: (_mod_index(i, nlat_blocks, bpb, n_batch), 0, 0)),
                  pl.BlockSpec((nq, d), lambda i: (0, 0)),
                  pl.BlockSpec((PEER_HEADS, 2, PEER_KEYS, LANES), lambda i: (0, 0, 0, 0))],
        out_specs=[pl.BlockSpec((1, d, tm), lambda i: (i, 0, 0)), hg_spec, hg_spec, hk_spec, hk_spec],
        out_shape=[jax.ShapeDtypeStruct((rows // tm, d, tm), BF16)] + [jax.ShapeDtypeStruct(hg, F32)] * 2
                  + [jax.ShapeDtypeStruct(hk, F32)] * 2,
        scratch_shapes=[pltpu.VMEM((nq, tm), F32)],
        compiler_params=_cparams(("parallel",)),
        name="peer_scores",
    )(s, g_ffn.reshape(1, d), modv, wq_t, keys)


def _peer_expert_kernel(ht_ref, u_ref, vt_ref, c_ref, e1_ref, s2_ref, e2_ref, s_ref, mod_ref,
                        o_ref, acc_scr, st0_scr, st1_scr, at0_scr, at1_scr, *, n_exp_blocks):
    g = pl.program_id(0)
    st_scr, at_scr = (st0_scr, st1_scr), (at0_scr, at1_scr)
    te, tm = st0_scr.shape
    jc = jnp.maximum(g - 2, 0) % n_exp_blocks

    @pl.when(g == 0)
    def _():
        for ref in st_scr + at_scr:
            ref[...] = jnp.zeros_like(ref)

    @pl.when(jc == 0)
    def _():
        acc_scr[...] = jnp.zeros_like(acc_scr)

    def stages(slot_a, slot_b):
        n_chunks = te // MXU_CHUNK_ROWS

        def value_chunk(q):
            ms = slice(q * MXU_CHUNK_ROWS, (q + 1) * MXU_CHUNK_ROWS)
            acc_scr[ms, :] += jnp.dot(vt_ref[0, ms, :], at_scr[slot_a][...], preferred_element_type=F32)

        def score_chunk(q):
            ms = slice(q * MXU_CHUNK_ROWS, (q + 1) * MXU_CHUNK_ROWS)
            st_scr[slot_a][ms, :] = jnp.dot(u_ref[ms, :], ht_ref[0], preferred_element_type=F32)

        rg_rows = 4 * SUBLANES
        al_group = 4
        col_tiles = 1
        tiles = [(rg, c0, al0)
                 for rg in range(LANES // rg_rows)
                 for c0 in range(0, tm // LANES, col_tiles)
                 for al0 in range(0, te // LANES, al_group)]
        assert acc_scr.shape[0] == te
        per_chunk = len(tiles) // (2 * n_chunks)
        for t, (rg, c0, al0) in enumerate(tiles):
            if t % per_chunk == 0:
                q = t // per_chunk
                (value_chunk if q % 2 == 0 else score_chunk)(q // 2)
            brows = slice(rg * rg_rows, (rg + 1) * rg_rows)
            lanes = [slice((c0 + ct) * LANES, (c0 + ct + 1) * LANES) for ct in range(col_tiles)]
            gate = [[jnp.zeros((rg_rows, LANES), F32) for _ in lanes] for _ in range(al_group)]
            for hh in range(PEER_HEADS):
                for ct, ln in enumerate(lanes):
                    s2t = s2_ref[0, hh, brows, ln]
                    e2t = e2_ref[0, hh, brows, ln]
                    for k in range(al_group):
                        al = al0 + k
                        c = c_ref[hh, 0, al:al + 1, ln]
                        e1 = e1_ref[hh, 0, al:al + 1, ln]
                        gate[k][ct] = gate[k][ct] + jnp.where(s2t >= c, e2t, 0.0) * e1
            for k in range(al_group):
                erows = slice((al0 + k) * LANES + rg * rg_rows, (al0 + k) * LANES + (rg + 1) * rg_rows)
                for ct, ln in enumerate(lanes):
                    sv = st_scr[slot_b][erows, ln]
                    act = sv * (1.0 + lax.erf(sv * (2.0 ** -0.5)))
                    at_scr[slot_b][erows, ln] = (gate[k][ct] * act).astype(BF16)

    for parity in range(2):
        @pl.when(g % 2 == parity)
        def _():
            stages(parity, 1 - parity)

    @pl.when((jc == n_exp_blocks - 1) & (g >= 2))
    def _():
        o_ref[...] = s_ref[...] + mod_ref[0][5:6] * acc_scr[...].T


def _peer_experts(ht, u, vt, c, e1, s2, e2, s, modv, rows, *, n_lat_rows, n_per_batch):
    d = s.shape[1]
    n_exp = u.shape[0]
    n_batch = modv.shape[0] - 1
    tm, te = PEER_TM, PEER_TE
    nlat_blocks = n_lat_rows // tm
    bpb = n_per_batch // tm
    nj = n_exp // te
    n_pairs = (rows // tm) * nj

    def pair(g, lag):
        p = jnp.clip(g - lag, 0, n_pairs - 1)
        return p // nj, p % nj

    assert SCORE_TM == tm
    hk_spec = pl.BlockSpec((1, PEER_HEADS, PEER_KEYS, tm), lambda g: (pair(g, 1)[0], 0, 0, 0))
    hg_spec = pl.BlockSpec((PEER_HEADS, 1, SUBLANES, tm), lambda g: (0, pair(g, 1)[1], 0, pair(g, 1)[0]))
    kern = functools.partial(_peer_expert_kernel, n_exp_blocks=nj)
    return pl.pallas_call(
        kern,
        grid=(n_pairs + 2,),
        in_specs=[pl.BlockSpec((1, d, tm), lambda g: (pair(g, 0)[0], 0, 0)),
                  pl.BlockSpec((te, d), lambda g: (pair(g, 0)[1], 0)),
                  pl.BlockSpec((1, d, te), lambda g: (pair(g, 2)[1], 0, 0)),
                  hg_spec, hg_spec, hk_spec, hk_spec,
                  pl.BlockSpec((tm, d), lambda g: (pair(g, 2)[0], 0)),
                  pl.BlockSpec((1, 8, d),
                               lambda g: (_mod_index(pair(g, 2)[0], nlat_blocks, bpb, n_batch), 0, 0))],
        out_specs=pl.BlockSpec((tm, d), lambda g: (pair(g, 2)[0], 0)),
        out_shape=jax.ShapeDtypeStruct((rows, d), F32),
        scratch_shapes=[pltpu.VMEM((d, tm), F32), pltpu.VMEM((te, tm), F32), pltpu.VMEM((te, tm), F32),
                        pltpu.VMEM((te, tm), BF16), pltpu.VMEM((te, tm), BF16)],
        compiler_params=_cparams(("arbitrary",)),
        name="peer_experts",
    )(ht, u, vt, c, e1, s2, e2, s, modv)


def _peer(s, modv, g_ffn, w_q, keys, u_tab, v_tab, rows, *, n_lat_rows, n_per_batch):
    geo = dict(n_lat_rows=n_lat_rows, n_per_batch=n_per_batch)
    ht, c, e1, s2, e2 = _peer_scores(s, modv, g_ffn, w_q.T.astype(BF16), keys.astype(BF16), rows, **geo)
    n_exp, d = v_tab.shape
    vt = v_tab.reshape(n_exp // PEER_TE, PEER_TE, d).transpose(0, 2, 1).astype(BF16)
    return _peer_experts(ht, u_tab.astype(BF16), vt, c, e1, s2, e2, s, modv, rows, **geo)


def kernel(x, c, ctx, c_ctx, ada_w, ada_b, norm_mix, norm_ffn, mla_w_down, mla_g_cq, mla_g_ckv, mla_w_uq, mla_w_ukv, mla_g_q, mla_g_k, mla_w_o, diff_w_qkv, diff_g_q, diff_g_k, diff_lambda, diff_g_sub, diff_w_o, swa_w_qkv, swa_g_q, swa_g_k, swa_sink, swa_w_o, peer_w_q, peer_keys, peer_u, peer_v):
    n_batch, n_lat, d = x.shape
    n_ctx = ctx.shape[1]
    depth = ada_w.shape[0]
    rows_lat = n_batch * n_lat
    rows_all = rows_lat + n_batch * n_ctx
    assert n_lat % 512 == 0 and n_ctx % 256 == 0 and (n_batch * n_ctx) % PEER_TM == 0
    assert n_lat % GRID_W == 0 and rows_lat % n_ctx == 0 and n_lat >= 3 * Q_BLOCK

    s = jnp.concatenate([x.reshape(rows_lat, d), ctx.reshape(n_batch * n_ctx, d)], axis=0)
    cc = jnp.zeros((16, d), F32).at[:n_batch].set(c).at[n_batch].set(c_ctx)
    mods = _ada_mods(cc, ada_w, ada_b)
    geo = dict(n_lat_rows=rows_lat, n_per_batch=n_lat)

    for i in range(depth):
        last = i == depth - 1
        kind, j = i % 3, i // 3
        modv = jnp.zeros((n_batch + 1, 8, d), F32).at[:, :6, :].set(
            mods[i, :n_batch + 1].reshape(n_batch + 1, 6, d))
        dims = dict(n_batch=n_batch, n_lat=n_lat, n_ctx=n_ctx, need_ctx=not last)
        if kind == 0:
            p = (mla_w_down[j], mla_g_cq[j], mla_g_ckv[j], mla_w_uq[j], mla_w_ukv[j], mla_g_q[j], mla_g_k[j])
            o, wo = _mla_mixer(s, modv, norm_mix[i], p, mla_w_o[j], **dims)
        elif kind == 1:
            lam_init = 0.8 - 0.6 * math.exp(-0.3 * i)
            o, wo = _diff_mixer(s, modv, norm_mix[i], diff_w_qkv[j], diff_g_q[j], diff_g_k[j],
                                diff_lambda[j], diff_g_sub[j], diff_w_o[j], lam_init, **dims)
        else:
            o, wo = _swa_mixer(s, modv, norm_mix[i], swa_w_qkv[j], swa_g_q[j], swa_g_k[j],
                               swa_sink[j], swa_w_o[j], **dims)
        rows = rows_lat if last else rows_all
        s = _proj_resid(o, wo, s, modv, rows, gate_row=2, **geo)
        s = _peer(s, modv, norm_ffn[i], peer_w_q[i], peer_keys[i], peer_u[i], peer_v[i], rows, **geo)
    return s.reshape(n_batch, n_lat, d)
```

```python
import functools
import math

import jax
import jax.numpy as jnp
from jax import lax
from jax.experimental import pallas as pl
from jax.experimental.pallas import tpu as pltpu

F32 = jnp.float32
BF16 = jnp.bfloat16

LANES = 128
EPS = 1e-6
NEG_INF = -1e30
NEG_BIG = -3.0e38
ROPE_THETA = 10000.0
GRID_W = 64
Q_BLOCK = 128
WINDOW = 128

MLA_HEADS, MLA_NOPE, MLA_ROPE, MLA_V = 16, 64, 32, 64
MLA_QK = MLA_NOPE + MLA_ROPE
MLA_Q_LORA, MLA_KV_LORA = 768, 256
DIFF_HEADS, DIFF_HD = 8, 64
SWA_HEADS, SWA_KV_HEADS, SWA_HD = 16, 4, 64
SWA_GROUP = SWA_HEADS // SWA_KV_HEADS
PEER_HEADS, PEER_KEYS, PEER_TOPK = 8, 128, 16
PEER_EXPERTS = PEER_KEYS * PEER_KEYS

VMEM_LIMIT = 56 * 1024 * 1024

TM = 512
SCORE_TM = 512
TQ = 256
PEER_TM = 512
SUBLANES = 8
PEER_TE = SUBLANES * LANES
MXU_CHUNK_ROWS = 128


def _cparams(sem):
    return pltpu.CompilerParams(dimension_semantics=sem, vmem_limit_bytes=VMEM_LIMIT)


def _ada_kernel(c_ref, w_ref, b_ref, o_ref):
    c = c_ref[...]
    sc = c * (1.0 / (1.0 + jnp.exp(-c)))
    o_ref[0] = jnp.dot(sc, w_ref[0], preferred_element_type=F32,
                       precision=lax.Precision.HIGHEST) + b_ref[0]


def _ada_mods(cc, ada_w, ada_b):
    depth, d, d6 = ada_w.shape
    rows = cc.shape[0]
    tn = 1536
    return pl.pallas_call(
        _ada_kernel,
        grid=(depth, d6 // tn),
        in_specs=[pl.BlockSpec((rows, d), lambda i, n: (0, 0)),
                  pl.BlockSpec((1, d, tn), lambda i, n: (i, 0, n)),
                  pl.BlockSpec((1, 1, tn), lambda i, n: (i, 0, n))],
        out_specs=pl.BlockSpec((1, rows, tn), lambda i, n: (i, 0, n)),
        out_shape=jax.ShapeDtypeStruct((depth, rows, d6), F32),
        compiler_params=_cparams(("arbitrary", "arbitrary")),
        name="ada_mods",
    )(cc, ada_w, ada_b.reshape(depth, 1, d6))


def _norm_mod(x, g, mod, shift_row, scale_row):
    ms = jnp.mean(x * x, axis=-1, keepdims=True)
    y = x * lax.rsqrt(ms + EPS) * g
    if mod is not None:
        y = y * (1.0 + mod[scale_row:scale_row + 1]) + mod[shift_row:shift_row + 1]
    return y


def _nm_kernel(x_ref, g_ref, mod_ref, w_ref, o_ref, h_scr, *, use_mod, shift_row, scale_row):
    @pl.when(pl.program_id(1) == 0)
    def _():
        mod = mod_ref[0] if use_mod else None
        h_scr[...] = _norm_mod(x_ref[...], g_ref[...], mod, shift_row, scale_row).astype(BF16)

    o_ref[...] = jnp.dot(h_scr[...], w_ref[...], preferred_element_type=F32).astype(o_ref.dtype)


def _mod_index(i, nlat_blocks, blocks_per_batch, n_batch):
    return jnp.where(i < nlat_blocks, i // blocks_per_batch, n_batch)


def _nm_matmul(x, kblk, k, g, modv, rows, w, *, n_lat_rows, n_per_batch, out_dtype=F32,
               use_mod=True, shift_row=0, scale_row=1, name="nm_matmul"):
    nout = w.shape[1]
    tn = nout
    for cand in (2048, 1536, 1280, 1024):
        if nout > 2048 and nout % cand == 0:
            tn = cand
            break
    n_batch = modv.shape[0] - 1
    nlat_blocks = n_lat_rows // TM
    bpb = n_per_batch // TM
    kern = functools.partial(_nm_kernel, use_mod=use_mod, shift_row=shift_row, scale_row=scale_row)
    return pl.pallas_call(
        kern,
        grid=(rows // TM, nout // tn),
        in_specs=[pl.BlockSpec((TM, k), lambda i, n: (i, kblk)),
                  pl.BlockSpec((1, k), lambda i, n: (0, 0)),
                  pl.BlockSpec((1, 8, k), lambda i, n: (_mod_index(i, nlat_blocks, bpb, n_batch), 0, 0)),
                  pl.BlockSpec((k, tn), lambda i, n: (0, n))],
        out_specs=pl.BlockSpec((TM, tn), lambda i, n: (i, n)),
        out_shape=jax.ShapeDtypeStruct((rows, nout), out_dtype),
        scratch_shapes=[pltpu.VMEM((TM, k), BF16)],
        compiler_params=_cparams(("parallel", "arbitrary")),
        name=name,
    )(x, g.reshape(1, k), modv if modv.shape[-1] == k else jnp.zeros((n_batch + 1, 8, k), F32), w)


def _proj_resid_kernel(a_ref, w_ref, s_ref, mod_ref, o_ref, *, gate_row):
    y = jnp.dot(a_ref[...].astype(BF16), w_ref[...], preferred_element_type=F32)
    o_ref[...] = s_ref[...] + mod_ref[0][gate_row:gate_row + 1] * y


def _proj_resid(a, w, s, modv, rows, *, n_lat_rows, n_per_batch, gate_row, name="proj_resid"):
    ka = a.shape[1]
    d = w.shape[1]
    n_batch = modv.shape[0] - 1
    nlat_blocks = n_lat_rows // TM
    bpb = n_per_batch // TM
    return pl.pallas_call(
        functools.partial(_proj_resid_kernel, gate_row=gate_row),
        grid=(rows // TM,),
        in_specs=[pl.BlockSpec((TM, ka), lambda i: (i, 0)),
                  pl.BlockSpec((ka, d), lambda i: (0, 0)),
                  pl.BlockSpec((TM, d), lambda i: (i, 0)),
                  pl.BlockSpec((1, 8, d), lambda i: (_mod_index(i, nlat_blocks, bpb, n_batch), 0, 0))],
        out_specs=pl.BlockSpec((TM, d), lambda i: (i, 0)),
        out_shape=jax.ShapeDtypeStruct((rows, d), F32),
        compiler_params=_cparams(("parallel",)),
        name=name,
    )(a, w, s, modv)


def _rope_tables(n_lat, extra_rows, segments, nf):
    pos = jnp.arange(n_lat, dtype=jnp.int32)
    row = (pos // GRID_W).astype(F32)
    col = (pos % GRID_W).astype(F32)
    inv = ROPE_THETA ** (-jnp.arange(nf, dtype=F32) / nf)
    cos = jnp.ones((n_lat, LANES), F32)
    sa = jnp.zeros((n_lat, LANES), F32)
    sb = jnp.zeros((n_lat, LANES), F32)
    for first, use_row in segments:
        ang = (row if use_row else col)[:, None] * inv[None, :]
        c, s = jnp.cos(ang), jnp.sin(ang)
        cos = cos.at[:, first:first + nf].set(c).at[:, first + nf:first + 2 * nf].set(c)
        sa = sa.at[:, first:first + nf].set(-s)
        sb = sb.at[:, first + nf:first + 2 * nf].set(s)
    pad = lambda t, v: jnp.concatenate([t, jnp.full((extra_rows, LANES), v, F32)], axis=0)
    return pad(cos, 1.0), pad(sa, 0.0), pad(sb, 0.0)


def _head_norm(x, g, d):
    ssq = jnp.sum(x * x, axis=-1, keepdims=True) * (1.0 / d)
    return x * lax.rsqrt(ssq + EPS) * g


def _rope(y, cos, sa, sb, nf):
    return y * cos + pltpu.roll(y, LANES - nf, 1) * sa + pltpu.roll(y, nf, 1) * sb


def _with_ones_lane(v, lane):
    return jnp.where(lax.broadcasted_iota(jnp.int32, v.shape, 1) == lane, jnp.ones_like(v), v)


def _row_block(b, qi, nlat_q, nctx_q, n_batch):
    return jnp.where(qi < nlat_q, b * nlat_q + qi, n_batch * nlat_q + b * nctx_q + (qi - nlat_q))


def _prep_keys(k_scr, v_scr, n_lat, n_ctx, load_k_lat, load_k_ctx, load_v_lat, load_v_ctx,
               gk, tabs, d, nf):
    cos_ref, sa_ref, sb_ref = tabs
    ch = 256

    def lat_body(c, carry):
        r = pl.ds(pl.multiple_of(c * ch, ch), ch)
        kk = _head_norm(load_k_lat(r).astype(F32), gk, d)
        kk = _rope(kk, cos_ref[r, :], sa_ref[r, :], sb_ref[r, :], nf)
        k_scr[r, :] = kk.astype(BF16)
        if v_scr is not None:
            v_scr[r, :] = load_v_lat(r).astype(BF16)
        return carry

    lax.fori_loop(0, n_lat // ch, lat_body, 0, unroll=2)

    def ctx_body(c, carry):
        r = pl.ds(pl.multiple_of(c * ch, ch), ch)
        ro = pl.ds(pl.multiple_of(n_lat + c * ch, ch), ch)
        k_scr[ro, :] = _head_norm(load_k_ctx(r).astype(F32), gk, d).astype(BF16)
        if v_scr is not None:
            v_scr[ro, :] = load_v_ctx(r).astype(BF16)
        return carry

    lax.fori_loop(0, n_ctx // ch, ctx_body, 0)


def _nt_dot(a, b):
    return lax.dot_general(a, b, (((1,), (1,)), ((), ())), preferred_element_type=F32)


LOG2E = math.log2(math.e)
CHAIN_ROWS = 128
MLA_HEADS_PER_STEP = 4
HEADS_PER_STEP = 4


def _row_chains(n_rows):
    return [slice(r, r + CHAIN_ROWS) for r in range(0, n_rows, CHAIN_ROWS)]


def _emit_pipelined(chains):
    pending, active = list(chains), []
    while pending or active:
        if pending:
            active.append(pending.pop(0))
        for ch in list(active):
            try:
                next(ch)
            except StopIteration:
                active.remove(ch)


def _exp2_scores(q, k):
    s = _nt_dot(q, k)
    p = jnp.exp2(s - jnp.max(s, axis=-1, keepdims=True))
    return p, jnp.sum(p, axis=-1, keepdims=True)


def _mla_attn_kernel(q_ref, kl_ref, kc_ref, krl_ref, krc_ref, vl_ref, vc_ref,
                     cq_ref, saq_ref, sbq_ref, ck_ref, sak_ref, sbk_ref, gq_ref, gk_ref,
                     o_ref, k_scr, v_scr, *, n_lat, n_ctx, nlat_q, scale):
    qi = pl.program_id(2)
    nf = MLA_ROPE // 4
    heads = [slice(hd * LANES, (hd + 1) * LANES) for hd in range(q_ref.shape[1] // LANES)]

    @pl.when(qi == 0)
    def _():
        for hd, ln in enumerate(heads):
            _prep_keys(k_scr.at[hd], v_scr.at[hd], n_lat, n_ctx,
                       lambda r, ln=ln: kl_ref[r, ln].astype(F32) + krl_ref[r, :],
                       lambda r, ln=ln: kc_ref[r, ln].astype(F32) + krc_ref[r, :],
                       lambda r, ln=ln: _with_ones_lane(vl_ref[r, ln], MLA_V),
                       lambda r, ln=ln: _with_ones_lane(vc_ref[r, ln], MLA_V),
                       gk_ref[...], (ck_ref, sak_ref, sbk_ref), MLA_QK, nf)

    def attend(keys):
        def chain(hd, ln, rows):
            q = _head_norm(q_ref[rows, ln].astype(F32), gq_ref[...], MLA_QK)
            q = _rope(q, cq_ref[rows, :], saq_ref[rows, :], sbq_ref[rows, :], nf) * (scale * LOG2E)
            s = _nt_dot(q.astype(BF16), k_scr[hd, keys, :])
            yield
            p = jnp.exp2(s - jnp.max(s, axis=-1, keepdims=True)).astype(BF16)
            yield
            o = jnp.dot(p, v_scr[hd, keys, :], preferred_element_type=F32)
            o_ref[rows, ln] = (o * (1.0 / o[:, MLA_V:MLA_V + 1])).astype(o_ref.dtype)

        _emit_pipelined([chain(hd, ln, rows) for rows in _row_chains(q_ref.shape[0])
                         for hd, ln in enumerate(heads)])

    @pl.when(qi < nlat_q)
    def _():
        attend(slice(0, n_lat + n_ctx))

    @pl.when(qi >= nlat_q)
    def _():
        attend(slice(n_lat, n_lat + n_ctx))


def _mla_attention(q, kv, lat, g_q, g_k, *, n_batch, n_lat, n_ctx, need_ctx):
    nlat_q, nctx_q = n_lat // TQ, n_ctx // TQ
    nq = nlat_q + (nctx_q if need_ctx else 0)
    rows = n_batch * (n_lat + (n_ctx if need_ctx else 0))
    nf = MLA_ROPE // 4
    tabs = _rope_tables(n_lat, TQ, ((MLA_NOPE, True), (MLA_NOPE + 2 * nf, False)), nf)
    h = MLA_HEADS
    ctx0 = n_batch * n_lat // n_ctx
    kr_blk = (MLA_Q_LORA + MLA_KV_LORA) // LANES
    gq = jnp.zeros((1, LANES), F32).at[0, :MLA_QK].set(g_q)
    gk = jnp.zeros((1, LANES), F32).at[0, :MLA_QK].set(g_k)
    hps = MLA_HEADS_PER_STEP
    hw = hps * LANES
    qspec = pl.BlockSpec((TQ, hw), lambda b, hh, qi: (_row_block(b, qi, nlat_q, nctx_q, n_batch), hh))
    tq_spec = pl.BlockSpec((TQ, LANES), lambda b, hh, qi: (jnp.minimum(qi, nlat_q), 0))
    tk_spec = pl.BlockSpec((n_lat, LANES), lambda b, hh, qi: (0, 0))
    vec = pl.BlockSpec((1, LANES), lambda b, hh, qi: (0, 0))
    kern = functools.partial(_mla_attn_kernel, n_lat=n_lat, n_ctx=n_ctx, nlat_q=nlat_q,
                             scale=MLA_QK ** -0.5)
    return pl.pallas_call(
        kern,
        grid=(n_batch, h // hps, nq),
        in_specs=[qspec,
                  pl.BlockSpec((n_lat, hw), lambda b, hh, qi: (b, hh)),
                  pl.BlockSpec((n_ctx, hw), lambda b, hh, qi: (ctx0 + b, hh)),
                  pl.BlockSpec((n_lat, LANES), lambda b, hh, qi: (b, kr_blk)),
                  pl.BlockSpec((n_ctx, LANES), lambda b, hh, qi: (ctx0 + b, kr_blk)),
                  pl.BlockSpec((n_lat, hw), lambda b, hh, qi: (b, h // hps + hh)),
                  pl.BlockSpec((n_ctx, hw), lambda b, hh, qi: (ctx0 + b, h // hps + hh)),
                  tq_spec, tq_spec, tq_spec, tk_spec, tk_spec, tk_spec, vec, vec],
        out_specs=qspec,
        out_shape=jax.ShapeDtypeStruct((rows, h * LANES), BF16),
        scratch_shapes=[pltpu.VMEM((hps, n_lat + n_ctx, LANES), BF16),
                        pltpu.VMEM((hps, n_lat + n_ctx, LANES), BF16)],
        compiler_params=_cparams(("parallel", "parallel", "arbitrary")),
        name="mla_attention",
    )(q, kv, kv, lat, lat, kv, kv, *tabs, *tabs, gq, gk)


def _pad_cols(w, groups, width, first=0):
    k = w.shape[0]
    w3 = w.reshape(k, groups, width)
    out = jnp.zeros((k, groups, LANES), w.dtype).at[:, :, first:first + width].set(w3)
    return out.reshape(k, groups * LANES)


def _pad_rows(w, groups, width):
    d = w.shape[1]
    w3 = w.reshape(groups, width, d)
    return jnp.zeros((groups, LANES, d), w.dtype).at[:, :width, :].set(w3).reshape(groups * LANES, d)


def _mla_mixer(s, modv, g_mix, p, w_o, *, n_batch, n_lat, n_ctx, need_ctx):
    w_down, g_cq, g_ckv, w_uq, w_ukv, g_q, g_k = p
    d = s.shape[1]
    rows_all = n_batch * (n_lat + n_ctx)
    geo = dict(n_lat_rows=n_batch * n_lat, n_per_batch=n_lat)
    wd = jnp.concatenate([w_down[:, :MLA_Q_LORA + MLA_KV_LORA],
                          _pad_cols(w_down[:, MLA_Q_LORA + MLA_KV_LORA:], 1, MLA_ROPE, MLA_NOPE)], axis=1)
    lat = _nm_matmul(s, 0, d, g_mix, modv, rows_all, wd.astype(BF16), name="mla_down", **geo)
    wq = _pad_cols(w_uq, MLA_HEADS, MLA_QK).astype(BF16)
    q = _nm_matmul(lat, 0, MLA_Q_LORA, g_cq, modv, rows_all, wq, use_mod=False, name="mla_uq",
                   out_dtype=BF16, **geo)
    wkv3 = w_ukv.reshape(MLA_KV_LORA, MLA_HEADS, MLA_NOPE + MLA_V)
    wkv = jnp.concatenate([_pad_cols(wkv3[:, :, :MLA_NOPE].reshape(MLA_KV_LORA, -1), MLA_HEADS, MLA_NOPE),
                           _pad_cols(wkv3[:, :, MLA_NOPE:].reshape(MLA_KV_LORA, -1), MLA_HEADS, MLA_V)],
                          axis=1).astype(BF16)
    kv = _nm_matmul(lat, MLA_Q_LORA // MLA_KV_LORA, MLA_KV_LORA, g_ckv, modv, rows_all, wkv,
                    use_mod=False, name="mla_ukv", out_dtype=BF16, **geo)
    o = _mla_attention(q, kv, lat, g_q, g_k, n_batch=n_batch, n_lat=n_lat, n_ctx=n_ctx, need_ctx=need_ctx)
    wo = _pad_rows(w_o, MLA_HEADS, MLA_V).astype(BF16)
    return o, wo


def _diff_attn_kernel(q_ref, kl_ref, kc_ref, vl_ref, vc_ref,
                      cq_ref, saq_ref, sbq_ref, ck_ref, sak_ref, sbk_ref, gq_ref, gk_ref,
                      lam_ref, gsub_ref, o_ref, k_scr, v_scr,
                      *, n_lat, n_ctx, nlat_q, scale, lam_init):
    qi = pl.program_id(2)
    nf = DIFF_HD // 4
    tabs = (ck_ref, sak_ref, sbk_ref)
    tile = lambda t: slice(t * LANES, (t + 1) * LANES)
    n_heads = o_ref.shape[1] // LANES

    @pl.when(qi == 0)
    def _():
        for hd in range(n_heads):
            ln = tile(hd)
            for sub in range(2):
                kt = tile(2 * hd + sub)
                _prep_keys(k_scr.at[2 * hd + sub], v_scr.at[hd] if sub == 0 else None, n_lat, n_ctx,
                           lambda r, kt=kt: kl_ref[r, kt], lambda r, kt=kt: kc_ref[r, kt],
                           lambda r, ln=ln: vl_ref[r, ln], lambda r, ln=ln: vc_ref[r, ln],
                           gk_ref[...], tabs, DIFF_HD, nf)

    lp = lam_ref[...]
    lam = (jnp.exp(jnp.sum(lp[0:1] * lp[1:2], axis=-1, keepdims=True))
           - jnp.exp(jnp.sum(lp[2:3] * lp[3:4], axis=-1, keepdims=True)) + lam_init)

    def attend(keys):
        def scores(t, rows):
            q = _head_norm(q_ref[rows, tile(t)].astype(F32), gq_ref[...], DIFF_HD)
            q = _rope(q, cq_ref[rows, :], saq_ref[rows, :], sbq_ref[rows, :], nf) * (scale * LOG2E)
            return _nt_dot(q.astype(BF16), k_scr[t, keys, :])

        def chain(hd, rows):
            s0 = scores(2 * hd, rows)
            s1 = scores(2 * hd + 1, rows)
            yield
            p0 = jnp.exp2(s0 - jnp.max(s0, axis=-1, keepdims=True))
            p1 = jnp.exp2(s1 - jnp.max(s1, axis=-1, keepdims=True))
            l0 = jnp.sum(p0, axis=-1, keepdims=True)
            l1 = jnp.sum(p1, axis=-1, keepdims=True)
            a = (p0 * (1.0 / l0) - p1 * (lam / l1)).astype(BF16)
            yield
            o = jnp.dot(a, v_scr[hd, keys, :], preferred_element_type=F32)
            o = _head_norm(o, gsub_ref[...], 2 * DIFF_HD) * (1.0 - lam_init)
            o_ref[rows, tile(hd)] = o.astype(o_ref.dtype)

        _emit_pipelined([chain(hd, rows) for rows in _row_chains(q_ref.shape[0])
                         for hd in range(n_heads)])

    @pl.when(qi < nlat_q)
    def _():
        attend(slice(0, n_lat + n_ctx))

    @pl.when(qi >= nlat_q)
    def _():
        attend(slice(n_lat, n_lat + n_ctx))


def _diff_attention(qkv, g_q, g_k, lam_p, g_sub, lam_init, *, n_batch, n_lat, n_ctx, need_ctx):
    nlat_q, nctx_q = n_lat // TQ, n_ctx // TQ
    nq = nlat_q + (nctx_q if need_ctx else 0)
    rows = n_batch * (n_lat + (n_ctx if need_ctx else 0))
    nf = DIFF_HD // 4
    tabs = _rope_tables(n_lat, TQ, ((0, True), (2 * nf, False)), nf)
    nh = DIFF_HEADS
    ctx0 = n_batch * n_lat // n_ctx
    gq = jnp.zeros((1, LANES), F32).at[0, :DIFF_HD].set(g_q)
    gk = jnp.zeros((1, LANES), F32).at[0, :DIFF_HD].set(g_k)
    lam8 = jnp.zeros((8, LANES), F32).at[:4, :DIFF_HD].set(lam_p.astype(F32))
    rb = lambda b, qi: _row_block(b, qi, nlat_q, nctx_q, n_batch)
    tq_spec = pl.BlockSpec((TQ, LANES), lambda b, hh, qi: (jnp.minimum(qi, nlat_q), 0))
    tk_spec = pl.BlockSpec((n_lat, LANES), lambda b, hh, qi: (0, 0))
    vec = pl.BlockSpec((1, LANES), lambda b, hh, qi: (0, 0))
    hps = HEADS_PER_STEP
    qk_w, v_w = 2 * hps * LANES, hps * LANES
    k_blk0 = 2 * nh // (2 * hps)
    v_blk0 = 4 * nh // hps
    kern = functools.partial(_diff_attn_kernel, n_lat=n_lat, n_ctx=n_ctx, nlat_q=nlat_q,
                             scale=DIFF_HD ** -0.5, lam_init=lam_init)
    n_all = n_lat + n_ctx
    return pl.pallas_call(
        kern,
        grid=(n_batch, nh // hps, nq),
        in_specs=[pl.BlockSpec((TQ, qk_w), lambda b, hh, qi: (rb(b, qi), hh)),
                  pl.BlockSpec((n_lat, qk_w), lambda b, hh, qi: (b, k_blk0 + hh)),
                  pl.BlockSpec((n_ctx, qk_w), lambda b, hh, qi: (ctx0 + b, k_blk0 + hh)),
                  pl.BlockSpec((n_lat, v_w), lambda b, hh, qi: (b, v_blk0 + hh)),
                  pl.BlockSpec((n_ctx, v_w), lambda b, hh, qi: (ctx0 + b, v_blk0 + hh)),
                  tq_spec, tq_spec, tq_spec, tk_spec, tk_spec, tk_spec, vec, vec,
                  pl.BlockSpec((8, LANES), lambda b, hh, qi: (0, 0)), vec],
        out_specs=pl.BlockSpec((TQ, v_w), lambda b, hh, qi: (rb(b, qi), hh)),
        out_shape=jax.ShapeDtypeStruct((rows, nh * LANES), BF16),
        scratch_shapes=[pltpu.VMEM((2 * hps, n_all, LANES), BF16), pltpu.VMEM((hps, n_all, LANES), BF16)],
        compiler_params=_cparams(("parallel", "parallel", "arbitrary")),
        name="diff_attention",
    )(qkv, qkv, qkv, qkv, qkv, *tabs, *tabs, gq, gk, lam8, g_sub.reshape(1, LANES))


def _diff_mixer(s, modv, g_mix, w_qkv, g_q, g_k, lam_p, g_sub, w_o, lam_init,
                *, n_batch, n_lat, n_ctx, need_ctx):
    d = s.shape[1]
    width = DIFF_HEADS * 2 * DIFF_HD
    rows_all = n_batch * (n_lat + n_ctx)
    w = jnp.concatenate([_pad_cols(w_qkv[:, :width], 2 * DIFF_HEADS, DIFF_HD),
                         _pad_cols(w_qkv[:, width:2 * width], 2 * DIFF_HEADS, DIFF_HD),
                         w_qkv[:, 2 * width:]], axis=1).astype(BF16)
    qkv = _nm_matmul(s, 0, d, g_mix, modv, rows_all, w, n_lat_rows=n_batch * n_lat,
                     n_per_batch=n_lat, name="diff_qkv", out_dtype=BF16)
    o = _diff_attention(qkv, g_q, g_k, lam_p, g_sub, lam_init,
                        n_batch=n_batch, n_lat=n_lat, n_ctx=n_ctx, need_ctx=need_ctx)
    return o, w_o.astype(BF16)


def _swa_attn_kernel(q_ref, kl_ref, kc_ref, vl_ref, vc_ref,
                     cq_ref, saq_ref, sbq_ref, ck_ref, sak_ref, sbk_ref, gq_ref, gk_ref, sink_ref,
                     o_ref, k_scr, v_scr, *, n_lat, n_ctx, nlat_q, scale):
    qi = pl.program_id(2)
    nf = SWA_HD // 4
    band = 3 * Q_BLOCK

    @pl.when(qi == 0)
    def _():
        _prep_keys(k_scr, v_scr, n_lat, n_ctx, lambda r: kl_ref[r, :], lambda r: kc_ref[r, :],
                   lambda r: _with_ones_lane(vl_ref[r, :], SWA_HD),
                   lambda r: _with_ones_lane(vc_ref[r, :], SWA_HD), gk_ref[...],
                   (ck_ref, sak_ref, sbk_ref), SWA_HD, nf)

    row_chains = _row_chains(q_ref.shape[0])
    reach = jnp.where(qi < nlat_q, WINDOW, -1)

    def chain(ci, rows, gi):
        qb = qi * len(row_chains) + ci
        k0 = pl.multiple_of(jnp.clip((qb - 1) * Q_BLOCK, 0, n_lat - band), Q_BLOCK)
        kpos = k0 + lax.broadcasted_iota(jnp.int32, (Q_BLOCK, band), 1)
        qpos = qb * Q_BLOCK + lax.broadcasted_iota(jnp.int32, (Q_BLOCK, band), 0)
        lanes = slice(gi * LANES, (gi + 1) * LANES)
        q = _head_norm(q_ref[rows, lanes].astype(F32), gq_ref[...], SWA_HD)
        q = _rope(q, cq_ref[rows, :], saq_ref[rows, :], sbq_ref[rows, :], nf) * (scale * LOG2E)
        q = q.astype(BF16)
        sb = _nt_dot(q, k_scr[pl.ds(k0, band), :])
        sc = _nt_dot(q, k_scr[n_lat:n_lat + n_ctx, :])
        yield
        sb = jnp.where(jnp.abs(kpos - qpos) <= reach, sb, NEG_INF)
        sink = sink_ref[gi][0:1, 0:1] * LOG2E
        m = jnp.maximum(jnp.maximum(jnp.max(sb, axis=-1, keepdims=True),
                                    jnp.max(sc, axis=-1, keepdims=True)), sink)
        pb = jnp.exp2(sb - m).astype(BF16)
        pc = jnp.exp2(sc - m).astype(BF16)
        p_sink = jnp.exp2(sink - m)
        yield
        o = (jnp.dot(pb, v_scr[pl.ds(k0, band), :], preferred_element_type=F32)
             + jnp.dot(pc, v_scr[n_lat:n_lat + n_ctx, :], preferred_element_type=F32))
        l = o[:, SWA_HD:SWA_HD + 1] + p_sink
        o_ref[rows, lanes] = (o * (1.0 / l)).astype(o_ref.dtype)

    _emit_pipelined([chain(ci, rows, gi) for ci, rows in enumerate(row_chains)
                     for gi in range(SWA_GROUP)])


def _swa_attention(qkv, g_q, g_k, sink, *, n_batch, n_lat, n_ctx, need_ctx):
    nlat_q, nctx_q = n_lat // TQ, n_ctx // TQ
    nq = nlat_q + (nctx_q if need_ctx else 0)
    rows = n_batch * (n_lat + (n_ctx if need_ctx else 0))
    nf = SWA_HD // 4
    tabs = _rope_tables(n_lat, TQ, ((0, True), (2 * nf, False)), nf)
    nh, nkv, grp = SWA_HEADS, SWA_KV_HEADS, SWA_GROUP
    ctx0 = n_batch * n_lat // n_ctx
    gq = jnp.zeros((1, LANES), F32).at[0, :SWA_HD].set(g_q)
    gk = jnp.zeros((1, LANES), F32).at[0, :SWA_HD].set(g_k)
    sink_t = jnp.broadcast_to(sink.astype(F32).reshape(nh, 1, 1), (nh, 8, LANES))
    rb = lambda b, qi: _row_block(b, qi, nlat_q, nctx_q, n_batch)
    tq_spec = pl.BlockSpec((TQ, LANES), lambda b, kh, qi: (jnp.minimum(qi, nlat_q), 0))
    tk_spec = pl.BlockSpec((n_lat, LANES), lambda b, kh, qi: (0, 0))
    vec = pl.BlockSpec((1, LANES), lambda b, kh, qi: (0, 0))
    kern = functools.partial(_swa_attn_kernel, n_lat=n_lat, n_ctx=n_ctx, nlat_q=nlat_q,
                             scale=SWA_HD ** -0.5)
    n_all = n_lat + n_ctx
    return pl.pallas_call(
        kern,
        grid=(n_batch, nkv, nq),
        in_specs=[pl.BlockSpec((TQ, grp * LANES), lambda b, kh, qi: (rb(b, qi), kh)),
                  pl.BlockSpec((n_lat, LANES), lambda b, kh, qi: (b, nh + kh)),
                  pl.BlockSpec((n_ctx, LANES), lambda b, kh, qi: (ctx0 + b, nh + kh)),
                  pl.BlockSpec((n_lat, LANES), lambda b, kh, qi: (b, nh + nkv + kh)),
                  pl.BlockSpec((n_ctx, LANES), lambda b, kh, qi: (ctx0 + b, nh + nkv + kh)),
                  tq_spec, tq_spec, tq_spec, tk_spec, tk_spec, tk_spec, vec, vec,
                  pl.BlockSpec((grp, 8, LANES), lambda b, kh, qi: (kh, 0, 0))],
        out_specs=pl.BlockSpec((TQ, grp * LANES), lambda b, kh, qi: (rb(b, qi), kh)),
        out_shape=jax.ShapeDtypeStruct((rows, nh * LANES), BF16),
        scratch_shapes=[pltpu.VMEM((n_all, LANES), BF16), pltpu.VMEM((n_all, LANES), BF16)],
        compiler_params=_cparams(("parallel", "parallel", "arbitrary")),
        name="swa_attention",
    )(qkv, qkv, qkv, qkv, qkv, *tabs, *tabs, gq, gk, sink_t)


def _swa_mixer(s, modv, g_mix, w_qkv, g_q, g_k, sink, w_o, *, n_batch, n_lat, n_ctx, need_ctx):
    d = s.shape[1]
    qw, kw = SWA_HEADS * SWA_HD, SWA_KV_HEADS * SWA_HD
    rows_all = n_batch * (n_lat + n_ctx)
    w = jnp.concatenate([_pad_cols(w_qkv[:, :qw], SWA_HEADS, SWA_HD),
                         _pad_cols(w_qkv[:, qw:qw + kw], SWA_KV_HEADS, SWA_HD),
                         _pad_cols(w_qkv[:, qw + kw:], SWA_KV_HEADS, SWA_HD)], axis=1).astype(BF16)
    qkv = _nm_matmul(s, 0, d, g_mix, modv, rows_all, w, n_lat_rows=n_batch * n_lat,
                     n_per_batch=n_lat, name="swa_qkv", out_dtype=BF16)
    o = _swa_attention(qkv, g_q, g_k, sink, n_batch=n_batch, n_lat=n_lat, n_ctx=n_ctx, need_ctx=need_ctx)
    return o, _pad_rows(w_o, SWA_HEADS, SWA_HD).astype(BF16)


def _merge_exchange_pairs(n):
    pairs = []
    t = max(1, (n - 1).bit_length())
    p = 1 << (t - 1)
    while p > 0:
        q, r, d = 1 << (t - 1), 0, p
        while d > 0:
            pairs += [(i, i + d) for i in range(n - d) if (i & p) == r]
            d, q, r = q - p, q >> 1, p
        p >>= 1
    return pairs


def _top_values(s, count):
    n = s.shape[0] // SUBLANES
    v = [s[j * SUBLANES:(j + 1) * SUBLANES, :] for j in range(n)]
    for i, j in _merge_exchange_pairs(n):
        v[i], v[j] = jnp.maximum(v[i], v[j]), jnp.minimum(v[i], v[j])
    vals = []
    for r in range(count):
        m = jnp.max(v[0], axis=0, keepdims=True)
        vals.append(m)
        depth = min(n, count - 1 - r)
        head = v[0] == m
        for j in range(depth):
            nxt = v[j + 1] if j + 1 < n else jnp.full_like(v[j], NEG_BIG)
            v[j] = jnp.where(head, nxt, v[j])
    return vals


def _peer_score_kernel(s_ref, g_ref, mod_ref, wq_ref, keys_ref,
                       ht_ref, c_ref, e1_ref, s2_ref, e2_ref, qt_scr):
    k = PEER_TOPK
    h = _norm_mod(s_ref[...], g_ref[...], mod_ref[0], 3, 4)
    ht = h.T.astype(BF16)
    ht_ref[0] = ht

    tm = ht.shape[1]
    group = 2
    q_rows = group * 2 * LANES

    def group_chain(gidx):
        heads = [gidx * group + i for i in range(group)]
        rows = slice(gidx * q_rows, (gidx + 1) * q_rows)
        qt_scr[rows, :] = jnp.dot(wq_ref[rows, :], ht, preferred_element_type=F32)
        yield
        scores = []
        for hh in heads:
            for half in range(2):
                r = slice((hh * 2 + half) * LANES, (hh * 2 + half + 1) * LANES)
                scores.append(jnp.dot(keys_ref[hh, half], qt_scr[r, :].astype(BF16),
                                      preferred_element_type=F32))
        tops = _top_values(jnp.concatenate(scores, axis=1), k + 1)
        cands = []
        for i in range(group):
            v1 = [t[:, (2 * i) * tm:(2 * i + 1) * tm] for t in tops]
            v2 = [t[:, (2 * i + 1) * tm:(2 * i + 2) * tm] for t in tops]
            top8 = jnp.concatenate(v2[:8], axis=0)
            pieces = [v1[0] + jnp.concatenate(v2[:k], axis=0)]
            pieces += [v1[j] + top8 for j in range(1, 8)]
            pieces += [jnp.concatenate(v1[8:k], axis=0) + v2[0]]
            pieces += [jnp.concatenate([v1[0] + v2[k], v1[k] + v2[0]]
                                       + [jnp.full_like(v1[0], NEG_BIG)] * 6, axis=0)]
            cands.append(jnp.concatenate(pieces, axis=0))
        best = _top_values(jnp.concatenate(cands, axis=1), k + 1)
        grp = (PEER_KEYS // SUBLANES, SUBLANES, tm)
        for i, hh in enumerate(heads):
            ms = [t[:, i * tm:(i + 1) * tm] for t in best]
            s1, s2 = scores[2 * i], scores[2 * i + 1]
            m1, m2 = tops[0][:, (2 * i) * tm:(2 * i + 1) * tm], tops[0][:, (2 * i + 1) * tm:(2 * i + 2) * tm]
            thr = 0.5 * (ms[k - 1] + ms[k])
            z = jnp.ones_like(ms[0])
            for r in range(1, k):
                z = z + jnp.exp(ms[r] - ms[0])
            c_ref[hh] = (thr - s1).reshape(grp)
            e1_ref[hh] = (jnp.exp(s1 - m1) * (0.5 / z)).reshape(grp)
            s2_ref[0, hh] = s2
            e2_ref[0, hh] = jnp.exp(s2 - m2)

    _emit_pipelined([group_chain(gidx) for gidx in range(PEER_HEADS // group)])


def _peer_scores(s, modv, g_ffn, wq_t, keys, rows, *, n_lat_rows, n_per_batch):
    d = s.shape[1]
    n_batch = modv.shape[0] - 1
    tm = SCORE_TM
    nlat_blocks = n_lat_rows // tm
    bpb = n_per_batch // tm
    nq = wq_t.shape[0]
    hk = (rows // tm, PEER_HEADS, PEER_KEYS, tm)
    hk_spec = pl.BlockSpec((1, PEER_HEADS, PEER_KEYS, tm), lambda i: (i, 0, 0, 0))
    hg = (PEER_HEADS, PEER_KEYS // SUBLANES, SUBLANES, rows)
    hg_spec = pl.BlockSpec((PEER_HEADS, PEER_KEYS // SUBLANES, SUBLANES, tm), lambda i: (0, 0, 0, i))
    return pl.pallas_call(
        _peer_score_kernel,
        grid=(rows // tm,),
        in_specs=[pl.BlockSpec((tm, d), lambda i: (i, 0)),
                  pl.BlockSpec((1, d), lambda i: (0, 0)),
                  pl.BlockSpec((1, 8, d), lambda i: (_mod_index(i, nlat_blocks, bpb, n_batch), 0, 0)),
                  pl.BlockSpec((nq, d), lambda i: (0, 0)),
                  pl.BlockSpec((PEER_HEADS, 2, PEER_KEYS, LANES), lambda i: (0, 0, 0, 0))],
        out_specs=[pl.BlockSpec((1, d, tm), lambda i: (i, 0, 0)), hg_spec, hg_spec, hk_spec, hk_spec],
        out_shape=[jax.ShapeDtypeStruct((rows // tm, d, tm), BF16)] + [jax.ShapeDtypeStruct(hg, F32)] * 2
                  + [jax.ShapeDtypeStruct(hk, F32)] * 2,
        scratch_shapes=[pltpu.VMEM((nq, tm), F32)],
        compiler_params=_cparams(("parallel",)),
        name="peer_scores",
    )(s, g_ffn.reshape(1, d), modv, wq_t, keys)


def _peer_expert_kernel(ht_ref, u_ref, vt_ref, c_ref, e1_ref, s2_ref, e2_ref, s_ref, mod_ref,
                        o_ref, acc_scr, st0_scr, st1_scr, at0_scr, at1_scr, *, n_exp_blocks):
    g = pl.program_id(0)
    st_scr, at_scr = (st0_scr, st1_scr), (at0_scr, at1_scr)
    te, tm = st0_scr.shape
    jc = jnp.maximum(g - 2, 0) % n_exp_blocks

    @pl.when(g == 0)
    def _():
        for ref in st_scr + at_scr:
            ref[...] = jnp.zeros_like(ref)

    @pl.when(jc == 0)
    def _():
        acc_scr[...] = jnp.zeros_like(acc_scr)

    def stages(slot_a, slot_b):
        n_chunks = te // MXU_CHUNK_ROWS

        def value_chunk(q):
            ms = slice(q * MXU_CHUNK_ROWS, (q + 1) * MXU_CHUNK_ROWS)
            acc_scr[ms, :] += jnp.dot(vt_ref[0, 0, ms, :], at_scr[slot_a][...], preferred_element_type=F32)

        def score_chunk(q):
            ms = slice(q * MXU_CHUNK_ROWS, (q + 1) * MXU_CHUNK_ROWS)
            st_scr[slot_a][ms, :] = jnp.dot(u_ref[0, ms, :], ht_ref[0], preferred_element_type=F32)

        rg_rows = 4 * SUBLANES
        al_group = 4
        col_tiles = 1
        tiles = [(rg, c0, al0)
                 for rg in range(LANES // rg_rows)
                 for c0 in range(0, tm // LANES, col_tiles)
                 for al0 in range(0, te // LANES, al_group)]
        assert acc_scr.shape[0] == te
        per_chunk = len(tiles) // (2 * n_chunks)
        for t, (rg, c0, al0) in enumerate(tiles):
            if t % per_chunk == 0:
                q = t // per_chunk
                (value_chunk if q % 2 == 0 else score_chunk)(q // 2)
            brows = slice(rg * rg_rows, (rg + 1) * rg_rows)
            lanes = [slice((c0 + ct) * LANES, (c0 + ct + 1) * LANES) for ct in range(col_tiles)]
            gate = [[jnp.zeros((rg_rows, LANES), F32) for _ in lanes] for _ in range(al_group)]
            for hh in range(PEER_HEADS):
                for ct, ln in enumerate(lanes):
                    s2t = s2_ref[0, hh, brows, ln]
                    e2t = e2_ref[0, hh, brows, ln]
                    for k in range(al_group):
                        al = al0 + k
                        c = c_ref[hh, 0, al:al + 1, ln]
                        e1 = e1_ref[hh, 0, al:al + 1, ln]
                        gate[k][ct] = gate[k][ct] + jnp.where(s2t >= c, e2t, 0.0) * e1
            for k in range(al_group):
                erows = slice((al0 + k) * LANES + rg * rg_rows, (al0 + k) * LANES + (rg + 1) * rg_rows)
                for ct, ln in enumerate(lanes):
                    sv = st_scr[slot_b][erows, ln]
                    act = sv * (1.0 + lax.erf(sv * (2.0 ** -0.5)))
                    at_scr[slot_b][erows, ln] = (gate[k][ct] * act).astype(BF16)

    for parity in range(2):
        @pl.when(g % 2 == parity)
        def _():
            stages(parity, 1 - parity)

    @pl.when((jc == n_exp_blocks - 1) & (g >= 2))
    def _():
        o_ref[...] = s_ref[...] + mod_ref[0][5:6] * acc_scr[...].T


def _peer_experts(ht, u, vt, layer, c, e1, s2, e2, s, modv, rows, *, n_lat_rows, n_per_batch):
    d = s.shape[1]
    n_exp = u.shape[1]
    n_batch = modv.shape[0] - 1
    tm, te = PEER_TM, PEER_TE
    nlat_blocks = n_lat_rows // tm
    bpb = n_per_batch // tm
    nj = n_exp // te
    n_pairs = (rows // tm) * nj

    def pair(g, lag):
        p = jnp.clip(g - lag, 0, n_pairs - 1)
        return p // nj, p % nj

    assert SCORE_TM == tm
    hk_spec = pl.BlockSpec((1, PEER_HEADS, PEER_KEYS, tm), lambda g: (pair(g, 1)[0], 0, 0, 0))
    hg_spec = pl.BlockSpec((PEER_HEADS, 1, SUBLANES, tm), lambda g: (0, pair(g, 1)[1], 0, pair(g, 1)[0]))
    kern = functools.partial(_peer_expert_kernel, n_exp_blocks=nj)
    return pl.pallas_call(
        kern,
        grid=(n_pairs + 2,),
        in_specs=[pl.BlockSpec((1, d, tm), lambda g: (pair(g, 0)[0], 0, 0)),
                  pl.BlockSpec((1, te, d), lambda g: (layer, pair(g, 0)[1], 0)),
                  pl.BlockSpec((1, 1, d, te), lambda g: (layer, pair(g, 2)[1], 0, 0)),
                  hg_spec, hg_spec, hk_spec, hk_spec,
                  pl.BlockSpec((tm, d), lambda g: (pair(g, 2)[0], 0)),
                  pl.BlockSpec((1, 8, d),
                               lambda g: (_mod_index(pair(g, 2)[0], nlat_blocks, bpb, n_batch), 0, 0))],
        out_specs=pl.BlockSpec((tm, d), lambda g: (pair(g, 2)[0], 0)),
        out_shape=jax.ShapeDtypeStruct((rows, d), F32),
        scratch_shapes=[pltpu.VMEM((d, tm), F32), pltpu.VMEM((te, tm), F32), pltpu.VMEM((te, tm), F32),
                        pltpu.VMEM((te, tm), BF16), pltpu.VMEM((te, tm), BF16)],
        compiler_params=_cparams(("arbitrary",)),
        name="peer_experts",
    )(ht, u, vt, c, e1, s2, e2, s, modv)


def _peer(s, modv, g_ffn, w_q, keys, u_all, vt_all, layer, rows, *, n_lat_rows, n_per_batch):
    geo = dict(n_lat_rows=n_lat_rows, n_per_batch=n_per_batch)
    ht, c, e1, s2, e2 = _peer_scores(s, modv, g_ffn, w_q.T.astype(BF16), keys.astype(BF16), rows, **geo)
    return _peer_experts(ht, u_all, vt_all, layer, c, e1, s2, e2, s, modv, rows, **geo)


def kernel(x, c, ctx, c_ctx, ada_w, ada_b, norm_mix, norm_ffn, mla_w_down, mla_g_cq, mla_g_ckv, mla_w_uq, mla_w_ukv, mla_g_q, mla_g_k, mla_w_o, diff_w_qkv, diff_g_q, diff_g_k, diff_lambda, diff_g_sub, diff_w_o, swa_w_qkv, swa_g_q, swa_g_k, swa_sink, swa_w_o, peer_w_q, peer_keys, peer_u, peer_v):
    n_batch, n_lat, d = x.shape
    n_ctx = ctx.shape[1]
    depth = ada_w.shape[0]
    rows_lat = n_batch * n_lat
    rows_all = rows_lat + n_batch * n_ctx
    assert n_lat % 512 == 0 and n_ctx % 256 == 0 and (n_batch * n_ctx) % PEER_TM == 0
    assert n_lat % GRID_W == 0 and rows_lat % n_ctx == 0 and n_lat >= 3 * Q_BLOCK

    s = jnp.concatenate([x.reshape(rows_lat, d), ctx.reshape(n_batch * n_ctx, d)], axis=0)
    cc = jnp.zeros((16, d), F32).at[:n_batch].set(c).at[n_batch].set(c_ctx)
    mods = _ada_mods(cc, ada_w, ada_b)
    geo = dict(n_lat_rows=rows_lat, n_per_batch=n_lat)
    n_exp = peer_u.shape[1]
    u_all = peer_u.astype(BF16)
    vt_all = peer_v.reshape(depth, n_exp // PEER_TE, PEER_TE, d).transpose(0, 1, 3, 2).astype(BF16)

    for i in range(depth):
        last = i == depth - 1
        kind, j = i % 3, i // 3
        modv = jnp.zeros((n_batch + 1, 8, d), F32).at[:, :6, :].set(
            mods[i, :n_batch + 1].reshape(n_batch + 1, 6, d))
        dims = dict(n_batch=n_batch, n_lat=n_lat, n_ctx=n_ctx, need_ctx=not last)
        if kind == 0:
            p = (mla_w_down[j], mla_g_cq[j], mla_g_ckv[j], mla_w_uq[j], mla_w_ukv[j], mla_g_q[j], mla_g_k[j])
            o, wo = _mla_mixer(s, modv, norm_mix[i], p, mla_w_o[j], **dims)
        elif kind == 1:
            lam_init = 0.8 - 0.6 * math.exp(-0.3 * i)
            o, wo = _diff_mixer(s, modv, norm_mix[i], diff_w_qkv[j], diff_g_q[j], diff_g_k[j],
                                diff_lambda[j], diff_g_sub[j], diff_w_o[j], lam_init, **dims)
        else:
            o, wo = _swa_mixer(s, modv, norm_mix[i], swa_w_qkv[j], swa_g_q[j], swa_g_k[j],
                               swa_sink[j], swa_w_o[j], **dims)
        rows = rows_lat if last else rows_all
        s = _proj_resid(o, wo, s, modv, rows, gate_row=2, **geo)
        s = _peer(s, modv, norm_ffn[i], peer_w_q[i], peer_keys[i], u_all, vt_all, i, rows, **geo)
    return s.reshape(n_batch, n_lat, d)
```

```python
import functools
import math

import jax
import jax.numpy as jnp
from jax import lax
from jax.experimental import pallas as pl
from jax.experimental.pallas import tpu as pltpu

F32 = jnp.float32
BF16 = jnp.bfloat16

LANES = 128
EPS = 1e-6
NEG_INF = -1e30
NEG_BIG = -3.0e38
ROPE_THETA = 10000.0
GRID_W = 64
Q_BLOCK = 128
WINDOW = 128

MLA_HEADS, MLA_NOPE, MLA_ROPE, MLA_V = 16, 64, 32, 64
MLA_QK = MLA_NOPE + MLA_ROPE
MLA_Q_LORA, MLA_KV_LORA = 768, 256
DIFF_HEADS, DIFF_HD = 8, 64
SWA_HEADS, SWA_KV_HEADS, SWA_HD = 16, 4, 64
SWA_GROUP = SWA_HEADS // SWA_KV_HEADS
PEER_HEADS, PEER_KEYS, PEER_TOPK = 8, 128, 16
PEER_EXPERTS = PEER_KEYS * PEER_KEYS

VMEM_LIMIT = 56 * 1024 * 1024

TM = 512
SCORE_TM = 512
TQ = 256
PEER_TM = 512
SUBLANES = 8
PEER_TE = SUBLANES * LANES
MXU_CHUNK_ROWS = 256


def _cparams(sem):
    return pltpu.CompilerParams(dimension_semantics=sem, vmem_limit_bytes=VMEM_LIMIT)


def _ada_kernel(c_ref, w_ref, b_ref, o_ref):
    c = c_ref[...]
    sc = c * (1.0 / (1.0 + jnp.exp(-c)))
    o_ref[0] = jnp.dot(sc, w_ref[0], preferred_element_type=F32,
                       precision=lax.Precision.HIGHEST) + b_ref[0]


def _ada_mods(cc, ada_w, ada_b):
    depth, d, d6 = ada_w.shape
    rows = cc.shape[0]
    tn = 1536
    return pl.pallas_call(
        _ada_kernel,
        grid=(depth, d6 // tn),
        in_specs=[pl.BlockSpec((rows, d), lambda i, n: (0, 0)),
                  pl.BlockSpec((1, d, tn), lambda i, n: (i, 0, n)),
                  pl.BlockSpec((1, 1, tn), lambda i, n: (i, 0, n))],
        out_specs=pl.BlockSpec((1, rows, tn), lambda i, n: (i, 0, n)),
        out_shape=jax.ShapeDtypeStruct((depth, rows, d6), F32),
        compiler_params=_cparams(("arbitrary", "arbitrary")),
        name="ada_mods",
    )(cc, ada_w, ada_b.reshape(depth, 1, d6))


def _norm_mod(x, g, mod, shift_row, scale_row):
    ms = jnp.mean(x * x, axis=-1, keepdims=True)
    y = x * lax.rsqrt(ms + EPS) * g
    if mod is not None:
        y = y * (1.0 + mod[scale_row:scale_row + 1]) + mod[shift_row:shift_row + 1]
    return y


def _nm_kernel(x_ref, g_ref, mod_ref, w_ref, o_ref, h_scr, *, use_mod, shift_row, scale_row):
    @pl.when(pl.program_id(1) == 0)
    def _():
        mod = mod_ref[0] if use_mod else None
        h_scr[...] = _norm_mod(x_ref[...], g_ref[...], mod, shift_row, scale_row).astype(BF16)

    o_ref[...] = jnp.dot(h_scr[...], w_ref[...], preferred_element_type=F32).astype(o_ref.dtype)


def _mod_index(i, nlat_blocks, blocks_per_batch, n_batch):
    return jnp.where(i < nlat_blocks, i // blocks_per_batch, n_batch)


def _nm_matmul(x, kblk, k, g, modv, rows, w, *, n_lat_rows, n_per_batch, out_dtype=F32,
               use_mod=True, shift_row=0, scale_row=1, name="nm_matmul"):
    nout = w.shape[1]
    tn = nout
    for cand in (2048, 1536, 1280, 1024):
        if nout > 2048 and nout % cand == 0:
            tn = cand
            break
    n_batch = modv.shape[0] - 1
    nlat_blocks = n_lat_rows // TM
    bpb = n_per_batch // TM
    kern = functools.partial(_nm_kernel, use_mod=use_mod, shift_row=shift_row, scale_row=scale_row)
    return pl.pallas_call(
        kern,
        grid=(rows // TM, nout // tn),
        in_specs=[pl.BlockSpec((TM, k), lambda i, n: (i, kblk)),
                  pl.BlockSpec((1, k), lambda i, n: (0, 0)),
                  pl.BlockSpec((1, 8, k), lambda i, n: (_mod_index(i, nlat_blocks, bpb, n_batch), 0, 0)),
                  pl.BlockSpec((k, tn), lambda i, n: (0, n))],
        out_specs=pl.BlockSpec((TM, tn), lambda i, n: (i, n)),
        out_shape=jax.ShapeDtypeStruct((rows, nout), out_dtype),
        scratch_shapes=[pltpu.VMEM((TM, k), BF16)],
        compiler_params=_cparams(("parallel", "arbitrary")),
        name=name,
    )(x, g.reshape(1, k), modv if modv.shape[-1] == k else jnp.zeros((n_batch + 1, 8, k), F32), w)


def _proj_resid_kernel(a_ref, w_ref, s_ref, mod_ref, o_ref, *, gate_row):
    y = jnp.dot(a_ref[...].astype(BF16), w_ref[...], preferred_element_type=F32)
    o_ref[...] = s_ref[...] + mod_ref[0][gate_row:gate_row + 1] * y


def _proj_resid(a, w, s, modv, rows, *, n_lat_rows, n_per_batch, gate_row, name="proj_resid"):
    ka = a.shape[1]
    d = w.shape[1]
    n_batch = modv.shape[0] - 1
    nlat_blocks = n_lat_rows // TM
    bpb = n_per_batch // TM
    return pl.pallas_call(
        functools.partial(_proj_resid_kernel, gate_row=gate_row),
        grid=(rows // TM,),
        in_specs=[pl.BlockSpec((TM, ka), lambda i: (i, 0)),
                  pl.BlockSpec((ka, d), lambda i: (0, 0)),
                  pl.BlockSpec((TM, d), lambda i: (i, 0)),
                  pl.BlockSpec((1, 8, d), lambda i: (_mod_index(i, nlat_blocks, bpb, n_batch), 0, 0))],
        out_specs=pl.BlockSpec((TM, d), lambda i: (i, 0)),
        out_shape=jax.ShapeDtypeStruct((rows, d), F32),
        compiler_params=_cparams(("parallel",)),
        name=name,
    )(a, w, s, modv)


def _rope_tables(n_lat, extra_rows, segments, nf):
    pos = jnp.arange(n_lat, dtype=jnp.int32)
    row = (pos // GRID_W).astype(F32)
    col = (pos % GRID_W).astype(F32)
    inv = ROPE_THETA ** (-jnp.arange(nf, dtype=F32) / nf)
    cos = jnp.ones((n_lat, LANES), F32)
    sa = jnp.zeros((n_lat, LANES), F32)
    sb = jnp.zeros((n_lat, LANES), F32)
    for first, use_row in segments:
        ang = (row if use_row else col)[:, None] * inv[None, :]
        c, s = jnp.cos(ang), jnp.sin(ang)
        cos = cos.at[:, first:first + nf].set(c).at[:, first + nf:first + 2 * nf].set(c)
        sa = sa.at[:, first:first + nf].set(-s)
        sb = sb.at[:, first + nf:first + 2 * nf].set(s)
    pad = lambda t, v: jnp.concatenate([t, jnp.full((extra_rows, LANES), v, F32)], axis=0)
    return pad(cos, 1.0), pad(sa, 0.0), pad(sb, 0.0)


def _head_norm(x, g, d):
    ssq = jnp.sum(x * x, axis=-1, keepdims=True) * (1.0 / d)
    return x * lax.rsqrt(ssq + EPS) * g


def _rope(y, cos, sa, sb, nf):
    return y * cos + pltpu.roll(y, LANES - nf, 1) * sa + pltpu.roll(y, nf, 1) * sb


def _with_ones_lane(v, lane):
    return jnp.where(lax.broadcasted_iota(jnp.int32, v.shape, 1) == lane, jnp.ones_like(v), v)


def _row_block(b, qi, nlat_q, nctx_q, n_batch):
    return jnp.where(qi < nlat_q, b * nlat_q + qi, n_batch * nlat_q + b * nctx_q + (qi - nlat_q))


def _prep_keys(k_scr, v_scr, n_lat, n_ctx, load_k_lat, load_k_ctx, load_v_lat, load_v_ctx,
               gk, tabs, d, nf):
    cos_ref, sa_ref, sb_ref = tabs
    ch = 256

    def lat_body(c, carry):
        r = pl.ds(pl.multiple_of(c * ch, ch), ch)
        kk = _head_norm(load_k_lat(r).astype(F32), gk, d)
        kk = _rope(kk, cos_ref[r, :], sa_ref[r, :], sb_ref[r, :], nf)
        k_scr[r, :] = kk.astype(BF16)
        if v_scr is not None:
            v_scr[r, :] = load_v_lat(r).astype(BF16)
        return carry

    lax.fori_loop(0, n_lat // ch, lat_body, 0, unroll=2)

    def ctx_body(c, carry):
        r = pl.ds(pl.multiple_of(c * ch, ch), ch)
        ro = pl.ds(pl.multiple_of(n_lat + c * ch, ch), ch)
        k_scr[ro, :] = _head_norm(load_k_ctx(r).astype(F32), gk, d).astype(BF16)
        if v_scr is not None:
            v_scr[ro, :] = load_v_ctx(r).astype(BF16)
        return carry

    lax.fori_loop(0, n_ctx // ch, ctx_body, 0)


def _nt_dot(a, b):
    return lax.dot_general(a, b, (((1,), (1,)), ((), ())), preferred_element_type=F32)


LOG2E = math.log2(math.e)
CHAIN_ROWS = 128
MLA_HEADS_PER_STEP = 4
HEADS_PER_STEP = 4


def _row_chains(n_rows):
    return [slice(r, r + CHAIN_ROWS) for r in range(0, n_rows, CHAIN_ROWS)]


def _emit_pipelined(chains, width=1):
    pending, active = list(chains), []
    while pending or active:
        for _ in range(width):
            if pending:
                active.append(pending.pop(0))
        for ch in list(active):
            try:
                next(ch)
            except StopIteration:
                active.remove(ch)


def _exp2_scores(q, k):
    s = _nt_dot(q, k)
    p = jnp.exp2(s - jnp.max(s, axis=-1, keepdims=True))
    return p, jnp.sum(p, axis=-1, keepdims=True)


def _mla_attn_kernel(q_ref, kl_ref, kc_ref, krl_ref, krc_ref, vl_ref, vc_ref,
                     cq_ref, saq_ref, sbq_ref, ck_ref, sak_ref, sbk_ref, gq_ref, gk_ref,
                     o_ref, k_scr, v_scr, *, n_lat, n_ctx, nlat_q, scale):
    qi = pl.program_id(2)
    nf = MLA_ROPE // 4
    heads = [slice(hd * LANES, (hd + 1) * LANES) for hd in range(q_ref.shape[1] // LANES)]

    @pl.when(qi == 0)
    def _():
        for hd, ln in enumerate(heads):
            _prep_keys(k_scr.at[hd], v_scr.at[hd], n_lat, n_ctx,
                       lambda r, ln=ln: kl_ref[r, ln].astype(F32) + krl_ref[r, :],
                       lambda r, ln=ln: kc_ref[r, ln].astype(F32) + krc_ref[r, :],
                       lambda r, ln=ln: _with_ones_lane(vl_ref[r, ln], MLA_V),
                       lambda r, ln=ln: _with_ones_lane(vc_ref[r, ln], MLA_V),
                       gk_ref[...], (ck_ref, sak_ref, sbk_ref), MLA_QK, nf)

    def attend(keys):
        def chain(hd, ln, rows):
            q = _head_norm(q_ref[rows, ln].astype(F32), gq_ref[...], MLA_QK)
            q = _rope(q, cq_ref[rows, :], saq_ref[rows, :], sbq_ref[rows, :], nf) * (scale * LOG2E)
            s = _nt_dot(q.astype(BF16), k_scr[hd, keys, :])
            yield
            p = jnp.exp2(s - jnp.max(s, axis=-1, keepdims=True)).astype(BF16)
            yield
            o = jnp.dot(p, v_scr[hd, keys, :], preferred_element_type=F32)
            o_ref[rows, ln] = (o * (1.0 / o[:, MLA_V:MLA_V + 1])).astype(o_ref.dtype)

        _emit_pipelined([chain(hd, ln, rows) for rows in _row_chains(q_ref.shape[0])
                         for hd, ln in enumerate(heads)])

    @pl.when(qi < nlat_q)
    def _():
        attend(slice(0, n_lat + n_ctx))

    @pl.when(qi >= nlat_q)
    def _():
        attend(slice(n_lat, n_lat + n_ctx))


def _mla_attention(q, kv, lat, g_q, g_k, *, n_batch, n_lat, n_ctx, need_ctx):
    nlat_q, nctx_q = n_lat // TQ, n_ctx // TQ
    nq = nlat_q + (nctx_q if need_ctx else 0)
    rows = n_batch * (n_lat + (n_ctx if need_ctx else 0))
    nf = MLA_ROPE // 4
    tabs = _rope_tables(n_lat, TQ, ((MLA_NOPE, True), (MLA_NOPE + 2 * nf, False)), nf)
    h = MLA_HEADS
    ctx0 = n_batch * n_lat // n_ctx
    kr_blk = (MLA_Q_LORA + MLA_KV_LORA) // LANES
    gq = jnp.zeros((1, LANES), F32).at[0, :MLA_QK].set(g_q)
    gk = jnp.zeros((1, LANES), F32).at[0, :MLA_QK].set(g_k)
    hps = MLA_HEADS_PER_STEP
    hw = hps * LANES
    qspec = pl.BlockSpec((TQ, hw), lambda b, hh, qi: (_row_block(b, qi, nlat_q, nctx_q, n_batch), hh))
    tq_spec = pl.BlockSpec((TQ, LANES), lambda b, hh, qi: (jnp.minimum(qi, nlat_q), 0))
    tk_spec = pl.BlockSpec((n_lat, LANES), lambda b, hh, qi: (0, 0))
    vec = pl.BlockSpec((1, LANES), lambda b, hh, qi: (0, 0))
    kern = functools.partial(_mla_attn_kernel, n_lat=n_lat, n_ctx=n_ctx, nlat_q=nlat_q,
                             scale=MLA_QK ** -0.5)
    return pl.pallas_call(
        kern,
        grid=(n_batch, h // hps, nq),
        in_specs=[qspec,
                  pl.BlockSpec((n_lat, hw), lambda b, hh, qi: (b, hh)),
                  pl.BlockSpec((n_ctx, hw), lambda b, hh, qi: (ctx0 + b, hh)),
                  pl.BlockSpec((n_lat, LANES), lambda b, hh, qi: (b, kr_blk)),
                  pl.BlockSpec((n_ctx, LANES), lambda b, hh, qi: (ctx0 + b, kr_blk)),
                  pl.BlockSpec((n_lat, hw), lambda b, hh, qi: (b, h // hps + hh)),
                  pl.BlockSpec((n_ctx, hw), lambda b, hh, qi: (ctx0 + b, h // hps + hh)),
                  tq_spec, tq_spec, tq_spec, tk_spec, tk_spec, tk_spec, vec, vec],
        out_specs=qspec,
        out_shape=jax.ShapeDtypeStruct((rows, h * LANES), BF16),
        scratch_shapes=[pltpu.VMEM((hps, n_lat + n_ctx, LANES), BF16),
                        pltpu.VMEM((hps, n_lat + n_ctx, LANES), BF16)],
        compiler_params=_cparams(("parallel", "parallel", "arbitrary")),
        name="mla_attention",
    )(q, kv, kv, lat, lat, kv, kv, *tabs, *tabs, gq, gk)


def _pad_cols(w, groups, width, first=0):
    k = w.shape[0]
    w3 = w.reshape(k, groups, width)
    out = jnp.zeros((k, groups, LANES), w.dtype).at[:, :, first:first + width].set(w3)
    return out.reshape(k, groups * LANES)


def _pad_rows(w, groups, width):
    d = w.shape[1]
    w3 = w.reshape(groups, width, d)
    return jnp.zeros((groups, LANES, d), w.dtype).at[:, :width, :].set(w3).reshape(groups * LANES, d)


def _mla_mixer(s, modv, g_mix, p, w_o, *, n_batch, n_lat, n_ctx, need_ctx):
    w_down, g_cq, g_ckv, w_uq, w_ukv, g_q, g_k = p
    d = s.shape[1]
    rows_all = n_batch * (n_lat + n_ctx)
    geo = dict(n_lat_rows=n_batch * n_lat, n_per_batch=n_lat)
    wd = jnp.concatenate([w_down[:, :MLA_Q_LORA + MLA_KV_LORA],
                          _pad_cols(w_down[:, MLA_Q_LORA + MLA_KV_LORA:], 1, MLA_ROPE, MLA_NOPE)], axis=1)
    lat = _nm_matmul(s, 0, d, g_mix, modv, rows_all, wd.astype(BF16), name="mla_down", **geo)
    wq = _pad_cols(w_uq, MLA_HEADS, MLA_QK).astype(BF16)
    q = _nm_matmul(lat, 0, MLA_Q_LORA, g_cq, modv, rows_all, wq, use_mod=False, name="mla_uq",
                   out_dtype=BF16, **geo)
    wkv3 = w_ukv.reshape(MLA_KV_LORA, MLA_HEADS, MLA_NOPE + MLA_V)
    wkv = jnp.concatenate([_pad_cols(wkv3[:, :, :MLA_NOPE].reshape(MLA_KV_LORA, -1), MLA_HEADS, MLA_NOPE),
                           _pad_cols(wkv3[:, :, MLA_NOPE:].reshape(MLA_KV_LORA, -1), MLA_HEADS, MLA_V)],
                          axis=1).astype(BF16)
    kv = _nm_matmul(lat, MLA_Q_LORA // MLA_KV_LORA, MLA_KV_LORA, g_ckv, modv, rows_all, wkv,
                    use_mod=False, name="mla_ukv", out_dtype=BF16, **geo)
    o = _mla_attention(q, kv, lat, g_q, g_k, n_batch=n_batch, n_lat=n_lat, n_ctx=n_ctx, need_ctx=need_ctx)
    wo = _pad_rows(w_o, MLA_HEADS, MLA_V).astype(BF16)
    return o, wo


def _diff_attn_kernel(q_ref, kl_ref, kc_ref, vl_ref, vc_ref,
                      cq_ref, saq_ref, sbq_ref, ck_ref, sak_ref, sbk_ref, gq_ref, gk_ref,
                      lam_ref, gsub_ref, o_ref, k_scr, v_scr,
                      *, n_lat, n_ctx, nlat_q, scale, lam_init):
    qi = pl.program_id(2)
    nf = DIFF_HD // 4
    tabs = (ck_ref, sak_ref, sbk_ref)
    tile = lambda t: slice(t * LANES, (t + 1) * LANES)
    n_heads = o_ref.shape[1] // LANES

    @pl.when(qi == 0)
    def _():
        for hd in range(n_heads):
            ln = tile(hd)
            for sub in range(2):
                kt = tile(2 * hd + sub)
                _prep_keys(k_scr.at[2 * hd + sub], v_scr.at[hd] if sub == 0 else None, n_lat, n_ctx,
                           lambda r, kt=kt: kl_ref[r, kt], lambda r, kt=kt: kc_ref[r, kt],
                           lambda r, ln=ln: vl_ref[r, ln], lambda r, ln=ln: vc_ref[r, ln],
                           gk_ref[...], tabs, DIFF_HD, nf)

    lp = lam_ref[...]
    lam = (jnp.exp(jnp.sum(lp[0:1] * lp[1:2], axis=-1, keepdims=True))
           - jnp.exp(jnp.sum(lp[2:3] * lp[3:4], axis=-1, keepdims=True)) + lam_init)

    def attend(keys):
        def scores(t, rows):
            q = _head_norm(q_ref[rows, tile(t)].astype(F32), gq_ref[...], DIFF_HD)
            q = _rope(q, cq_ref[rows, :], saq_ref[rows, :], sbq_ref[rows, :], nf) * (scale * LOG2E)
            return _nt_dot(q.astype(BF16), k_scr[t, keys, :])

        def chain(hd, rows):
            s0 = scores(2 * hd, rows)
            s1 = scores(2 * hd + 1, rows)
            yield
            p0 = jnp.exp2(s0 - jnp.max(s0, axis=-1, keepdims=True))
            p1 = jnp.exp2(s1 - jnp.max(s1, axis=-1, keepdims=True))
            l0 = jnp.sum(p0, axis=-1, keepdims=True)
            l1 = jnp.sum(p1, axis=-1, keepdims=True)
            a = (p0 * (1.0 / l0) - p1 * (lam / l1)).astype(BF16)
            yield
            o = jnp.dot(a, v_scr[hd, keys, :], preferred_element_type=F32)
            o = _head_norm(o, gsub_ref[...], 2 * DIFF_HD) * (1.0 - lam_init)
            o_ref[rows, tile(hd)] = o.astype(o_ref.dtype)

        _emit_pipelined([chain(hd, rows) for rows in _row_chains(q_ref.shape[0])
                         for hd in range(n_heads)])

    @pl.when(qi < nlat_q)
    def _():
        attend(slice(0, n_lat + n_ctx))

    @pl.when(qi >= nlat_q)
    def _():
        attend(slice(n_lat, n_lat + n_ctx))


def _diff_attention(qkv, g_q, g_k, lam_p, g_sub, lam_init, *, n_batch, n_lat, n_ctx, need_ctx):
    nlat_q, nctx_q = n_lat // TQ, n_ctx // TQ
    nq = nlat_q + (nctx_q if need_ctx else 0)
    rows = n_batch * (n_lat + (n_ctx if need_ctx else 0))
    nf = DIFF_HD // 4
    tabs = _rope_tables(n_lat, TQ, ((0, True), (2 * nf, False)), nf)
    nh = DIFF_HEADS
    ctx0 = n_batch * n_lat // n_ctx
    gq = jnp.zeros((1, LANES), F32).at[0, :DIFF_HD].set(g_q)
    gk = jnp.zeros((1, LANES), F32).at[0, :DIFF_HD].set(g_k)
    lam8 = jnp.zeros((8, LANES), F32).at[:4, :DIFF_HD].set(lam_p.astype(F32))
    rb = lambda b, qi: _row_block(b, qi, nlat_q, nctx_q, n_batch)
    tq_spec = pl.BlockSpec((TQ, LANES), lambda b, hh, qi: (jnp.minimum(qi, nlat_q), 0))
    tk_spec = pl.BlockSpec((n_lat, LANES), lambda b, hh, qi: (0, 0))
    vec = pl.BlockSpec((1, LANES), lambda b, hh, qi: (0, 0))
    hps = HEADS_PER_STEP
    qk_w, v_w = 2 * hps * LANES, hps * LANES
    k_blk0 = 2 * nh // (2 * hps)
    v_blk0 = 4 * nh // hps
    kern = functools.partial(_diff_attn_kernel, n_lat=n_lat, n_ctx=n_ctx, nlat_q=nlat_q,
                             scale=DIFF_HD ** -0.5, lam_init=lam_init)
    n_all = n_lat + n_ctx
    return pl.pallas_call(
        kern,
        grid=(n_batch, nh // hps, nq),
        in_specs=[pl.BlockSpec((TQ, qk_w), lambda b, hh, qi: (rb(b, qi), hh)),
                  pl.BlockSpec((n_lat, qk_w), lambda b, hh, qi: (b, k_blk0 + hh)),
                  pl.BlockSpec((n_ctx, qk_w), lambda b, hh, qi: (ctx0 + b, k_blk0 + hh)),
                  pl.BlockSpec((n_lat, v_w), lambda b, hh, qi: (b, v_blk0 + hh)),
                  pl.BlockSpec((n_ctx, v_w), lambda b, hh, qi: (ctx0 + b, v_blk0 + hh)),
                  tq_spec, tq_spec, tq_spec, tk_spec, tk_spec, tk_spec, vec, vec,
                  pl.BlockSpec((8, LANES), lambda b, hh, qi: (0, 0)), vec],
        out_specs=pl.BlockSpec((TQ, v_w), lambda b, hh, qi: (rb(b, qi), hh)),
        out_shape=jax.ShapeDtypeStruct((rows, nh * LANES), BF16),
        scratch_shapes=[pltpu.VMEM((2 * hps, n_all, LANES), BF16), pltpu.VMEM((hps, n_all, LANES), BF16)],
        compiler_params=_cparams(("parallel", "parallel", "arbitrary")),
        name="diff_attention",
    )(qkv, qkv, qkv, qkv, qkv, *tabs, *tabs, gq, gk, lam8, g_sub.reshape(1, LANES))


def _diff_mixer(s, modv, g_mix, w_qkv, g_q, g_k, lam_p, g_sub, w_o, lam_init,
                *, n_batch, n_lat, n_ctx, need_ctx):
    d = s.shape[1]
    width = DIFF_HEADS * 2 * DIFF_HD
    rows_all = n_batch * (n_lat + n_ctx)
    w = jnp.concatenate([_pad_cols(w_qkv[:, :width], 2 * DIFF_HEADS, DIFF_HD),
                         _pad_cols(w_qkv[:, width:2 * width], 2 * DIFF_HEADS, DIFF_HD),
                         w_qkv[:, 2 * width:]], axis=1).astype(BF16)
    qkv = _nm_matmul(s, 0, d, g_mix, modv, rows_all, w, n_lat_rows=n_batch * n_lat,
                     n_per_batch=n_lat, name="diff_qkv", out_dtype=BF16)
    o = _diff_attention(qkv, g_q, g_k, lam_p, g_sub, lam_init,
                        n_batch=n_batch, n_lat=n_lat, n_ctx=n_ctx, need_ctx=need_ctx)
    return o, w_o.astype(BF16)


def _swa_attn_kernel(q_ref, kl_ref, kc_ref, vl_ref, vc_ref,
                     cq_ref, saq_ref, sbq_ref, ck_ref, sak_ref, sbk_ref, gq_ref, gk_ref, sink_ref,
                     o_ref, k_scr, v_scr, *, n_lat, n_ctx, nlat_q, scale):
    qi = pl.program_id(2)
    nf = SWA_HD // 4
    band = 3 * Q_BLOCK

    @pl.when(qi == 0)
    def _():
        _prep_keys(k_scr, v_scr, n_lat, n_ctx, lambda r: kl_ref[r, :], lambda r: kc_ref[r, :],
                   lambda r: _with_ones_lane(vl_ref[r, :], SWA_HD),
                   lambda r: _with_ones_lane(vc_ref[r, :], SWA_HD), gk_ref[...],
                   (ck_ref, sak_ref, sbk_ref), SWA_HD, nf)

    row_chains = _row_chains(q_ref.shape[0])
    reach = jnp.where(qi < nlat_q, WINDOW, -1)

    def chain(ci, rows, gi):
        qb = qi * len(row_chains) + ci
        k0 = pl.multiple_of(jnp.clip((qb - 1) * Q_BLOCK, 0, n_lat - band), Q_BLOCK)
        kpos = k0 + lax.broadcasted_iota(jnp.int32, (Q_BLOCK, band), 1)
        qpos = qb * Q_BLOCK + lax.broadcasted_iota(jnp.int32, (Q_BLOCK, band), 0)
        lanes = slice(gi * LANES, (gi + 1) * LANES)
        q = _head_norm(q_ref[rows, lanes].astype(F32), gq_ref[...], SWA_HD)
        q = _rope(q, cq_ref[rows, :], saq_ref[rows, :], sbq_ref[rows, :], nf) * (scale * LOG2E)
        q = q.astype(BF16)
        sb = _nt_dot(q, k_scr[pl.ds(k0, band), :])
        sc = _nt_dot(q, k_scr[n_lat:n_lat + n_ctx, :])
        yield
        sb = jnp.where(jnp.abs(kpos - qpos) <= reach, sb, NEG_INF)
        sink = sink_ref[gi][0:1, 0:1] * LOG2E
        m = jnp.maximum(jnp.maximum(jnp.max(sb, axis=-1, keepdims=True),
                                    jnp.max(sc, axis=-1, keepdims=True)), sink)
        pb = jnp.exp2(sb - m).astype(BF16)
        pc = jnp.exp2(sc - m).astype(BF16)
        p_sink = jnp.exp2(sink - m)
        yield
        o = (jnp.dot(pb, v_scr[pl.ds(k0, band), :], preferred_element_type=F32)
             + jnp.dot(pc, v_scr[n_lat:n_lat + n_ctx, :], preferred_element_type=F32))
        l = o[:, SWA_HD:SWA_HD + 1] + p_sink
        o_ref[rows, lanes] = (o * (1.0 / l)).astype(o_ref.dtype)

    _emit_pipelined([chain(ci, rows, gi) for ci, rows in enumerate(row_chains)
                     for gi in range(SWA_GROUP)])


def _swa_attention(qkv, g_q, g_k, sink, *, n_batch, n_lat, n_ctx, need_ctx):
    nlat_q, nctx_q = n_lat // TQ, n_ctx // TQ
    nq = nlat_q + (nctx_q if need_ctx else 0)
    rows = n_batch * (n_lat + (n_ctx if need_ctx else 0))
    nf = SWA_HD // 4
    tabs = _rope_tables(n_lat, TQ, ((0, True), (2 * nf, False)), nf)
    nh, nkv, grp = SWA_HEADS, SWA_KV_HEADS, SWA_GROUP
    ctx0 = n_batch * n_lat // n_ctx
    gq = jnp.zeros((1, LANES), F32).at[0, :SWA_HD].set(g_q)
    gk = jnp.zeros((1, LANES), F32).at[0, :SWA_HD].set(g_k)
    sink_t = jnp.broadcast_to(sink.astype(F32).reshape(nh, 1, 1), (nh, 8, LANES))
    rb = lambda b, qi: _row_block(b, qi, nlat_q, nctx_q, n_batch)
    tq_spec = pl.BlockSpec((TQ, LANES), lambda b, kh, qi: (jnp.minimum(qi, nlat_q), 0))
    tk_spec = pl.BlockSpec((n_lat, LANES), lambda b, kh, qi: (0, 0))
    vec = pl.BlockSpec((1, LANES), lambda b, kh, qi: (0, 0))
    kern = functools.partial(_swa_attn_kernel, n_lat=n_lat, n_ctx=n_ctx, nlat_q=nlat_q,
                             scale=SWA_HD ** -0.5)
    n_all = n_lat + n_ctx
    return pl.pallas_call(
        kern,
        grid=(n_batch, nkv, nq),
        in_specs=[pl.BlockSpec((TQ, grp * LANES), lambda b, kh, qi: (rb(b, qi), kh)),
                  pl.BlockSpec((n_lat, LANES), lambda b, kh, qi: (b, nh + kh)),
                  pl.BlockSpec((n_ctx, LANES), lambda b, kh, qi: (ctx0 + b, nh + kh)),
                  pl.BlockSpec((n_lat, LANES), lambda b, kh, qi: (b, nh + nkv + kh)),
                  pl.BlockSpec((n_ctx, LANES), lambda b, kh, qi: (ctx0 + b, nh + nkv + kh)),
                  tq_spec, tq_spec, tq_spec, tk_spec, tk_spec, tk_spec, vec, vec,
                  pl.BlockSpec((grp, 8, LANES), lambda b, kh, qi: (kh, 0, 0))],
        out_specs=pl.BlockSpec((TQ, grp * LANES), lambda b, kh, qi: (rb(b, qi), kh)),
        out_shape=jax.ShapeDtypeStruct((rows, nh * LANES), BF16),
        scratch_shapes=[pltpu.VMEM((n_all, LANES), BF16), pltpu.VMEM((n_all, LANES), BF16)],
        compiler_params=_cparams(("parallel", "parallel", "arbitrary")),
        name="swa_attention",
    )(qkv, qkv, qkv, qkv, qkv, *tabs, *tabs, gq, gk, sink_t)


def _swa_mixer(s, modv, g_mix, w_qkv, g_q, g_k, sink, w_o, *, n_batch, n_lat, n_ctx, need_ctx):
    d = s.shape[1]
    qw, kw = SWA_HEADS * SWA_HD, SWA_KV_HEADS * SWA_HD
    rows_all = n_batch * (n_lat + n_ctx)
    w = jnp.concatenate([_pad_cols(w_qkv[:, :qw], SWA_HEADS, SWA_HD),
                         _pad_cols(w_qkv[:, qw:qw + kw], SWA_KV_HEADS, SWA_HD),
                         _pad_cols(w_qkv[:, qw + kw:], SWA_KV_HEADS, SWA_HD)], axis=1).astype(BF16)
    qkv = _nm_matmul(s, 0, d, g_mix, modv, rows_all, w, n_lat_rows=n_batch * n_lat,
                     n_per_batch=n_lat, name="swa_qkv", out_dtype=BF16)
    o = _swa_attention(qkv, g_q, g_k, sink, n_batch=n_batch, n_lat=n_lat, n_ctx=n_ctx, need_ctx=need_ctx)
    return o, _pad_rows(w_o, SWA_HEADS, SWA_HD).astype(BF16)


def _merge_exchange_pairs(n):
    pairs = []
    t = max(1, (n - 1).bit_length())
    p = 1 << (t - 1)
    while p > 0:
        q, r, d = 1 << (t - 1), 0, p
        while d > 0:
            pairs += [(i, i + d) for i in range(n - d) if (i & p) == r]
            d, q, r = q - p, q >> 1, p
        p >>= 1
    return pairs


def _top_values(s, count):
    n = s.shape[0] // SUBLANES
    v = [s[j * SUBLANES:(j + 1) * SUBLANES, :] for j in range(n)]
    for i, j in _merge_exchange_pairs(n):
        v[i], v[j] = jnp.maximum(v[i], v[j]), jnp.minimum(v[i], v[j])
    vals = []
    for r in range(count):
        m = jnp.max(v[0], axis=0, keepdims=True)
        vals.append(m)
        depth = min(n, count - 1 - r)
        head = v[0] == m
        for j in range(depth):
            nxt = v[j + 1] if j + 1 < n else jnp.full_like(v[j], NEG_BIG)
            v[j] = jnp.where(head, nxt, v[j])
    return vals


def _peer_score_kernel(s_ref, g_ref, mod_ref, wq_ref, keys_ref,
                       ht_ref, c_ref, e1_ref, s2_ref, e2_ref, qt_scr):
    k = PEER_TOPK
    h = _norm_mod(s_ref[...], g_ref[...], mod_ref[0], 3, 4)
    ht = h.T.astype(BF16)
    ht_ref[0] = ht

    tm = ht.shape[1]
    group = 2
    q_rows = group * 2 * LANES

    def group_chain(gidx):
        heads = [gidx * group + i for i in range(group)]
        rows = slice(gidx * q_rows, (gidx + 1) * q_rows)
        qt_scr[rows, :] = jnp.dot(wq_ref[rows, :], ht, preferred_element_type=F32)
        yield
        scores = []
        for hh in heads:
            for half in range(2):
                r = slice((hh * 2 + half) * LANES, (hh * 2 + half + 1) * LANES)
                scores.append(jnp.dot(keys_ref[hh, half], qt_scr[r, :].astype(BF16),
                                      preferred_element_type=F32))
        tops = _top_values(jnp.concatenate(scores, axis=1), k + 1)
        cands = []
        for i in range(group):
            v1 = [t[:, (2 * i) * tm:(2 * i + 1) * tm] for t in tops]
            v2 = [t[:, (2 * i + 1) * tm:(2 * i + 2) * tm] for t in tops]
            top8 = jnp.concatenate(v2[:8], axis=0)
            pieces = [v1[0] + jnp.concatenate(v2[:k], axis=0)]
            pieces += [v1[j] + top8 for j in range(1, 8)]
            pieces += [jnp.concatenate(v1[8:k], axis=0) + v2[0]]
            pieces += [jnp.concatenate([v1[0] + v2[k], v1[k] + v2[0]]
                                       + [jnp.full_like(v1[0], NEG_BIG)] * 6, axis=0)]
            cands.append(jnp.concatenate(pieces, axis=0))
        best = _top_values(jnp.concatenate(cands, axis=1), k + 1)
        grp = (PEER_KEYS // SUBLANES, SUBLANES, tm)
        for i, hh in enumerate(heads):
            ms = [t[:, i * tm:(i + 1) * tm] for t in best]
            s1, s2 = scores[2 * i], scores[2 * i + 1]
            m1, m2 = tops[0][:, (2 * i) * tm:(2 * i + 1) * tm], tops[0][:, (2 * i + 1) * tm:(2 * i + 2) * tm]
            thr = 0.5 * (ms[k - 1] + ms[k])
            z = jnp.ones_like(ms[0])
            for r in range(1, k):
                z = z + jnp.exp(ms[r] - ms[0])
            c_ref[hh] = (thr - s1).reshape(grp)
            e1_ref[hh] = (jnp.exp(s1 - m1) * (0.5 / z)).reshape(grp)
            s2_ref[0, hh] = s2
            e2_ref[0, hh] = jnp.exp(s2 - m2)

    _emit_pipelined([group_chain(gidx) for gidx in range(PEER_HEADS // group)])


def _peer_scores(s, modv, g_ffn, wq_t, keys, rows, *, n_lat_rows, n_per_batch):
    d = s.shape[1]
    n_batch = modv.shape[0] - 1
    tm = SCORE_TM
    nlat_blocks = n_lat_rows // tm
    bpb = n_per_batch // tm
    nq = wq_t.shape[0]
    hk = (rows // tm, PEER_HEADS, PEER_KEYS, tm)
    hk_spec = pl.BlockSpec((1, PEER_HEADS, PEER_KEYS, tm), lambda i: (i, 0, 0, 0))
    hg = (PEER_HEADS, PEER_KEYS // SUBLANES, SUBLANES, rows)
    hg_spec = pl.BlockSpec((PEER_HEADS, PEER_KEYS // SUBLANES, SUBLANES, tm), lambda i: (0, 0, 0, i))
    return pl.pallas_call(
        _peer_score_kernel,
        grid=(rows // tm,),
        in_specs=[pl.BlockSpec((tm, d), lambda i: (i, 0)),
                  pl.BlockSpec((1, d), lambda i: (0, 0)),
                  pl.BlockSpec((1, 8, d), lambda i: (_mod_index(i, nlat_blocks, bpb, n_batch), 0, 0)),
                  pl.BlockSpec((nq, d), lambda i: (0, 0)),
                  pl.BlockSpec((PEER_HEADS, 2, PEER_KEYS, LANES), lambda i: (0, 0, 0, 0))],
        out_specs=[pl.BlockSpec((1, d, tm), lambda i: (i, 0, 0)), hg_spec, hg_spec, hk_spec, hk_spec],
        out_shape=[jax.ShapeDtypeStruct((rows // tm, d, tm), BF16)] + [jax.ShapeDtypeStruct(hg, F32)] * 2
                  + [jax.ShapeDtypeStruct(hk, F32)] * 2,
        scratch_shapes=[pltpu.VMEM((nq, tm), F32)],
        compiler_params=_cparams(("parallel",)),
        name="peer_scores",
    )(s, g_ffn.reshape(1, d), modv, wq_t, keys)


def _peer_expert_kernel(ht_ref, u_ref, vt_ref, c_ref, e1_ref, s2_ref, e2_ref, s_ref, mod_ref,
                        o_ref, acc_scr, st0_scr, st1_scr, at0_scr, at1_scr, *, n_exp_blocks):
    g = pl.program_id(0)
    st_scr, at_scr = (st0_scr, st1_scr), (at0_scr, at1_scr)
    te, tm = st0_scr.shape
    jc = jnp.maximum(g - 2, 0) % n_exp_blocks

    @pl.when(g == 0)
    def _():
        for ref in st_scr + at_scr:
            ref[...] = jnp.zeros_like(ref)

    @pl.when(jc == 0)
    def _():
        acc_scr[...] = jnp.zeros_like(acc_scr)

    def stages(slot_a, slot_b):
        n_chunks = te // MXU_CHUNK_ROWS

        def value_chunk(q):
            ms = slice(q * MXU_CHUNK_ROWS, (q + 1) * MXU_CHUNK_ROWS)
            acc_scr[ms, :] += jnp.dot(vt_ref[0, 0, ms, :], at_scr[slot_a][...], preferred_element_type=F32)

        def score_chunk(q):
            ms = slice(q * MXU_CHUNK_ROWS, (q + 1) * MXU_CHUNK_ROWS)
            st_scr[slot_a][ms, :] = jnp.dot(u_ref[0, ms, :], ht_ref[0], preferred_element_type=F32)

        rg_rows = 4 * SUBLANES
        al_group = 4
        col_tiles = 1
        tiles = [(rg, c0, al0)
                 for rg in range(LANES // rg_rows)
                 for c0 in range(0, tm // LANES, col_tiles)
                 for al0 in range(0, te // LANES, al_group)]
        assert acc_scr.shape[0] == te
        per_chunk = len(tiles) // (2 * n_chunks)
        for t, (rg, c0, al0) in enumerate(tiles):
            if t % per_chunk == 0:
                q = t // per_chunk
                (value_chunk if q % 2 == 0 else score_chunk)(q // 2)
            brows = slice(rg * rg_rows, (rg + 1) * rg_rows)
            lanes = [slice((c0 + ct) * LANES, (c0 + ct + 1) * LANES) for ct in range(col_tiles)]
            gate = [[jnp.zeros((rg_rows, LANES), F32) for _ in lanes] for _ in range(al_group)]
            for hh in range(PEER_HEADS):
                for ct, ln in enumerate(lanes):
                    s2t = s2_ref[0, hh, brows, ln]
                    e2t = e2_ref[0, hh, brows, ln]
                    for k in range(al_group):
                        al = al0 + k
                        c = c_ref[hh, 0, al:al + 1, ln]
                        e1 = e1_ref[hh, 0, al:al + 1, ln]
                        gate[k][ct] = gate[k][ct] + jnp.where(s2t >= c, e2t, 0.0) * e1
            for k in range(al_group):
                erows = slice((al0 + k) * LANES + rg * rg_rows, (al0 + k) * LANES + (rg + 1) * rg_rows)
                for ct, ln in enumerate(lanes):
                    sv = st_scr[slot_b][erows, ln]
                    act = sv * (1.0 + lax.erf(sv * (2.0 ** -0.5)))
                    at_scr[slot_b][erows, ln] = (gate[k][ct] * act).astype(BF16)

    for parity in range(2):
        @pl.when(g % 2 == parity)
        def _():
            stages(parity, 1 - parity)

    @pl.when((jc == n_exp_blocks - 1) & (g >= 2))
    def _():
        o_ref[...] = s_ref[...] + mod_ref[0][5:6] * acc_scr[...].T


def _peer_experts(ht, u, vt, layer, c, e1, s2, e2, s, modv, rows, *, n_lat_rows, n_per_batch):
    d = s.shape[1]
    n_exp = u.shape[1]
    n_batch = modv.shape[0] - 1
    tm, te = PEER_TM, PEER_TE
    nlat_blocks = n_lat_rows // tm
    bpb = n_per_batch // tm
    nj = n_exp // te
    n_pairs = (rows // tm) * nj

    def pair(g, lag):
        p = jnp.clip(g - lag, 0, n_pairs - 1)
        return p // nj, p % nj

    assert SCORE_TM == tm
    hk_spec = pl.BlockSpec((1, PEER_HEADS, PEER_KEYS, tm), lambda g: (pair(g, 1)[0], 0, 0, 0))
    hg_spec = pl.BlockSpec((PEER_HEADS, 1, SUBLANES, tm), lambda g: (0, pair(g, 1)[1], 0, pair(g, 1)[0]))
    kern = functools.partial(_peer_expert_kernel, n_exp_blocks=nj)
    return pl.pallas_call(
        kern,
        grid=(n_pairs + 2,),
        in_specs=[pl.BlockSpec((1, d, tm), lambda g: (pair(g, 0)[0], 0, 0)),
                  pl.BlockSpec((1, te, d), lambda g: (layer, pair(g, 0)[1], 0)),
                  pl.BlockSpec((1, 1, d, te), lambda g: (layer, pair(g, 2)[1], 0, 0)),
                  hg_spec, hg_spec, hk_spec, hk_spec,
                  pl.BlockSpec((tm, d), lambda g: (pair(g, 2)[0], 0)),
                  pl.BlockSpec((1, 8, d),
                               lambda g: (_mod_index(pair(g, 2)[0], nlat_blocks, bpb, n_batch), 0, 0))],
        out_specs=pl.BlockSpec((tm, d), lambda g: (pair(g, 2)[0], 0)),
        out_shape=jax.ShapeDtypeStruct((rows, d), F32),
        scratch_shapes=[pltpu.VMEM((d, tm), F32), pltpu.VMEM((te, tm), F32), pltpu.VMEM((te, tm), F32),
                        pltpu.VMEM((te, tm), BF16), pltpu.VMEM((te, tm), BF16)],
        compiler_params=_cparams(("arbitrary",)),
        name="peer_experts",
    )(ht, u, vt, c, e1, s2, e2, s, modv)


def _peer(s, modv, g_ffn, w_q, keys, u_all, vt_all, layer, rows, *, n_lat_rows, n_per_batch):
    geo = dict(n_lat_rows=n_lat_rows, n_per_batch=n_per_batch)
    ht, c, e1, s2, e2 = _peer_scores(s, modv, g_ffn, w_q.T.astype(BF16), keys.astype(BF16), rows, **geo)
    return _peer_experts(ht, u_all, vt_all, layer, c, e1, s2, e2, s, modv, rows, **geo)


def kernel(x, c, ctx, c_ctx, ada_w, ada_b, norm_mix, norm_ffn, mla_w_down, mla_g_cq, mla_g_ckv, mla_w_uq, mla_w_ukv, mla_g_q, mla_g_k, mla_w_o, diff_w_qkv, diff_g_q, diff_g_k, diff_lambda, diff_g_sub, diff_w_o, swa_w_qkv, swa_g_q, swa_g_k, swa_sink, swa_w_o, peer_w_q, peer_keys, peer_u, peer_v):
    n_batch, n_lat, d = x.shape
    n_ctx = ctx.shape[1]
    depth = ada_w.shape[0]
    rows_lat = n_batch * n_lat
    rows_all = rows_lat + n_batch * n_ctx
    assert n_lat % 512 == 0 and n_ctx % 256 == 0 and (n_batch * n_ctx) % PEER_TM == 0
    assert n_lat % GRID_W == 0 and rows_lat % n_ctx == 0 and n_lat >= 3 * Q_BLOCK

    s = jnp.concatenate([x.reshape(rows_lat, d), ctx.reshape(n_batch * n_ctx, d)], axis=0)
    cc = jnp.zeros((16, d), F32).at[:n_batch].set(c).at[n_batch].set(c_ctx)
    mods = _ada_mods(cc, ada_w, ada_b)
    geo = dict(n_lat_rows=rows_lat, n_per_batch=n_lat)
    n_exp = peer_u.shape[1]
    u_all = peer_u.astype(BF16)
    vt_all = peer_v.reshape(depth, n_exp // PEER_TE, PEER_TE, d).transpose(0, 1, 3, 2).astype(BF16)

    for i in range(depth):
        last = i == depth - 1
        kind, j = i % 3, i // 3
        modv = jnp.zeros((n_batch + 1, 8, d), F32).at[:, :6, :].set(
            mods[i, :n_batch + 1].reshape(n_batch + 1, 6, d))
        dims = dict(n_batch=n_batch, n_lat=n_lat, n_ctx=n_ctx, need_ctx=not last)
        if kind == 0:
            p = (mla_w_down[j], mla_g_cq[j], mla_g_ckv[j], mla_w_uq[j], mla_w_ukv[j], mla_g_q[j], mla_g_k[j])
            o, wo = _mla_mixer(s, modv, norm_mix[i], p, mla_w_o[j], **dims)
        elif kind == 1:
            lam_init = 0.8 - 0.6 * math.exp(-0.3 * i)
            o, wo = _diff_mixer(s, modv, norm_mix[i], diff_w_qkv[j], diff_g_q[j], diff_g_k[j],
                                diff_lambda[j], diff_g_sub[j], diff_w_o[j], lam_init, **dims)
        else:
            o, wo = _swa_mixer(s, modv, norm_mix[i], swa_w_qkv[j], swa_g_q[j], swa_g_k[j],
                               swa_sink[j], swa_w_o[j], **dims)
        rows = rows_lat if last else rows_all
        s = _proj_resid(o, wo, s, modv, rows, gate_row=2, **geo)
        s = _peer(s, modv, norm_ffn[i], peer_w_q[i], peer_keys[i], u_all, vt_all, i, rows, **geo)
    return s.reshape(n_batch, n_lat, d)
```

```python
import functools
import math

import jax
import jax.numpy as jnp
from jax import lax
from jax.experimental import pallas as pl
from jax.experimental.pallas import tpu as pltpu

F32 = jnp.float32
BF16 = jnp.bfloat16

LANES = 128
EPS = 1e-6
NEG_INF = -1e30
NEG_BIG = -3.0e38
ROPE_THETA = 10000.0
GRID_W = 64
Q_BLOCK = 128
WINDOW = 128

MLA_HEADS, MLA_NOPE, MLA_ROPE, MLA_V = 16, 64, 32, 64
MLA_QK = MLA_NOPE + MLA_ROPE
MLA_Q_LORA, MLA_KV_LORA = 768, 256
DIFF_HEADS, DIFF_HD = 8, 64
SWA_HEADS, SWA_KV_HEADS, SWA_HD = 16, 4, 64
SWA_GROUP = SWA_HEADS // SWA_KV_HEADS
PEER_HEADS, PEER_KEYS, PEER_TOPK = 8, 128, 16
PEER_EXPERTS = PEER_KEYS * PEER_KEYS

VMEM_LIMIT = 56 * 1024 * 1024

TM = 512
SCORE_TM = 512
TQ = 256
PEER_TM = 512
SUBLANES = 8
PEER_TE = SUBLANES * LANES
MXU_CHUNK_ROWS = 128


def _cparams(sem):
    return pltpu.CompilerParams(dimension_semantics=sem, vmem_limit_bytes=VMEM_LIMIT)


def _ada_kernel(c_ref, w_ref, b_ref, o_ref):
    c = c_ref[...]
    sc = c * (1.0 / (1.0 + jnp.exp(-c)))
    o_ref[0] = jnp.dot(sc, w_ref[0], preferred_element_type=F32,
                       precision=lax.Precision.HIGHEST) + b_ref[0]


def _ada_mods(cc, ada_w, ada_b):
    depth, d, d6 = ada_w.shape
    rows = cc.shape[0]
    tn = 1536
    return pl.pallas_call(
        _ada_kernel,
        grid=(depth, d6 // tn),
        in_specs=[pl.BlockSpec((rows, d), lambda i, n: (0, 0)),
                  pl.BlockSpec((1, d, tn), lambda i, n: (i, 0, n)),
                  pl.BlockSpec((1, 1, tn), lambda i, n: (i, 0, n))],
        out_specs=pl.BlockSpec((1, rows, tn), lambda i, n: (i, 0, n)),
        out_shape=jax.ShapeDtypeStruct((depth, rows, d6), F32),
        compiler_params=_cparams(("arbitrary", "arbitrary")),
        name="ada_mods",
    )(cc, ada_w, ada_b.reshape(depth, 1, d6))


def _norm_mod(x, g, mod, shift_row, scale_row):
    ms = jnp.mean(x * x, axis=-1, keepdims=True)
    y = x * lax.rsqrt(ms + EPS) * g
    if mod is not None:
        y = y * (1.0 + mod[scale_row:scale_row + 1]) + mod[shift_row:shift_row + 1]
    return y


def _nm_kernel(x_ref, g_ref, mod_ref, w_ref, o_ref, h_scr, *, use_mod, shift_row, scale_row):
    @pl.when(pl.program_id(1) == 0)
    def _():
        mod = mod_ref[0] if use_mod else None
        h_scr[...] = _norm_mod(x_ref[...], g_ref[...], mod, shift_row, scale_row).astype(BF16)

    o_ref[...] = jnp.dot(h_scr[...], w_ref[...], preferred_element_type=F32).astype(o_ref.dtype)


def _mod_index(i, nlat_blocks, blocks_per_batch, n_batch):
    return jnp.where(i < nlat_blocks, i // blocks_per_batch, n_batch)


def _nm_matmul(x, kblk, k, g, modv, rows, w, *, n_lat_rows, n_per_batch, out_dtype=F32,
               use_mod=True, shift_row=0, scale_row=1, name="nm_matmul"):
    nout = w.shape[1]
    tn = nout
    for cand in (2048, 1536, 1280, 1024):
        if nout > 2048 and nout % cand == 0:
            tn = cand
            break
    n_batch = modv.shape[0] - 1
    nlat_blocks = n_lat_rows // TM
    bpb = n_per_batch // TM
    kern = functools.partial(_nm_kernel, use_mod=use_mod, shift_row=shift_row, scale_row=scale_row)
    return pl.pallas_call(
        kern,
        grid=(rows // TM, nout // tn),
        in_specs=[pl.BlockSpec((TM, k), lambda i, n: (i, kblk)),
                  pl.BlockSpec((1, k), lambda i, n: (0, 0)),
                  pl.BlockSpec((1, 8, k), lambda i, n: (_mod_index(i, nlat_blocks, bpb, n_batch), 0, 0)),
                  pl.BlockSpec((k, tn), lambda i, n: (0, n))],
        out_specs=pl.BlockSpec((TM, tn), lambda i, n: (i, n)),
        out_shape=jax.ShapeDtypeStruct((rows, nout), out_dtype),
        scratch_shapes=[pltpu.VMEM((TM, k), BF16)],
        compiler_params=_cparams(("parallel", "arbitrary")),
        name=name,
    )(x, g.reshape(1, k), modv if modv.shape[-1] == k else jnp.zeros((n_batch + 1, 8, k), F32), w)


def _proj_resid_kernel(a_ref, w_ref, s_ref, mod_ref, o_ref, *, gate_row):
    y = jnp.dot(a_ref[...].astype(BF16), w_ref[...], preferred_element_type=F32)
    o_ref[...] = s_ref[...] + mod_ref[0][gate_row:gate_row + 1] * y


def _proj_resid(a, w, s, modv, rows, *, n_lat_rows, n_per_batch, gate_row, name="proj_resid"):
    ka = a.shape[1]
    d = w.shape[1]
    n_batch = modv.shape[0] - 1
    nlat_blocks = n_lat_rows // TM
    bpb = n_per_batch // TM
    return pl.pallas_call(
        functools.partial(_proj_resid_kernel, gate_row=gate_row),
        grid=(rows // TM,),
        in_specs=[pl.BlockSpec((TM, ka), lambda i: (i, 0)),
                  pl.BlockSpec((ka, d), lambda i: (0, 0)),
                  pl.BlockSpec((TM, d), lambda i: (i, 0)),
                  pl.BlockSpec((1, 8, d), lambda i: (_mod_index(i, nlat_blocks, bpb, n_batch), 0, 0))],
        out_specs=pl.BlockSpec((TM, d), lambda i: (i, 0)),
        out_shape=jax.ShapeDtypeStruct((rows, d), F32),
        compiler_params=_cparams(("parallel",)),
        name=name,
    )(a, w, s, modv)


def _rope_tables(n_lat, extra_rows, segments, nf):
    pos = jnp.arange(n_lat, dtype=jnp.int32)
    row = (pos // GRID_W).astype(F32)
    col = (pos % GRID_W).astype(F32)
    inv = ROPE_THETA ** (-jnp.arange(nf, dtype=F32) / nf)
    cos = jnp.ones((n_lat, LANES), F32)
    sa = jnp.zeros((n_lat, LANES), F32)
    sb = jnp.zeros((n_lat, LANES), F32)
    for first, use_row in segments:
        ang = (row if use_row else col)[:, None] * inv[None, :]
        c, s = jnp.cos(ang), jnp.sin(ang)
        cos = cos.at[:, first:first + nf].set(c).at[:, first + nf:first + 2 * nf].set(c)
        sa = sa.at[:, first:first + nf].set(-s)
        sb = sb.at[:, first + nf:first + 2 * nf].set(s)
    pad = lambda t, v: jnp.concatenate([t, jnp.full((extra_rows, LANES), v, F32)], axis=0)
    return pad(cos, 1.0), pad(sa, 0.0), pad(sb, 0.0)


def _head_norm(x, g, d):
    ssq = jnp.sum(x * x, axis=-1, keepdims=True) * (1.0 / d)
    return x * lax.rsqrt(ssq + EPS) * g


def _rope(y, cos, sa, sb, nf):
    return y * cos + pltpu.roll(y, LANES - nf, 1) * sa + pltpu.roll(y, nf, 1) * sb


def _with_ones_lane(v, lane):
    return jnp.where(lax.broadcasted_iota(jnp.int32, v.shape, 1) == lane, jnp.ones_like(v), v)


def _row_block(b, qi, nlat_q, nctx_q, n_batch):
    return jnp.where(qi < nlat_q, b * nlat_q + qi, n_batch * nlat_q + b * nctx_q + (qi - nlat_q))


def _prep_keys(k_scr, v_scr, n_lat, n_ctx, load_k_lat, load_k_ctx, load_v_lat, load_v_ctx,
               gk, tabs, d, nf):
    cos_ref, sa_ref, sb_ref = tabs
    ch = 256

    def lat_body(c, carry):
        r = pl.ds(pl.multiple_of(c * ch, ch), ch)
        kk = _head_norm(load_k_lat(r).astype(F32), gk, d)
        kk = _rope(kk, cos_ref[r, :], sa_ref[r, :], sb_ref[r, :], nf)
        k_scr[r, :] = kk.astype(BF16)
        if v_scr is not None:
            v_scr[r, :] = load_v_lat(r).astype(BF16)
        return carry

    lax.fori_loop(0, n_lat // ch, lat_body, 0, unroll=2)

    def ctx_body(c, carry):
        r = pl.ds(pl.multiple_of(c * ch, ch), ch)
        ro = pl.ds(pl.multiple_of(n_lat + c * ch, ch), ch)
        k_scr[ro, :] = _head_norm(load_k_ctx(r).astype(F32), gk, d).astype(BF16)
        if v_scr is not None:
            v_scr[ro, :] = load_v_ctx(r).astype(BF16)
        return carry

    lax.fori_loop(0, n_ctx // ch, ctx_body, 0)


def _nt_dot(a, b):
    return lax.dot_general(a, b, (((1,), (1,)), ((), ())), preferred_element_type=F32)


LOG2E = math.log2(math.e)
CHAIN_ROWS = 128
MLA_HEADS_PER_STEP = 8
HEADS_PER_STEP = 4
SWA_KV_PER_STEP = 1


def _row_chains(n_rows):
    return [slice(r, r + CHAIN_ROWS) for r in range(0, n_rows, CHAIN_ROWS)]


def _emit_pipelined(chains):
    pending, active = list(chains), []
    while pending or active:
        if pending:
            active.append(pending.pop(0))
        for ch in list(active):
            try:
                next(ch)
            except StopIteration:
                active.remove(ch)


def _exp2_scores(q, k):
    s = _nt_dot(q, k)
    p = jnp.exp2(s - jnp.max(s, axis=-1, keepdims=True))
    return p, jnp.sum(p, axis=-1, keepdims=True)


def _mla_attn_kernel(q_ref, kl_ref, kc_ref, krl_ref, krc_ref, vl_ref, vc_ref,
                     cq_ref, saq_ref, sbq_ref, ck_ref, sak_ref, sbk_ref, gq_ref, gk_ref,
                     o_ref, k_scr, v_scr, *, n_lat, n_ctx, nlat_q, scale):
    qi = pl.program_id(2)
    nf = MLA_ROPE // 4
    heads = [slice(hd * LANES, (hd + 1) * LANES) for hd in range(q_ref.shape[1] // LANES)]

    @pl.when(qi == 0)
    def _():
        for hd, ln in enumerate(heads):
            _prep_keys(k_scr.at[hd], v_scr.at[hd], n_lat, n_ctx,
                       lambda r, ln=ln: kl_ref[r, ln].astype(F32) + krl_ref[r, :],
                       lambda r, ln=ln: kc_ref[r, ln].astype(F32) + krc_ref[r, :],
                       lambda r, ln=ln: _with_ones_lane(vl_ref[r, ln], MLA_V),
                       lambda r, ln=ln: _with_ones_lane(vc_ref[r, ln], MLA_V),
                       gk_ref[...], (ck_ref, sak_ref, sbk_ref), MLA_QK, nf)

    def attend(keys):
        def chain(hd, ln, rows):
            q = _head_norm(q_ref[rows, ln].astype(F32), gq_ref[...], MLA_QK)
            q = _rope(q, cq_ref[rows, :], saq_ref[rows, :], sbq_ref[rows, :], nf) * (scale * LOG2E)
            s = _nt_dot(q.astype(BF16), k_scr[hd, keys, :])
            yield
            p = jnp.exp2(s - jnp.max(s, axis=-1, keepdims=True)).astype(BF16)
            yield
            o = jnp.dot(p, v_scr[hd, keys, :], preferred_element_type=F32)
            o_ref[rows, ln] = (o * (1.0 / o[:, MLA_V:MLA_V + 1])).astype(o_ref.dtype)

        _emit_pipelined([chain(hd, ln, rows) for rows in _row_chains(q_ref.shape[0])
                         for hd, ln in enumerate(heads)])

    @pl.when(qi < nlat_q)
    def _():
        attend(slice(0, n_lat + n_ctx))

    @pl.when(qi >= nlat_q)
    def _():
        attend(slice(n_lat, n_lat + n_ctx))


def _mla_attention(q, kv, lat, g_q, g_k, *, n_batch, n_lat, n_ctx, need_ctx):
    nlat_q, nctx_q = n_lat // TQ, n_ctx // TQ
    nq = nlat_q + (nctx_q if need_ctx else 0)
    rows = n_batch * (n_lat + (n_ctx if need_ctx else 0))
    nf = MLA_ROPE // 4
    tabs = _rope_tables(n_lat, TQ, ((MLA_NOPE, True), (MLA_NOPE + 2 * nf, False)), nf)
    h = MLA_HEADS
    ctx0 = n_batch * n_lat // n_ctx
    kr_blk = (MLA_Q_LORA + MLA_KV_LORA) // LANES
    gq = jnp.zeros((1, LANES), F32).at[0, :MLA_QK].set(g_q)
    gk = jnp.zeros((1, LANES), F32).at[0, :MLA_QK].set(g_k)
    hps = MLA_HEADS_PER_STEP
    hw = hps * LANES
    qspec = pl.BlockSpec((TQ, hw), lambda b, hh, qi: (_row_block(b, qi, nlat_q, nctx_q, n_batch), hh))
    tq_spec = pl.BlockSpec((TQ, LANES), lambda b, hh, qi: (jnp.minimum(qi, nlat_q), 0))
    tk_spec = pl.BlockSpec((n_lat, LANES), lambda b, hh, qi: (0, 0))
    vec = pl.BlockSpec((1, LANES), lambda b, hh, qi: (0, 0))
    kern = functools.partial(_mla_attn_kernel, n_lat=n_lat, n_ctx=n_ctx, nlat_q=nlat_q,
                             scale=MLA_QK ** -0.5)
    return pl.pallas_call(
        kern,
        grid=(n_batch, h // hps, nq),
        in_specs=[qspec,
                  pl.BlockSpec((n_lat, hw), lambda b, hh, qi: (b, hh)),
                  pl.BlockSpec((n_ctx, hw), lambda b, hh, qi: (ctx0 + b, hh)),
                  pl.BlockSpec((n_lat, LANES), lambda b, hh, qi: (b, kr_blk)),
                  pl.BlockSpec((n_ctx, LANES), lambda b, hh, qi: (ctx0 + b, kr_blk)),
                  pl.BlockSpec((n_lat, hw), lambda b, hh, qi: (b, h // hps + hh)),
                  pl.BlockSpec((n_ctx, hw), lambda b, hh, qi: (ctx0 + b, h // hps + hh)),
                  tq_spec, tq_spec, tq_spec, tk_spec, tk_spec, tk_spec, vec, vec],
        out_specs=qspec,
        out_shape=jax.ShapeDtypeStruct((rows, h * LANES), BF16),
        scratch_shapes=[pltpu.VMEM((hps, n_lat + n_ctx, LANES), BF16),
                        pltpu.VMEM((hps, n_lat + n_ctx, LANES), BF16)],
        compiler_params=_cparams(("parallel", "parallel", "arbitrary")),
        name="mla_attention",
    )(q, kv, kv, lat, lat, kv, kv, *tabs, *tabs, gq, gk)


def _pad_cols(w, groups, width, first=0):
    k = w.shape[0]
    w3 = w.reshape(k, groups, width)
    out = jnp.zeros((k, groups, LANES), w.dtype).at[:, :, first:first + width].set(w3)
    return out.reshape(k, groups * LANES)


def _pad_rows(w, groups, width):
    d = w.shape[1]
    w3 = w.reshape(groups, width, d)
    return jnp.zeros((groups, LANES, d), w.dtype).at[:, :width, :].set(w3).reshape(groups * LANES, d)


def _mla_mixer(s, modv, g_mix, p, w_o, *, n_batch, n_lat, n_ctx, need_ctx):
    w_down, g_cq, g_ckv, w_uq, w_ukv, g_q, g_k = p
    d = s.shape[1]
    rows_all = n_batch * (n_lat + n_ctx)
    geo = dict(n_lat_rows=n_batch * n_lat, n_per_batch=n_lat)
    wd = jnp.concatenate([w_down[:, :MLA_Q_LORA + MLA_KV_LORA],
                          _pad_cols(w_down[:, MLA_Q_LORA + MLA_KV_LORA:], 1, MLA_ROPE, MLA_NOPE)], axis=1)
    lat = _nm_matmul(s, 0, d, g_mix, modv, rows_all, wd.astype(BF16), name="mla_down", **geo)
    wq = _pad_cols(w_uq, MLA_HEADS, MLA_QK).astype(BF16)
    q = _nm_matmul(lat, 0, MLA_Q_LORA, g_cq, modv, rows_all, wq, use_mod=False, name="mla_uq",
                   out_dtype=BF16, **geo)
    wkv3 = w_ukv.reshape(MLA_KV_LORA, MLA_HEADS, MLA_NOPE + MLA_V)
    wkv = jnp.concatenate([_pad_cols(wkv3[:, :, :MLA_NOPE].reshape(MLA_KV_LORA, -1), MLA_HEADS, MLA_NOPE),
                           _pad_cols(wkv3[:, :, MLA_NOPE:].reshape(MLA_KV_LORA, -1), MLA_HEADS, MLA_V)],
                          axis=1).astype(BF16)
    kv = _nm_matmul(lat, MLA_Q_LORA // MLA_KV_LORA, MLA_KV_LORA, g_ckv, modv, rows_all, wkv,
                    use_mod=False, name="mla_ukv", out_dtype=BF16, **geo)
    o = _mla_attention(q, kv, lat, g_q, g_k, n_batch=n_batch, n_lat=n_lat, n_ctx=n_ctx, need_ctx=need_ctx)
    wo = _pad_rows(w_o, MLA_HEADS, MLA_V).astype(BF16)
    return o, wo


def _diff_attn_kernel(q_ref, kl_ref, kc_ref, vl_ref, vc_ref,
                      cq_ref, saq_ref, sbq_ref, ck_ref, sak_ref, sbk_ref, gq_ref, gk_ref,
                      lam_ref, gsub_ref, o_ref, k_scr, v_scr,
                      *, n_lat, n_ctx, nlat_q, scale, lam_init):
    qi = pl.program_id(2)
    nf = DIFF_HD // 4
    tabs = (ck_ref, sak_ref, sbk_ref)
    tile = lambda t: slice(t * LANES, (t + 1) * LANES)
    n_heads = o_ref.shape[1] // LANES

    @pl.when(qi == 0)
    def _():
        for hd in range(n_heads):
            ln = tile(hd)
            for sub in range(2):
                kt = tile(2 * hd + sub)
                _prep_keys(k_scr.at[2 * hd + sub], v_scr.at[hd] if sub == 0 else None, n_lat, n_ctx,
                           lambda r, kt=kt: kl_ref[r, kt], lambda r, kt=kt: kc_ref[r, kt],
                           lambda r, ln=ln: vl_ref[r, ln], lambda r, ln=ln: vc_ref[r, ln],
                           gk_ref[...], tabs, DIFF_HD, nf)

    lp = lam_ref[...]
    lam = (jnp.exp(jnp.sum(lp[0:1] * lp[1:2], axis=-1, keepdims=True))
           - jnp.exp(jnp.sum(lp[2:3] * lp[3:4], axis=-1, keepdims=True)) + lam_init)

    def attend(keys):
        def scores(t, rows):
            q = _head_norm(q_ref[rows, tile(t)].astype(F32), gq_ref[...], DIFF_HD)
            q = _rope(q, cq_ref[rows, :], saq_ref[rows, :], sbq_ref[rows, :], nf) * (scale * LOG2E)
            return _nt_dot(q.astype(BF16), k_scr[t, keys, :])

        def chain(hd, rows):
            s0 = scores(2 * hd, rows)
            s1 = scores(2 * hd + 1, rows)
            yield
            p0 = jnp.exp2(s0 - jnp.max(s0, axis=-1, keepdims=True))
            p1 = jnp.exp2(s1 - jnp.max(s1, axis=-1, keepdims=True))
            l0 = jnp.sum(p0, axis=-1, keepdims=True)
            l1 = jnp.sum(p1, axis=-1, keepdims=True)
            a = (p0 * (1.0 / l0) - p1 * (lam / l1)).astype(BF16)
            yield
            o = jnp.dot(a, v_scr[hd, keys, :], preferred_element_type=F32)
            o = _head_norm(o, gsub_ref[...], 2 * DIFF_HD) * (1.0 - lam_init)
            o_ref[rows, tile(hd)] = o.astype(o_ref.dtype)

        _emit_pipelined([chain(hd, rows) for rows in _row_chains(q_ref.shape[0])
                         for hd in range(n_heads)])

    @pl.when(qi < nlat_q)
    def _():
        attend(slice(0, n_lat + n_ctx))

    @pl.when(qi >= nlat_q)
    def _():
        attend(slice(n_lat, n_lat + n_ctx))


def _diff_attention(qkv, g_q, g_k, lam_p, g_sub, lam_init, *, n_batch, n_lat, n_ctx, need_ctx):
    nlat_q, nctx_q = n_lat // TQ, n_ctx // TQ
    nq = nlat_q + (nctx_q if need_ctx else 0)
    rows = n_batch * (n_lat + (n_ctx if need_ctx else 0))
    nf = DIFF_HD // 4
    tabs = _rope_tables(n_lat, TQ, ((0, True), (2 * nf, False)), nf)
    nh = DIFF_HEADS
    ctx0 = n_batch * n_lat // n_ctx
    gq = jnp.zeros((1, LANES), F32).at[0, :DIFF_HD].set(g_q)
    gk = jnp.zeros((1, LANES), F32).at[0, :DIFF_HD].set(g_k)
    lam8 = jnp.zeros((8, LANES), F32).at[:4, :DIFF_HD].set(lam_p.astype(F32))
    rb = lambda b, qi: _row_block(b, qi, nlat_q, nctx_q, n_batch)
    tq_spec = pl.BlockSpec((TQ, LANES), lambda b, hh, qi: (jnp.minimum(qi, nlat_q), 0))
    tk_spec = pl.BlockSpec((n_lat, LANES), lambda b, hh, qi: (0, 0))
    vec = pl.BlockSpec((1, LANES), lambda b, hh, qi: (0, 0))
    hps = HEADS_PER_STEP
    qk_w, v_w = 2 * hps * LANES, hps * LANES
    k_blk0 = 2 * nh // (2 * hps)
    v_blk0 = 4 * nh // hps
    kern = functools.partial(_diff_attn_kernel, n_lat=n_lat, n_ctx=n_ctx, nlat_q=nlat_q,
                             scale=DIFF_HD ** -0.5, lam_init=lam_init)
    n_all = n_lat + n_ctx
    return pl.pallas_call(
        kern,
        grid=(n_batch, nh // hps, nq),
        in_specs=[pl.BlockSpec((TQ, qk_w), lambda b, hh, qi: (rb(b, qi), hh)),
                  pl.BlockSpec((n_lat, qk_w), lambda b, hh, qi: (b, k_blk0 + hh)),
                  pl.BlockSpec((n_ctx, qk_w), lambda b, hh, qi: (ctx0 + b, k_blk0 + hh)),
                  pl.BlockSpec((n_lat, v_w), lambda b, hh, qi: (b, v_blk0 + hh)),
                  pl.BlockSpec((n_ctx, v_w), lambda b, hh, qi: (ctx0 + b, v_blk0 + hh)),
                  tq_spec, tq_spec, tq_spec, tk_spec, tk_spec, tk_spec, vec, vec,
                  pl.BlockSpec((8, LANES), lambda b, hh, qi: (0, 0)), vec],
        out_specs=pl.BlockSpec((TQ, v_w), lambda b, hh, qi: (rb(b, qi), hh)),
        out_shape=jax.ShapeDtypeStruct((rows, nh * LANES), BF16),
        scratch_shapes=[pltpu.VMEM((2 * hps, n_all, LANES), BF16), pltpu.VMEM((hps, n_all, LANES), BF16)],
        compiler_params=_cparams(("parallel", "parallel", "arbitrary")),
        name="diff_attention",
    )(qkv, qkv, qkv, qkv, qkv, *tabs, *tabs, gq, gk, lam8, g_sub.reshape(1, LANES))


def _diff_mixer(s, modv, g_mix, w_qkv, g_q, g_k, lam_p, g_sub, w_o, lam_init,
                *, n_batch, n_lat, n_ctx, need_ctx):
    d = s.shape[1]
    width = DIFF_HEADS * 2 * DIFF_HD
    rows_all = n_batch * (n_lat + n_ctx)
    w = jnp.concatenate([_pad_cols(w_qkv[:, :width], 2 * DIFF_HEADS, DIFF_HD),
                         _pad_cols(w_qkv[:, width:2 * width], 2 * DIFF_HEADS, DIFF_HD),
                         w_qkv[:, 2 * width:]], axis=1).astype(BF16)
    qkv = _nm_matmul(s, 0, d, g_mix, modv, rows_all, w, n_lat_rows=n_batch * n_lat,
                     n_per_batch=n_lat, name="diff_qkv", out_dtype=BF16)
    o = _diff_attention(qkv, g_q, g_k, lam_p, g_sub, lam_init,
                        n_batch=n_batch, n_lat=n_lat, n_ctx=n_ctx, need_ctx=need_ctx)
    return o, w_o.astype(BF16)


def _swa_attn_kernel(q_ref, kl_ref, kc_ref, vl_ref, vc_ref,
                     cq_ref, saq_ref, sbq_ref, ck_ref, sak_ref, sbk_ref, gq_ref, gk_ref, sink_ref,
                     o_ref, k_scr, v_scr, *, n_lat, n_ctx, nlat_q, scale):
    qi = pl.program_id(2)
    nf = SWA_HD // 4
    band = 3 * Q_BLOCK
    n_kv = k_scr.shape[0]
    tile = lambda t: slice(t * LANES, (t + 1) * LANES)

    @pl.when(qi == 0)
    def _():
        for kv in range(n_kv):
            ln = tile(kv)
            _prep_keys(k_scr.at[kv], v_scr.at[kv], n_lat, n_ctx,
                       lambda r, ln=ln: kl_ref[r, ln], lambda r, ln=ln: kc_ref[r, ln],
                       lambda r, ln=ln: _with_ones_lane(vl_ref[r, ln], SWA_HD),
                       lambda r, ln=ln: _with_ones_lane(vc_ref[r, ln], SWA_HD), gk_ref[...],
                       (ck_ref, sak_ref, sbk_ref), SWA_HD, nf)

    row_chains = _row_chains(q_ref.shape[0])
    reach = jnp.where(qi < nlat_q, WINDOW, -1)

    def chain(ci, rows, gi):
        kv = gi // SWA_GROUP
        qb = qi * len(row_chains) + ci
        k0 = pl.multiple_of(jnp.clip((qb - 1) * Q_BLOCK, 0, n_lat - band), Q_BLOCK)
        kpos = k0 + lax.broadcasted_iota(jnp.int32, (Q_BLOCK, band), 1)
        qpos = qb * Q_BLOCK + lax.broadcasted_iota(jnp.int32, (Q_BLOCK, band), 0)
        lanes = tile(gi)
        q = _head_norm(q_ref[rows, lanes].astype(F32), gq_ref[...], SWA_HD)
        q = _rope(q, cq_ref[rows, :], saq_ref[rows, :], sbq_ref[rows, :], nf) * (scale * LOG2E)
        q = q.astype(BF16)
        sb = _nt_dot(q, k_scr[kv, pl.ds(k0, band), :])
        sc = _nt_dot(q, k_scr[kv, n_lat:n_lat + n_ctx, :])
        yield
        sb = jnp.where(jnp.abs(kpos - qpos) <= reach, sb, NEG_INF)
        sink = sink_ref[gi][0:1, 0:1] * LOG2E
        m = jnp.maximum(jnp.maximum(jnp.max(sb, axis=-1, keepdims=True),
                                    jnp.max(sc, axis=-1, keepdims=True)), sink)
        pb = jnp.exp2(sb - m).astype(BF16)
        pc = jnp.exp2(sc - m).astype(BF16)
        p_sink = jnp.exp2(sink - m)
        yield
        o = (jnp.dot(pb, v_scr[kv, pl.ds(k0, band), :], preferred_element_type=F32)
             + jnp.dot(pc, v_scr[kv, n_lat:n_lat + n_ctx, :], preferred_element_type=F32))
        l = o[:, SWA_HD:SWA_HD + 1] + p_sink
        o_ref[rows, lanes] = (o * (1.0 / l)).astype(o_ref.dtype)

    _emit_pipelined([chain(ci, rows, gi) for ci, rows in enumerate(row_chains)
                     for gi in range(n_kv * SWA_GROUP)])


def _swa_attention(qkv, g_q, g_k, sink, *, n_batch, n_lat, n_ctx, need_ctx):
    nlat_q, nctx_q = n_lat // TQ, n_ctx // TQ
    nq = nlat_q + (nctx_q if need_ctx else 0)
    rows = n_batch * (n_lat + (n_ctx if need_ctx else 0))
    nf = SWA_HD // 4
    tabs = _rope_tables(n_lat, TQ, ((0, True), (2 * nf, False)), nf)
    nh, nkv, grp = SWA_HEADS, SWA_KV_HEADS, SWA_GROUP
    ctx0 = n_batch * n_lat // n_ctx
    gq = jnp.zeros((1, LANES), F32).at[0, :SWA_HD].set(g_q)
    gk = jnp.zeros((1, LANES), F32).at[0, :SWA_HD].set(g_k)
    sink_t = jnp.broadcast_to(sink.astype(F32).reshape(nh, 1, 1), (nh, 8, LANES))
    rb = lambda b, qi: _row_block(b, qi, nlat_q, nctx_q, n_batch)
    tq_spec = pl.BlockSpec((TQ, LANES), lambda b, kh, qi: (jnp.minimum(qi, nlat_q), 0))
    tk_spec = pl.BlockSpec((n_lat, LANES), lambda b, kh, qi: (0, 0))
    vec = pl.BlockSpec((1, LANES), lambda b, kh, qi: (0, 0))
    kern = functools.partial(_swa_attn_kernel, n_lat=n_lat, n_ctx=n_ctx, nlat_q=nlat_q,
                             scale=SWA_HD ** -0.5)
    n_all = n_lat + n_ctx
    kps = SWA_KV_PER_STEP
    q_w, kv_w = kps * grp * LANES, kps * LANES
    k_blk0, v_blk0 = nh // kps, (nh + nkv) // kps
    return pl.pallas_call(
        kern,
        grid=(n_batch, nkv // kps, nq),
        in_specs=[pl.BlockSpec((TQ, q_w), lambda b, kh, qi: (rb(b, qi), kh)),
                  pl.BlockSpec((n_lat, kv_w), lambda b, kh, qi: (b, k_blk0 + kh)),
                  pl.BlockSpec((n_ctx, kv_w), lambda b, kh, qi: (ctx0 + b, k_blk0 + kh)),
                  pl.BlockSpec((n_lat, kv_w), lambda b, kh, qi: (b, v_blk0 + kh)),
                  pl.BlockSpec((n_ctx, kv_w), lambda b, kh, qi: (ctx0 + b, v_blk0 + kh)),
                  tq_spec, tq_spec, tq_spec, tk_spec, tk_spec, tk_spec, vec, vec,
                  pl.BlockSpec((kps * grp, 8, LANES), lambda b, kh, qi: (kh, 0, 0))],
        out_specs=pl.BlockSpec((TQ, q_w), lambda b, kh, qi: (rb(b, qi), kh)),
        out_shape=jax.ShapeDtypeStruct((rows, nh * LANES), BF16),
        scratch_shapes=[pltpu.VMEM((kps, n_all, LANES), BF16), pltpu.VMEM((kps, n_all, LANES), BF16)],
        compiler_params=_cparams(("parallel", "parallel", "arbitrary")),
        name="swa_attention",
    )(qkv, qkv, qkv, qkv, qkv, *tabs, *tabs, gq, gk, sink_t)


def _swa_mixer(s, modv, g_mix, w_qkv, g_q, g_k, sink, w_o, *, n_batch, n_lat, n_ctx, need_ctx):
    d = s.shape[1]
    qw, kw = SWA_HEADS * SWA_HD, SWA_KV_HEADS * SWA_HD
    rows_all = n_batch * (n_lat + n_ctx)
    w = jnp.concatenate([_pad_cols(w_qkv[:, :qw], SWA_HEADS, SWA_HD),
                         _pad_cols(w_qkv[:, qw:qw + kw], SWA_KV_HEADS, SWA_HD),
                         _pad_cols(w_qkv[:, qw + kw:], SWA_KV_HEADS, SWA_HD)], axis=1).astype(BF16)
    qkv = _nm_matmul(s, 0, d, g_mix, modv, rows_all, w, n_lat_rows=n_batch * n_lat,
                     n_per_batch=n_lat, name="swa_qkv", out_dtype=BF16)
    o = _swa_attention(qkv, g_q, g_k, sink, n_batch=n_batch, n_lat=n_lat, n_ctx=n_ctx, need_ctx=need_ctx)
    return o, _pad_rows(w_o, SWA_HEADS, SWA_HD).astype(BF16)


def _merge_exchange_pairs(n):
    pairs = []
    t = max(1, (n - 1).bit_length())
    p = 1 << (t - 1)
    while p > 0:
        q, r, d = 1 << (t - 1), 0, p
        while d > 0:
            pairs += [(i, i + d) for i in range(n - d) if (i & p) == r]
            d, q, r = q - p, q >> 1, p
        p >>= 1
    return pairs


def _top_values(s, count):
    n = s.shape[0] // SUBLANES
    v = [s[j * SUBLANES:(j + 1) * SUBLANES, :] for j in range(n)]
    for i, j in _merge_exchange_pairs(n):
        v[i], v[j] = jnp.maximum(v[i], v[j]), jnp.minimum(v[i], v[j])
    vals = []
    for r in range(count):
        m = jnp.max(v[0], axis=0, keepdims=True)
        vals.append(m)
        depth = min(n, count - 1 - r)
        head = v[0] == m
        for j in range(depth):
            nxt = v[j + 1] if j + 1 < n else jnp.full_like(v[j], NEG_BIG)
            v[j] = jnp.where(head, nxt, v[j])
    return vals


def _peer_score_kernel(s_ref, g_ref, mod_ref, wq_ref, keys_ref,
                       ht_ref, c_ref, e1_ref, s2_ref, e2_ref, qt_scr):
    k = PEER_TOPK
    h = _norm_mod(s_ref[...], g_ref[...], mod_ref[0], 3, 4)
    ht = h.T.astype(BF16)
    ht_ref[0] = ht

    tm = ht.shape[1]
    group = 2
    q_rows = group * 2 * LANES

    def group_chain(gidx):
        heads = [gidx * group + i for i in range(group)]
        rows = slice(gidx * q_rows, (gidx + 1) * q_rows)
        qt_scr[rows, :] = jnp.dot(wq_ref[rows, :], ht, preferred_element_type=F32)
        yield
        scores = []
        for hh in heads:
            for half in range(2):
                r = slice((hh * 2 + half) * LANES, (hh * 2 + half + 1) * LANES)
                scores.append(jnp.dot(keys_ref[hh, half], qt_scr[r, :].astype(BF16),
                                      preferred_element_type=F32))
        tops = _top_values(jnp.concatenate(scores, axis=1), k + 1)
        cands = []
        for i in range(group):
            v1 = [t[:, (2 * i) * tm:(2 * i + 1) * tm] for t in tops]
            v2 = [t[:, (2 * i + 1) * tm:(2 * i + 2) * tm] for t in tops]
            top8 = jnp.concatenate(v2[:8], axis=0)
            pieces = [v1[0] + jnp.concatenate(v2[:k], axis=0)]
            pieces += [v1[j] + top8 for j in range(1, 8)]
            pieces += [jnp.concatenate(v1[8:k], axis=0) + v2[0]]
            pieces += [jnp.concatenate([v1[0] + v2[k], v1[k] + v2[0]]
                                       + [jnp.full_like(v1[0], NEG_BIG)] * 6, axis=0)]
            cands.append(jnp.concatenate(pieces, axis=0))
        best = _top_values(jnp.concatenate(cands, axis=1), k + 1)
        grp = (PEER_KEYS // SUBLANES, SUBLANES, tm)
        for i, hh in enumerate(heads):
            ms = [t[:, i * tm:(i + 1) * tm] for t in best]
            s1, s2 = scores[2 * i], scores[2 * i + 1]
            m1, m2 = tops[0][:, (2 * i) * tm:(2 * i + 1) * tm], tops[0][:, (2 * i + 1) * tm:(2 * i + 2) * tm]
            thr = 0.5 * (ms[k - 1] + ms[k])
            z = jnp.ones_like(ms[0])
            for r in range(1, k):
                z = z + jnp.exp(ms[r] - ms[0])
            c_ref[hh] = (thr - s1).reshape(grp)
            e1_ref[hh] = (jnp.exp(s1 - m1) * (0.5 / z)).reshape(grp)
            s2_ref[0, hh] = s2
            e2_ref[0, hh] = jnp.exp(s2 - m2)

    _emit_pipelined([group_chain(gidx) for gidx in range(PEER_HEADS // group)])


def _peer_scores(s, modv, g_ffn, wq_t, keys, rows, *, n_lat_rows, n_per_batch):
    d = s.shape[1]
    n_batch = modv.shape[0] - 1
    tm = SCORE_TM
    nlat_blocks = n_lat_rows // tm
    bpb = n_per_batch // tm
    nq = wq_t.shape[0]
    hk = (rows // tm, PEER_HEADS, PEER_KEYS, tm)
    hk_spec = pl.BlockSpec((1, PEER_HEADS, PEER_KEYS, tm), lambda i: (i, 0, 0, 0))
    hg = (PEER_HEADS, PEER_KEYS // SUBLANES, SUBLANES, rows)
    hg_spec = pl.BlockSpec((PEER_HEADS, PEER_KEYS // SUBLANES, SUBLANES, tm), lambda i: (0, 0, 0, i))
    return pl.pallas_call(
        _peer_score_kernel,
        grid=(rows // tm,),
        in_specs=[pl.BlockSpec((tm, d), lambda i: (i, 0)),
                  pl.BlockSpec((1, d), lambda i: (0, 0)),
                  pl.BlockSpec((1, 8, d), lambda i: (_mod_index(i, nlat_blocks, bpb, n_batch), 0, 0)),
                  pl.BlockSpec((nq, d), lambda i: (0, 0)),
                  pl.BlockSpec((PEER_HEADS, 2, PEER_KEYS, LANES), lambda i: (0, 0, 0, 0))],
        out_specs=[pl.BlockSpec((1, d, tm), lambda i: (i, 0, 0)), hg_spec, hg_spec, hk_spec, hk_spec],
        out_shape=[jax.ShapeDtypeStruct((rows // tm, d, tm), BF16)] + [jax.ShapeDtypeStruct(hg, F32)] * 2
                  + [jax.ShapeDtypeStruct(hk, F32)] * 2,
        scratch_shapes=[pltpu.VMEM((nq, tm), F32)],
        compiler_params=_cparams(("parallel",)),
        name="peer_scores",
    )(s, g_ffn.reshape(1, d), modv, wq_t, keys)


def _peer_expert_kernel(ht_ref, u_ref, vt_ref, c_ref, e1_ref, s2_ref, e2_ref, s_ref, mod_ref,
                        o_ref, acc_scr, st0_scr, st1_scr, at0_scr, at1_scr, *, n_exp_blocks):
    g = pl.program_id(0)
    st_scr, at_scr = (st0_scr, st1_scr), (at0_scr, at1_scr)
    te, tm = st0_scr.shape
    jc = jnp.maximum(g - 2, 0) % n_exp_blocks

    @pl.when(g == 0)
    def _():
        for ref in st_scr + at_scr:
            ref[...] = jnp.zeros_like(ref)

    @pl.when(jc == 0)
    def _():
        acc_scr[...] = jnp.zeros_like(acc_scr)

    def stages(slot_a, slot_b):
        n_chunks = te // MXU_CHUNK_ROWS

        def value_chunk(q):
            ms = slice(q * MXU_CHUNK_ROWS, (q + 1) * MXU_CHUNK_ROWS)
            acc_scr[ms, :] += jnp.dot(vt_ref[0, 0, ms, :], at_scr[slot_a][...], preferred_element_type=F32)

        def score_chunk(q):
            ms = slice(q * MXU_CHUNK_ROWS, (q + 1) * MXU_CHUNK_ROWS)
            st_scr[slot_a][ms, :] = jnp.dot(u_ref[0, ms, :], ht_ref[0], preferred_element_type=F32)

        rg_rows = 4 * SUBLANES
        al_group = 4
        col_tiles = 1
        tiles = [(rg, c0, al0)
                 for rg in range(LANES // rg_rows)
                 for c0 in range(0, tm // LANES, col_tiles)
                 for al0 in range(0, te // LANES, al_group)]
        assert acc_scr.shape[0] == te
        per_chunk = len(tiles) // (2 * n_chunks)
        for t, (rg, c0, al0) in enumerate(tiles):
            if t % per_chunk == 0:
                q = t // per_chunk
                (value_chunk if q % 2 == 0 else score_chunk)(q // 2)
            brows = slice(rg * rg_rows, (rg + 1) * rg_rows)
            lanes = [slice((c0 + ct) * LANES, (c0 + ct + 1) * LANES) for ct in range(col_tiles)]
            gate = [[jnp.zeros((rg_rows, LANES), F32) for _ in lanes] for _ in range(al_group)]
            for hh in range(PEER_HEADS):
                for ct, ln in enumerate(lanes):
                    s2t = s2_ref[0, hh, brows, ln]
                    e2t = e2_ref[0, hh, brows, ln]
                    for k in range(al_group):
                        al = al0 + k
                        c = c_ref[hh, 0, al:al + 1, ln]
                        e1 = e1_ref[hh, 0, al:al + 1, ln]
                        gate[k][ct] = gate[k][ct] + jnp.where(s2t >= c, e2t, 0.0) * e1
            for k in range(al_group):
                erows = slice((al0 + k) * LANES + rg * rg_rows, (al0 + k) * LANES + (rg + 1) * rg_rows)
                for ct, ln in enumerate(lanes):
                    sv = st_scr[slot_b][erows, ln]
                    act = sv * (1.0 + lax.erf(sv * (2.0 ** -0.5)))
                    at_scr[slot_b][erows, ln] = (gate[k][ct] * act).astype(BF16)

    for parity in range(2):
        @pl.when(g % 2 == parity)
        def _():
            stages(parity, 1 - parity)

    @pl.when((jc == n_exp_blocks - 1) & (g >= 2))
    def _():
        o_ref[...] = s_ref[...] + mod_ref[0][5:6] * acc_scr[...].T


def _peer_experts(ht, u, vt, layer, c, e1, s2, e2, s, modv, rows, *, n_lat_rows, n_per_batch):
    d = s.shape[1]
    n_exp = u.shape[1]
    n_batch = modv.shape[0] - 1
    tm, te = PEER_TM, PEER_TE
    nlat_blocks = n_lat_rows // tm
    bpb = n_per_batch // tm
    nj = n_exp // te
    n_pairs = (rows // tm) * nj

    def pair(g, lag):
        p = jnp.clip(g - lag, 0, n_pairs - 1)
        return p // nj, p % nj

    assert SCORE_TM == tm
    hk_spec = pl.BlockSpec((1, PEER_HEADS, PEER_KEYS, tm), lambda g: (pair(g, 1)[0], 0, 0, 0))
    hg_spec = pl.BlockSpec((PEER_HEADS, 1, SUBLANES, tm), lambda g: (0, pair(g, 1)[1], 0, pair(g, 1)[0]))
    kern = functools.partial(_peer_expert_kernel, n_exp_blocks=nj)
    return pl.pallas_call(
        kern,
        grid=(n_pairs + 2,),
        in_specs=[pl.BlockSpec((1, d, tm), lambda g: (pair(g, 0)[0], 0, 0)),
                  pl.BlockSpec((1, te, d), lambda g: (layer, pair(g, 0)[1], 0)),
                  pl.BlockSpec((1, 1, d, te), lambda g: (layer, pair(g, 2)[1], 0, 0)),
                  hg_spec, hg_spec, hk_spec, hk_spec,
                  pl.BlockSpec((tm, d), lambda g: (pair(g, 2)[0], 0)),
                  pl.BlockSpec((1, 8, d),
                               lambda g: (_mod_index(pair(g, 2)[0], nlat_blocks, bpb, n_batch), 0, 0))],
        out_specs=pl.BlockSpec((tm, d), lambda g: (pair(g, 2)[0], 0)),
        out_shape=jax.ShapeDtypeStruct((rows, d), F32),
        scratch_shapes=[pltpu.VMEM((d, tm), F32), pltpu.VMEM((te, tm), F32), pltpu.VMEM((te, tm), F32),
                        pltpu.VMEM((te, tm), BF16), pltpu.VMEM((te, tm), BF16)],
        compiler_params=_cparams(("arbitrary",)),
        name="peer_experts",
    )(ht, u, vt, c, e1, s2, e2, s, modv)


def _peer(s, modv, g_ffn, w_q, keys, u_all, vt_all, layer, rows, *, n_lat_rows, n_per_batch):
    geo = dict(n_lat_rows=n_lat_rows, n_per_batch=n_per_batch)
    ht, c, e1, s2, e2 = _peer_scores(s, modv, g_ffn, w_q.T.astype(BF16), keys.astype(BF16), rows, **geo)
    return _peer_experts(ht, u_all, vt_all, layer, c, e1, s2, e2, s, modv, rows, **geo)


def kernel(x, c, ctx, c_ctx, ada_w, ada_b, norm_mix, norm_ffn, mla_w_down, mla_g_cq, mla_g_ckv, mla_w_uq, mla_w_ukv, mla_g_q, mla_g_k, mla_w_o, diff_w_qkv, diff_g_q, diff_g_k, diff_lambda, diff_g_sub, diff_w_o, swa_w_qkv, swa_g_q, swa_g_k, swa_sink, swa_w_o, peer_w_q, peer_keys, peer_u, peer_v):
    n_batch, n_lat, d = x.shape
    n_ctx = ctx.shape[1]
    depth = ada_w.shape[0]
    rows_lat = n_batch * n_lat
    rows_all = rows_lat + n_batch * n_ctx
    assert n_lat % 512 == 0 and n_ctx % 256 == 0 and (n_batch * n_ctx) % PEER_TM == 0
    assert n_lat % GRID_W == 0 and rows_lat % n_ctx == 0 and n_lat >= 3 * Q_BLOCK

    s = jnp.concatenate([x.reshape(rows_lat, d), ctx.reshape(n_batch * n_ctx, d)], axis=0)
    cc = jnp.zeros((16, d), F32).at[:n_batch].set(c).at[n_batch].set(c_ctx)
    mods = _ada_mods(cc, ada_w, ada_b)
    geo = dict(n_lat_rows=rows_lat, n_per_batch=n_lat)
    n_exp = peer_u.shape[1]
    u_all = peer_u.astype(BF16)
    vt_all = peer_v.reshape(depth, n_exp // PEER_TE, PEER_TE, d).transpose(0, 1, 3, 2).astype(BF16)

    for i in range(depth):
        last = i == depth - 1
        kind, j = i % 3, i // 3
        modv = jnp.zeros((n_batch + 1, 8, d), F32).at[:, :6, :].set(
            mods[i, :n_batch + 1].reshape(n_batch + 1, 6, d))
        dims = dict(n_batch=n_batch, n_lat=n_lat, n_ctx=n_ctx, need_ctx=not last)
        if kind == 0:
            p = (mla_w_down[j], mla_g_cq[j], mla_g_ckv[j], mla_w_uq[j], mla_w_ukv[j], mla_g_q[j], mla_g_k[j])
            o, wo = _mla_mixer(s, modv, norm_mix[i], p, mla_w_o[j], **dims)
        elif kind == 1:
            lam_init = 0.8 - 0.6 * math.exp(-0.3 * i)
            o, wo = _diff_mixer(s, modv, norm_mix[i], diff_w_qkv[j], diff_g_q[j], diff_g_k[j],
                                diff_lambda[j], diff_g_sub[j], diff_w_o[j], lam_init, **dims)
        else:
            o, wo = _swa_mixer(s, modv, norm_mix[i], swa_w_qkv[j], swa_g_q[j], swa_g_k[j],
                               swa_sink[j], swa_w_o[j], **dims)
        rows = rows_lat if last else rows_all
        s = _proj_resid(o, wo, s, modv, rows, gate_row=2, **geo)
        s = _peer(s, modv, norm_ffn[i], peer_w_q[i], peer_keys[i], u_all, vt_all, i, rows, **geo)
    return s.reshape(n_batch, n_lat, d)
```

```python
import functools
import math

import jax
import jax.numpy as jnp
from jax import lax
from jax.experimental import pallas as pl
from jax.experimental.pallas import tpu as pltpu

F32 = jnp.float32
BF16 = jnp.bfloat16

LANES = 128
EPS = 1e-6
NEG_INF = -1e30
NEG_BIG = -3.0e38
ROPE_THETA = 10000.0
GRID_W = 64
Q_BLOCK = 128
WINDOW = 128

MLA_HEADS, MLA_NOPE, MLA_ROPE, MLA_V = 16, 64, 32, 64
MLA_QK = MLA_NOPE + MLA_ROPE
MLA_Q_LORA, MLA_KV_LORA = 768, 256
DIFF_HEADS, DIFF_HD = 8, 64
SWA_HEADS, SWA_KV_HEADS, SWA_HD = 16, 4, 64
SWA_GROUP = SWA_HEADS // SWA_KV_HEADS
PEER_HEADS, PEER_KEYS, PEER_TOPK = 8, 128, 16

VMEM_LIMIT = 56 * 1024 * 1024

TM = 512
NM_MAX_TN = 2048
ADA_TN = 1536
KEY_PREP_ROWS = 256
SCORE_TM = 512
TQ = 256
PEER_TM = 512
SUBLANES = 8
PEER_TE = SUBLANES * LANES
MXU_CHUNK_ROWS = 128


def _cparams(sem):
    return pltpu.CompilerParams(dimension_semantics=sem, vmem_limit_bytes=VMEM_LIMIT)


def _ada_kernel(c_ref, w_ref, b_ref, o_ref):
    c = c_ref[...]
    sc = c * (1.0 / (1.0 + jnp.exp(-c)))
    o_ref[0] = jnp.dot(sc, w_ref[0], preferred_element_type=F32,
                       precision=lax.Precision.HIGHEST) + b_ref[0]


def _ada_mods(cc, ada_w, ada_b):
    depth, d, d6 = ada_w.shape
    rows = cc.shape[0]
    tn = ADA_TN
    return pl.pallas_call(
        _ada_kernel,
        grid=(depth, d6 // tn),
        in_specs=[pl.BlockSpec((rows, d), lambda i, n: (0, 0)),
                  pl.BlockSpec((1, d, tn), lambda i, n: (i, 0, n)),
                  pl.BlockSpec((1, 1, tn), lambda i, n: (i, 0, n))],
        out_specs=pl.BlockSpec((1, rows, tn), lambda i, n: (i, 0, n)),
        out_shape=jax.ShapeDtypeStruct((depth, rows, d6), F32),
        compiler_params=_cparams(("arbitrary", "arbitrary")),
        name="ada_mods",
    )(cc, ada_w, ada_b.reshape(depth, 1, d6))


def _norm_mod(x, g, mod, shift_row, scale_row):
    ms = jnp.mean(x * x, axis=-1, keepdims=True)
    y = x * lax.rsqrt(ms + EPS) * g
    if mod is not None:
        y = y * (1.0 + mod[scale_row:scale_row + 1]) + mod[shift_row:shift_row + 1]
    return y


def _nm_kernel(*refs, use_mod, shift_row, scale_row):
    if use_mod:
        x_ref, g_ref, mod_ref, w_ref, o_ref, h_scr = refs
    else:
        (x_ref, g_ref, w_ref, o_ref, h_scr), mod_ref = refs, None

    @pl.when(pl.program_id(1) == 0)
    def _():
        mod = mod_ref[0] if use_mod else None
        h_scr[...] = _norm_mod(x_ref[...], g_ref[...], mod, shift_row, scale_row).astype(BF16)

    o_ref[...] = jnp.dot(h_scr[...], w_ref[...], preferred_element_type=F32).astype(o_ref.dtype)


def _mod_index(i, nlat_blocks, blocks_per_batch, n_batch):
    return jnp.where(i < nlat_blocks, i // blocks_per_batch, n_batch)


def _nm_matmul(x, kblk, k, g, modv, rows, w, *, n_lat_rows, n_per_batch, out_dtype=F32,
               shift_row=0, scale_row=1, name="nm_matmul"):
    nout = w.shape[1]
    tn = next(t for t in range(min(nout, NM_MAX_TN), 0, -LANES) if nout % t == 0)
    use_mod = modv is not None
    in_specs = [pl.BlockSpec((TM, k), lambda i, n: (i, kblk)),
                pl.BlockSpec((1, k), lambda i, n: (0, 0))]
    operands = [x, g.reshape(1, k)]
    if use_mod:
        n_batch = modv.shape[0] - 1
        nlat_blocks = n_lat_rows // TM
        bpb = n_per_batch // TM
        in_specs.append(pl.BlockSpec(
            (1, SUBLANES, k), lambda i, n: (_mod_index(i, nlat_blocks, bpb, n_batch), 0, 0)))
        operands.append(modv)
    in_specs.append(pl.BlockSpec((k, tn), lambda i, n: (0, n)))
    operands.append(w)
    kern = functools.partial(_nm_kernel, use_mod=use_mod, shift_row=shift_row, scale_row=scale_row)
    return pl.pallas_call(
        kern,
        grid=(rows // TM, nout // tn),
        in_specs=in_specs,
        out_specs=pl.BlockSpec((TM, tn), lambda i, n: (i, n)),
        out_shape=jax.ShapeDtypeStruct((rows, nout), out_dtype),
        scratch_shapes=[pltpu.VMEM((TM, k), BF16)],
        compiler_params=_cparams(("parallel", "arbitrary")),
        name=name,
    )(*operands)


def _proj_resid_kernel(a_ref, w_ref, s_ref, mod_ref, o_ref, *, gate_row):
    y = jnp.dot(a_ref[...].astype(BF16), w_ref[...], preferred_element_type=F32)
    o_ref[...] = s_ref[...] + mod_ref[0][gate_row:gate_row + 1] * y


def _proj_resid(a, w, s, modv, rows, *, n_lat_rows, n_per_batch, gate_row, name="proj_resid"):
    ka = a.shape[1]
    d = w.shape[1]
    n_batch = modv.shape[0] - 1
    nlat_blocks = n_lat_rows // TM
    bpb = n_per_batch // TM
    return pl.pallas_call(
        functools.partial(_proj_resid_kernel, gate_row=gate_row),
        grid=(rows // TM,),
        in_specs=[pl.BlockSpec((TM, ka), lambda i: (i, 0)),
                  pl.BlockSpec((ka, d), lambda i: (0, 0)),
                  pl.BlockSpec((TM, d), lambda i: (i, 0)),
                  pl.BlockSpec((1, SUBLANES, d), lambda i: (_mod_index(i, nlat_blocks, bpb, n_batch), 0, 0))],
        out_specs=pl.BlockSpec((TM, d), lambda i: (i, 0)),
        out_shape=jax.ShapeDtypeStruct((rows, d), F32),
        compiler_params=_cparams(("parallel",)),
        name=name,
    )(a, w, s, modv)


def _rope_tables(n_lat, extra_rows, segments, nf):
    pos = jnp.arange(n_lat, dtype=jnp.int32)
    row = (pos // GRID_W).astype(F32)
    col = (pos % GRID_W).astype(F32)
    inv = ROPE_THETA ** (-jnp.arange(nf, dtype=F32) / nf)
    cos = jnp.ones((n_lat, LANES), F32)
    sa = jnp.zeros((n_lat, LANES), F32)
    sb = jnp.zeros((n_lat, LANES), F32)
    for first, use_row in segments:
        ang = (row if use_row else col)[:, None] * inv[None, :]
        c, s = jnp.cos(ang), jnp.sin(ang)
        cos = cos.at[:, first:first + nf].set(c).at[:, first + nf:first + 2 * nf].set(c)
        sa = sa.at[:, first:first + nf].set(-s)
        sb = sb.at[:, first + nf:first + 2 * nf].set(s)
    pad = lambda t, v: jnp.concatenate([t, jnp.full((extra_rows, LANES), v, F32)], axis=0)
    return pad(cos, 1.0), pad(sa, 0.0), pad(sb, 0.0)


def _head_norm(x, g, d):
    ssq = jnp.sum(x * x, axis=-1, keepdims=True) * (1.0 / d)
    return x * lax.rsqrt(ssq + EPS) * g


def _rope(y, cos, sa, sb, nf):
    return y * cos + pltpu.roll(y, LANES - nf, 1) * sa + pltpu.roll(y, nf, 1) * sb


def _with_ones_lane(v, lane):
    return jnp.where(lax.broadcasted_iota(jnp.int32, v.shape, 1) == lane, jnp.ones_like(v), v)


def _row_block(b, qi, nlat_q, nctx_q, n_batch):
    return jnp.where(qi < nlat_q, b * nlat_q + qi, n_batch * nlat_q + b * nctx_q + (qi - nlat_q))


def _prep_keys(k_scr, v_scr, n_lat, n_ctx, load_k_lat, load_k_ctx, load_v_lat, load_v_ctx,
               gk, tabs, d, nf):
    cos_ref, sa_ref, sb_ref = tabs
    ch = KEY_PREP_ROWS

    def lat_body(c, carry):
        r = pl.ds(pl.multiple_of(c * ch, ch), ch)
        kk = _head_norm(load_k_lat(r).astype(F32), gk, d)
        kk = _rope(kk, cos_ref[r, :], sa_ref[r, :], sb_ref[r, :], nf)
        k_scr[r, :] = kk.astype(BF16)
        if v_scr is not None:
            v_scr[r, :] = load_v_lat(r).astype(BF16)
        return carry

    lax.fori_loop(0, n_lat // ch, lat_body, 0, unroll=2)

    def ctx_body(c, carry):
        r = pl.ds(pl.multiple_of(c * ch, ch), ch)
        ro = pl.ds(pl.multiple_of(n_lat + c * ch, ch), ch)
        k_scr[ro, :] = _head_norm(load_k_ctx(r).astype(F32), gk, d).astype(BF16)
        if v_scr is not None:
            v_scr[ro, :] = load_v_ctx(r).astype(BF16)
        return carry

    lax.fori_loop(0, n_ctx // ch, ctx_body, 0)


def _nt_dot(a, b):
    return lax.dot_general(a, b, (((1,), (1,)), ((), ())), preferred_element_type=F32)


LOG2E = math.log2(math.e)
CHAIN_ROWS = 128
MLA_HEADS_PER_STEP = 8
HEADS_PER_STEP = 4
SWA_KV_PER_STEP = 1


def _row_chains(n_rows):
    return [slice(r, r + CHAIN_ROWS) for r in range(0, n_rows, CHAIN_ROWS)]


def _emit_pipelined(chains):
    pending, active = list(chains), []
    while pending or active:
        if pending:
            active.append(pending.pop(0))
        for ch in list(active):
            try:
                next(ch)
            except StopIteration:
                active.remove(ch)


def _mla_attn_kernel(q_ref, kl_ref, kc_ref, krl_ref, krc_ref, vl_ref, vc_ref,
                     cq_ref, saq_ref, sbq_ref, ck_ref, sak_ref, sbk_ref, gq_ref, gk_ref,
                     o_ref, k_scr, v_scr, *, n_lat, n_ctx, nlat_q, scale):
    qi = pl.program_id(2)
    nf = MLA_ROPE // 4
    heads = [slice(hd * LANES, (hd + 1) * LANES) for hd in range(q_ref.shape[1] // LANES)]

    @pl.when(qi == 0)
    def _():
        for hd, ln in enumerate(heads):
            _prep_keys(k_scr.at[hd], v_scr.at[hd], n_lat, n_ctx,
                       lambda r, ln=ln: kl_ref[r, ln].astype(F32) + krl_ref[r, :],
                       lambda r, ln=ln: kc_ref[r, ln].astype(F32) + krc_ref[r, :],
                       lambda r, ln=ln: _with_ones_lane(vl_ref[r, ln], MLA_V),
                       lambda r, ln=ln: _with_ones_lane(vc_ref[r, ln], MLA_V),
                       gk_ref[...], (ck_ref, sak_ref, sbk_ref), MLA_QK, nf)

    def attend(keys):
        def chain(hd, ln, rows):
            q = _head_norm(q_ref[rows, ln].astype(F32), gq_ref[...], MLA_QK)
            q = _rope(q, cq_ref[rows, :], saq_ref[rows, :], sbq_ref[rows, :], nf) * (scale * LOG2E)
            s = _nt_dot(q.astype(BF16), k_scr[hd, keys, :])
            yield
            p = jnp.exp2(s - jnp.max(s, axis=-1, keepdims=True)).astype(BF16)
            yield
            o = jnp.dot(p, v_scr[hd, keys, :], preferred_element_type=F32)
            o_ref[rows, ln] = (o * (1.0 / o[:, MLA_V:MLA_V + 1])).astype(o_ref.dtype)

        _emit_pipelined([chain(hd, ln, rows) for rows in _row_chains(q_ref.shape[0])
                         for hd, ln in enumerate(heads)])

    @pl.when(qi < nlat_q)
    def _():
        attend(slice(0, n_lat + n_ctx))

    @pl.when(qi >= nlat_q)
    def _():
        attend(slice(n_lat, n_lat + n_ctx))


def _mla_attention(q, kv, lat, g_q, g_k, *, n_batch, n_lat, n_ctx, need_ctx):
    nlat_q, nctx_q = n_lat // TQ, n_ctx // TQ
    nq = nlat_q + (nctx_q if need_ctx else 0)
    rows = n_batch * (n_lat + (n_ctx if need_ctx else 0))
    nf = MLA_ROPE // 4
    tabs = _rope_tables(n_lat, TQ, ((MLA_NOPE, True), (MLA_NOPE + 2 * nf, False)), nf)
    h = MLA_HEADS
    ctx0 = n_batch * n_lat // n_ctx
    kr_blk = (MLA_Q_LORA + MLA_KV_LORA) // LANES
    gq = jnp.zeros((1, LANES), F32).at[0, :MLA_QK].set(g_q)
    gk = jnp.zeros((1, LANES), F32).at[0, :MLA_QK].set(g_k)
    hps = MLA_HEADS_PER_STEP
    hw = hps * LANES
    qspec = pl.BlockSpec((TQ, hw), lambda b, hh, qi: (_row_block(b, qi, nlat_q, nctx_q, n_batch), hh))
    tq_spec = pl.BlockSpec((TQ, LANES), lambda b, hh, qi: (jnp.minimum(qi, nlat_q), 0))
    tk_spec = pl.BlockSpec((n_lat, LANES), lambda b, hh, qi: (0, 0))
    vec = pl.BlockSpec((1, LANES), lambda b, hh, qi: (0, 0))
    kern = functools.partial(_mla_attn_kernel, n_lat=n_lat, n_ctx=n_ctx, nlat_q=nlat_q,
                             scale=MLA_QK ** -0.5)
    return pl.pallas_call(
        kern,
        grid=(n_batch, h // hps, nq),
        in_specs=[qspec,
                  pl.BlockSpec((n_lat, hw), lambda b, hh, qi: (b, hh)),
                  pl.BlockSpec((n_ctx, hw), lambda b, hh, qi: (ctx0 + b, hh)),
                  pl.BlockSpec((n_lat, LANES), lambda b, hh, qi: (b, kr_blk)),
                  pl.BlockSpec((n_ctx, LANES), lambda b, hh, qi: (ctx0 + b, kr_blk)),
                  pl.BlockSpec((n_lat, hw), lambda b, hh, qi: (b, h // hps + hh)),
                  pl.BlockSpec((n_ctx, hw), lambda b, hh, qi: (ctx0 + b, h // hps + hh)),
                  tq_spec, tq_spec, tq_spec, tk_spec, tk_spec, tk_spec, vec, vec],
        out_specs=qspec,
        out_shape=jax.ShapeDtypeStruct((rows, h * LANES), BF16),
        scratch_shapes=[pltpu.VMEM((hps, n_lat + n_ctx, LANES), BF16),
                        pltpu.VMEM((hps, n_lat + n_ctx, LANES), BF16)],
        compiler_params=_cparams(("parallel", "parallel", "arbitrary")),
        name="mla_attention",
    )(q, kv, kv, lat, lat, kv, kv, *tabs, *tabs, gq, gk)


def _pad_cols(w, groups, width, first=0):
    k = w.shape[0]
    w3 = w.reshape(k, groups, width)
    out = jnp.zeros((k, groups, LANES), w.dtype).at[:, :, first:first + width].set(w3)
    return out.reshape(k, groups * LANES)


def _pad_rows(w, groups, width):
    d = w.shape[1]
    w3 = w.reshape(groups, width, d)
    return jnp.zeros((groups, LANES, d), w.dtype).at[:, :width, :].set(w3).reshape(groups * LANES, d)


def _mla_mixer(s, modv, g_mix, p, w_o, *, n_batch, n_lat, n_ctx, need_ctx):
    w_down, g_cq, g_ckv, w_uq, w_ukv, g_q, g_k = p
    d = s.shape[1]
    rows_all = n_batch * (n_lat + n_ctx)
    geo = dict(n_lat_rows=n_batch * n_lat, n_per_batch=n_lat)
    wd = jnp.concatenate([w_down[:, :MLA_Q_LORA + MLA_KV_LORA],
                          _pad_cols(w_down[:, MLA_Q_LORA + MLA_KV_LORA:], 1, MLA_ROPE, MLA_NOPE)], axis=1)
    lat = _nm_matmul(s, 0, d, g_mix, modv, rows_all, wd.astype(BF16), name="mla_down", **geo)
    wq = _pad_cols(w_uq, MLA_HEADS, MLA_QK).astype(BF16)
    q = _nm_matmul(lat, 0, MLA_Q_LORA, g_cq, None, rows_all, wq, name="mla_uq", out_dtype=BF16, **geo)
    wkv3 = w_ukv.reshape(MLA_KV_LORA, MLA_HEADS, MLA_NOPE + MLA_V)
    wkv = jnp.concatenate([_pad_cols(wkv3[:, :, :MLA_NOPE].reshape(MLA_KV_LORA, -1), MLA_HEADS, MLA_NOPE),
                           _pad_cols(wkv3[:, :, MLA_NOPE:].reshape(MLA_KV_LORA, -1), MLA_HEADS, MLA_V)],
                          axis=1).astype(BF16)
    kv = _nm_matmul(lat, MLA_Q_LORA // MLA_KV_LORA, MLA_KV_LORA, g_ckv, None, rows_all, wkv,
                    name="mla_ukv", out_dtype=BF16, **geo)
    o = _mla_attention(q, kv, lat, g_q, g_k, n_batch=n_batch, n_lat=n_lat, n_ctx=n_ctx, need_ctx=need_ctx)
    wo = _pad_rows(w_o, MLA_HEADS, MLA_V).astype(BF16)
    return o, wo


def _diff_attn_kernel(q_ref, kl_ref, kc_ref, vl_ref, vc_ref,
                      cq_ref, saq_ref, sbq_ref, ck_ref, sak_ref, sbk_ref, gq_ref, gk_ref,
                      lam_ref, gsub_ref, o_ref, k_scr, v_scr,
                      *, n_lat, n_ctx, nlat_q, scale, lam_init):
    qi = pl.program_id(2)
    nf = DIFF_HD // 4
    tabs = (ck_ref, sak_ref, sbk_ref)
    tile = lambda t: slice(t * LANES, (t + 1) * LANES)
    n_heads = o_ref.shape[1] // LANES

    @pl.when(qi == 0)
    def _():
        for hd in range(n_heads):
            ln = tile(hd)
            for sub in range(2):
                kt = tile(2 * hd + sub)
                _prep_keys(k_scr.at[2 * hd + sub], v_scr.at[hd] if sub == 0 else None, n_lat, n_ctx,
                           lambda r, kt=kt: kl_ref[r, kt], lambda r, kt=kt: kc_ref[r, kt],
                           lambda r, ln=ln: vl_ref[r, ln], lambda r, ln=ln: vc_ref[r, ln],
                           gk_ref[...], tabs, DIFF_HD, nf)

    lp = lam_ref[...]
    lam = (jnp.exp(jnp.sum(lp[0:1] * lp[1:2], axis=-1, keepdims=True))
           - jnp.exp(jnp.sum(lp[2:3] * lp[3:4], axis=-1, keepdims=True)) + lam_init)

    def attend(keys):
        def scores(t, rows):
            q = _head_norm(q_ref[rows, tile(t)].astype(F32), gq_ref[...], DIFF_HD)
            q = _rope(q, cq_ref[rows, :], saq_ref[rows, :], sbq_ref[rows, :], nf) * (scale * LOG2E)
            return _nt_dot(q.astype(BF16), k_scr[t, keys, :])

        def chain(hd, rows):
            s0 = scores(2 * hd, rows)
            s1 = scores(2 * hd + 1, rows)
            yield
            p0 = jnp.exp2(s0 - jnp.max(s0, axis=-1, keepdims=True))
            p1 = jnp.exp2(s1 - jnp.max(s1, axis=-1, keepdims=True))
            l0 = jnp.sum(p0, axis=-1, keepdims=True)
            l1 = jnp.sum(p1, axis=-1, keepdims=True)
            a = (p0 - p1 * (lam * l0 / l1)).astype(BF16)
            yield
            o = jnp.dot(a, v_scr[hd, keys, :], preferred_element_type=F32) * (1.0 / l0)
            o = _head_norm(o, gsub_ref[...], 2 * DIFF_HD) * (1.0 - lam_init)
            o_ref[rows, tile(hd)] = o.astype(o_ref.dtype)

        _emit_pipelined([chain(hd, rows) for rows in _row_chains(q_ref.shape[0])
                         for hd in range(n_heads)])

    @pl.when(qi < nlat_q)
    def _():
        attend(slice(0, n_lat + n_ctx))

    @pl.when(qi >= nlat_q)
    def _():
        attend(slice(n_lat, n_lat + n_ctx))


def _diff_attention(qkv, g_q, g_k, lam_p, g_sub, lam_init, *, n_batch, n_lat, n_ctx, need_ctx):
    nlat_q, nctx_q = n_lat // TQ, n_ctx // TQ
    nq = nlat_q + (nctx_q if need_ctx else 0)
    rows = n_batch * (n_lat + (n_ctx if need_ctx else 0))
    nf = DIFF_HD // 4
    tabs = _rope_tables(n_lat, TQ, ((0, True), (2 * nf, False)), nf)
    nh = DIFF_HEADS
    ctx0 = n_batch * n_lat // n_ctx
    gq = jnp.zeros((1, LANES), F32).at[0, :DIFF_HD].set(g_q)
    gk = jnp.zeros((1, LANES), F32).at[0, :DIFF_HD].set(g_k)
    lam8 = jnp.zeros((SUBLANES, LANES), F32).at[:4, :DIFF_HD].set(lam_p.astype(F32))
    rb = lambda b, qi: _row_block(b, qi, nlat_q, nctx_q, n_batch)
    tq_spec = pl.BlockSpec((TQ, LANES), lambda b, hh, qi: (jnp.minimum(qi, nlat_q), 0))
    tk_spec = pl.BlockSpec((n_lat, LANES), lambda b, hh, qi: (0, 0))
    vec = pl.BlockSpec((1, LANES), lambda b, hh, qi: (0, 0))
    hps = HEADS_PER_STEP
    qk_w, v_w = 2 * hps * LANES, hps * LANES
    k_blk0 = 2 * nh // (2 * hps)
    v_blk0 = 4 * nh // hps
    kern = functools.partial(_diff_attn_kernel, n_lat=n_lat, n_ctx=n_ctx, nlat_q=nlat_q,
                             scale=DIFF_HD ** -0.5, lam_init=lam_init)
    n_all = n_lat + n_ctx
    return pl.pallas_call(
        kern,
        grid=(n_batch, nh // hps, nq),
        in_specs=[pl.BlockSpec((TQ, qk_w), lambda b, hh, qi: (rb(b, qi), hh)),
                  pl.BlockSpec((n_lat, qk_w), lambda b, hh, qi: (b, k_blk0 + hh)),
                  pl.BlockSpec((n_ctx, qk_w), lambda b, hh, qi: (ctx0 + b, k_blk0 + hh)),
                  pl.BlockSpec((n_lat, v_w), lambda b, hh, qi: (b, v_blk0 + hh)),
                  pl.BlockSpec((n_ctx, v_w), lambda b, hh, qi: (ctx0 + b, v_blk0 + hh)),
                  tq_spec, tq_spec, tq_spec, tk_spec, tk_spec, tk_spec, vec, vec,
                  pl.BlockSpec((SUBLANES, LANES), lambda b, hh, qi: (0, 0)), vec],
        out_specs=pl.BlockSpec((TQ, v_w), lambda b, hh, qi: (rb(b, qi), hh)),
        out_shape=jax.ShapeDtypeStruct((rows, nh * LANES), BF16),
        scratch_shapes=[pltpu.VMEM((2 * hps, n_all, LANES), BF16), pltpu.VMEM((hps, n_all, LANES), BF16)],
        compiler_params=_cparams(("parallel", "parallel", "arbitrary")),
        name="diff_attention",
    )(qkv, qkv, qkv, qkv, qkv, *tabs, *tabs, gq, gk, lam8, g_sub.reshape(1, LANES))


def _diff_mixer(s, modv, g_mix, w_qkv, g_q, g_k, lam_p, g_sub, w_o, lam_init,
                *, n_batch, n_lat, n_ctx, need_ctx):
    d = s.shape[1]
    width = DIFF_HEADS * 2 * DIFF_HD
    rows_all = n_batch * (n_lat + n_ctx)
    w = jnp.concatenate([_pad_cols(w_qkv[:, :width], 2 * DIFF_HEADS, DIFF_HD),
                         _pad_cols(w_qkv[:, width:2 * width], 2 * DIFF_HEADS, DIFF_HD),
                         w_qkv[:, 2 * width:]], axis=1).astype(BF16)
    qkv = _nm_matmul(s, 0, d, g_mix, modv, rows_all, w, n_lat_rows=n_batch * n_lat,
                     n_per_batch=n_lat, name="diff_qkv", out_dtype=BF16)
    o = _diff_attention(qkv, g_q, g_k, lam_p, g_sub, lam_init,
                        n_batch=n_batch, n_lat=n_lat, n_ctx=n_ctx, need_ctx=need_ctx)
    return o, w_o.astype(BF16)


def _swa_attn_kernel(q_ref, kl_ref, kc_ref, vl_ref, vc_ref,
                     cq_ref, saq_ref, sbq_ref, ck_ref, sak_ref, sbk_ref, gq_ref, gk_ref, sink_ref,
                     o_ref, k_scr, v_scr, *, n_lat, n_ctx, nlat_q, scale):
    qi = pl.program_id(2)
    nf = SWA_HD // 4
    band = 3 * Q_BLOCK
    n_kv = k_scr.shape[0]
    tile = lambda t: slice(t * LANES, (t + 1) * LANES)

    @pl.when(qi == 0)
    def _():
        for kv in range(n_kv):
            ln = tile(kv)
            _prep_keys(k_scr.at[kv], v_scr.at[kv], n_lat, n_ctx,
                       lambda r, ln=ln: kl_ref[r, ln], lambda r, ln=ln: kc_ref[r, ln],
                       lambda r, ln=ln: _with_ones_lane(vl_ref[r, ln], SWA_HD),
                       lambda r, ln=ln: _with_ones_lane(vc_ref[r, ln], SWA_HD), gk_ref[...],
                       (ck_ref, sak_ref, sbk_ref), SWA_HD, nf)

    row_chains = _row_chains(q_ref.shape[0])
    reach = jnp.where(qi < nlat_q, WINDOW, -1)

    def chain(ci, rows, gi):
        kv = gi // SWA_GROUP
        qb = qi * len(row_chains) + ci
        k0 = pl.multiple_of(jnp.clip((qb - 1) * Q_BLOCK, 0, n_lat - band), Q_BLOCK)
        kpos = k0 + lax.broadcasted_iota(jnp.int32, (Q_BLOCK, band), 1)
        qpos = qb * Q_BLOCK + lax.broadcasted_iota(jnp.int32, (Q_BLOCK, band), 0)
        lanes = tile(gi)
        q = _head_norm(q_ref[rows, lanes].astype(F32), gq_ref[...], SWA_HD)
        q = _rope(q, cq_ref[rows, :], saq_ref[rows, :], sbq_ref[rows, :], nf) * (scale * LOG2E)
        q = q.astype(BF16)
        sb = _nt_dot(q, k_scr[kv, pl.ds(k0, band), :])
        sc = _nt_dot(q, k_scr[kv, n_lat:n_lat + n_ctx, :])
        yield
        sb = jnp.where(jnp.abs(kpos - qpos) <= reach, sb, NEG_INF)
        sink = sink_ref[gi][0:1, 0:1] * LOG2E
        m = jnp.maximum(jnp.maximum(jnp.max(sb, axis=-1, keepdims=True),
                                    jnp.max(sc, axis=-1, keepdims=True)), sink)
        pb = jnp.exp2(sb - m).astype(BF16)
        pc = jnp.exp2(sc - m).astype(BF16)
        p_sink = jnp.exp2(sink - m)
        yield
        o = (jnp.dot(pb, v_scr[kv, pl.ds(k0, band), :], preferred_element_type=F32)
             + jnp.dot(pc, v_scr[kv, n_lat:n_lat + n_ctx, :], preferred_element_type=F32))
        l = o[:, SWA_HD:SWA_HD + 1] + p_sink
        o_ref[rows, lanes] = (o * (1.0 / l)).astype(o_ref.dtype)

    _emit_pipelined([chain(ci, rows, gi) for ci, rows in enumerate(row_chains)
                     for gi in range(n_kv * SWA_GROUP)])


def _swa_attention(qkv, g_q, g_k, sink, *, n_batch, n_lat, n_ctx, need_ctx):
    nlat_q, nctx_q = n_lat // TQ, n_ctx // TQ
    nq = nlat_q + (nctx_q if need_ctx else 0)
    rows = n_batch * (n_lat + (n_ctx if need_ctx else 0))
    nf = SWA_HD // 4
    tabs = _rope_tables(n_lat, TQ, ((0, True), (2 * nf, False)), nf)
    nh, nkv, grp = SWA_HEADS, SWA_KV_HEADS, SWA_GROUP
    ctx0 = n_batch * n_lat // n_ctx
    gq = jnp.zeros((1, LANES), F32).at[0, :SWA_HD].set(g_q)
    gk = jnp.zeros((1, LANES), F32).at[0, :SWA_HD].set(g_k)
    sink_t = jnp.broadcast_to(sink.astype(F32).reshape(nh, 1, 1), (nh, SUBLANES, LANES))
    rb = lambda b, qi: _row_block(b, qi, nlat_q, nctx_q, n_batch)
    tq_spec = pl.BlockSpec((TQ, LANES), lambda b, kh, qi: (jnp.minimum(qi, nlat_q), 0))
    tk_spec = pl.BlockSpec((n_lat, LANES), lambda b, kh, qi: (0, 0))
    vec = pl.BlockSpec((1, LANES), lambda b, kh, qi: (0, 0))
    kern = functools.partial(_swa_attn_kernel, n_lat=n_lat, n_ctx=n_ctx, nlat_q=nlat_q,
                             scale=SWA_HD ** -0.5)
    n_all = n_lat + n_ctx
    kps = SWA_KV_PER_STEP
    q_w, kv_w = kps * grp * LANES, kps * LANES
    k_blk0, v_blk0 = nh // kps, (nh + nkv) // kps
    return pl.pallas_call(
        kern,
        grid=(n_batch, nkv // kps, nq),
        in_specs=[pl.BlockSpec((TQ, q_w), lambda b, kh, qi: (rb(b, qi), kh)),
                  pl.BlockSpec((n_lat, kv_w), lambda b, kh, qi: (b, k_blk0 + kh)),
                  pl.BlockSpec((n_ctx, kv_w), lambda b, kh, qi: (ctx0 + b, k_blk0 + kh)),
                  pl.BlockSpec((n_lat, kv_w), lambda b, kh, qi: (b, v_blk0 + kh)),
                  pl.BlockSpec((n_ctx, kv_w), lambda b, kh, qi: (ctx0 + b, v_blk0 + kh)),
                  tq_spec, tq_spec, tq_spec, tk_spec, tk_spec, tk_spec, vec, vec,
                  pl.BlockSpec((kps * grp, SUBLANES, LANES), lambda b, kh, qi: (kh, 0, 0))],
        out_specs=pl.BlockSpec((TQ, q_w), lambda b, kh, qi: (rb(b, qi), kh)),
        out_shape=jax.ShapeDtypeStruct((rows, nh * LANES), BF16),
        scratch_shapes=[pltpu.VMEM((kps, n_all, LANES), BF16), pltpu.VMEM((kps, n_all, LANES), BF16)],
        compiler_params=_cparams(("parallel", "parallel", "arbitrary")),
        name="swa_attention",
    )(qkv, qkv, qkv, qkv, qkv, *tabs, *tabs, gq, gk, sink_t)


def _swa_mixer(s, modv, g_mix, w_qkv, g_q, g_k, sink, w_o, *, n_batch, n_lat, n_ctx, need_ctx):
    d = s.shape[1]
    qw, kw = SWA_HEADS * SWA_HD, SWA_KV_HEADS * SWA_HD
    rows_all = n_batch * (n_lat + n_ctx)
    w = jnp.concatenate([_pad_cols(w_qkv[:, :qw], SWA_HEADS, SWA_HD),
                         _pad_cols(w_qkv[:, qw:qw + kw], SWA_KV_HEADS, SWA_HD),
                         _pad_cols(w_qkv[:, qw + kw:], SWA_KV_HEADS, SWA_HD)], axis=1).astype(BF16)
    qkv = _nm_matmul(s, 0, d, g_mix, modv, rows_all, w, n_lat_rows=n_batch * n_lat,
                     n_per_batch=n_lat, name="swa_qkv", out_dtype=BF16)
    o = _swa_attention(qkv, g_q, g_k, sink, n_batch=n_batch, n_lat=n_lat, n_ctx=n_ctx, need_ctx=need_ctx)
    return o, _pad_rows(w_o, SWA_HEADS, SWA_HD).astype(BF16)


def _merge_exchange_pairs(n):
    pairs = []
    t = max(1, (n - 1).bit_length())
    p = 1 << (t - 1)
    while p > 0:
        q, r, d = 1 << (t - 1), 0, p
        while d > 0:
            pairs += [(i, i + d) for i in range(n - d) if (i & p) == r]
            d, q, r = q - p, q >> 1, p
        p >>= 1
    return pairs


def _top_values(s, count):
    n = s.shape[0] // SUBLANES
    v = [s[j * SUBLANES:(j + 1) * SUBLANES, :] for j in range(n)]
    for i, j in _merge_exchange_pairs(n):
        v[i], v[j] = jnp.maximum(v[i], v[j]), jnp.minimum(v[i], v[j])
    vals = []
    for r in range(count):
        m = jnp.max(v[0], axis=0, keepdims=True)
        vals.append(m)
        depth = min(n, count - 1 - r)
        head = v[0] == m
        for j in range(depth):
            nxt = v[j + 1] if j + 1 < n else jnp.full_like(v[j], NEG_BIG)
            v[j] = jnp.where(head, nxt, v[j])
    return vals


def _peer_score_kernel(s_ref, g_ref, mod_ref, wq_ref, keys_ref,
                       ht_ref, c_ref, e1_ref, s2_ref, e2_ref, qt_scr):
    k = PEER_TOPK
    h = _norm_mod(s_ref[...], g_ref[...], mod_ref[0], 3, 4)
    ht = h.T.astype(BF16)
    ht_ref[0] = ht

    tm = ht.shape[1]
    group = 2
    q_rows = group * 2 * LANES

    def group_chain(gidx):
        heads = [gidx * group + i for i in range(group)]
        rows = slice(gidx * q_rows, (gidx + 1) * q_rows)
        qt_scr[rows, :] = jnp.dot(wq_ref[rows, :], ht, preferred_element_type=F32)
        yield
        scores = []
        for hh in heads:
            for half in range(2):
                r = slice((hh * 2 + half) * LANES, (hh * 2 + half + 1) * LANES)
                scores.append(jnp.dot(keys_ref[hh, half], qt_scr[r, :].astype(BF16),
                                      preferred_element_type=F32))
        tops = _top_values(jnp.concatenate(scores, axis=1), k + 1)
        cands = []
        for i in range(group):
            v1 = [t[:, (2 * i) * tm:(2 * i + 1) * tm] for t in tops]
            v2 = [t[:, (2 * i + 1) * tm:(2 * i + 2) * tm] for t in tops]
            top8 = jnp.concatenate(v2[:8], axis=0)
            pieces = [v1[0] + jnp.concatenate(v2[:k], axis=0)]
            pieces += [v1[j] + top8 for j in range(1, 8)]
            pieces += [jnp.concatenate(v1[8:k], axis=0) + v2[0]]
            pieces += [jnp.concatenate([v1[0] + v2[k], v1[k] + v2[0]]
                                       + [jnp.full_like(v1[0], NEG_BIG)] * 6, axis=0)]
            cands.append(jnp.concatenate(pieces, axis=0))
        best = _top_values(jnp.concatenate(cands, axis=1), k + 1)
        grp = (PEER_KEYS // SUBLANES, SUBLANES, tm)
        for i, hh in enumerate(heads):
            ms = [t[:, i * tm:(i + 1) * tm] for t in best]
            s1, s2 = scores[2 * i], scores[2 * i + 1]
            m1, m2 = tops[0][:, (2 * i) * tm:(2 * i + 1) * tm], tops[0][:, (2 * i + 1) * tm:(2 * i + 2) * tm]
            thr = 0.5 * (ms[k - 1] + ms[k])
            z = jnp.ones_like(ms[0])
            for r in range(1, k):
                z = z + jnp.exp(ms[r] - ms[0])
            c_ref[hh] = (thr - s1).reshape(grp)
            e1_ref[hh] = (jnp.exp(s1 - m1) * (0.5 / z)).reshape(grp)
            s2_ref[0, hh] = s2
            e2_ref[0, hh] = jnp.exp(s2 - m2)

    _emit_pipelined([group_chain(gidx) for gidx in range(PEER_HEADS // group)])


def _peer_scores(s, modv, g_ffn, wq_t, keys, rows, *, n_lat_rows, n_per_batch):
    d = s.shape[1]
    n_batch = modv.shape[0] - 1
    tm = SCORE_TM
    nlat_blocks = n_lat_rows // tm
    bpb = n_per_batch // tm
    nq = wq_t.shape[0]
    hk = (rows // tm, PEER_HEADS, PEER_KEYS, tm)
    hk_spec = pl.BlockSpec((1, PEER_HEADS, PEER_KEYS, tm), lambda i: (i, 0, 0, 0))
    hg = (PEER_HEADS, PEER_KEYS // SUBLANES, SUBLANES, rows)
    hg_spec = pl.BlockSpec((PEER_HEADS, PEER_KEYS // SUBLANES, SUBLANES, tm), lambda i: (0, 0, 0, i))
    return pl.pallas_call(
        _peer_score_kernel,
        grid=(rows // tm,),
        in_specs=[pl.BlockSpec((tm, d), lambda i: (i, 0)),
                  pl.BlockSpec((1, d), lambda i: (0, 0)),
                  pl.BlockSpec((1, SUBLANES, d), lambda i: (_mod_index(i, nlat_blocks, bpb, n_batch), 0, 0)),
                  pl.BlockSpec((nq, d), lambda i: (0, 0)),
                  pl.BlockSpec((PEER_HEADS, 2, PEER_KEYS, LANES), lambda i: (0, 0, 0, 0))],
        out_specs=[pl.BlockSpec((1, d, tm), lambda i: (i, 0, 0)), hg_spec, hg_spec, hk_spec, hk_spec],
        out_shape=[jax.ShapeDtypeStruct((rows // tm, d, tm), BF16)] + [jax.ShapeDtypeStruct(hg, F32)] * 2
                  + [jax.ShapeDtypeStruct(hk, F32)] * 2,
        scratch_shapes=[pltpu.VMEM((nq, tm), F32)],
        compiler_params=_cparams(("parallel",)),
        name="peer_scores",
    )(s, g_ffn.reshape(1, d), modv, wq_t, keys)


def _peer_expert_kernel(ht_ref, u_ref, vt_ref, c_ref, e1_ref, s2_ref, e2_ref, s_ref, mod_ref,
                        o_ref, acc_scr, st0_scr, st1_scr, at0_scr, at1_scr, *, n_exp_blocks):
    g = pl.program_id(0)
    st_scr, at_scr = (st0_scr, st1_scr), (at0_scr, at1_scr)
    te, tm = st0_scr.shape
    jc = jnp.maximum(g - 2, 0) % n_exp_blocks

    @pl.when(g == 0)
    def _():
        for ref in st_scr + at_scr:
            ref[...] = jnp.zeros_like(ref)

    @pl.when(jc == 0)
    def _():
        acc_scr[...] = jnp.zeros_like(acc_scr)

    def stages(slot_a, slot_b):
        n_chunks = te // MXU_CHUNK_ROWS

        def value_chunk(q):
            ms = slice(q * MXU_CHUNK_ROWS, (q + 1) * MXU_CHUNK_ROWS)
            acc_scr[ms, :] += jnp.dot(vt_ref[0, 0, ms, :], at_scr[slot_a][...], preferred_element_type=F32)

        def score_chunk(q):
            ms = slice(q * MXU_CHUNK_ROWS, (q + 1) * MXU_CHUNK_ROWS)
            st_scr[slot_a][ms, :] = jnp.dot(u_ref[0, ms, :], ht_ref[0], preferred_element_type=F32)

        rg_rows = 4 * SUBLANES
        al_group = 4
        col_tiles = 1
        tiles = [(rg, c0, al0)
                 for rg in range(LANES // rg_rows)
                 for c0 in range(0, tm // LANES, col_tiles)
                 for al0 in range(0, te // LANES, al_group)]
        assert acc_scr.shape[0] == te
        per_chunk = len(tiles) // (2 * n_chunks)
        for t, (rg, c0, al0) in enumerate(tiles):
            if t % per_chunk == 0:
                q = t // per_chunk
                (value_chunk if q % 2 == 0 else score_chunk)(q // 2)
            brows = slice(rg * rg_rows, (rg + 1) * rg_rows)
            lanes = [slice((c0 + ct) * LANES, (c0 + ct + 1) * LANES) for ct in range(col_tiles)]
            gate = [[jnp.zeros((rg_rows, LANES), F32) for _ in lanes] for _ in range(al_group)]
            for hh in range(PEER_HEADS):
                for ct, ln in enumerate(lanes):
                    s2t = s2_ref[0, hh, brows, ln]
                    e2t = e2_ref[0, hh, brows, ln]
                    for k in range(al_group):
                        al = al0 + k
                        c = c_ref[hh, 0, al:al + 1, ln]
                        e1 = e1_ref[hh, 0, al:al + 1, ln]
                        gate[k][ct] = gate[k][ct] + jnp.where(s2t >= c, e2t, 0.0) * e1
            for k in range(al_group):
                erows = slice((al0 + k) * LANES + rg * rg_rows, (al0 + k) * LANES + (rg + 1) * rg_rows)
                for ct, ln in enumerate(lanes):
                    sv = st_scr[slot_b][erows, ln]
                    act = sv * (1.0 + lax.erf(sv * (2.0 ** -0.5)))
                    at_scr[slot_b][erows, ln] = (gate[k][ct] * act).astype(BF16)

    for parity in range(2):
        @pl.when(g % 2 == parity)
        def _():
            stages(parity, 1 - parity)

    @pl.when((jc == n_exp_blocks - 1) & (g >= 2))
    def _():
        o_ref[...] = s_ref[...] + mod_ref[0][5:6] * acc_scr[...].T


def _peer_experts(ht, u, vt, layer, c, e1, s2, e2, s, modv, rows, *, n_lat_rows, n_per_batch):
    d = s.shape[1]
    n_exp = u.shape[1]
    n_batch = modv.shape[0] - 1
    tm, te = PEER_TM, PEER_TE
    nlat_blocks = n_lat_rows // tm
    bpb = n_per_batch // tm
    nj = n_exp // te
    n_pairs = (rows // tm) * nj

    def pair(g, lag):
        p = jnp.clip(g - lag, 0, n_pairs - 1)
        return p // nj, p % nj

    assert SCORE_TM == tm
    hk_spec = pl.BlockSpec((1, PEER_HEADS, PEER_KEYS, tm), lambda g: (pair(g, 1)[0], 0, 0, 0))
    hg_spec = pl.BlockSpec((PEER_HEADS, 1, SUBLANES, tm), lambda g: (0, pair(g, 1)[1], 0, pair(g, 1)[0]))
    kern = functools.partial(_peer_expert_kernel, n_exp_blocks=nj)
    return pl.pallas_call(
        kern,
        grid=(n_pairs + 2,),
        in_specs=[pl.BlockSpec((1, d, tm), lambda g: (pair(g, 0)[0], 0, 0)),
                  pl.BlockSpec((1, te, d), lambda g: (layer, pair(g, 0)[1], 0)),
                  pl.BlockSpec((1, 1, d, te), lambda g: (layer, pair(g, 2)[1], 0, 0)),
                  hg_spec, hg_spec, hk_spec, hk_spec,
                  pl.BlockSpec((tm, d), lambda g: (pair(g, 2)[0], 0)),
                  pl.BlockSpec((1, SUBLANES, d),
                               lambda g: (_mod_index(pair(g, 2)[0], nlat_blocks, bpb, n_batch), 0, 0))],
        out_specs=pl.BlockSpec((tm, d), lambda g: (pair(g, 2)[0], 0)),
        out_shape=jax.ShapeDtypeStruct((rows, d), F32),
        scratch_shapes=[pltpu.VMEM((d, tm), F32), pltpu.VMEM((te, tm), F32), pltpu.VMEM((te, tm), F32),
                        pltpu.VMEM((te, tm), BF16), pltpu.VMEM((te, tm), BF16)],
        compiler_params=_cparams(("arbitrary",)),
        name="peer_experts",
    )(ht, u, vt, c, e1, s2, e2, s, modv)


def _peer(s, modv, g_ffn, w_q, keys, u_all, vt_all, layer, rows, *, n_lat_rows, n_per_batch):
    geo = dict(n_lat_rows=n_lat_rows, n_per_batch=n_per_batch)
    ht, c, e1, s2, e2 = _peer_scores(s, modv, g_ffn, w_q.T.astype(BF16), keys.astype(BF16), rows, **geo)
    return _peer_experts(ht, u_all, vt_all, layer, c, e1, s2, e2, s, modv, rows, **geo)


def kernel(x, c, ctx, c_ctx, ada_w, ada_b, norm_mix, norm_ffn, mla_w_down, mla_g_cq, mla_g_ckv, mla_w_uq, mla_w_ukv, mla_g_q, mla_g_k, mla_w_o, diff_w_qkv, diff_g_q, diff_g_k, diff_lambda, diff_g_sub, diff_w_o, swa_w_qkv, swa_g_q, swa_g_k, swa_sink, swa_w_o, peer_w_q, peer_keys, peer_u, peer_v):
    n_batch, n_lat, d = x.shape
    n_ctx = ctx.shape[1]
    depth = ada_w.shape[0]
    rows_lat = n_batch * n_lat
    rows_all = rows_lat + n_batch * n_ctx
    assert n_lat % max(TM, PEER_TM, SCORE_TM, TQ) == 0 and (n_batch * n_ctx) % max(TM, PEER_TM, SCORE_TM) == 0
    assert n_ctx % TQ == 0 and n_ctx % KEY_PREP_ROWS == 0 and n_lat % KEY_PREP_ROWS == 0
    assert n_lat % GRID_W == 0 and rows_lat % n_ctx == 0 and n_lat >= 3 * Q_BLOCK

    s = jnp.concatenate([x.reshape(rows_lat, d), ctx.reshape(n_batch * n_ctx, d)], axis=0)
    cond_rows = -(-(n_batch + 1) // SUBLANES) * SUBLANES
    cc = jnp.zeros((cond_rows, d), F32).at[:n_batch].set(c).at[n_batch].set(c_ctx)
    mods = _ada_mods(cc, ada_w, ada_b)
    geo = dict(n_lat_rows=rows_lat, n_per_batch=n_lat)
    n_exp = peer_u.shape[1]
    u_all = peer_u.astype(BF16)
    vt_all = peer_v.reshape(depth, n_exp // PEER_TE, PEER_TE, d).transpose(0, 1, 3, 2).astype(BF16)

    for i in range(depth):
        last = i == depth - 1
        kind, j = i % 3, i // 3
        modv = jnp.zeros((n_batch + 1, SUBLANES, d), F32).at[:, :6, :].set(
            mods[i, :n_batch + 1].reshape(n_batch + 1, 6, d))
        dims = dict(n_batch=n_batch, n_lat=n_lat, n_ctx=n_ctx, need_ctx=not last)
        if kind == 0:
            p = (mla_w_down[j], mla_g_cq[j], mla_g_ckv[j], mla_w_uq[j], mla_w_ukv[j], mla_g_q[j], mla_g_k[j])
            o, wo = _mla_mixer(s, modv, norm_mix[i], p, mla_w_o[j], **dims)
        elif kind == 1:
            lam_init = 0.8 - 0.6 * math.exp(-0.3 * i)
            o, wo = _diff_mixer(s, modv, norm_mix[i], diff_w_qkv[j], diff_g_q[j], diff_g_k[j],
                                diff_lambda[j], diff_g_sub[j], diff_w_o[j], lam_init, **dims)
        else:
            o, wo = _swa_mixer(s, modv, norm_mix[i], swa_w_qkv[j], swa_g_q[j], swa_g_k[j],
                               swa_sink[j], swa_w_o[j], **dims)
        rows = rows_lat if last else rows_all
        s = _proj_resid(o, wo, s, modv, rows, gate_row=2, **geo)
        s = _peer(s, modv, norm_ffn[i], peer_w_q[i], peer_keys[i], u_all, vt_all, i, rows, **geo)
    return s.reshape(n_batch, n_lat, d)
```

```python
import functools
import math

import jax
import jax.numpy as jnp
from jax import lax
from jax.experimental import pallas as pl
from jax.experimental.pallas import tpu as pltpu

F32 = jnp.float32
BF16 = jnp.bfloat16

LANES = 128
EPS = 1e-6
NEG_INF = -1e30
NEG_BIG = -3.0e38
ROPE_THETA = 10000.0
GRID_W = 64
Q_BLOCK = 128
WINDOW = 128

MLA_HEADS, MLA_NOPE, MLA_ROPE, MLA_V = 16, 64, 32, 64
MLA_QK = MLA_NOPE + MLA_ROPE
MLA_Q_LORA, MLA_KV_LORA = 768, 256
DIFF_HEADS, DIFF_HD = 8, 64
SWA_HEADS, SWA_KV_HEADS, SWA_HD = 16, 4, 64
SWA_GROUP = SWA_HEADS // SWA_KV_HEADS
PEER_HEADS, PEER_KEYS, PEER_TOPK = 8, 128, 16

VMEM_LIMIT = 56 * 1024 * 1024

TM = 1024
NM_MAX_TN = 2048
ADA_TN = 1536
KEY_PREP_ROWS = 256
SCORE_TM = 512
TQ = 256
PEER_TM = 512
SUBLANES = 8
PEER_TE = SUBLANES * LANES
MXU_CHUNK_ROWS = 128


def _cparams(sem):
    return pltpu.CompilerParams(dimension_semantics=sem, vmem_limit_bytes=VMEM_LIMIT)


def _ada_kernel(c_ref, w_ref, b_ref, o_ref):
    c = c_ref[...]
    sc = c * (1.0 / (1.0 + jnp.exp(-c)))
    o_ref[0] = jnp.dot(sc, w_ref[0], preferred_element_type=F32,
                       precision=lax.Precision.HIGHEST) + b_ref[0]


def _ada_mods(cc, ada_w, ada_b):
    depth, d, d6 = ada_w.shape
    rows = cc.shape[0]
    tn = ADA_TN
    return pl.pallas_call(
        _ada_kernel,
        grid=(depth, d6 // tn),
        in_specs=[pl.BlockSpec((rows, d), lambda i, n: (0, 0)),
                  pl.BlockSpec((1, d, tn), lambda i, n: (i, 0, n)),
                  pl.BlockSpec((1, 1, tn), lambda i, n: (i, 0, n))],
        out_specs=pl.BlockSpec((1, rows, tn), lambda i, n: (i, 0, n)),
        out_shape=jax.ShapeDtypeStruct((depth, rows, d6), F32),
        compiler_params=_cparams(("arbitrary", "arbitrary")),
        name="ada_mods",
    )(cc, ada_w, ada_b.reshape(depth, 1, d6))


def _norm_mod(x, g, mod, shift_row, scale_row):
    ms = jnp.mean(x * x, axis=-1, keepdims=True)
    y = x * lax.rsqrt(ms + EPS) * g
    if mod is not None:
        y = y * (1.0 + mod[scale_row:scale_row + 1]) + mod[shift_row:shift_row + 1]
    return y


def _nm_kernel(*refs, use_mod, shift_row, scale_row):
    if use_mod:
        x_ref, g_ref, mod_ref, w_ref, o_ref, h_scr = refs
    else:
        (x_ref, g_ref, w_ref, o_ref, h_scr), mod_ref = refs, None

    @pl.when(pl.program_id(1) == 0)
    def _():
        mod = mod_ref[0] if use_mod else None
        h_scr[...] = _norm_mod(x_ref[...], g_ref[...], mod, shift_row, scale_row).astype(BF16)

    o_ref[...] = jnp.dot(h_scr[...], w_ref[...], preferred_element_type=F32).astype(o_ref.dtype)


def _mod_index(i, nlat_blocks, blocks_per_batch, n_batch):
    return jnp.where(i < nlat_blocks, i // blocks_per_batch, n_batch)


def _nm_matmul(x, kblk, k, g, modv, rows, w, *, n_lat_rows, n_per_batch, out_dtype=F32,
               shift_row=0, scale_row=1, name="nm_matmul"):
    nout = w.shape[1]
    tn = next(t for t in range(min(nout, NM_MAX_TN), 0, -LANES) if nout % t == 0)
    use_mod = modv is not None
    in_specs = [pl.BlockSpec((TM, k), lambda i, n: (i, kblk)),
                pl.BlockSpec((1, k), lambda i, n: (0, 0))]
    operands = [x, g.reshape(1, k)]
    if use_mod:
        n_batch = modv.shape[0] - 1
        nlat_blocks = n_lat_rows // TM
        bpb = n_per_batch // TM
        in_specs.append(pl.BlockSpec(
            (1, SUBLANES, k), lambda i, n: (_mod_index(i, nlat_blocks, bpb, n_batch), 0, 0)))
        operands.append(modv)
    in_specs.append(pl.BlockSpec((k, tn), lambda i, n: (0, n)))
    operands.append(w)
    kern = functools.partial(_nm_kernel, use_mod=use_mod, shift_row=shift_row, scale_row=scale_row)
    return pl.pallas_call(
        kern,
        grid=(rows // TM, nout // tn),
        in_specs=in_specs,
        out_specs=pl.BlockSpec((TM, tn), lambda i, n: (i, n)),
        out_shape=jax.ShapeDtypeStruct((rows, nout), out_dtype),
        scratch_shapes=[pltpu.VMEM((TM, k), BF16)],
        compiler_params=_cparams(("parallel", "arbitrary")),
        name=name,
    )(*operands)


def _proj_resid_kernel(a_ref, w_ref, s_ref, mod_ref, o_ref, *, gate_row):
    y = jnp.dot(a_ref[...].astype(BF16), w_ref[...], preferred_element_type=F32)
    o_ref[...] = s_ref[...] + mod_ref[0][gate_row:gate_row + 1] * y


def _proj_resid(a, w, s, modv, rows, *, n_lat_rows, n_per_batch, gate_row, name="proj_resid"):
    ka = a.shape[1]
    d = w.shape[1]
    n_batch = modv.shape[0] - 1
    nlat_blocks = n_lat_rows // TM
    bpb = n_per_batch // TM
    return pl.pallas_call(
        functools.partial(_proj_resid_kernel, gate_row=gate_row),
        grid=(rows // TM,),
        in_specs=[pl.BlockSpec((TM, ka), lambda i: (i, 0)),
                  pl.BlockSpec((ka, d), lambda i: (0, 0)),
                  pl.BlockSpec((TM, d), lambda i: (i, 0)),
                  pl.BlockSpec((1, SUBLANES, d), lambda i: (_mod_index(i, nlat_blocks, bpb, n_batch), 0, 0))],
        out_specs=pl.BlockSpec((TM, d), lambda i: (i, 0)),
        out_shape=jax.ShapeDtypeStruct((rows, d), F32),
        compiler_params=_cparams(("parallel",)),
        name=name,
    )(a, w, s, modv)


def _rope_tables(n_lat, extra_rows, segments, nf):
    pos = jnp.arange(n_lat, dtype=jnp.int32)
    row = (pos // GRID_W).astype(F32)
    col = (pos % GRID_W).astype(F32)
    inv = ROPE_THETA ** (-jnp.arange(nf, dtype=F32) / nf)
    cos = jnp.ones((n_lat, LANES), F32)
    sa = jnp.zeros((n_lat, LANES), F32)
    sb = jnp.zeros((n_lat, LANES), F32)
    for first, use_row in segments:
        ang = (row if use_row else col)[:, None] * inv[None, :]
        c, s = jnp.cos(ang), jnp.sin(ang)
        cos = cos.at[:, first:first + nf].set(c).at[:, first + nf:first + 2 * nf].set(c)
        sa = sa.at[:, first:first + nf].set(-s)
        sb = sb.at[:, first + nf:first + 2 * nf].set(s)
    pad = lambda t, v: jnp.concatenate([t, jnp.full((extra_rows, LANES), v, F32)], axis=0)
    return pad(cos, 1.0), pad(sa, 0.0), pad(sb, 0.0)


def _head_norm(x, g, d):
    ssq = jnp.sum(x * x, axis=-1, keepdims=True) * (1.0 / d)
    return x * lax.rsqrt(ssq + EPS) * g


def _rope(y, cos, sa, sb, nf):
    return y * cos + pltpu.roll(y, LANES - nf, 1) * sa + pltpu.roll(y, nf, 1) * sb


def _with_ones_lane(v, lane):
    return jnp.where(lax.broadcasted_iota(jnp.int32, v.shape, 1) == lane, jnp.ones_like(v), v)


def _row_block(b, qi, nlat_q, nctx_q, n_batch):
    return jnp.where(qi < nlat_q, b * nlat_q + qi, n_batch * nlat_q + b * nctx_q + (qi - nlat_q))


def _prep_keys(k_scr, v_scr, n_lat, n_ctx, load_k_lat, load_k_ctx, load_v_lat, load_v_ctx,
               gk, tabs, d, nf):
    cos_ref, sa_ref, sb_ref = tabs
    ch = KEY_PREP_ROWS

    def lat_body(c, carry):
        r = pl.ds(pl.multiple_of(c * ch, ch), ch)
        kk = _head_norm(load_k_lat(r).astype(F32), gk, d)
        kk = _rope(kk, cos_ref[r, :], sa_ref[r, :], sb_ref[r, :], nf)
        k_scr[r, :] = kk.astype(BF16)
        if v_scr is not None:
            v_scr[r, :] = load_v_lat(r).astype(BF16)
        return carry

    lax.fori_loop(0, n_lat // ch, lat_body, 0, unroll=2)

    def ctx_body(c, carry):
        r = pl.ds(pl.multiple_of(c * ch, ch), ch)
        ro = pl.ds(pl.multiple_of(n_lat + c * ch, ch), ch)
        k_scr[ro, :] = _head_norm(load_k_ctx(r).astype(F32), gk, d).astype(BF16)
        if v_scr is not None:
            v_scr[ro, :] = load_v_ctx(r).astype(BF16)
        return carry

    lax.fori_loop(0, n_ctx // ch, ctx_body, 0)


def _nt_dot(a, b):
    return lax.dot_general(a, b, (((1,), (1,)), ((), ())), preferred_element_type=F32)


LOG2E = math.log2(math.e)
CHAIN_ROWS = 128
MLA_HEADS_PER_STEP = 8
HEADS_PER_STEP = 4
SWA_KV_PER_STEP = 1


def _row_chains(n_rows):
    return [slice(r, r + CHAIN_ROWS) for r in range(0, n_rows, CHAIN_ROWS)]


def _emit_pipelined(chains):
    pending, active = list(chains), []
    while pending or active:
        if pending:
            active.append(pending.pop(0))
        for ch in list(active):
            try:
                next(ch)
            except StopIteration:
                active.remove(ch)


def _mla_attn_kernel(q_ref, kl_ref, kc_ref, krl_ref, krc_ref, vl_ref, vc_ref,
                     cq_ref, saq_ref, sbq_ref, ck_ref, sak_ref, sbk_ref, gq_ref, gk_ref,
                     o_ref, k_scr, v_scr, *, n_lat, n_ctx, nlat_q, scale):
    qi = pl.program_id(2)
    nf = MLA_ROPE // 4
    heads = [slice(hd * LANES, (hd + 1) * LANES) for hd in range(q_ref.shape[1] // LANES)]

    @pl.when(qi == 0)
    def _():
        for hd, ln in enumerate(heads):
            _prep_keys(k_scr.at[hd], v_scr.at[hd], n_lat, n_ctx,
                       lambda r, ln=ln: kl_ref[r, ln].astype(F32) + krl_ref[r, :],
                       lambda r, ln=ln: kc_ref[r, ln].astype(F32) + krc_ref[r, :],
                       lambda r, ln=ln: _with_ones_lane(vl_ref[r, ln], MLA_V),
                       lambda r, ln=ln: _with_ones_lane(vc_ref[r, ln], MLA_V),
                       gk_ref[...], (ck_ref, sak_ref, sbk_ref), MLA_QK, nf)

    def attend(keys):
        def chain(hd, ln, rows):
            q = _head_norm(q_ref[rows, ln].astype(F32), gq_ref[...], MLA_QK)
            q = _rope(q, cq_ref[rows, :], saq_ref[rows, :], sbq_ref[rows, :], nf) * (scale * LOG2E)
            s = _nt_dot(q.astype(BF16), k_scr[hd, keys, :])
            yield
            p = jnp.exp2(s - jnp.max(s, axis=-1, keepdims=True)).astype(BF16)
            yield
            o = jnp.dot(p, v_scr[hd, keys, :], preferred_element_type=F32)
            o_ref[rows, ln] = (o * (1.0 / o[:, MLA_V:MLA_V + 1])).astype(o_ref.dtype)

        _emit_pipelined([chain(hd, ln, rows) for rows in _row_chains(q_ref.shape[0])
                         for hd, ln in enumerate(heads)])

    @pl.when(qi < nlat_q)
    def _():
        attend(slice(0, n_lat + n_ctx))

    @pl.when(qi >= nlat_q)
    def _():
        attend(slice(n_lat, n_lat + n_ctx))


def _mla_attention(q, kv, lat, g_q, g_k, *, n_batch, n_lat, n_ctx, need_ctx):
    nlat_q, nctx_q = n_lat // TQ, n_ctx // TQ
    nq = nlat_q + (nctx_q if need_ctx else 0)
    rows = n_batch * (n_lat + (n_ctx if need_ctx else 0))
    nf = MLA_ROPE // 4
    tabs = _rope_tables(n_lat, TQ, ((MLA_NOPE, True), (MLA_NOPE + 2 * nf, False)), nf)
    h = MLA_HEADS
    ctx0 = n_batch * n_lat // n_ctx
    kr_blk = (MLA_Q_LORA + MLA_KV_LORA) // LANES
    gq = jnp.zeros((1, LANES), F32).at[0, :MLA_QK].set(g_q)
    gk = jnp.zeros((1, LANES), F32).at[0, :MLA_QK].set(g_k)
    hps = MLA_HEADS_PER_STEP
    hw = hps * LANES
    qspec = pl.BlockSpec((TQ, hw), lambda b, hh, qi: (_row_block(b, qi, nlat_q, nctx_q, n_batch), hh))
    tq_spec = pl.BlockSpec((TQ, LANES), lambda b, hh, qi: (jnp.minimum(qi, nlat_q), 0))
    tk_spec = pl.BlockSpec((n_lat, LANES), lambda b, hh, qi: (0, 0))
    vec = pl.BlockSpec((1, LANES), lambda b, hh, qi: (0, 0))
    kern = functools.partial(_mla_attn_kernel, n_lat=n_lat, n_ctx=n_ctx, nlat_q=nlat_q,
                             scale=MLA_QK ** -0.5)
    return pl.pallas_call(
        kern,
        grid=(n_batch, h // hps, nq),
        in_specs=[qspec,
                  pl.BlockSpec((n_lat, hw), lambda b, hh, qi: (b, hh)),
                  pl.BlockSpec((n_ctx, hw), lambda b, hh, qi: (ctx0 + b, hh)),
                  pl.BlockSpec((n_lat, LANES), lambda b, hh, qi: (b, kr_blk)),
                  pl.BlockSpec((n_ctx, LANES), lambda b, hh, qi: (ctx0 + b, kr_blk)),
                  pl.BlockSpec((n_lat, hw), lambda b, hh, qi: (b, h // hps + hh)),
                  pl.BlockSpec((n_ctx, hw), lambda b, hh, qi: (ctx0 + b, h // hps + hh)),
                  tq_spec, tq_spec, tq_spec, tk_spec, tk_spec, tk_spec, vec, vec],
        out_specs=qspec,
        out_shape=jax.ShapeDtypeStruct((rows, h * LANES), BF16),
        scratch_shapes=[pltpu.VMEM((hps, n_lat + n_ctx, LANES), BF16),
                        pltpu.VMEM((hps, n_lat + n_ctx, LANES), BF16)],
        compiler_params=_cparams(("parallel", "parallel", "arbitrary")),
        name="mla_attention",
    )(q, kv, kv, lat, lat, kv, kv, *tabs, *tabs, gq, gk)


def _pad_cols(w, groups, width, first=0):
    k = w.shape[0]
    w3 = w.reshape(k, groups, width)
    out = jnp.zeros((k, groups, LANES), w.dtype).at[:, :, first:first + width].set(w3)
    return out.reshape(k, groups * LANES)


def _pad_rows(w, groups, width):
    d = w.shape[1]
    w3 = w.reshape(groups, width, d)
    return jnp.zeros((groups, LANES, d), w.dtype).at[:, :width, :].set(w3).reshape(groups * LANES, d)


def _mla_mixer(s, modv, g_mix, p, w_o, *, n_batch, n_lat, n_ctx, need_ctx):
    w_down, g_cq, g_ckv, w_uq, w_ukv, g_q, g_k = p
    d = s.shape[1]
    rows_all = n_batch * (n_lat + n_ctx)
    geo = dict(n_lat_rows=n_batch * n_lat, n_per_batch=n_lat)
    wd = jnp.concatenate([w_down[:, :MLA_Q_LORA + MLA_KV_LORA],
                          _pad_cols(w_down[:, MLA_Q_LORA + MLA_KV_LORA:], 1, MLA_ROPE, MLA_NOPE)], axis=1)
    lat = _nm_matmul(s, 0, d, g_mix, modv, rows_all, wd.astype(BF16), name="mla_down", **geo)
    wq = _pad_cols(w_uq, MLA_HEADS, MLA_QK).astype(BF16)
    q = _nm_matmul(lat, 0, MLA_Q_LORA, g_cq, None, rows_all, wq, name="mla_uq", out_dtype=BF16, **geo)
    wkv3 = w_ukv.reshape(MLA_KV_LORA, MLA_HEADS, MLA_NOPE + MLA_V)
    wkv = jnp.concatenate([_pad_cols(wkv3[:, :, :MLA_NOPE].reshape(MLA_KV_LORA, -1), MLA_HEADS, MLA_NOPE),
                           _pad_cols(wkv3[:, :, MLA_NOPE:].reshape(MLA_KV_LORA, -1), MLA_HEADS, MLA_V)],
                          axis=1).astype(BF16)
    kv = _nm_matmul(lat, MLA_Q_LORA // MLA_KV_LORA, MLA_KV_LORA, g_ckv, None, rows_all, wkv,
                    name="mla_ukv", out_dtype=BF16, **geo)
    o = _mla_attention(q, kv, lat, g_q, g_k, n_batch=n_batch, n_lat=n_lat, n_ctx=n_ctx, need_ctx=need_ctx)
    wo = _pad_rows(w_o, MLA_HEADS, MLA_V).astype(BF16)
    return o, wo


def _diff_attn_kernel(q_ref, kl_ref, kc_ref, vl_ref, vc_ref,
                      cq_ref, saq_ref, sbq_ref, ck_ref, sak_ref, sbk_ref, gq_ref, gk_ref,
                      lam_ref, gsub_ref, o_ref, k_scr, v_scr,
                      *, n_lat, n_ctx, nlat_q, scale, lam_init):
    qi = pl.program_id(2)
    nf = DIFF_HD // 4
    tabs = (ck_ref, sak_ref, sbk_ref)
    tile = lambda t: slice(t * LANES, (t + 1) * LANES)
    n_heads = o_ref.shape[1] // LANES

    @pl.when(qi == 0)
    def _():
        for hd in range(n_heads):
            ln = tile(hd)
            for sub in range(2):
                kt = tile(2 * hd + sub)
                _prep_keys(k_scr.at[2 * hd + sub], v_scr.at[hd] if sub == 0 else None, n_lat, n_ctx,
                           lambda r, kt=kt: kl_ref[r, kt], lambda r, kt=kt: kc_ref[r, kt],
                           lambda r, ln=ln: vl_ref[r, ln], lambda r, ln=ln: vc_ref[r, ln],
                           gk_ref[...], tabs, DIFF_HD, nf)

    lp = lam_ref[...]
    lam = (jnp.exp(jnp.sum(lp[0:1] * lp[1:2], axis=-1, keepdims=True))
           - jnp.exp(jnp.sum(lp[2:3] * lp[3:4], axis=-1, keepdims=True)) + lam_init)

    def attend(keys):
        def scores(t, rows):
            q = _head_norm(q_ref[rows, tile(t)].astype(F32), gq_ref[...], DIFF_HD)
            q = _rope(q, cq_ref[rows, :], saq_ref[rows, :], sbq_ref[rows, :], nf) * (scale * LOG2E)
            return _nt_dot(q.astype(BF16), k_scr[t, keys, :])

        def chain(hd, rows):
            s0 = scores(2 * hd, rows)
            s1 = scores(2 * hd + 1, rows)
            yield
            p0 = jnp.exp2(s0 - jnp.max(s0, axis=-1, keepdims=True))
            p1 = jnp.exp2(s1 - jnp.max(s1, axis=-1, keepdims=True))
            l0 = jnp.sum(p0, axis=-1, keepdims=True)
            l1 = jnp.sum(p1, axis=-1, keepdims=True)
            a = (p0 - p1 * (lam * l0 / l1)).astype(BF16)
            yield
            o = jnp.dot(a, v_scr[hd, keys, :], preferred_element_type=F32) * (1.0 / l0)
            o = _head_norm(o, gsub_ref[...], 2 * DIFF_HD) * (1.0 - lam_init)
            o_ref[rows, tile(hd)] = o.astype(o_ref.dtype)

        _emit_pipelined([chain(hd, rows) for rows in _row_chains(q_ref.shape[0])
                         for hd in range(n_heads)])

    @pl.when(qi < nlat_q)
    def _():
        attend(slice(0, n_lat + n_ctx))

    @pl.when(qi >= nlat_q)
    def _():
        attend(slice(n_lat, n_lat + n_ctx))


def _diff_attention(qkv, g_q, g_k, lam_p, g_sub, lam_init, *, n_batch, n_lat, n_ctx, need_ctx):
    nlat_q, nctx_q = n_lat // TQ, n_ctx // TQ
    nq = nlat_q + (nctx_q if need_ctx else 0)
    rows = n_batch * (n_lat + (n_ctx if need_ctx else 0))
    nf = DIFF_HD // 4
    tabs = _rope_tables(n_lat, TQ, ((0, True), (2 * nf, False)), nf)
    nh = DIFF_HEADS
    ctx0 = n_batch * n_lat // n_ctx
    gq = jnp.zeros((1, LANES), F32).at[0, :DIFF_HD].set(g_q)
    gk = jnp.zeros((1, LANES), F32).at[0, :DIFF_HD].set(g_k)
    lam8 = jnp.zeros((SUBLANES, LANES), F32).at[:4, :DIFF_HD].set(lam_p.astype(F32))
    rb = lambda b, qi: _row_block(b, qi, nlat_q, nctx_q, n_batch)
    tq_spec = pl.BlockSpec((TQ, LANES), lambda b, hh, qi: (jnp.minimum(qi, nlat_q), 0))
    tk_spec = pl.BlockSpec((n_lat, LANES), lambda b, hh, qi: (0, 0))
    vec = pl.BlockSpec((1, LANES), lambda b, hh, qi: (0, 0))
    hps = HEADS_PER_STEP
    qk_w, v_w = 2 * hps * LANES, hps * LANES
    k_blk0 = 2 * nh // (2 * hps)
    v_blk0 = 4 * nh // hps
    kern = functools.partial(_diff_attn_kernel, n_lat=n_lat, n_ctx=n_ctx, nlat_q=nlat_q,
                             scale=DIFF_HD ** -0.5, lam_init=lam_init)
    n_all = n_lat + n_ctx
    return pl.pallas_call(
        kern,
        grid=(n_batch, nh // hps, nq),
        in_specs=[pl.BlockSpec((TQ, qk_w), lambda b, hh, qi: (rb(b, qi), hh)),
                  pl.BlockSpec((n_lat, qk_w), lambda b, hh, qi: (b, k_blk0 + hh)),
                  pl.BlockSpec((n_ctx, qk_w), lambda b, hh, qi: (ctx0 + b, k_blk0 + hh)),
                  pl.BlockSpec((n_lat, v_w), lambda b, hh, qi: (b, v_blk0 + hh)),
                  pl.BlockSpec((n_ctx, v_w), lambda b, hh, qi: (ctx0 + b, v_blk0 + hh)),
                  tq_spec, tq_spec, tq_spec, tk_spec, tk_spec, tk_spec, vec, vec,
                  pl.BlockSpec((SUBLANES, LANES), lambda b, hh, qi: (0, 0)), vec],
        out_specs=pl.BlockSpec((TQ, v_w), lambda b, hh, qi: (rb(b, qi), hh)),
        out_shape=jax.ShapeDtypeStruct((rows, nh * LANES), BF16),
        scratch_shapes=[pltpu.VMEM((2 * hps, n_all, LANES), BF16), pltpu.VMEM((hps, n_all, LANES), BF16)],
        compiler_params=_cparams(("parallel", "parallel", "arbitrary")),
        name="diff_attention",
    )(qkv, qkv, qkv, qkv, qkv, *tabs, *tabs, gq, gk, lam8, g_sub.reshape(1, LANES))


def _diff_mixer(s, modv, g_mix, w_qkv, g_q, g_k, lam_p, g_sub, w_o, lam_init,
                *, n_batch, n_lat, n_ctx, need_ctx):
    d = s.shape[1]
    width = DIFF_HEADS * 2 * DIFF_HD
    rows_all = n_batch * (n_lat + n_ctx)
    w = jnp.concatenate([_pad_cols(w_qkv[:, :width], 2 * DIFF_HEADS, DIFF_HD),
                         _pad_cols(w_qkv[:, width:2 * width], 2 * DIFF_HEADS, DIFF_HD),
                         w_qkv[:, 2 * width:]], axis=1).astype(BF16)
    qkv = _nm_matmul(s, 0, d, g_mix, modv, rows_all, w, n_lat_rows=n_batch * n_lat,
                     n_per_batch=n_lat, name="diff_qkv", out_dtype=BF16)
    o = _diff_attention(qkv, g_q, g_k, lam_p, g_sub, lam_init,
                        n_batch=n_batch, n_lat=n_lat, n_ctx=n_ctx, need_ctx=need_ctx)
    return o, w_o.astype(BF16)


def _swa_attn_kernel(q_ref, kl_ref, kc_ref, vl_ref, vc_ref,
                     cq_ref, saq_ref, sbq_ref, ck_ref, sak_ref, sbk_ref, gq_ref, gk_ref, sink_ref,
                     o_ref, k_scr, v_scr, *, n_lat, n_ctx, nlat_q, scale):
    qi = pl.program_id(2)
    nf = SWA_HD // 4
    band = 3 * Q_BLOCK
    n_kv = k_scr.shape[0]
    tile = lambda t: slice(t * LANES, (t + 1) * LANES)

    @pl.when(qi == 0)
    def _():
        for kv in range(n_kv):
            ln = tile(kv)
            _prep_keys(k_scr.at[kv], v_scr.at[kv], n_lat, n_ctx,
                       lambda r, ln=ln: kl_ref[r, ln], lambda r, ln=ln: kc_ref[r, ln],
                       lambda r, ln=ln: _with_ones_lane(vl_ref[r, ln], SWA_HD),
                       lambda r, ln=ln: _with_ones_lane(vc_ref[r, ln], SWA_HD), gk_ref[...],
                       (ck_ref, sak_ref, sbk_ref), SWA_HD, nf)

    row_chains = _row_chains(q_ref.shape[0])
    reach = jnp.where(qi < nlat_q, WINDOW, -1)

    def chain(ci, rows, gi):
        kv = gi // SWA_GROUP
        qb = qi * len(row_chains) + ci
        k0 = pl.multiple_of(jnp.clip((qb - 1) * Q_BLOCK, 0, n_lat - band), Q_BLOCK)
        kpos = k0 + lax.broadcasted_iota(jnp.int32, (Q_BLOCK, band), 1)
        qpos = qb * Q_BLOCK + lax.broadcasted_iota(jnp.int32, (Q_BLOCK, band), 0)
        lanes = tile(gi)
        q = _head_norm(q_ref[rows, lanes].astype(F32), gq_ref[...], SWA_HD)
        q = _rope(q, cq_ref[rows, :], saq_ref[rows, :], sbq_ref[rows, :], nf) * (scale * LOG2E)
        q = q.astype(BF16)
        sb = _nt_dot(q, k_scr[kv, pl.ds(k0, band), :])
        sc = _nt_dot(q, k_scr[kv, n_lat:n_lat + n_ctx, :])
        yield
        sb = jnp.where(jnp.abs(kpos - qpos) <= reach, sb, NEG_INF)
        sink = sink_ref[gi][0:1, 0:1] * LOG2E
        m = jnp.maximum(jnp.maximum(jnp.max(sb, axis=-1, keepdims=True),
                                    jnp.max(sc, axis=-1, keepdims=True)), sink)
        pb = jnp.exp2(sb - m).astype(BF16)
        pc = jnp.exp2(sc - m).astype(BF16)
        p_sink = jnp.exp2(sink - m)
        yield
        o = (jnp.dot(pb, v_scr[kv, pl.ds(k0, band), :], preferred_element_type=F32)
             + jnp.dot(pc, v_scr[kv, n_lat:n_lat + n_ctx, :], preferred_element_type=F32))
        l = o[:, SWA_HD:SWA_HD + 1] + p_sink
        o_ref[rows, lanes] = (o * (1.0 / l)).astype(o_ref.dtype)

    _emit_pipelined([chain(ci, rows, gi) for ci, rows in enumerate(row_chains)
                     for gi in range(n_kv * SWA_GROUP)])


def _swa_attention(qkv, g_q, g_k, sink, *, n_batch, n_lat, n_ctx, need_ctx):
    nlat_q, nctx_q = n_lat // TQ, n_ctx // TQ
    nq = nlat_q + (nctx_q if need_ctx else 0)
    rows = n_batch * (n_lat + (n_ctx if need_ctx else 0))
    nf = SWA_HD // 4
    tabs = _rope_tables(n_lat, TQ, ((0, True), (2 * nf, False)), nf)
    nh, nkv, grp = SWA_HEADS, SWA_KV_HEADS, SWA_GROUP
    ctx0 = n_batch * n_lat // n_ctx
    gq = jnp.zeros((1, LANES), F32).at[0, :SWA_HD].set(g_q)
    gk = jnp.zeros((1, LANES), F32).at[0, :SWA_HD].set(g_k)
    sink_t = jnp.broadcast_to(sink.astype(F32).reshape(nh, 1, 1), (nh, SUBLANES, LANES))
    rb = lambda b, qi: _row_block(b, qi, nlat_q, nctx_q, n_batch)
    tq_spec = pl.BlockSpec((TQ, LANES), lambda b, kh, qi: (jnp.minimum(qi, nlat_q), 0))
    tk_spec = pl.BlockSpec((n_lat, LANES), lambda b, kh, qi: (0, 0))
    vec = pl.BlockSpec((1, LANES), lambda b, kh, qi: (0, 0))
    kern = functools.partial(_swa_attn_kernel, n_lat=n_lat, n_ctx=n_ctx, nlat_q=nlat_q,
                             scale=SWA_HD ** -0.5)
    n_all = n_lat + n_ctx
    kps = SWA_KV_PER_STEP
    q_w, kv_w = kps * grp * LANES, kps * LANES
    k_blk0, v_blk0 = nh // kps, (nh + nkv) // kps
    return pl.pallas_call(
        kern,
        grid=(n_batch, nkv // kps, nq),
        in_specs=[pl.BlockSpec((TQ, q_w), lambda b, kh, qi: (rb(b, qi), kh)),
                  pl.BlockSpec((n_lat, kv_w), lambda b, kh, qi: (b, k_blk0 + kh)),
                  pl.BlockSpec((n_ctx, kv_w), lambda b, kh, qi: (ctx0 + b, k_blk0 + kh)),
                  pl.BlockSpec((n_lat, kv_w), lambda b, kh, qi: (b, v_blk0 + kh)),
                  pl.BlockSpec((n_ctx, kv_w), lambda b, kh, qi: (ctx0 + b, v_blk0 + kh)),
                  tq_spec, tq_spec, tq_spec, tk_spec, tk_spec, tk_spec, vec, vec,
                  pl.BlockSpec((kps * grp, SUBLANES, LANES), lambda b, kh, qi: (kh, 0, 0))],
        out_specs=pl.BlockSpec((TQ, q_w), lambda b, kh, qi: (rb(b, qi), kh)),
        out_shape=jax.ShapeDtypeStruct((rows, nh * LANES), BF16),
        scratch_shapes=[pltpu.VMEM((kps, n_all, LANES), BF16), pltpu.VMEM((kps, n_all, LANES), BF16)],
        compiler_params=_cparams(("parallel", "parallel", "arbitrary")),
        name="swa_attention",
    )(qkv, qkv, qkv, qkv, qkv, *tabs, *tabs, gq, gk, sink_t)


def _swa_mixer(s, modv, g_mix, w_qkv, g_q, g_k, sink, w_o, *, n_batch, n_lat, n_ctx, need_ctx):
    d = s.shape[1]
    qw, kw = SWA_HEADS * SWA_HD, SWA_KV_HEADS * SWA_HD
    rows_all = n_batch * (n_lat + n_ctx)
    w = jnp.concatenate([_pad_cols(w_qkv[:, :qw], SWA_HEADS, SWA_HD),
                         _pad_cols(w_qkv[:, qw:qw + kw], SWA_KV_HEADS, SWA_HD),
                         _pad_cols(w_qkv[:, qw + kw:], SWA_KV_HEADS, SWA_HD)], axis=1).astype(BF16)
    qkv = _nm_matmul(s, 0, d, g_mix, modv, rows_all, w, n_lat_rows=n_batch * n_lat,
                     n_per_batch=n_lat, name="swa_qkv", out_dtype=BF16)
    o = _swa_attention(qkv, g_q, g_k, sink, n_batch=n_batch, n_lat=n_lat, n_ctx=n_ctx, need_ctx=need_ctx)
    return o, _pad_rows(w_o, SWA_HEADS, SWA_HD).astype(BF16)


def _merge_exchange_pairs(n):
    pairs = []
    t = max(1, (n - 1).bit_length())
    p = 1 << (t - 1)
    while p > 0:
        q, r, d = 1 << (t - 1), 0, p
        while d > 0:
            pairs += [(i, i + d) for i in range(n - d) if (i & p) == r]
            d, q, r = q - p, q >> 1, p
        p >>= 1
    return pairs


def _top_values(s, count):
    n = s.shape[0] // SUBLANES
    v = [s[j * SUBLANES:(j + 1) * SUBLANES, :] for j in range(n)]
    for i, j in _merge_exchange_pairs(n):
        v[i], v[j] = jnp.maximum(v[i], v[j]), jnp.minimum(v[i], v[j])
    vals = []
    for r in range(count):
        m = jnp.max(v[0], axis=0, keepdims=True)
        vals.append(m)
        depth = min(n, count - 1 - r)
        head = v[0] == m
        for j in range(depth):
            nxt = v[j + 1] if j + 1 < n else jnp.full_like(v[j], NEG_BIG)
            v[j] = jnp.where(head, nxt, v[j])
    return vals


def _peer_score_kernel(s_ref, g_ref, mod_ref, wq_ref, keys_ref,
                       ht_ref, c_ref, e1_ref, s2_ref, e2_ref, qt_scr):
    k = PEER_TOPK
    h = _norm_mod(s_ref[...], g_ref[...], mod_ref[0], 3, 4)
    ht = h.T.astype(BF16)
    ht_ref[0] = ht

    tm = ht.shape[1]
    group = 2
    q_rows = group * 2 * LANES

    def group_chain(gidx):
        heads = [gidx * group + i for i in range(group)]
        rows = slice(gidx * q_rows, (gidx + 1) * q_rows)
        qt_scr[rows, :] = jnp.dot(wq_ref[rows, :], ht, preferred_element_type=F32)
        yield
        scores = []
        for hh in heads:
            for half in range(2):
                r = slice((hh * 2 + half) * LANES, (hh * 2 + half + 1) * LANES)
                scores.append(jnp.dot(keys_ref[hh, half], qt_scr[r, :].astype(BF16),
                                      preferred_element_type=F32))
        tops = _top_values(jnp.concatenate(scores, axis=1), k + 1)
        cands = []
        for i in range(group):
            v1 = [t[:, (2 * i) * tm:(2 * i + 1) * tm] for t in tops]
            v2 = [t[:, (2 * i + 1) * tm:(2 * i + 2) * tm] for t in tops]
            top8 = jnp.concatenate(v2[:8], axis=0)
            pieces = [v1[0] + jnp.concatenate(v2[:k], axis=0)]
            pieces += [v1[j] + top8 for j in range(1, 8)]
            pieces += [jnp.concatenate(v1[8:k], axis=0) + v2[0]]
            pieces += [jnp.concatenate([v1[0] + v2[k], v1[k] + v2[0]]
                                       + [jnp.full_like(v1[0], NEG_BIG)] * 6, axis=0)]
            cands.append(jnp.concatenate(pieces, axis=0))
        best = _top_values(jnp.concatenate(cands, axis=1), k + 1)
        grp = (PEER_KEYS // SUBLANES, SUBLANES, tm)
        for i, hh in enumerate(heads):
            ms = [t[:, i * tm:(i + 1) * tm] for t in best]
            s1, s2 = scores[2 * i], scores[2 * i + 1]
            m1, m2 = tops[0][:, (2 * i) * tm:(2 * i + 1) * tm], tops[0][:, (2 * i + 1) * tm:(2 * i + 2) * tm]
            thr = 0.5 * (ms[k - 1] + ms[k])
            z = jnp.ones_like(ms[0])
            for r in range(1, k):
                z = z + jnp.exp(ms[r] - ms[0])
            c_ref[hh] = (thr - s1).reshape(grp)
            e1_ref[hh] = (jnp.exp(s1 - m1) * (0.5 / z)).reshape(grp)
            s2_ref[0, hh] = s2
            e2_ref[0, hh] = jnp.exp(s2 - m2)

    _emit_pipelined([group_chain(gidx) for gidx in range(PEER_HEADS // group)])


def _peer_scores(s, modv, g_ffn, wq_t, keys, rows, *, n_lat_rows, n_per_batch):
    d = s.shape[1]
    n_batch = modv.shape[0] - 1
    tm = SCORE_TM
    nlat_blocks = n_lat_rows // tm
    bpb = n_per_batch // tm
    nq = wq_t.shape[0]
    hk = (rows // tm, PEER_HEADS, PEER_KEYS, tm)
    hk_spec = pl.BlockSpec((1, PEER_HEADS, PEER_KEYS, tm), lambda i: (i, 0, 0, 0))
    hg = (PEER_HEADS, PEER_KEYS // SUBLANES, SUBLANES, rows)
    hg_spec = pl.BlockSpec((PEER_HEADS, PEER_KEYS // SUBLANES, SUBLANES, tm), lambda i: (0, 0, 0, i))
    return pl.pallas_call(
        _peer_score_kernel,
        grid=(rows // tm,),
        in_specs=[pl.BlockSpec((tm, d), lambda i: (i, 0)),
                  pl.BlockSpec((1, d), lambda i: (0, 0)),
                  pl.BlockSpec((1, SUBLANES, d), lambda i: (_mod_index(i, nlat_blocks, bpb, n_batch), 0, 0)),
                  pl.BlockSpec((nq, d), lambda i: (0, 0)),
                  pl.BlockSpec((PEER_HEADS, 2, PEER_KEYS, LANES), lambda i: (0, 0, 0, 0))],
        out_specs=[pl.BlockSpec((1, d, tm), lambda i: (i, 0, 0)), hg_spec, hg_spec, hk_spec, hk_spec],
        out_shape=[jax.ShapeDtypeStruct((rows // tm, d, tm), BF16)] + [jax.ShapeDtypeStruct(hg, F32)] * 2
                  + [jax.ShapeDtypeStruct(hk, F32)] * 2,
        scratch_shapes=[pltpu.VMEM((nq, tm), F32)],
        compiler_params=_cparams(("parallel",)),
        name="peer_scores",
    )(s, g_ffn.reshape(1, d), modv, wq_t, keys)


def _peer_expert_kernel(ht_ref, u_ref, vt_ref, c_ref, e1_ref, s2_ref, e2_ref, s_ref, mod_ref,
                        o_ref, acc_scr, st0_scr, st1_scr, at0_scr, at1_scr, *, n_exp_blocks):
    g = pl.program_id(0)
    st_scr, at_scr = (st0_scr, st1_scr), (at0_scr, at1_scr)
    te, tm = st0_scr.shape
    jc = jnp.maximum(g - 2, 0) % n_exp_blocks

    @pl.when(g == 0)
    def _():
        for ref in st_scr + at_scr:
            ref[...] = jnp.zeros_like(ref)

    @pl.when(jc == 0)
    def _():
        acc_scr[...] = jnp.zeros_like(acc_scr)

    def stages(slot_a, slot_b):
        n_chunks = te // MXU_CHUNK_ROWS

        def value_chunk(q):
            ms = slice(q * MXU_CHUNK_ROWS, (q + 1) * MXU_CHUNK_ROWS)
            acc_scr[ms, :] += jnp.dot(vt_ref[0, 0, ms, :], at_scr[slot_a][...], preferred_element_type=F32)

        def score_chunk(q):
            ms = slice(q * MXU_CHUNK_ROWS, (q + 1) * MXU_CHUNK_ROWS)
            st_scr[slot_a][ms, :] = jnp.dot(u_ref[0, ms, :], ht_ref[0], preferred_element_type=F32)

        rg_rows = 4 * SUBLANES
        al_group = 4
        col_tiles = 1
        tiles = [(rg, c0, al0)
                 for rg in range(LANES // rg_rows)
                 for c0 in range(0, tm // LANES, col_tiles)
                 for al0 in range(0, te // LANES, al_group)]
        assert acc_scr.shape[0] == te
        per_chunk = len(tiles) // (2 * n_chunks)
        for t, (rg, c0, al0) in enumerate(tiles):
            if t % per_chunk == 0:
                q = t // per_chunk
                (value_chunk if q % 2 == 0 else score_chunk)(q // 2)
            brows = slice(rg * rg_rows, (rg + 1) * rg_rows)
            lanes = [slice((c0 + ct) * LANES, (c0 + ct + 1) * LANES) for ct in range(col_tiles)]
            gate = [[jnp.zeros((rg_rows, LANES), F32) for _ in lanes] for _ in range(al_group)]
            for hh in range(PEER_HEADS):
                for ct, ln in enumerate(lanes):
                    s2t = s2_ref[0, hh, brows, ln]
                    e2t = e2_ref[0, hh, brows, ln]
                    for k in range(al_group):
                        al = al0 + k
                        c = c_ref[hh, 0, al:al + 1, ln]
                        e1 = e1_ref[hh, 0, al:al + 1, ln]
                        gate[k][ct] = gate[k][ct] + jnp.where(s2t >= c, e2t, 0.0) * e1
            for k in range(al_group):
                erows = slice((al0 + k) * LANES + rg * rg_rows, (al0 + k) * LANES + (rg + 1) * rg_rows)
                for ct, ln in enumerate(lanes):
                    sv = st_scr[slot_b][erows, ln]
                    act = sv * (1.0 + lax.erf(sv * (2.0 ** -0.5)))
                    at_scr[slot_b][erows, ln] = (gate[k][ct] * act).astype(BF16)

    for parity in range(2):
        @pl.when(g % 2 == parity)
        def _():
            stages(parity, 1 - parity)

    @pl.when((jc == n_exp_blocks - 1) & (g >= 2))
    def _():
        o_ref[...] = s_ref[...] + mod_ref[0][5:6] * acc_scr[...].T


def _peer_experts(ht, u, vt, layer, c, e1, s2, e2, s, modv, rows, *, n_lat_rows, n_per_batch):
    d = s.shape[1]
    n_exp = u.shape[1]
    n_batch = modv.shape[0] - 1
    tm, te = PEER_TM, PEER_TE
    nlat_blocks = n_lat_rows // tm
    bpb = n_per_batch // tm
    nj = n_exp // te
    n_pairs = (rows // tm) * nj

    def pair(g, lag):
        p = jnp.clip(g - lag, 0, n_pairs - 1)
        return p // nj, p % nj

    assert SCORE_TM == tm
    hk_spec = pl.BlockSpec((1, PEER_HEADS, PEER_KEYS, tm), lambda g: (pair(g, 1)[0], 0, 0, 0))
    hg_spec = pl.BlockSpec((PEER_HEADS, 1, SUBLANES, tm), lambda g: (0, pair(g, 1)[1], 0, pair(g, 1)[0]))
    kern = functools.partial(_peer_expert_kernel, n_exp_blocks=nj)
    return pl.pallas_call(
        kern,
        grid=(n_pairs + 2,),
        in_specs=[pl.BlockSpec((1, d, tm), lambda g: (pair(g, 0)[0], 0, 0)),
                  pl.BlockSpec((1, te, d), lambda g: (layer, pair(g, 0)[1], 0)),
                  pl.BlockSpec((1, 1, d, te), lambda g: (layer, pair(g, 2)[1], 0, 0)),
                  hg_spec, hg_spec, hk_spec, hk_spec,
                  pl.BlockSpec((tm, d), lambda g: (pair(g, 2)[0], 0)),
                  pl.BlockSpec((1, SUBLANES, d),
                               lambda g: (_mod_index(pair(g, 2)[0], nlat_blocks, bpb, n_batch), 0, 0))],
        out_specs=pl.BlockSpec((tm, d), lambda g: (pair(g, 2)[0], 0)),
        out_shape=jax.ShapeDtypeStruct((rows, d), F32),
        scratch_shapes=[pltpu.VMEM((d, tm), F32), pltpu.VMEM((te, tm), F32), pltpu.VMEM((te, tm), F32),
                        pltpu.VMEM((te, tm), BF16), pltpu.VMEM((te, tm), BF16)],
        compiler_params=_cparams(("arbitrary",)),
        name="peer_experts",
    )(ht, u, vt, c, e1, s2, e2, s, modv)


def _peer(s, modv, g_ffn, w_q, keys, u_all, vt_all, layer, rows, *, n_lat_rows, n_per_batch):
    geo = dict(n_lat_rows=n_lat_rows, n_per_batch=n_per_batch)
    ht, c, e1, s2, e2 = _peer_scores(s, modv, g_ffn, w_q.T.astype(BF16), keys.astype(BF16), rows, **geo)
    return _peer_experts(ht, u_all, vt_all, layer, c, e1, s2, e2, s, modv, rows, **geo)


def kernel(x, c, ctx, c_ctx, ada_w, ada_b, norm_mix, norm_ffn, mla_w_down, mla_g_cq, mla_g_ckv, mla_w_uq, mla_w_ukv, mla_g_q, mla_g_k, mla_w_o, diff_w_qkv, diff_g_q, diff_g_k, diff_lambda, diff_g_sub, diff_w_o, swa_w_qkv, swa_g_q, swa_g_k, swa_sink, swa_w_o, peer_w_q, peer_keys, peer_u, peer_v):
    n_batch, n_lat, d = x.shape
    n_ctx = ctx.shape[1]
    depth = ada_w.shape[0]
    rows_lat = n_batch * n_lat
    rows_all = rows_lat + n_batch * n_ctx
    assert n_lat % max(TM, PEER_TM, SCORE_TM, TQ) == 0 and (n_batch * n_ctx) % max(TM, PEER_TM, SCORE_TM) == 0
    assert n_ctx % TQ == 0 and n_ctx % KEY_PREP_ROWS == 0 and n_lat % KEY_PREP_ROWS == 0
    assert n_lat % GRID_W == 0 and rows_lat % n_ctx == 0 and n_lat >= 3 * Q_BLOCK

    s = jnp.concatenate([x.reshape(rows_lat, d), ctx.reshape(n_batch * n_ctx, d)], axis=0)
    cond_rows = -(-(n_batch + 1) // SUBLANES) * SUBLANES
    cc = jnp.zeros((cond_rows, d), F32).at[:n_batch].set(c).at[n_batch].set(c_ctx)
    mods = _ada_mods(cc, ada_w, ada_b)
    geo = dict(n_lat_rows=rows_lat, n_per_batch=n_lat)
    n_exp = peer_u.shape[1]
    u_all = peer_u.astype(BF16)
    vt_all = peer_v.reshape(depth, n_exp // PEER_TE, PEER_TE, d).transpose(0, 1, 3, 2).astype(BF16)

    for i in range(depth):
        last = i == depth - 1
        kind, j = i % 3, i // 3
        modv = jnp.zeros((n_batch + 1, SUBLANES, d), F32).at[:, :6, :].set(
            mods[i, :n_batch + 1].reshape(n_batch + 1, 6, d))
        dims = dict(n_batch=n_batch, n_lat=n_lat, n_ctx=n_ctx, need_ctx=not last)
        if kind == 0:
            p = (mla_w_down[j], mla_g_cq[j], mla_g_ckv[j], mla_w_uq[j], mla_w_ukv[j], mla_g_q[j], mla_g_k[j])
            o, wo = _mla_mixer(s, modv, norm_mix[i], p, mla_w_o[j], **dims)
        elif kind == 1:
            lam_init = 0.8 - 0.6 * math.exp(-0.3 * i)
            o, wo = _diff_mixer(s, modv, norm_mix[i], diff_w_qkv[j], diff_g_q[j], diff_g_k[j],
                                diff_lambda[j], diff_g_sub[j], diff_w_o[j], lam_init, **dims)
        else:
            o, wo = _swa_mixer(s, modv, norm_mix[i], swa_w_qkv[j], swa_g_q[j], swa_g_k[j],
                               swa_sink[j], swa_w_o[j], **dims)
        rows = rows_lat if last else rows_all
        s = _proj_resid(o, wo, s, modv, rows, gate_row=2, **geo)
        s = _peer(s, modv, norm_ffn[i], peer_w_q[i], peer_keys[i], u_all, vt_all, i, rows, **geo)
    return s.reshape(n_batch, n_lat, d)
```

```python
import functools
import math

import jax
import jax.numpy as jnp
from jax import lax
from jax.experimental import pallas as pl
from jax.experimental.pallas import tpu as pltpu

F32 = jnp.float32
BF16 = jnp.bfloat16

LANES = 128
EPS = 1e-6
NEG_INF = -1e30
NEG_BIG = -3.0e38
ROPE_THETA = 10000.0
GRID_W = 64
Q_BLOCK = 128
WINDOW = 128

MLA_HEADS, MLA_NOPE, MLA_ROPE, MLA_V = 16, 64, 32, 64
MLA_QK = MLA_NOPE + MLA_ROPE
MLA_Q_LORA, MLA_KV_LORA = 768, 256
DIFF_HEADS, DIFF_HD = 8, 64
SWA_HEADS, SWA_KV_HEADS, SWA_HD = 16, 4, 64
SWA_GROUP = SWA_HEADS // SWA_KV_HEADS
PEER_HEADS, PEER_KEYS, PEER_TOPK = 8, 128, 16

VMEM_LIMIT = 56 * 1024 * 1024

TM = 1024
NM_TM = 2048
NM_MAX_TN = 2048
ADA_TN = 1536
KEY_PREP_ROWS = 256
SCORE_TM = 512
TQ = 256
PEER_TM = 512
SUBLANES = 8
PEER_TE = SUBLANES * LANES
MXU_CHUNK_ROWS = 128


def _cparams(sem):
    return pltpu.CompilerParams(dimension_semantics=sem, vmem_limit_bytes=VMEM_LIMIT)


def _ada_kernel(c_ref, w_ref, b_ref, o_ref):
    c = c_ref[...]
    sc = c * (1.0 / (1.0 + jnp.exp(-c)))
    o_ref[0] = jnp.dot(sc, w_ref[0], preferred_element_type=F32,
                       precision=lax.Precision.HIGHEST) + b_ref[0]


def _ada_mods(cc, ada_w, ada_b):
    depth, d, d6 = ada_w.shape
    rows = cc.shape[0]
    tn = ADA_TN
    return pl.pallas_call(
        _ada_kernel,
        grid=(depth, d6 // tn),
        in_specs=[pl.BlockSpec((rows, d), lambda i, n: (0, 0)),
                  pl.BlockSpec((1, d, tn), lambda i, n: (i, 0, n)),
                  pl.BlockSpec((1, 1, tn), lambda i, n: (i, 0, n))],
        out_specs=pl.BlockSpec((1, rows, tn), lambda i, n: (i, 0, n)),
        out_shape=jax.ShapeDtypeStruct((depth, rows, d6), F32),
        compiler_params=_cparams(("arbitrary", "arbitrary")),
        name="ada_mods",
    )(cc, ada_w, ada_b.reshape(depth, 1, d6))


def _norm_mod(x, g, mod, shift_row, scale_row):
    ms = jnp.mean(x * x, axis=-1, keepdims=True)
    y = x * lax.rsqrt(ms + EPS) * g
    if mod is not None:
        y = y * (1.0 + mod[scale_row:scale_row + 1]) + mod[shift_row:shift_row + 1]
    return y


def _nm_kernel(*refs, use_mod, shift_row, scale_row):
    if use_mod:
        x_ref, g_ref, mod_ref, w_ref, o_ref, h_scr = refs
    else:
        (x_ref, g_ref, w_ref, o_ref, h_scr), mod_ref = refs, None

    @pl.when(pl.program_id(1) == 0)
    def _():
        mod = mod_ref[0] if use_mod else None
        h_scr[...] = _norm_mod(x_ref[...], g_ref[...], mod, shift_row, scale_row).astype(BF16)

    o_ref[...] = jnp.dot(h_scr[...], w_ref[...], preferred_element_type=F32).astype(o_ref.dtype)


def _mod_index(i, nlat_blocks, blocks_per_batch, n_batch):
    return jnp.where(i < nlat_blocks, i // blocks_per_batch, n_batch)


def _nm_matmul(x, kblk, k, g, modv, rows, w, *, n_lat_rows, n_per_batch, out_dtype=F32,
               shift_row=0, scale_row=1, name="nm_matmul"):
    nout = w.shape[1]
    tn = next(t for t in range(min(nout, NM_MAX_TN), 0, -LANES) if nout % t == 0)
    use_mod = modv is not None
    tm = NM_TM
    in_specs = [pl.BlockSpec((tm, k), lambda i, n: (i, kblk)),
                pl.BlockSpec((1, k), lambda i, n: (0, 0))]
    operands = [x, g.reshape(1, k)]
    if use_mod:
        n_batch = modv.shape[0] - 1
        nlat_blocks = n_lat_rows // tm
        bpb = n_per_batch // tm
        in_specs.append(pl.BlockSpec(
            (1, SUBLANES, k), lambda i, n: (_mod_index(i, nlat_blocks, bpb, n_batch), 0, 0)))
        operands.append(modv)
    in_specs.append(pl.BlockSpec((k, tn), lambda i, n: (0, n)))
    operands.append(w)
    kern = functools.partial(_nm_kernel, use_mod=use_mod, shift_row=shift_row, scale_row=scale_row)
    return pl.pallas_call(
        kern,
        grid=(rows // tm, nout // tn),
        in_specs=in_specs,
        out_specs=pl.BlockSpec((tm, tn), lambda i, n: (i, n)),
        out_shape=jax.ShapeDtypeStruct((rows, nout), out_dtype),
        scratch_shapes=[pltpu.VMEM((tm, k), BF16)],
        compiler_params=_cparams(("parallel", "arbitrary")),
        name=name,
    )(*operands)


def _proj_resid_kernel(a_ref, w_ref, s_ref, mod_ref, o_ref, *, gate_row):
    y = jnp.dot(a_ref[...].astype(BF16), w_ref[...], preferred_element_type=F32)
    o_ref[...] = s_ref[...] + mod_ref[0][gate_row:gate_row + 1] * y


def _proj_resid(a, w, s, modv, rows, *, n_lat_rows, n_per_batch, gate_row, name="proj_resid"):
    ka = a.shape[1]
    d = w.shape[1]
    n_batch = modv.shape[0] - 1
    nlat_blocks = n_lat_rows // TM
    bpb = n_per_batch // TM
    return pl.pallas_call(
        functools.partial(_proj_resid_kernel, gate_row=gate_row),
        grid=(rows // TM,),
        in_specs=[pl.BlockSpec((TM, ka), lambda i: (i, 0)),
                  pl.BlockSpec((ka, d), lambda i: (0, 0)),
                  pl.BlockSpec((TM, d), lambda i: (i, 0)),
                  pl.BlockSpec((1, SUBLANES, d), lambda i: (_mod_index(i, nlat_blocks, bpb, n_batch), 0, 0))],
        out_specs=pl.BlockSpec((TM, d), lambda i: (i, 0)),
        out_shape=jax.ShapeDtypeStruct((rows, d), F32),
        compiler_params=_cparams(("parallel",)),
        name=name,
    )(a, w, s, modv)


def _rope_tables(n_lat, extra_rows, segments, nf):
    pos = jnp.arange(n_lat, dtype=jnp.int32)
    row = (pos // GRID_W).astype(F32)
    col = (pos % GRID_W).astype(F32)
    inv = ROPE_THETA ** (-jnp.arange(nf, dtype=F32) / nf)
    cos = jnp.ones((n_lat, LANES), F32)
    sa = jnp.zeros((n_lat, LANES), F32)
    sb = jnp.zeros((n_lat, LANES), F32)
    for first, use_row in segments:
        ang = (row if use_row else col)[:, None] * inv[None, :]
        c, s = jnp.cos(ang), jnp.sin(ang)
        cos = cos.at[:, first:first + nf].set(c).at[:, first + nf:first + 2 * nf].set(c)
        sa = sa.at[:, first:first + nf].set(-s)
        sb = sb.at[:, first + nf:first + 2 * nf].set(s)
    pad = lambda t, v: jnp.concatenate([t, jnp.full((extra_rows, LANES), v, F32)], axis=0)
    return pad(cos, 1.0), pad(sa, 0.0), pad(sb, 0.0)


def _head_norm(x, g, d):
    ssq = jnp.sum(x * x, axis=-1, keepdims=True) * (1.0 / d)
    return x * lax.rsqrt(ssq + EPS) * g


def _rope(y, cos, sa, sb, nf):
    return y * cos + pltpu.roll(y, LANES - nf, 1) * sa + pltpu.roll(y, nf, 1) * sb


def _with_ones_lane(v, lane):
    return jnp.where(lax.broadcasted_iota(jnp.int32, v.shape, 1) == lane, jnp.ones_like(v), v)


def _row_block(b, qi, nlat_q, nctx_q, n_batch):
    return jnp.where(qi < nlat_q, b * nlat_q + qi, n_batch * nlat_q + b * nctx_q + (qi - nlat_q))


def _prep_keys(k_scr, v_scr, n_lat, n_ctx, load_k_lat, load_k_ctx, load_v_lat, load_v_ctx,
               gk, tabs, d, nf):
    cos_ref, sa_ref, sb_ref = tabs
    ch = KEY_PREP_ROWS

    def lat_body(c, carry):
        r = pl.ds(pl.multiple_of(c * ch, ch), ch)
        kk = _head_norm(load_k_lat(r).astype(F32), gk, d)
        kk = _rope(kk, cos_ref[r, :], sa_ref[r, :], sb_ref[r, :], nf)
        k_scr[r, :] = kk.astype(BF16)
        if v_scr is not None:
            v_scr[r, :] = load_v_lat(r).astype(BF16)
        return carry

    lax.fori_loop(0, n_lat // ch, lat_body, 0, unroll=2)

    def ctx_body(c, carry):
        r = pl.ds(pl.multiple_of(c * ch, ch), ch)
        ro = pl.ds(pl.multiple_of(n_lat + c * ch, ch), ch)
        k_scr[ro, :] = _head_norm(load_k_ctx(r).astype(F32), gk, d).astype(BF16)
        if v_scr is not None:
            v_scr[ro, :] = load_v_ctx(r).astype(BF16)
        return carry

    lax.fori_loop(0, n_ctx // ch, ctx_body, 0)


def _nt_dot(a, b):
    return lax.dot_general(a, b, (((1,), (1,)), ((), ())), preferred_element_type=F32)


LOG2E = math.log2(math.e)
CHAIN_ROWS = 128
MLA_HEADS_PER_STEP = 8
HEADS_PER_STEP = 4
SWA_KV_PER_STEP = 1


def _row_chains(n_rows):
    return [slice(r, r + CHAIN_ROWS) for r in range(0, n_rows, CHAIN_ROWS)]


def _emit_pipelined(chains):
    pending, active = list(chains), []
    while pending or active:
        if pending:
            active.append(pending.pop(0))
        for ch in list(active):
            try:
                next(ch)
            except StopIteration:
                active.remove(ch)


def _mla_attn_kernel(q_ref, kl_ref, kc_ref, krl_ref, krc_ref, vl_ref, vc_ref,
                     cq_ref, saq_ref, sbq_ref, ck_ref, sak_ref, sbk_ref, gq_ref, gk_ref,
                     o_ref, k_scr, v_scr, *, n_lat, n_ctx, nlat_q, scale):
    qi = pl.program_id(2)
    nf = MLA_ROPE // 4
    heads = [slice(hd * LANES, (hd + 1) * LANES) for hd in range(q_ref.shape[1] // LANES)]

    @pl.when(qi == 0)
    def _():
        for hd, ln in enumerate(heads):
            _prep_keys(k_scr.at[hd], v_scr.at[hd], n_lat, n_ctx,
                       lambda r, ln=ln: kl_ref[r, ln].astype(F32) + krl_ref[r, :],
                       lambda r, ln=ln: kc_ref[r, ln].astype(F32) + krc_ref[r, :],
                       lambda r, ln=ln: _with_ones_lane(vl_ref[r, ln], MLA_V),
                       lambda r, ln=ln: _with_ones_lane(vc_ref[r, ln], MLA_V),
                       gk_ref[...], (ck_ref, sak_ref, sbk_ref), MLA_QK, nf)

    def attend(keys):
        def chain(hd, ln, rows):
            q = _head_norm(q_ref[rows, ln].astype(F32), gq_ref[...], MLA_QK)
            q = _rope(q, cq_ref[rows, :], saq_ref[rows, :], sbq_ref[rows, :], nf) * (scale * LOG2E)
            s = _nt_dot(q.astype(BF16), k_scr[hd, keys, :])
            yield
            p = jnp.exp2(s - jnp.max(s, axis=-1, keepdims=True)).astype(BF16)
            yield
            o = jnp.dot(p, v_scr[hd, keys, :], preferred_element_type=F32)
            o_ref[rows, ln] = (o * (1.0 / o[:, MLA_V:MLA_V + 1])).astype(o_ref.dtype)

        _emit_pipelined([chain(hd, ln, rows) for rows in _row_chains(q_ref.shape[0])
                         for hd, ln in enumerate(heads)])

    @pl.when(qi < nlat_q)
    def _():
        attend(slice(0, n_lat + n_ctx))

    @pl.when(qi >= nlat_q)
    def _():
        attend(slice(n_lat, n_lat + n_ctx))


def _mla_attention(q, kv, lat, g_q, g_k, *, n_batch, n_lat, n_ctx, need_ctx):
    nlat_q, nctx_q = n_lat // TQ, n_ctx // TQ
    nq = nlat_q + (nctx_q if need_ctx else 0)
    rows = n_batch * (n_lat + (n_ctx if need_ctx else 0))
    nf = MLA_ROPE // 4
    tabs = _rope_tables(n_lat, TQ, ((MLA_NOPE, True), (MLA_NOPE + 2 * nf, False)), nf)
    h = MLA_HEADS
    ctx0 = n_batch * n_lat // n_ctx
    kr_blk = (MLA_Q_LORA + MLA_KV_LORA) // LANES
    gq = jnp.zeros((1, LANES), F32).at[0, :MLA_QK].set(g_q)
    gk = jnp.zeros((1, LANES), F32).at[0, :MLA_QK].set(g_k)
    hps = MLA_HEADS_PER_STEP
    hw = hps * LANES
    qspec = pl.BlockSpec((TQ, hw), lambda b, hh, qi: (_row_block(b, qi, nlat_q, nctx_q, n_batch), hh))
    tq_spec = pl.BlockSpec((TQ, LANES), lambda b, hh, qi: (jnp.minimum(qi, nlat_q), 0))
    tk_spec = pl.BlockSpec((n_lat, LANES), lambda b, hh, qi: (0, 0))
    vec = pl.BlockSpec((1, LANES), lambda b, hh, qi: (0, 0))
    kern = functools.partial(_mla_attn_kernel, n_lat=n_lat, n_ctx=n_ctx, nlat_q=nlat_q,
                             scale=MLA_QK ** -0.5)
    return pl.pallas_call(
        kern,
        grid=(n_batch, h // hps, nq),
        in_specs=[qspec,
                  pl.BlockSpec((n_lat, hw), lambda b, hh, qi: (b, hh)),
                  pl.BlockSpec((n_ctx, hw), lambda b, hh, qi: (ctx0 + b, hh)),
                  pl.BlockSpec((n_lat, LANES), lambda b, hh, qi: (b, kr_blk)),
                  pl.BlockSpec((n_ctx, LANES), lambda b, hh, qi: (ctx0 + b, kr_blk)),
                  pl.BlockSpec((n_lat, hw), lambda b, hh, qi: (b, h // hps + hh)),
                  pl.BlockSpec((n_ctx, hw), lambda b, hh, qi: (ctx0 + b, h // hps + hh)),
                  tq_spec, tq_spec, tq_spec, tk_spec, tk_spec, tk_spec, vec, vec],
        out_specs=qspec,
        out_shape=jax.ShapeDtypeStruct((rows, h * LANES), BF16),
        scratch_shapes=[pltpu.VMEM((hps, n_lat + n_ctx, LANES), BF16),
                        pltpu.VMEM((hps, n_lat + n_ctx, LANES), BF16)],
        compiler_params=_cparams(("parallel", "parallel", "arbitrary")),
        name="mla_attention",
    )(q, kv, kv, lat, lat, kv, kv, *tabs, *tabs, gq, gk)


def _pad_cols(w, groups, width, first=0):
    k = w.shape[0]
    w3 = w.reshape(k, groups, width)
    out = jnp.zeros((k, groups, LANES), w.dtype).at[:, :, first:first + width].set(w3)
    return out.reshape(k, groups * LANES)


def _pad_rows(w, groups, width):
    d = w.shape[1]
    w3 = w.reshape(groups, width, d)
    return jnp.zeros((groups, LANES, d), w.dtype).at[:, :width, :].set(w3).reshape(groups * LANES, d)


def _mla_mixer(s, modv, g_mix, p, w_o, *, n_batch, n_lat, n_ctx, need_ctx):
    w_down, g_cq, g_ckv, w_uq, w_ukv, g_q, g_k = p
    d = s.shape[1]
    rows_all = n_batch * (n_lat + n_ctx)
    geo = dict(n_lat_rows=n_batch * n_lat, n_per_batch=n_lat)
    wd = jnp.concatenate([w_down[:, :MLA_Q_LORA + MLA_KV_LORA],
                          _pad_cols(w_down[:, MLA_Q_LORA + MLA_KV_LORA:], 1, MLA_ROPE, MLA_NOPE)], axis=1)
    lat = _nm_matmul(s, 0, d, g_mix, modv, rows_all, wd.astype(BF16), name="mla_down", **geo)
    wq = _pad_cols(w_uq, MLA_HEADS, MLA_QK).astype(BF16)
    q = _nm_matmul(lat, 0, MLA_Q_LORA, g_cq, None, rows_all, wq, name="mla_uq", out_dtype=BF16, **geo)
    wkv3 = w_ukv.reshape(MLA_KV_LORA, MLA_HEADS, MLA_NOPE + MLA_V)
    wkv = jnp.concatenate([_pad_cols(wkv3[:, :, :MLA_NOPE].reshape(MLA_KV_LORA, -1), MLA_HEADS, MLA_NOPE),
                           _pad_cols(wkv3[:, :, MLA_NOPE:].reshape(MLA_KV_LORA, -1), MLA_HEADS, MLA_V)],
                          axis=1).astype(BF16)
    kv = _nm_matmul(lat, MLA_Q_LORA // MLA_KV_LORA, MLA_KV_LORA, g_ckv, None, rows_all, wkv,
                    name="mla_ukv", out_dtype=BF16, **geo)
    o = _mla_attention(q, kv, lat, g_q, g_k, n_batch=n_batch, n_lat=n_lat, n_ctx=n_ctx, need_ctx=need_ctx)
    wo = _pad_rows(w_o, MLA_HEADS, MLA_V).astype(BF16)
    return o, wo


def _diff_attn_kernel(q_ref, kl_ref, kc_ref, vl_ref, vc_ref,
                      cq_ref, saq_ref, sbq_ref, ck_ref, sak_ref, sbk_ref, gq_ref, gk_ref,
                      lam_ref, gsub_ref, o_ref, k_scr, v_scr,
                      *, n_lat, n_ctx, nlat_q, scale, lam_init):
    qi = pl.program_id(2)
    nf = DIFF_HD // 4
    tabs = (ck_ref, sak_ref, sbk_ref)
    tile = lambda t: slice(t * LANES, (t + 1) * LANES)
    n_heads = o_ref.shape[1] // LANES

    @pl.when(qi == 0)
    def _():
        for hd in range(n_heads):
            ln = tile(hd)
            for sub in range(2):
                kt = tile(2 * hd + sub)
                _prep_keys(k_scr.at[2 * hd + sub], v_scr.at[hd] if sub == 0 else None, n_lat, n_ctx,
                           lambda r, kt=kt: kl_ref[r, kt], lambda r, kt=kt: kc_ref[r, kt],
                           lambda r, ln=ln: vl_ref[r, ln], lambda r, ln=ln: vc_ref[r, ln],
                           gk_ref[...], tabs, DIFF_HD, nf)

    lp = lam_ref[...]
    lam = (jnp.exp(jnp.sum(lp[0:1] * lp[1:2], axis=-1, keepdims=True))
           - jnp.exp(jnp.sum(lp[2:3] * lp[3:4], axis=-1, keepdims=True)) + lam_init)

    def attend(keys):
        def scores(t, rows):
            q = _head_norm(q_ref[rows, tile(t)].astype(F32), gq_ref[...], DIFF_HD)
            q = _rope(q, cq_ref[rows, :], saq_ref[rows, :], sbq_ref[rows, :], nf) * (scale * LOG2E)
            return _nt_dot(q.astype(BF16), k_scr[t, keys, :])

        def chain(hd, rows):
            s0 = scores(2 * hd, rows)
            s1 = scores(2 * hd + 1, rows)
            yield
            p0 = jnp.exp2(s0 - jnp.max(s0, axis=-1, keepdims=True))
            p1 = jnp.exp2(s1 - jnp.max(s1, axis=-1, keepdims=True))
            l0 = jnp.sum(p0, axis=-1, keepdims=True)
            l1 = jnp.sum(p1, axis=-1, keepdims=True)
            a = (p0 - p1 * (lam * l0 / l1)).astype(BF16)
            yield
            o = jnp.dot(a, v_scr[hd, keys, :], preferred_element_type=F32) * (1.0 / l0)
            o = _head_norm(o, gsub_ref[...], 2 * DIFF_HD) * (1.0 - lam_init)
            o_ref[rows, tile(hd)] = o.astype(o_ref.dtype)

        _emit_pipelined([chain(hd, rows) for rows in _row_chains(q_ref.shape[0])
                         for hd in range(n_heads)])

    @pl.when(qi < nlat_q)
    def _():
        attend(slice(0, n_lat + n_ctx))

    @pl.when(qi >= nlat_q)
    def _():
        attend(slice(n_lat, n_lat + n_ctx))


def _diff_attention(qkv, g_q, g_k, lam_p, g_sub, lam_init, *, n_batch, n_lat, n_ctx, need_ctx):
    nlat_q, nctx_q = n_lat // TQ, n_ctx // TQ
    nq = nlat_q + (nctx_q if need_ctx else 0)
    rows = n_batch * (n_lat + (n_ctx if need_ctx else 0))
    nf = DIFF_HD // 4
    tabs = _rope_tables(n_lat, TQ, ((0, True), (2 * nf, False)), nf)
    nh = DIFF_HEADS
    ctx0 = n_batch * n_lat // n_ctx
    gq = jnp.zeros((1, LANES), F32).at[0, :DIFF_HD].set(g_q)
    gk = jnp.zeros((1, LANES), F32).at[0, :DIFF_HD].set(g_k)
    lam8 = jnp.zeros((SUBLANES, LANES), F32).at[:4, :DIFF_HD].set(lam_p.astype(F32))
    rb = lambda b, qi: _row_block(b, qi, nlat_q, nctx_q, n_batch)
    tq_spec = pl.BlockSpec((TQ, LANES), lambda b, hh, qi: (jnp.minimum(qi, nlat_q), 0))
    tk_spec = pl.BlockSpec((n_lat, LANES), lambda b, hh, qi: (0, 0))
    vec = pl.BlockSpec((1, LANES), lambda b, hh, qi: (0, 0))
    hps = HEADS_PER_STEP
    qk_w, v_w = 2 * hps * LANES, hps * LANES
    k_blk0 = 2 * nh // (2 * hps)
    v_blk0 = 4 * nh // hps
    kern = functools.partial(_diff_attn_kernel, n_lat=n_lat, n_ctx=n_ctx, nlat_q=nlat_q,
                             scale=DIFF_HD ** -0.5, lam_init=lam_init)
    n_all = n_lat + n_ctx
    return pl.pallas_call(
        kern,
        grid=(n_batch, nh // hps, nq),
        in_specs=[pl.BlockSpec((TQ, qk_w), lambda b, hh, qi: (rb(b, qi), hh)),
                  pl.BlockSpec((n_lat, qk_w), lambda b, hh, qi: (b, k_blk0 + hh)),
                  pl.BlockSpec((n_ctx, qk_w), lambda b, hh, qi: (ctx0 + b, k_blk0 + hh)),
                  pl.BlockSpec((n_lat, v_w), lambda b, hh, qi: (b, v_blk0 + hh)),
                  pl.BlockSpec((n_ctx, v_w), lambda b, hh, qi: (ctx0 + b, v_blk0 + hh)),
                  tq_spec, tq_spec, tq_spec, tk_spec, tk_spec, tk_spec, vec, vec,
                  pl.BlockSpec((SUBLANES, LANES), lambda b, hh, qi: (0, 0)), vec],
        out_specs=pl.BlockSpec((TQ, v_w), lambda b, hh, qi: (rb(b, qi), hh)),
        out_shape=jax.ShapeDtypeStruct((rows, nh * LANES), BF16),
        scratch_shapes=[pltpu.VMEM((2 * hps, n_all, LANES), BF16), pltpu.VMEM((hps, n_all, LANES), BF16)],
        compiler_params=_cparams(("parallel", "parallel", "arbitrary")),
        name="diff_attention",
    )(qkv, qkv, qkv, qkv, qkv, *tabs, *tabs, gq, gk, lam8, g_sub.reshape(1, LANES))


def _diff_mixer(s, modv, g_mix, w_qkv, g_q, g_k, lam_p, g_sub, w_o, lam_init,
                *, n_batch, n_lat, n_ctx, need_ctx):
    d = s.shape[1]
    width = DIFF_HEADS * 2 * DIFF_HD
    rows_all = n_batch * (n_lat + n_ctx)
    w = jnp.concatenate([_pad_cols(w_qkv[:, :width], 2 * DIFF_HEADS, DIFF_HD),
                         _pad_cols(w_qkv[:, width:2 * width], 2 * DIFF_HEADS, DIFF_HD),
                         w_qkv[:, 2 * width:]], axis=1).astype(BF16)
    qkv = _nm_matmul(s, 0, d, g_mix, modv, rows_all, w, n_lat_rows=n_batch * n_lat,
                     n_per_batch=n_lat, name="diff_qkv", out_dtype=BF16)
    o = _diff_attention(qkv, g_q, g_k, lam_p, g_sub, lam_init,
                        n_batch=n_batch, n_lat=n_lat, n_ctx=n_ctx, need_ctx=need_ctx)
    return o, w_o.astype(BF16)


def _swa_attn_kernel(q_ref, kl_ref, kc_ref, vl_ref, vc_ref,
                     cq_ref, saq_ref, sbq_ref, ck_ref, sak_ref, sbk_ref, gq_ref, gk_ref, sink_ref,
                     o_ref, k_scr, v_scr, *, n_lat, n_ctx, nlat_q, scale):
    qi = pl.program_id(2)
    nf = SWA_HD // 4
    band = 3 * Q_BLOCK
    n_kv = k_scr.shape[0]
    tile = lambda t: slice(t * LANES, (t + 1) * LANES)

    @pl.when(qi == 0)
    def _():
        for kv in range(n_kv):
            ln = tile(kv)
            _prep_keys(k_scr.at[kv], v_scr.at[kv], n_lat, n_ctx,
                       lambda r, ln=ln: kl_ref[r, ln], lambda r, ln=ln: kc_ref[r, ln],
                       lambda r, ln=ln: _with_ones_lane(vl_ref[r, ln], SWA_HD),
                       lambda r, ln=ln: _with_ones_lane(vc_ref[r, ln], SWA_HD), gk_ref[...],
                       (ck_ref, sak_ref, sbk_ref), SWA_HD, nf)

    row_chains = _row_chains(q_ref.shape[0])
    reach = jnp.where(qi < nlat_q, WINDOW, -1)

    def chain(ci, rows, gi):
        kv = gi // SWA_GROUP
        qb = qi * len(row_chains) + ci
        k0 = pl.multiple_of(jnp.clip((qb - 1) * Q_BLOCK, 0, n_lat - band), Q_BLOCK)
        kpos = k0 + lax.broadcasted_iota(jnp.int32, (Q_BLOCK, band), 1)
        qpos = qb * Q_BLOCK + lax.broadcasted_iota(jnp.int32, (Q_BLOCK, band), 0)
        lanes = tile(gi)
        q = _head_norm(q_ref[rows, lanes].astype(F32), gq_ref[...], SWA_HD)
        q = _rope(q, cq_ref[rows, :], saq_ref[rows, :], sbq_ref[rows, :], nf) * (scale * LOG2E)
        q = q.astype(BF16)
        sb = _nt_dot(q, k_scr[kv, pl.ds(k0, band), :])
        sc = _nt_dot(q, k_scr[kv, n_lat:n_lat + n_ctx, :])
        yield
        sb = jnp.where(jnp.abs(kpos - qpos) <= reach, sb, NEG_INF)
        sink = sink_ref[gi][0:1, 0:1] * LOG2E
        m = jnp.maximum(jnp.maximum(jnp.max(sb, axis=-1, keepdims=True),
                                    jnp.max(sc, axis=-1, keepdims=True)), sink)
        pb = jnp.exp2(sb - m).astype(BF16)
        pc = jnp.exp2(sc - m).astype(BF16)
        p_sink = jnp.exp2(sink - m)
        yield
        o = (jnp.dot(pb, v_scr[kv, pl.ds(k0, band), :], preferred_element_type=F32)
             + jnp.dot(pc, v_scr[kv, n_lat:n_lat + n_ctx, :], preferred_element_type=F32))
        l = o[:, SWA_HD:SWA_HD + 1] + p_sink
        o_ref[rows, lanes] = (o * (1.0 / l)).astype(o_ref.dtype)

    _emit_pipelined([chain(ci, rows, gi) for ci, rows in enumerate(row_chains)
                     for gi in range(n_kv * SWA_GROUP)])


def _swa_attention(qkv, g_q, g_k, sink, *, n_batch, n_lat, n_ctx, need_ctx):
    nlat_q, nctx_q = n_lat // TQ, n_ctx // TQ
    nq = nlat_q + (nctx_q if need_ctx else 0)
    rows = n_batch * (n_lat + (n_ctx if need_ctx else 0))
    nf = SWA_HD // 4
    tabs = _rope_tables(n_lat, TQ, ((0, True), (2 * nf, False)), nf)
    nh, nkv, grp = SWA_HEADS, SWA_KV_HEADS, SWA_GROUP
    ctx0 = n_batch * n_lat // n_ctx
    gq = jnp.zeros((1, LANES), F32).at[0, :SWA_HD].set(g_q)
    gk = jnp.zeros((1, LANES), F32).at[0, :SWA_HD].set(g_k)
    sink_t = jnp.broadcast_to(sink.astype(F32).reshape(nh, 1, 1), (nh, SUBLANES, LANES))
    rb = lambda b, qi: _row_block(b, qi, nlat_q, nctx_q, n_batch)
    tq_spec = pl.BlockSpec((TQ, LANES), lambda b, kh, qi: (jnp.minimum(qi, nlat_q), 0))
    tk_spec = pl.BlockSpec((n_lat, LANES), lambda b, kh, qi: (0, 0))
    vec = pl.BlockSpec((1, LANES), lambda b, kh, qi: (0, 0))
    kern = functools.partial(_swa_attn_kernel, n_lat=n_lat, n_ctx=n_ctx, nlat_q=nlat_q,
                             scale=SWA_HD ** -0.5)
    n_all = n_lat + n_ctx
    kps = SWA_KV_PER_STEP
    q_w, kv_w = kps * grp * LANES, kps * LANES
    k_blk0, v_blk0 = nh // kps, (nh + nkv) // kps
    return pl.pallas_call(
        kern,
        grid=(n_batch, nkv // kps, nq),
        in_specs=[pl.BlockSpec((TQ, q_w), lambda b, kh, qi: (rb(b, qi), kh)),
                  pl.BlockSpec((n_lat, kv_w), lambda b, kh, qi: (b, k_blk0 + kh)),
                  pl.BlockSpec((n_ctx, kv_w), lambda b, kh, qi: (ctx0 + b, k_blk0 + kh)),
                  pl.BlockSpec((n_lat, kv_w), lambda b, kh, qi: (b, v_blk0 + kh)),
                  pl.BlockSpec((n_ctx, kv_w), lambda b, kh, qi: (ctx0 + b, v_blk0 + kh)),
                  tq_spec, tq_spec, tq_spec, tk_spec, tk_spec, tk_spec, vec, vec,
                  pl.BlockSpec((kps * grp, SUBLANES, LANES), lambda b, kh, qi: (kh, 0, 0))],
        out_specs=pl.BlockSpec((TQ, q_w), lambda b, kh, qi: (rb(b, qi), kh)),
        out_shape=jax.ShapeDtypeStruct((rows, nh * LANES), BF16),
        scratch_shapes=[pltpu.VMEM((kps, n_all, LANES), BF16), pltpu.VMEM((kps, n_all, LANES), BF16)],
        compiler_params=_cparams(("parallel", "parallel", "arbitrary")),
        name="swa_attention",
    )(qkv, qkv, qkv, qkv, qkv, *tabs, *tabs, gq, gk, sink_t)


def _swa_mixer(s, modv, g_mix, w_qkv, g_q, g_k, sink, w_o, *, n_batch, n_lat, n_ctx, need_ctx):
    d = s.shape[1]
    qw, kw = SWA_HEADS * SWA_HD, SWA_KV_HEADS * SWA_HD
    rows_all = n_batch * (n_lat + n_ctx)
    w = jnp.concatenate([_pad_cols(w_qkv[:, :qw], SWA_HEADS, SWA_HD),
                         _pad_cols(w_qkv[:, qw:qw + kw], SWA_KV_HEADS, SWA_HD),
                         _pad_cols(w_qkv[:, qw + kw:], SWA_KV_HEADS, SWA_HD)], axis=1).astype(BF16)
    qkv = _nm_matmul(s, 0, d, g_mix, modv, rows_all, w, n_lat_rows=n_batch * n_lat,
                     n_per_batch=n_lat, name="swa_qkv", out_dtype=BF16)
    o = _swa_attention(qkv, g_q, g_k, sink, n_batch=n_batch, n_lat=n_lat, n_ctx=n_ctx, need_ctx=need_ctx)
    return o, _pad_rows(w_o, SWA_HEADS, SWA_HD).astype(BF16)


def _merge_exchange_pairs(n):
    pairs = []
    t = max(1, (n - 1).bit_length())
    p = 1 << (t - 1)
    while p > 0:
        q, r, d = 1 << (t - 1), 0, p
        while d > 0:
            pairs += [(i, i + d) for i in range(n - d) if (i & p) == r]
            d, q, r = q - p, q >> 1, p
        p >>= 1
    return pairs


def _top_values(s, count):
    n = s.shape[0] // SUBLANES
    v = [s[j * SUBLANES:(j + 1) * SUBLANES, :] for j in range(n)]
    for i, j in _merge_exchange_pairs(n):
        v[i], v[j] = jnp.maximum(v[i], v[j]), jnp.minimum(v[i], v[j])
    vals = []
    for r in range(count):
        m = jnp.max(v[0], axis=0, keepdims=True)
        vals.append(m)
        depth = min(n, count - 1 - r)
        head = v[0] == m
        for j in range(depth):
            nxt = v[j + 1] if j + 1 < n else jnp.full_like(v[j], NEG_BIG)
            v[j] = jnp.where(head, nxt, v[j])
    return vals


def _peer_score_kernel(s_ref, g_ref, mod_ref, wq_ref, keys_ref,
                       ht_ref, c_ref, e1_ref, s2_ref, e2_ref, qt_scr):
    k = PEER_TOPK
    h = _norm_mod(s_ref[...], g_ref[...], mod_ref[0], 3, 4)
    ht = h.T.astype(BF16)
    ht_ref[0] = ht

    tm = ht.shape[1]
    group = 2
    q_rows = group * 2 * LANES

    def group_chain(gidx):
        heads = [gidx * group + i for i in range(group)]
        rows = slice(gidx * q_rows, (gidx + 1) * q_rows)
        qt_scr[rows, :] = jnp.dot(wq_ref[rows, :], ht, preferred_element_type=F32)
        yield
        scores = []
        for hh in heads:
            for half in range(2):
                r = slice((hh * 2 + half) * LANES, (hh * 2 + half + 1) * LANES)
                scores.append(jnp.dot(keys_ref[hh, half], qt_scr[r, :].astype(BF16),
                                      preferred_element_type=F32))
        tops = _top_values(jnp.concatenate(scores, axis=1), k + 1)
        cands = []
        for i in range(group):
            v1 = [t[:, (2 * i) * tm:(2 * i + 1) * tm] for t in tops]
            v2 = [t[:, (2 * i + 1) * tm:(2 * i + 2) * tm] for t in tops]
            top8 = jnp.concatenate(v2[:8], axis=0)
            pieces = [v1[0] + jnp.concatenate(v2[:k], axis=0)]
            pieces += [v1[j] + top8 for j in range(1, 8)]
            pieces += [jnp.concatenate(v1[8:k], axis=0) + v2[0]]
            pieces += [jnp.concatenate([v1[0] + v2[k], v1[k] + v2[0]]
                                       + [jnp.full_like(v1[0], NEG_BIG)] * 6, axis=0)]
            cands.append(jnp.concatenate(pieces, axis=0))
        best = _top_values(jnp.concatenate(cands, axis=1), k + 1)
        grp = (PEER_KEYS // SUBLANES, SUBLANES, tm)
        for i, hh in enumerate(heads):
            ms = [t[:, i * tm:(i + 1) * tm] for t in best]
            s1, s2 = scores[2 * i], scores[2 * i + 1]
            m1, m2 = tops[0][:, (2 * i) * tm:(2 * i + 1) * tm], tops[0][:, (2 * i + 1) * tm:(2 * i + 2) * tm]
            thr = 0.5 * (ms[k - 1] + ms[k])
            z = jnp.ones_like(ms[0])
            for r in range(1, k):
                z = z + jnp.exp(ms[r] - ms[0])
            c_ref[hh] = (thr - s1).reshape(grp)
            e1_ref[hh] = (jnp.exp(s1 - m1) * (0.5 / z)).reshape(grp)
            s2_ref[0, hh] = s2
            e2_ref[0, hh] = jnp.exp(s2 - m2)

    _emit_pipelined([group_chain(gidx) for gidx in range(PEER_HEADS // group)])


def _peer_scores(s, modv, g_ffn, wq_t, keys, rows, *, n_lat_rows, n_per_batch):
    d = s.shape[1]
    n_batch = modv.shape[0] - 1
    tm = SCORE_TM
    nlat_blocks = n_lat_rows // tm
    bpb = n_per_batch // tm
    nq = wq_t.shape[0]
    hk = (rows // tm, PEER_HEADS, PEER_KEYS, tm)
    hk_spec = pl.BlockSpec((1, PEER_HEADS, PEER_KEYS, tm), lambda i: (i, 0, 0, 0))
    hg = (PEER_HEADS, PEER_KEYS // SUBLANES, SUBLANES, rows)
    hg_spec = pl.BlockSpec((PEER_HEADS, PEER_KEYS // SUBLANES, SUBLANES, tm), lambda i: (0, 0, 0, i))
    return pl.pallas_call(
        _peer_score_kernel,
        grid=(rows // tm,),
        in_specs=[pl.BlockSpec((tm, d), lambda i: (i, 0)),
                  pl.BlockSpec((1, d), lambda i: (0, 0)),
                  pl.BlockSpec((1, SUBLANES, d), lambda i: (_mod_index(i, nlat_blocks, bpb, n_batch), 0, 0)),
                  pl.BlockSpec((nq, d), lambda i: (0, 0)),
                  pl.BlockSpec((PEER_HEADS, 2, PEER_KEYS, LANES), lambda i: (0, 0, 0, 0))],
        out_specs=[pl.BlockSpec((1, d, tm), lambda i: (i, 0, 0)), hg_spec, hg_spec, hk_spec, hk_spec],
        out_shape=[jax.ShapeDtypeStruct((rows // tm, d, tm), BF16)] + [jax.ShapeDtypeStruct(hg, F32)] * 2
                  + [jax.ShapeDtypeStruct(hk, F32)] * 2,
        scratch_shapes=[pltpu.VMEM((nq, tm), F32)],
        compiler_params=_cparams(("parallel",)),
        name="peer_scores",
    )(s, g_ffn.reshape(1, d), modv, wq_t, keys)


def _peer_expert_kernel(ht_ref, u_ref, vt_ref, c_ref, e1_ref, s2_ref, e2_ref, s_ref, mod_ref,
                        o_ref, acc_scr, st0_scr, st1_scr, at0_scr, at1_scr, *, n_exp_blocks):
    g = pl.program_id(0)
    st_scr, at_scr = (st0_scr, st1_scr), (at0_scr, at1_scr)
    te, tm = st0_scr.shape
    jc = jnp.maximum(g - 2, 0) % n_exp_blocks

    @pl.when(g == 0)
    def _():
        for ref in st_scr + at_scr:
            ref[...] = jnp.zeros_like(ref)

    @pl.when(jc == 0)
    def _():
        acc_scr[...] = jnp.zeros_like(acc_scr)

    def stages(slot_a, slot_b):
        n_chunks = te // MXU_CHUNK_ROWS

        def value_chunk(q):
            ms = slice(q * MXU_CHUNK_ROWS, (q + 1) * MXU_CHUNK_ROWS)
            acc_scr[ms, :] += jnp.dot(vt_ref[0, 0, ms, :], at_scr[slot_a][...], preferred_element_type=F32)

        def score_chunk(q):
            ms = slice(q * MXU_CHUNK_ROWS, (q + 1) * MXU_CHUNK_ROWS)
            st_scr[slot_a][ms, :] = jnp.dot(u_ref[0, ms, :], ht_ref[0], preferred_element_type=F32)

        rg_rows = 4 * SUBLANES
        al_group = 4
        col_tiles = 1
        tiles = [(rg, c0, al0)
                 for rg in range(LANES // rg_rows)
                 for c0 in range(0, tm // LANES, col_tiles)
                 for al0 in range(0, te // LANES, al_group)]
        assert acc_scr.shape[0] == te
        per_chunk = len(tiles) // (2 * n_chunks)
        for t, (rg, c0, al0) in enumerate(tiles):
            if t % per_chunk == 0:
                q = t // per_chunk
                (value_chunk if q % 2 == 0 else score_chunk)(q // 2)
            brows = slice(rg * rg_rows, (rg + 1) * rg_rows)
            lanes = [slice((c0 + ct) * LANES, (c0 + ct + 1) * LANES) for ct in range(col_tiles)]
            gate = [[jnp.zeros((rg_rows, LANES), F32) for _ in lanes] for _ in range(al_group)]
            for hh in range(PEER_HEADS):
                for ct, ln in enumerate(lanes):
                    s2t = s2_ref[0, hh, brows, ln]
                    e2t = e2_ref[0, hh, brows, ln]
                    for k in range(al_group):
                        al = al0 + k
                        c = c_ref[hh, 0, al:al + 1, ln]
                        e1 = e1_ref[hh, 0, al:al + 1, ln]
                        gate[k][ct] = gate[k][ct] + jnp.where(s2t >= c, e2t, 0.0) * e1
            for k in range(al_group):
                erows = slice((al0 + k) * LANES + rg * rg_rows, (al0 + k) * LANES + (rg + 1) * rg_rows)
                for ct, ln in enumerate(lanes):
                    sv = st_scr[slot_b][erows, ln]
                    act = sv * (1.0 + lax.erf(sv * (2.0 ** -0.5)))
                    at_scr[slot_b][erows, ln] = (gate[k][ct] * act).astype(BF16)

    for parity in range(2):
        @pl.when(g % 2 == parity)
        def _():
            stages(parity, 1 - parity)

    @pl.when((jc == n_exp_blocks - 1) & (g >= 2))
    def _():
        o_ref[...] = s_ref[...] + mod_ref[0][5:6] * acc_scr[...].T


def _peer_experts(ht, u, vt, layer, c, e1, s2, e2, s, modv, rows, *, n_lat_rows, n_per_batch):
    d = s.shape[1]
    n_exp = u.shape[1]
    n_batch = modv.shape[0] - 1
    tm, te = PEER_TM, PEER_TE
    nlat_blocks = n_lat_rows // tm
    bpb = n_per_batch // tm
    nj = n_exp // te
    n_pairs = (rows // tm) * nj

    def pair(g, lag):
        p = jnp.clip(g - lag, 0, n_pairs - 1)
        return p // nj, p % nj

    assert SCORE_TM == tm
    hk_spec = pl.BlockSpec((1, PEER_HEADS, PEER_KEYS, tm), lambda g: (pair(g, 1)[0], 0, 0, 0))
    hg_spec = pl.BlockSpec((PEER_HEADS, 1, SUBLANES, tm), lambda g: (0, pair(g, 1)[1], 0, pair(g, 1)[0]))
    kern = functools.partial(_peer_expert_kernel, n_exp_blocks=nj)
    return pl.pallas_call(
        kern,
        grid=(n_pairs + 2,),
        in_specs=[pl.BlockSpec((1, d, tm), lambda g: (pair(g, 0)[0], 0, 0)),
                  pl.BlockSpec((1, te, d), lambda g: (layer, pair(g, 0)[1], 0)),
                  pl.BlockSpec((1, 1, d, te), lambda g: (layer, pair(g, 2)[1], 0, 0)),
                  hg_spec, hg_spec, hk_spec, hk_spec,
                  pl.BlockSpec((tm, d), lambda g: (pair(g, 2)[0], 0)),
                  pl.BlockSpec((1, SUBLANES, d),
                               lambda g: (_mod_index(pair(g, 2)[0], nlat_blocks, bpb, n_batch), 0, 0))],
        out_specs=pl.BlockSpec((tm, d), lambda g: (pair(g, 2)[0], 0)),
        out_shape=jax.ShapeDtypeStruct((rows, d), F32),
        scratch_shapes=[pltpu.VMEM((d, tm), F32), pltpu.VMEM((te, tm), F32), pltpu.VMEM((te, tm), F32),
                        pltpu.VMEM((te, tm), BF16), pltpu.VMEM((te, tm), BF16)],
        compiler_params=_cparams(("arbitrary",)),
        name="peer_experts",
    )(ht, u, vt, c, e1, s2, e2, s, modv)


def _peer(s, modv, g_ffn, w_q, keys, u_all, vt_all, layer, rows, *, n_lat_rows, n_per_batch):
    geo = dict(n_lat_rows=n_lat_rows, n_per_batch=n_per_batch)
    ht, c, e1, s2, e2 = _peer_scores(s, modv, g_ffn, w_q.T.astype(BF16), keys.astype(BF16), rows, **geo)
    return _peer_experts(ht, u_all, vt_all, layer, c, e1, s2, e2, s, modv, rows, **geo)


def kernel(x, c, ctx, c_ctx, ada_w, ada_b, norm_mix, norm_ffn, mla_w_down, mla_g_cq, mla_g_ckv, mla_w_uq, mla_w_ukv, mla_g_q, mla_g_k, mla_w_o, diff_w_qkv, diff_g_q, diff_g_k, diff_lambda, diff_g_sub, diff_w_o, swa_w_qkv, swa_g_q, swa_g_k, swa_sink, swa_w_o, peer_w_q, peer_keys, peer_u, peer_v):
    n_batch, n_lat, d = x.shape
    n_ctx = ctx.shape[1]
    depth = ada_w.shape[0]
    rows_lat = n_batch * n_lat
    rows_all = rows_lat + n_batch * n_ctx
    blk = max(TM, NM_TM, PEER_TM, SCORE_TM)
    assert n_lat % blk == 0 and n_lat % TQ == 0 and (n_batch * n_ctx) % blk == 0
    assert n_ctx % TQ == 0 and n_ctx % KEY_PREP_ROWS == 0 and n_lat % KEY_PREP_ROWS == 0
    assert n_lat % GRID_W == 0 and rows_lat % n_ctx == 0 and n_lat >= 3 * Q_BLOCK

    s = jnp.concatenate([x.reshape(rows_lat, d), ctx.reshape(n_batch * n_ctx, d)], axis=0)
    cond_rows = -(-(n_batch + 1) // SUBLANES) * SUBLANES
    cc = jnp.zeros((cond_rows, d), F32).at[:n_batch].set(c).at[n_batch].set(c_ctx)
    mods = _ada_mods(cc, ada_w, ada_b)
    geo = dict(n_lat_rows=rows_lat, n_per_batch=n_lat)
    n_exp = peer_u.shape[1]
    u_all = peer_u.astype(BF16)
    vt_all = peer_v.reshape(depth, n_exp // PEER_TE, PEER_TE, d).transpose(0, 1, 3, 2).astype(BF16)

    for i in range(depth):
        last = i == depth - 1
        kind, j = i % 3, i // 3
        modv = jnp.zeros((n_batch + 1, SUBLANES, d), F32).at[:, :6, :].set(
            mods[i, :n_batch + 1].reshape(n_batch + 1, 6, d))
        dims = dict(n_batch=n_batch, n_lat=n_lat, n_ctx=n_ctx, need_ctx=not last)
        if kind == 0:
            p = (mla_w_down[j], mla_g_cq[j], mla_g_ckv[j], mla_w_uq[j], mla_w_ukv[j], mla_g_q[j], mla_g_k[j])
            o, wo = _mla_mixer(s, modv, norm_mix[i], p, mla_w_o[j], **dims)
        elif kind == 1:
            lam_init = 0.8 - 0.6 * math.exp(-0.3 * i)
            o, wo = _diff_mixer(s, modv, norm_mix[i], diff_w_qkv[j], diff_g_q[j], diff_g_k[j],
                                diff_lambda[j], diff_g_sub[j], diff_w_o[j], lam_init, **dims)
        else:
            o, wo = _swa_mixer(s, modv, norm_mix[i], swa_w_qkv[j], swa_g_q[j], swa_g_k[j],
                               swa_sink[j], swa_w_o[j], **dims)
        rows = rows_lat if last else rows_all
        s = _proj_resid(o, wo, s, modv, rows, gate_row=2, **geo)
        s = _peer(s, modv, norm_ffn[i], peer_w_q[i], peer_keys[i], u_all, vt_all, i, rows, **geo)
    return s.reshape(n_batch, n_lat, d)
```
